```python
import jax, jax.numpy as jnp
from jax import lax
import numpy as np

D_MODEL = 2048
BATCH = 4
SEQ = 2048
DEPTH = 1

CHUNK = 64
Q_BLOCK = 128
FOX_HEADS = 8
FOX_HEAD_DIM = 128
FOX_WIDTH = FOX_HEADS * FOX_HEAD_DIM
RWKV_HEAD_DIM = 64
RWKV_HEADS = 16
RWKV_WIDTH = RWKV_HEADS * RWKV_HEAD_DIM
DECAY_LORA = 96
AAA_LORA = 96
GATE_LORA = 256
RWKV_LNX_EPS = 64e-5
D_FF = ((8 * D_MODEL // 3 + 255) // 256) * 256
DEEPNORM_ALPHA = (2.0 * DEPTH) ** 0.25
DEEPNORM_BETA = (8.0 * DEPTH) ** -0.25
LN_EPS = 1e-5
RMS_EPS = 1e-6
FOX_COLS = 4 * FOX_WIDTH + FOX_HEADS
RWKV_COLS = 3 * RWKV_WIDTH + DECAY_LORA + AAA_LORA + GATE_LORA
IN_COLS = FOX_COLS + RWKV_COLS + 2 * D_MODEL

kernel_name = "fox_rwkv7_gated_hybrid_deepnorm_adaln"


def _layernorm(x):
    xf = x.astype(jnp.float32)
    mu = jnp.mean(xf, axis=-1, keepdims=True)
    var = jnp.mean(jnp.square(xf - mu), axis=-1, keepdims=True)
    return ((xf - mu) * lax.rsqrt(var + LN_EPS)).astype(x.dtype)


def _rmsnorm(x, g):
    xf = x.astype(jnp.float32)
    y = xf * lax.rsqrt(jnp.mean(jnp.square(xf), axis=-1, keepdims=True) + RMS_EPS)
    return y.astype(x.dtype) * g


def _forgetting_attention(q, k, v, log_f):
    s_len = q.shape[2]
    cum = jnp.cumsum(log_f, axis=-1)
    scale = FOX_HEAD_DIM ** -0.5
    outs = []
    for blk in range(s_len // Q_BLOCK):
        q0, q1 = blk * Q_BLOCK, (blk + 1) * Q_BLOCK
        qb = q[:, :, q0:q1]
        kb, vb = k[:, :, :q1], v[:, :, :q1]
        logits = jnp.einsum('bhqd,bhkd->bhqk', qb, kb).astype(jnp.float32) * scale
        logits = logits + cum[:, :, q0:q1, None] - cum[:, :, None, :q1]
        qpos = jnp.arange(q0, q1)[:, None]
        kpos = jnp.arange(q1)[None, :]
        logits = jnp.where(kpos <= qpos, logits, -jnp.inf)
        p = jax.nn.softmax(logits, axis=-1)
        outs.append(jnp.einsum('bhqk,bhkd->bhqd', p.astype(vb.dtype), vb))
    return jnp.concatenate(outs, axis=2)


def _rwkv7_recurrence(r, w, k, v, a, b):
    bsz, _, heads, n = r.shape

    def step(state, inp):
        r_t, w_t, k_t, v_t, a_t, b_t = inp
        sa = jnp.einsum('bhvk,bhk->bhv', state, a_t)
        state = (state * w_t[:, :, None, :]
                 + sa[..., None] * b_t[:, :, None, :]
                 + v_t[..., None] * k_t[:, :, None, :])
        y = jnp.einsum('bhvk,bhk->bhv', state, r_t)
        return state, y

    xs = tuple(jnp.moveaxis(t, 1, 0) for t in (r, w, k, v, a, b))
    s0 = jnp.zeros((bsz, heads, n, n), jnp.float32)
    _, ys = lax.scan(step, s0, xs)
    return jnp.moveaxis(ys, 0, 1)


def setup_inputs(seed: int = 0) -> dict:
    key = jax.random.key(seed)
    ks = jax.random.split(key, 32)
    f32 = jnp.float32
    nrm = lambda i, shape, s: (jax.random.normal(ks[i], shape, f32) * s).astype(f32)
    D = D_MODEL
    return {
        "x": nrm(0, (BATCH, SEQ, D), 1.0),
        "c": nrm(1, (BATCH, D), 1.0),
        "w_ada": nrm(2, (D, 6 * D), 0.5 * D ** -0.5),
        "b_ada": nrm(3, (6 * D,), 0.01),
        "w_in": nrm(4, (D, IN_COLS), D ** -0.5),
        "b_fgate": 2.0 + nrm(5, (FOX_HEADS,), 0.5),
        "q_norm_g": 1.0 + nrm(6, (FOX_HEAD_DIM,), 0.02),
        "k_norm_g": 1.0 + nrm(7, (FOX_HEAD_DIM,), 0.02),
        "rwkv_mu": jax.random.uniform(ks[8], (RWKV_COLS,), f32, 0.1, 0.9),
        "rwkv_w0": nrm(9, (RWKV_WIDTH,), 0.5),
        "rwkv_w2": nrm(10, (DECAY_LORA, RWKV_WIDTH), 0.5 * DECAY_LORA ** -0.5),
        "rwkv_a0": nrm(11, (RWKV_WIDTH,), 0.1),
        "rwkv_a2": nrm(12, (AAA_LORA, RWKV_WIDTH), AAA_LORA ** -0.5),
        "rwkv_g2": nrm(13, (GATE_LORA, RWKV_WIDTH), GATE_LORA ** -0.5),
        "rwkv_k_k": 0.85 + nrm(14, (RWKV_WIDTH,), 0.02),
        "rwkv_k_a": 1.0 + nrm(15, (RWKV_WIDTH,), 0.02),
        "rwkv_r_k": nrm(16, (RWKV_HEADS, RWKV_HEAD_DIM), 0.1),
        "rwkv_lnx_g": 1.0 + nrm(17, (RWKV_WIDTH,), 0.02),
        "rwkv_lnx_b": nrm(18, (RWKV_WIDTH,), 0.01),
        "w_branch_a": nrm(19, (FOX_WIDTH, D), DEEPNORM_BETA * FOX_WIDTH ** -0.5),
        "w_branch_b": nrm(20, (RWKV_WIDTH, D), DEEPNORM_BETA * RWKV_WIDTH ** -0.5),
        "w_out": nrm(21, (D, D), DEEPNORM_BETA * D ** -0.5),
        "ln1_g": 1.0 + nrm(22, (D,), 0.02),
        "ln1_b": nrm(23, (D,), 0.01),
        "w_ffn_gu": nrm(24, (D, 2 * D_FF), D ** -0.5),
        "w_ffn_down": nrm(25, (D_FF, D), DEEPNORM_BETA * D_FF ** -0.5),
        "ln2_g": 1.0 + nrm(26, (D,), 0.02),
        "ln2_b": nrm(27, (D,), 0.01),
    }


def reference(x, c, w_ada, b_ada, w_in, b_fgate, q_norm_g, k_norm_g,
              rwkv_mu, rwkv_w0, rwkv_w2, rwkv_a0, rwkv_a2, rwkv_g2,
              rwkv_k_k, rwkv_k_a, rwkv_r_k, rwkv_lnx_g, rwkv_lnx_b,
              w_branch_a, w_branch_b, w_out, ln1_g, ln1_b,
              w_ffn_gu, w_ffn_down, ln2_g, ln2_b):
    assert x.shape[1] % CHUNK == 0
    B, S, _ = x.shape
    f32 = jnp.float32

    mod = jnp.einsum('bd,de->be', jax.nn.silu(c), w_ada) + b_ada
    shift1, scale1, gate1, shift2, scale2, gate2 = jnp.split(mod[:, None, :], 6, axis=-1)

    for _layer in range(DEPTH):
        h = _layernorm(x) * (1.0 + scale1) + shift1
        proj = jnp.einsum('bsd,de->bse', h, w_in)
        fox = proj[..., :FOX_COLS]
        rw = proj[..., FOX_COLS:FOX_COLS + RWKV_COLS]
        gate_a, gate_b = jnp.split(proj[..., FOX_COLS + RWKV_COLS:], 2, axis=-1)

        fq, fk, fv, f_og, f_logit = jnp.split(
            fox, [FOX_WIDTH, 2 * FOX_WIDTH, 3 * FOX_WIDTH, 4 * FOX_WIDTH], axis=-1)
        q = _rmsnorm(fq.reshape(B, S, FOX_HEADS, FOX_HEAD_DIM), q_norm_g)
        k = _rmsnorm(fk.reshape(B, S, FOX_HEADS, FOX_HEAD_DIM), k_norm_g)
        v = fv.reshape(B, S, FOX_HEADS, FOX_HEAD_DIM)
        log_f = jax.nn.log_sigmoid((f_logit + b_fgate).astype(f32))
        o = _forgetting_attention(q.transpose(0, 2, 1, 3), k.transpose(0, 2, 1, 3),
                                  v.transpose(0, 2, 1, 3), log_f.transpose(0, 2, 1))
        o = o.transpose(0, 2, 1, 3).reshape(B, S, FOX_WIDTH) * jax.nn.sigmoid(f_og)
        u_a = jnp.einsum('bse,ed->bsd', o, w_branch_a)

        rw_prev = jnp.pad(rw, ((0, 0), (1, 0), (0, 0)))[:, :-1]
        rw = rw + (rw_prev - rw) * rwkv_mu
        r, kr, vr, xw, xa, xg = jnp.split(
            rw, [RWKV_WIDTH, 2 * RWKV_WIDTH, 3 * RWKV_WIDTH, 3 * RWKV_WIDTH + DECAY_LORA,
                 3 * RWKV_WIDTH + DECAY_LORA + AAA_LORA], axis=-1)
        w_raw = -jax.nn.softplus(-(rwkv_w0 + jnp.tanh(xw) @ rwkv_w2)) - 0.5
        w = jnp.exp(-jnp.exp(w_raw.astype(f32)))
        a = jax.nn.sigmoid(rwkv_a0 + xa @ rwkv_a2)
        g = jax.nn.sigmoid(xg) @ rwkv_g2
        hs = lambda t: t.reshape(B, S, RWKV_HEADS, RWKV_HEAD_DIM).astype(f32)
        kk = hs(kr * rwkv_k_k)
        kk = kk / jnp.maximum(jnp.linalg.norm(kk, axis=-1, keepdims=True), 1e-12)
        kr = kr * (1.0 + (a - 1.0) * rwkv_k_a)
        rh, kh, vh, ah = hs(r), hs(kr), hs(vr), hs(a)
        y = _rwkv7_recurrence(rh, hs(w), kh, vh, -kk, kk * ah)
        mu_y = jnp.mean(y, axis=-1, keepdims=True)
        var_y = jnp.mean(jnp.square(y - mu_y), axis=-1, keepdims=True)
        y = (y - mu_y) * lax.rsqrt(var_y + RWKV_LNX_EPS)
        y = y.reshape(B, S, RWKV_WIDTH) * rwkv_lnx_g + rwkv_lnx_b
        bonus = jnp.sum(rh * kh * rwkv_r_k, axis=-1, keepdims=True) * vh
        y = (y + bonus.reshape(B, S, RWKV_WIDTH)).astype(x.dtype) * g
        u_b = jnp.einsum('bse,ed->bsd', y, w_branch_b)

        merged = jax.nn.sigmoid(gate_a) * u_a + jax.nn.sigmoid(gate_b) * u_b
        mix_out = jnp.einsum('bsd,de->bse', merged, w_out)
        x = _layernorm(DEEPNORM_ALPHA * x + gate1 * mix_out) * ln1_g + ln1_b

        h2 = _layernorm(x) * (1.0 + scale2) + shift2
        gu = jnp.einsum('bsd,df->bsf', h2, w_ffn_gu)
        ff_gate, ff_up = jnp.split(gu, 2, axis=-1)
        ffn = jnp.einsum('bsf,fd->bsd', jax.nn.silu(ff_gate) * ff_up, w_ffn_down)
        x = _layernorm(DEEPNORM_ALPHA * x + gate2 * ffn) * ln2_g + ln2_b

    return x
```

```python
import functools

import jax
import jax.numpy as jnp
from jax import lax
from jax.experimental import pallas as pl
from jax.experimental.pallas import tpu as pltpu

F32 = jnp.float32
BF16 = jnp.bfloat16

LANES = 128
VMEM_CAP_BYTES = 60000 * 1024

FOX_HEADS = 8
FOX_HEAD_DIM = 128
RWKV_HEADS = 16
RWKV_HEAD_DIM = 64
DECAY_LORA = 96
AAA_LORA = 96
GATE_LORA = 256
LORA_PAD = 128
RWKV_LNX_EPS = 64e-5
DEPTH = 1
DEEPNORM_ALPHA = (2.0 * DEPTH) ** 0.25
LN_EPS = 1e-5
RMS_EPS = 1e-6
RWKV_CHUNK = 64
HEADS_PER_TILE = LANES // RWKV_HEAD_DIM


def _vmem_limit(pipelined_bytes, resident_bytes=0, temp_bytes=0):
    need = 2 * pipelined_bytes + resident_bytes + temp_bytes + (2 << 20)
    return int(min(VMEM_CAP_BYTES, need))


def _nbytes(shape, dtype):
    n = 1
    for s in shape:
        n *= s
    return n * jnp.dtype(dtype).itemsize


def _layernorm(x):
    mu = jnp.mean(x, axis=-1, keepdims=True)
    xc = x - mu
    var = jnp.mean(xc * xc, axis=-1, keepdims=True)
    return xc * lax.rsqrt(var + LN_EPS)


def _sigmoid(x):
    return 1.0 / (1.0 + jnp.exp(-x))


def _dot(a, b):
    return jnp.dot(a, b, preferred_element_type=F32)


def _dot_nt(a, b):
    return lax.dot_general(a, b, (((1,), (1,)), ((), ())), preferred_element_type=F32)


def _dot_tn(a, b):
    return lax.dot_general(a, b, (((0,), (0,)), ((), ())), preferred_element_type=F32)


def _split3(x):
    hi = x.astype(BF16)
    r1 = x - hi.astype(F32)
    mid = r1.astype(BF16)
    lo = (r1 - mid.astype(F32)).astype(BF16)
    return hi, mid, lo


def _ada_kernel(c_ref, w_ref, b_ref, o_ref):
    c = c_ref[...]
    s = (c * _sigmoid(c)).astype(BF16)
    o_ref[...] = _dot(s, w_ref[...].astype(BF16)) + b_ref[...]


def _ada(c_pad, w_ada, b_ada, *, tn=1024):
    rows, d = c_pad.shape
    n = w_ada.shape[1]
    return pl.pallas_call(
        _ada_kernel,
        grid=(n // tn,),
        in_specs=[
            pl.BlockSpec((rows, d), lambda j: (0, 0)),
            pl.BlockSpec((d, tn), lambda j: (0, j)),
            pl.BlockSpec((1, tn), lambda j: (0, j)),
        ],
        out_specs=pl.BlockSpec((rows, tn), lambda j: (0, j)),
        out_shape=jax.ShapeDtypeStruct((rows, n), F32),
        compiler_params=pltpu.CompilerParams(
            dimension_semantics=("arbitrary",),
            vmem_limit_bytes=_vmem_limit(_nbytes((d, tn), F32), temp_bytes=_nbytes((d, tn), BF16)),
        ),
        name="ada",
    )(c_pad, w_ada, b_ada)


def _inproj_kernel(x_ref, scale_ref, shift_ref, w_ref, wft_ref, qg_ref, kg_ref,
                   main_ref, small_ref, ft_ref, h_scr, *, n_q, n_k, n_main):
    j = pl.program_id(1)
    tn = w_ref.shape[1]

    @pl.when(j == 0)
    def _():
        h = _layernorm(x_ref[...]) * (1.0 + scale_ref[0]) + shift_ref[0]
        h_scr[...] = h.astype(BF16)

    acc = _dot(h_scr[...], w_ref[...])

    @pl.when(j < n_q + n_k)
    def _():
        gain = jnp.where(j < n_q, qg_ref[...] * (FOX_HEAD_DIM ** -0.5), kg_ref[...])
        for hh in range(tn // FOX_HEAD_DIM):
            sl = slice(hh * FOX_HEAD_DIM, (hh + 1) * FOX_HEAD_DIM)
            a = acc[:, sl]
            ms = jnp.mean(a * a, axis=-1, keepdims=True)
            main_ref[:, sl] = (a * lax.rsqrt(ms + RMS_EPS) * gain).astype(BF16)

    @pl.when((j >= n_q + n_k) & (j < n_main))
    def _():
        main_ref[...] = acc.astype(BF16)

    @pl.when(j == n_main)
    def _():
        small_ref[...] = acc
        ft_ref[...] = _dot_nt(wft_ref[...], h_scr[...])


def _inproj(x2d, scale1, shift1, w_pack, wft, qg, kg, *, seq, tm=1024, tn=512):
    t, d = x2d.shape
    n_pack = w_pack.shape[1]
    n_main = n_pack // tn - 1
    fox_w = FOX_HEADS * FOX_HEAD_DIM
    tiles_per_batch = seq // tm
    kern = functools.partial(_inproj_kernel, n_q=fox_w // tn, n_k=fox_w // tn, n_main=n_main)
    pipelined = (_nbytes((tm, d), F32) + _nbytes((d, tn), BF16) + _nbytes((tm, tn), BF16)
                 + _nbytes((tm, tn), F32) + _nbytes((8, tm), F32))
    return pl.pallas_call(
        kern,
        grid=(t // tm, n_main + 1),
        in_specs=[
            pl.BlockSpec((tm, d), lambda i, j: (i, 0)),
            pl.BlockSpec((1, 1, d), lambda i, j: (i // tiles_per_batch, 0, 0)),
            pl.BlockSpec((1, 1, d), lambda i, j: (i // tiles_per_batch, 0, 0)),
            pl.BlockSpec((d, tn), lambda i, j: (0, j)),
            pl.BlockSpec((FOX_HEADS, d), lambda i, j: (0, 0)),
            pl.BlockSpec((1, FOX_HEAD_DIM), lambda i, j: (0, 0)),
            pl.BlockSpec((1, FOX_HEAD_DIM), lambda i, j: (0, 0)),
        ],
        out_specs=[
            pl.BlockSpec((tm, tn), lambda i, j: (i, jnp.minimum(j, n_main - 1))),
            pl.BlockSpec((tm, tn), lambda i, j: (i, 0)),
            pl.BlockSpec((FOX_HEADS, tm), lambda i, j: (0, i)),
        ],
        out_shape=[
            jax.ShapeDtypeStruct((t, n_main * tn), BF16),
            jax.ShapeDtypeStruct((t, tn), F32),
            jax.ShapeDtypeStruct((FOX_HEADS, t), F32),
        ],
        scratch_shapes=[pltpu.VMEM((tm, d), BF16)],
        compiler_params=pltpu.CompilerParams(
            dimension_semantics=("arbitrary", "arbitrary"),
            vmem_limit_bytes=_vmem_limit(pipelined, _nbytes((tm, d), BF16),
                                         3 * _nbytes((tm, d), F32)),
        ),
        name="inproj",
    )(x2d, scale1, shift1, w_pack, wft, qg, kg)


def _fox_prep_kernel(ft_ref, bias_ref, o_ref, *, blocks_per_seq):
    x = ft_ref[...] + bias_ref[...]
    lf = jnp.minimum(x, 0.0) - jnp.log1p(jnp.exp(-jnp.abs(x)))
    rows, width = lf.shape
    r = lax.broadcasted_iota(jnp.int32, (width, width), 0)
    c = lax.broadcasted_iota(jnp.int32, (width, width), 1)
    upper = (r <= c).astype(BF16)
    within = sum(_dot(p, upper) for p in _split3(lf))
    tot = jnp.broadcast_to(within[:, width - 1:width], (rows, width))
    rr = lax.broadcasted_iota(jnp.int32, (rows, rows), 0)
    cc = lax.broadcasted_iota(jnp.int32, (rows, rows), 1)
    prior = ((rr // blocks_per_seq == cc // blocks_per_seq) & (cc < rr)).astype(BF16)
    offset = sum(_dot(prior, p) for p in _split3(tot))
    o_ref[...] = within + offset


def _fox_prep(ft_rows, bias_rows, *, blocks_per_seq):
    return pl.pallas_call(
        functools.partial(_fox_prep_kernel, blocks_per_seq=blocks_per_seq),
        out_shape=jax.ShapeDtypeStruct(ft_rows.shape, F32),
        name="fox_prep",
    )(ft_rows, bias_rows)


def _fox_attn_kernel(q_ref, k_ref, v_ref, og_ref, cum_ref, o_ref, *, tq):
    seq = q_ref.shape[0]
    negcum = -cum_ref[0, 0]
    for qi in range(seq // tq):
        q0, kend = qi * tq, (qi + 1) * tq
        s = _dot_nt(q_ref[q0:kend, :], k_ref[0:kend, :]) + negcum[:, 0:kend]
        row = q0 + lax.broadcasted_iota(jnp.int32, (tq, kend), 0)
        col = lax.broadcasted_iota(jnp.int32, (tq, kend), 1)
        s = jnp.where(col <= row, s, -jnp.inf)
        m = jnp.max(s, axis=-1, keepdims=True)
        p = jnp.exp(s - m)
        l = jnp.sum(p, axis=-1, keepdims=True)
        o = _dot(p.astype(BF16), v_ref[0:kend, :]) / l
        gate = _sigmoid(og_ref[q0:kend, :].astype(F32))
        o_ref[q0:kend, :] = (o * gate).astype(BF16)


def _fox_attn(main, cum, *, batch, seq, tq=256):
    t = main.shape[0]
    hd = FOX_HEAD_DIM
    nh = FOX_HEADS
    blk = lambda off: pl.BlockSpec((seq, hd), lambda b, h: (b, off + h))
    pipelined = 5 * _nbytes((seq, hd), BF16) + _nbytes((8, seq), F32)
    return pl.pallas_call(
        functools.partial(_fox_attn_kernel, tq=tq),
        grid=(batch, nh),
        in_specs=[blk(0), blk(nh), blk(2 * nh), blk(3 * nh),
                  pl.BlockSpec((1, 1, 1, seq), lambda b, h: (h, b, 0, 0))],
        out_specs=pl.BlockSpec((seq, hd), lambda b, h: (b, h)),
        out_shape=jax.ShapeDtypeStruct((t, nh * hd), BF16),
        compiler_params=pltpu.CompilerParams(
            dimension_semantics=("arbitrary", "arbitrary"),
            vmem_limit_bytes=_vmem_limit(pipelined, 0, 6 * _nbytes((tq, seq), F32)),
        ),
        name="fox_attn",
    )(main, main, main, main, cum)


def _rwkv_prep_kernel(r_ref, k_ref, v_ref, small_ref, mu_r_ref, mu_k_ref, mu_v_ref, mu_s_ref,
                      w0_ref, w2_ref, a0_ref, a2_ref, g2_ref,
                      r_o, k_o, v_o, lw_o, a_o, g_o, prev_r, prev_k, prev_v, prev_s):
    prevs = (prev_r, prev_k, prev_v, prev_s)

    @pl.when(pl.program_id(1) == 0)
    def _():
        for p in prevs:
            p[...] = jnp.zeros_like(p)

    def shift_mix(x, prev_ref, mu):
        tm = x.shape[0]
        xp = pltpu.roll(x, 1, axis=0)
        row = lax.broadcasted_iota(jnp.int32, x.shape, 0)
        xp = jnp.where(row == 0, prev_ref[0:1, :], xp)
        prev_ref[0:1, :] = x[tm - 1:tm, :]
        return x + (xp - x) * mu

    r = shift_mix(r_ref[...].astype(F32), prev_r, mu_r_ref[...])
    k = shift_mix(k_ref[...].astype(F32), prev_k, mu_k_ref[...])
    v = shift_mix(v_ref[...].astype(F32), prev_v, mu_v_ref[...])
    sm = shift_mix(small_ref[...], prev_s, mu_s_ref[...])
    xw = sm[:, 0:LORA_PAD]
    xa = sm[:, LORA_PAD:2 * LORA_PAD]
    xg = sm[:, 2 * LORA_PAD:2 * LORA_PAD + GATE_LORA]

    z = w0_ref[...] + _dot(jnp.tanh(xw).astype(BF16), w2_ref[...])
    w_raw = -(jnp.maximum(-z, 0.0) + jnp.log1p(jnp.exp(-jnp.abs(z)))) - 0.5
    lw_o[...] = -jnp.exp(w_raw)
    a_o[...] = _sigmoid(a0_ref[...] + _dot(xa.astype(BF16), a2_ref[...]))
    g_o[...] = _dot(_sigmoid(xg).astype(BF16), g2_ref[...])
    r_o[...] = r
    k_o[...] = k
    v_o[...] = v


def _rwkv_prep(main, small, mu_r, mu_k, mu_v, mu_s, w0, w2p, a0, a2p, g2, *, batch, seq, tm=256):
    t = main.shape[0]
    width = RWKV_HEADS * RWKV_HEAD_DIM
    nsm = small.shape[1]
    col0 = (4 * FOX_HEADS * FOX_HEAD_DIM) // width
    steps = seq // tm
    act = lambda off: pl.BlockSpec((tm, width), lambda b, s: (b * steps + s, off))
    row_spec = lambda n: pl.BlockSpec((1, n), lambda b, s: (0, 0))
    full = lambda a: pl.BlockSpec(a.shape, lambda b, s: (0, 0))
    out_spec = pl.BlockSpec((tm, width), lambda b, s: (b * steps + s, 0))
    pipelined = (3 * _nbytes((tm, width), BF16) + _nbytes((tm, nsm), F32)
                 + 6 * _nbytes((tm, width), F32)
                 + _nbytes(w2p.shape, BF16) + _nbytes(a2p.shape, BF16) + _nbytes(g2.shape, BF16))
    return pl.pallas_call(
        _rwkv_prep_kernel,
        grid=(batch, steps),
        in_specs=[act(col0), act(col0 + 1), act(col0 + 2),
                  pl.BlockSpec((tm, nsm), lambda b, s: (b * steps + s, 0)),
                  row_spec(width), row_spec(width), row_spec(width), row_spec(nsm),
                  row_spec(width), full(w2p), row_spec(width), full(a2p), full(g2)],
        out_specs=[out_spec] * 6,
        out_shape=[jax.ShapeDtypeStruct((t, width), F32)] * 6,
        scratch_shapes=[pltpu.VMEM((8, width), F32)] * 3 + [pltpu.VMEM((8, nsm), F32)],
        compiler_params=pltpu.CompilerParams(
            dimension_semantics=("arbitrary", "arbitrary"),
            vmem_limit_bytes=_vmem_limit(pipelined, 0, 12 * _nbytes((tm, width), F32)),
        ),
        name="rwkv_prep",
    )(main, main, main, small, mu_r, mu_k, mu_v, mu_s, w0, w2p, a0, a2p, g2)


def _rwkv_rec_kernel(r_ref, k_ref, v_ref, lw_ref, a_ref, g_ref,
                     kk_ref, ka_ref, rk_ref, lng_ref, lnb_ref, y_ref, state, *, n_tiles):
    c = pl.program_id(1)

    @pl.when(c == 0)
    def _():
        state[...] = jnp.zeros_like(state)

    ch = r_ref.shape[0]
    hd = RWKV_HEAD_DIM
    lane = lax.broadcasted_iota(jnp.int32, (1, LANES), 1)
    m0 = lane < hd
    n2 = HEADS_PER_TILE * ch
    rr = lax.broadcasted_iota(jnp.int32, (n2, n2), 0)
    cc = lax.broadcasted_iota(jnp.int32, (n2, n2), 1)
    same = (rr // ch) == (cc // ch)
    strict = same & (rr > cc)
    incl = same & (rr >= cc)
    lr = lax.broadcasted_iota(jnp.int32, (ch, ch), 0)
    lc = lax.broadcasted_iota(jnp.int32, (ch, ch), 1)
    lower = (lc <= lr).astype(BF16)

    def head_sum(x):
        s0 = jnp.sum(jnp.where(m0, x, 0.0), axis=-1, keepdims=True)
        s1 = jnp.sum(jnp.where(m0, 0.0, x), axis=-1, keepdims=True)
        return jnp.where(m0, s0, s1)

    def stack(x):
        return jnp.concatenate([jnp.where(m0, x, 0.0), jnp.where(m0, 0.0, x)], axis=0)

    for p in range(n_tiles):
        sl = slice(p * LANES, (p + 1) * LANES)
        r, k, v = r_ref[:, sl], k_ref[:, sl], v_ref[:, sl]
        lw, a, g = lw_ref[:, sl], a_ref[:, sl], g_ref[:, sl]

        kk = k * kk_ref[:, sl]
        kk = kk / jnp.maximum(jnp.sqrt(head_sum(kk * kk)), 1e-12)
        kr = k * (1.0 + (a - 1.0) * ka_ref[:, sl])
        bvec = kk * a

        cum = sum(_dot(lower, part) for part in _split3(lw))
        gam = jnp.exp(cum)
        ginv = jnp.exp(-cum)
        g_end = gam[ch - 1:ch, :]
        a_t = -kk * jnp.exp(cum - lw)
        r_t = r * gam
        b_h = bvec * ginv
        k_h = kr * ginv

        lhs = jnp.concatenate([stack(a_t), stack(r_t)], axis=0).astype(BF16)
        rhs = jnp.concatenate([stack(b_h), stack(k_h)], axis=0).astype(BF16)
        sc = _dot_nt(lhs, rhs)
        ab = jnp.where(strict, sc[0:n2, 0:n2], 0.0)
        ak = jnp.where(strict, sc[0:n2, n2:2 * n2], 0.0)
        rb = jnp.where(incl, sc[n2:2 * n2, 0:n2], 0.0)
        rkm = jnp.where(incl, sc[n2:2 * n2, n2:2 * n2], 0.0)

        st = state[p]
        ah = _dot_nt(lhs, st.astype(BF16))
        vs = stack(v)
        vs16 = vs.astype(BF16)
        x = ah[0:n2] + _dot(ak.astype(BF16), vs16)
        pw = ab
        n_lvl = ch.bit_length() - 1
        for lvl in range(n_lvl):
            pw16 = pw.astype(BF16)
            x = x + _dot(pw16, x.astype(BF16))
            if lvl + 1 < n_lvl:
                pw = _dot(pw16, pw16)
        uv = jnp.concatenate([x.astype(BF16), vs16], axis=0)
        ys = ah[n2:2 * n2] + _dot(jnp.concatenate([rb, rkm], axis=1).astype(BF16), uv)
        y = ys[0:ch] + ys[ch:n2]

        bk = jnp.concatenate([stack(b_h * g_end), stack(k_h * g_end)], axis=0).astype(BF16)
        state[p] = st * g_end + _dot_tn(uv, bk)

        mu = head_sum(y) * (1.0 / hd)
        yc = y - mu
        var = head_sum(yc * yc) * (1.0 / hd)
        yn = yc * lax.rsqrt(var + RWKV_LNX_EPS) * lng_ref[:, sl] + lnb_ref[:, sl]
        bonus = head_sum(r * kr * rk_ref[:, sl]) * v
        y_ref[:, sl] = ((yn + bonus) * g).astype(BF16)


def _rwkv_rec(r, k, v, lw, a, g, kk, ka, rk, lng, lnb, *, batch, seq):
    t, width = r.shape
    ch = RWKV_CHUNK
    steps = seq // ch
    n_tiles = width // LANES
    act = pl.BlockSpec((ch, width), lambda b, c: (b * steps + c, 0))
    row = pl.BlockSpec((1, width), lambda b, c: (0, 0))
    pipelined = 6 * _nbytes((ch, width), F32) + _nbytes((ch, width), BF16)
    return pl.pallas_call(
        functools.partial(_rwkv_rec_kernel, n_tiles=n_tiles),
        grid=(batch, steps),
        in_specs=[act] * 6 + [row] * 5,
        out_specs=act,
        out_shape=jax.ShapeDtypeStruct((t, width), BF16),
        scratch_shapes=[pltpu.VMEM((n_tiles, LANES, LANES), F32)],
        compiler_params=pltpu.CompilerParams(
            dimension_semantics=("arbitrary", "arbitrary"),
            vmem_limit_bytes=_vmem_limit(pipelined, _nbytes((n_tiles, LANES, LANES), F32), 16 << 20),
        ),
        name="rwkv_rec",
    )(r, k, v, lw, a, g, kk, ka, rk, lng, lnb)


def _merge_kernel(o_ref, y_ref, ga_ref, gb_ref, wa_ref, wb_ref, m_ref):
    ua = _dot(o_ref[...], wa_ref[...])
    ub = _dot(y_ref[...], wb_ref[...])
    m = _sigmoid(ga_ref[...].astype(F32)) * ua + _sigmoid(gb_ref[...].astype(F32)) * ub
    m_ref[...] = m.astype(BF16)


def _merge(o, y, main, wa, wb, *, tm=1024, tn=512):
    t, kdim = o.shape
    d = wa.shape[1]
    gate_col0 = (main.shape[1] - 2 * d) // tn
    nj = d // tn
    pipelined = (2 * _nbytes((tm, kdim), BF16) + 2 * _nbytes((kdim, tn), BF16)
                 + 3 * _nbytes((tm, tn), BF16))
    return pl.pallas_call(
        _merge_kernel,
        grid=(t // tm, nj),
        in_specs=[
            pl.BlockSpec((tm, kdim), lambda i, j: (i, 0)),
            pl.BlockSpec((tm, kdim), lambda i, j: (i, 0)),
            pl.BlockSpec((tm, tn), lambda i, j: (i, gate_col0 + j)),
            pl.BlockSpec((tm, tn), lambda i, j: (i, gate_col0 + nj + j)),
            pl.BlockSpec((kdim, tn), lambda i, j: (0, j)),
            pl.BlockSpec((kdim, tn), lambda i, j: (0, j)),
        ],
        out_specs=pl.BlockSpec((tm, tn), lambda i, j: (i, j)),
        out_shape=jax.ShapeDtypeStruct((t, d), BF16),
        compiler_params=pltpu.CompilerParams(
            dimension_semantics=("arbitrary", "arbitrary"),
            vmem_limit_bytes=_vmem_limit(pipelined, 0, 6 * _nbytes((tm, tn), F32)),
        ),
        name="merge",
    )(o, y, main, main, wa, wb)


def _outproj_kernel(m_ref, w_ref, x_ref, gate_ref, lng_ref, lnb_ref, scale_ref, shift_ref,
                    x1_ref, h2_ref, acc):
    j = pl.program_id(1)
    tn = w_ref.shape[1]
    acc[:, pl.ds(pl.multiple_of(j * tn, tn), tn)] = _dot(m_ref[...], w_ref[...])

    @pl.when(j == pl.num_programs(1) - 1)
    def _():
        z = DEEPNORM_ALPHA * x_ref[...] + gate_ref[0] * acc[...]
        x1 = _layernorm(z) * lng_ref[...] + lnb_ref[...]
        x1_ref[...] = x1
        h2_ref[...] = (_layernorm(x1) * (1.0 + scale_ref[0]) + shift_ref[0]).astype(BF16)


def _outproj(m, w_out, x2d, gate1, ln_g, ln_b, scale2, shift2, *, seq, tm=512, tn=512):
    t, d = x2d.shape
    tiles_per_batch = seq // tm
    mod = pl.BlockSpec((1, 1, d), lambda i, j: (i // tiles_per_batch, 0, 0))
    row = pl.BlockSpec((1, d), lambda i, j: (0, 0))
    pipelined = (_nbytes((tm, d), BF16) + _nbytes((d, tn), BF16) + 2 * _nbytes((tm, d), F32)
                 + _nbytes((tm, d), BF16))
    return pl.pallas_call(
        _outproj_kernel,
        grid=(t // tm, d // tn),
        in_specs=[
            pl.BlockSpec((tm, d), lambda i, j: (i, 0)),
            pl.BlockSpec((d, tn), lambda i, j: (0, j)),
            pl.BlockSpec((tm, d), lambda i, j: (i, 0)),
            mod, row, row, mod, mod,
        ],
        out_specs=[pl.BlockSpec((tm, d), lambda i, j: (i, 0)),
                   pl.BlockSpec((tm, d), lambda i, j: (i, 0))],
        out_shape=[jax.ShapeDtypeStruct((t, d), F32), jax.ShapeDtypeStruct((t, d), BF16)],
        scratch_shapes=[pltpu.VMEM((tm, d), F32)],
        compiler_params=pltpu.CompilerParams(
            dimension_semantics=("arbitrary", "arbitrary"),
            vmem_limit_bytes=_vmem_limit(pipelined, _nbytes((tm, d), F32), 3 * _nbytes((tm, d), F32)),
        ),
        name="outproj",
    )(m, w_out, x2d, gate1, ln_g, ln_b, scale2, shift2)


def _ffn_kernel(h_ref, wg_ref, wu_ref, wd_ref, x1_ref, gate_ref, lng_ref, lnb_ref, o_ref, acc):
    f = pl.program_id(1)
    h = h_ref[...]
    g = _dot(h, wg_ref[...])
    u = _dot(h, wu_ref[...])
    act = (g * _sigmoid(g) * u).astype(BF16)
    part = _dot(act, wd_ref[...])

    @pl.when(f == 0)
    def _():
        acc[...] = part

    @pl.when(f > 0)
    def _():
        acc[...] += part

    @pl.when(f == pl.num_programs(1) - 1)
    def _():
        z = DEEPNORM_ALPHA * x1_ref[...] + gate_ref[0] * acc[...]
        o_ref[...] = _layernorm(z) * lng_ref[...] + lnb_ref[...]


def _ffn(h2, w_gu, w_down, x1, gate2, ln_g, ln_b, *, seq, tm=512, tf=512):
    t, d = x1.shape
    d_ff = w_down.shape[0]
    nf = d_ff // tf
    tiles_per_batch = seq // tm
    mod = pl.BlockSpec((1, 1, d), lambda i, f: (i // tiles_per_batch, 0, 0))
    row = pl.BlockSpec((1, d), lambda i, f: (0, 0))
    pipelined = (_nbytes((tm, d), BF16) + 3 * _nbytes((d, tf), BF16) + 2 * _nbytes((tm, d), F32))
    return pl.pallas_call(
        _ffn_kernel,
        grid=(t // tm, nf),
        in_specs=[
            pl.BlockSpec((tm, d), lambda i, f: (i, 0)),
            pl.BlockSpec((d, tf), lambda i, f: (0, f)),
            pl.BlockSpec((d, tf), lambda i, f: (0, nf + f)),
            pl.BlockSpec((tf, d), lambda i, f: (f, 0)),
            pl.BlockSpec((tm, d), lambda i, f: (i, 0)),
            mod, row, row,
        ],
        out_specs=pl.BlockSpec((tm, d), lambda i, f: (i, 0)),
        out_shape=jax.ShapeDtypeStruct((t, d), F32),
        scratch_shapes=[pltpu.VMEM((tm, d), F32)],
        compiler_params=pltpu.CompilerParams(
            dimension_semantics=("arbitrary", "arbitrary"),
            vmem_limit_bytes=_vmem_limit(pipelined, _nbytes((tm, d), F32),
                                         2 * _nbytes((tm, d), F32) + 4 * _nbytes((tm, tf), F32)),
        ),
        name="ffn",
    )(h2, w_gu, w_gu, w_down, x1, gate2, ln_g, ln_b)


def _pad_rows(w, rows):
    return jnp.zeros((rows, w.shape[1]), w.dtype).at[:w.shape[0]].set(w)


def kernel(x, c, w_ada, b_ada, w_in, b_fgate, q_norm_g, k_norm_g, rwkv_mu, rwkv_w0, rwkv_w2, rwkv_a0, rwkv_a2, rwkv_g2, rwkv_k_k, rwkv_k_a, rwkv_r_k, rwkv_lnx_g, rwkv_lnx_b, w_branch_a, w_branch_b, w_out, ln1_g, ln1_b, w_ffn_gu, w_ffn_down, ln2_g, ln2_b):
    batch, seq, d = x.shape
    t = batch * seq
    fox_w = FOX_HEADS * FOX_HEAD_DIM
    rw_w = RWKV_HEADS * RWKV_HEAD_DIM
    fox_cols = 4 * fox_w + FOX_HEADS
    lora0 = fox_cols + 3 * rw_w
    gate0 = lora0 + DECAY_LORA + AAA_LORA + GATE_LORA
    assert seq % RWKV_CHUNK == 0 and w_in.shape[1] == gate0 + 2 * d

    zpad = lambda n: jnp.zeros((d, n), w_in.dtype)
    w_pack = jnp.concatenate([
        w_in[:, :4 * fox_w],
        w_in[:, fox_cols:lora0],
        w_in[:, gate0:],
        w_in[:, lora0:lora0 + DECAY_LORA], zpad(LORA_PAD - DECAY_LORA),
        w_in[:, lora0 + DECAY_LORA:lora0 + DECAY_LORA + AAA_LORA], zpad(LORA_PAD - AAA_LORA),
        w_in[:, lora0 + DECAY_LORA + AAA_LORA:gate0],
    ], axis=1).astype(BF16)
    wft = w_in[:, 4 * fox_w:fox_cols].T.astype(BF16)
    mu_r = rwkv_mu[None, 0:rw_w]
    mu_k = rwkv_mu[None, rw_w:2 * rw_w]
    mu_v = rwkv_mu[None, 2 * rw_w:3 * rw_w]
    mu_l = rwkv_mu[3 * rw_w:]
    zrow = lambda n: jnp.zeros((n,), rwkv_mu.dtype)
    mu_s = jnp.concatenate([
        mu_l[:DECAY_LORA], zrow(LORA_PAD - DECAY_LORA),
        mu_l[DECAY_LORA:DECAY_LORA + AAA_LORA], zrow(LORA_PAD - AAA_LORA),
        mu_l[DECAY_LORA + AAA_LORA:]])[None]
    w2p = _pad_rows(rwkv_w2, LORA_PAD).astype(BF16)
    a2p = _pad_rows(rwkv_a2, LORA_PAD).astype(BF16)
    g2 = rwkv_g2.astype(BF16)
    wa = w_branch_a.astype(BF16)
    wb = w_branch_b.astype(BF16)
    wo = w_out.astype(BF16)
    wgu = w_ffn_gu.astype(BF16)
    wdn = w_ffn_down.astype(BF16)
    row = lambda v: v.reshape(1, -1)

    c_pad = jnp.zeros((8, d), c.dtype).at[:batch].set(c)
    mod = _ada(c_pad, w_ada, row(b_ada))[:batch].reshape(batch, 6, 1, d)
    shift1, scale1, gate1, shift2, scale2, gate2 = (mod[:, i] for i in range(6))

    x2d = x.reshape(t, d)
    main, small, ft = _inproj(x2d, scale1, shift1, w_pack, wft, row(q_norm_g), row(k_norm_g), seq=seq)

    blocks_per_seq = seq // LANES
    ft_rows = ft.reshape(FOX_HEADS * batch * blocks_per_seq, LANES)
    bias_rows = jnp.broadcast_to(b_fgate[:, None, None], (FOX_HEADS, batch * blocks_per_seq, LANES))
    cum = _fox_prep(ft_rows, bias_rows.reshape(ft_rows.shape), blocks_per_seq=blocks_per_seq)
    cum = cum.reshape(FOX_HEADS, batch, 1, seq)
    o = _fox_attn(main, cum, batch=batch, seq=seq)

    r, k, v, lw, a, g = _rwkv_prep(main, small, mu_r, mu_k, mu_v, mu_s, row(rwkv_w0), w2p,
                                   row(rwkv_a0), a2p, g2, batch=batch, seq=seq)
    y = _rwkv_rec(r, k, v, lw, a, g, row(rwkv_k_k), row(rwkv_k_a), row(rwkv_r_k),
                  row(rwkv_lnx_g), row(rwkv_lnx_b), batch=batch, seq=seq)

    merged = _merge(o, y, main, wa, wb)
    x1, h2 = _outproj(merged, wo, x2d, gate1, row(ln1_g), row(ln1_b), scale2, shift2, seq=seq)

    out = _ffn(h2, wgu, wdn, x1, gate2, row(ln2_g), row(ln2_b), seq=seq)
    return out.reshape(batch, seq, d)
```

```python
import functools

import jax
import jax.numpy as jnp
from jax import lax
from jax.experimental import pallas as pl
from jax.experimental.pallas import tpu as pltpu

F32 = jnp.float32
BF16 = jnp.bfloat16

LANES = 128
VMEM_CAP_BYTES = 60000 * 1024

FOX_HEADS = 8
FOX_HEAD_DIM = 128
RWKV_HEADS = 16
RWKV_HEAD_DIM = 64
DECAY_LORA = 96
AAA_LORA = 96
GATE_LORA = 256
LORA_PAD = 128
RWKV_LNX_EPS = 64e-5
DEPTH = 1
DEEPNORM_ALPHA = (2.0 * DEPTH) ** 0.25
LN_EPS = 1e-5
RMS_EPS = 1e-6
RWKV_CHUNK = 64
HEADS_PER_TILE = LANES // RWKV_HEAD_DIM


def _vmem_limit(pipelined_bytes, resident_bytes=0, temp_bytes=0):
    need = 2 * pipelined_bytes + resident_bytes + temp_bytes + (2 << 20)
    return int(min(VMEM_CAP_BYTES, need))


def _nbytes(shape, dtype):
    n = 1
    for s in shape:
        n *= s
    return n * jnp.dtype(dtype).itemsize


def _layernorm(x):
    mu = jnp.mean(x, axis=-1, keepdims=True)
    xc = x - mu
    var = jnp.mean(xc * xc, axis=-1, keepdims=True)
    return xc * lax.rsqrt(var + LN_EPS)


def _sigmoid(x):
    return 1.0 / (1.0 + jnp.exp(-x))


def _dot(a, b):
    return jnp.dot(a, b, preferred_element_type=F32)


def _dot_nt(a, b):
    return lax.dot_general(a, b, (((1,), (1,)), ((), ())), preferred_element_type=F32)


def _dot_tn(a, b):
    return lax.dot_general(a, b, (((0,), (0,)), ((), ())), preferred_element_type=F32)


def _split3(x):
    hi = x.astype(BF16)
    r1 = x - hi.astype(F32)
    mid = r1.astype(BF16)
    lo = (r1 - mid.astype(F32)).astype(BF16)
    return hi, mid, lo


def _ada_kernel(c_ref, w_ref, b_ref, o_ref):
    c = c_ref[...]
    s = (c * _sigmoid(c)).astype(BF16)
    o_ref[...] = _dot(s, w_ref[...].astype(BF16)) + b_ref[...]


def _ada(c_pad, w_ada, b_ada, *, tn=1024):
    rows, d = c_pad.shape
    n = w_ada.shape[1]
    return pl.pallas_call(
        _ada_kernel,
        grid=(n // tn,),
        in_specs=[
            pl.BlockSpec((rows, d), lambda j: (0, 0)),
            pl.BlockSpec((d, tn), lambda j: (0, j)),
            pl.BlockSpec((1, tn), lambda j: (0, j)),
        ],
        out_specs=pl.BlockSpec((rows, tn), lambda j: (0, j)),
        out_shape=jax.ShapeDtypeStruct((rows, n), F32),
        compiler_params=pltpu.CompilerParams(
            dimension_semantics=("arbitrary",),
            vmem_limit_bytes=_vmem_limit(_nbytes((d, tn), F32), temp_bytes=_nbytes((d, tn), BF16)),
        ),
        name="ada",
    )(c_pad, w_ada, b_ada)


def _inproj_kernel(x_ref, scale_ref, shift_ref, w_ref, wft_ref, qg_ref, kg_ref,
                   main_ref, small_ref, ft_ref, h_scr, *, n_q, n_k, n_main):
    j = pl.program_id(1)
    tn = w_ref.shape[1]

    @pl.when(j == 0)
    def _():
        h = _layernorm(x_ref[...]) * (1.0 + scale_ref[0]) + shift_ref[0]
        h_scr[...] = h.astype(BF16)

    acc = _dot(h_scr[...], w_ref[...])

    @pl.when(j < n_q + n_k)
    def _():
        gain = jnp.where(j < n_q, qg_ref[...] * (FOX_HEAD_DIM ** -0.5), kg_ref[...])
        for hh in range(tn // FOX_HEAD_DIM):
            sl = slice(hh * FOX_HEAD_DIM, (hh + 1) * FOX_HEAD_DIM)
            a = acc[:, sl]
            ms = jnp.mean(a * a, axis=-1, keepdims=True)
            main_ref[:, sl] = (a * lax.rsqrt(ms + RMS_EPS) * gain).astype(BF16)

    @pl.when((j >= n_q + n_k) & (j < n_main))
    def _():
        main_ref[...] = acc.astype(BF16)

    @pl.when(j == n_main)
    def _():
        small_ref[...] = acc
        ft_ref[...] = _dot_nt(wft_ref[...], h_scr[...])


def _inproj(x2d, scale1, shift1, w_pack, wft, qg, kg, *, seq, tm=1024, tn=512):
    t, d = x2d.shape
    n_pack = w_pack.shape[1]
    n_main = n_pack // tn - 1
    fox_w = FOX_HEADS * FOX_HEAD_DIM
    tiles_per_batch = seq // tm
    kern = functools.partial(_inproj_kernel, n_q=fox_w // tn, n_k=fox_w // tn, n_main=n_main)
    pipelined = (_nbytes((tm, d), F32) + _nbytes((d, tn), BF16) + _nbytes((tm, tn), BF16)
                 + _nbytes((tm, tn), F32) + _nbytes((8, tm), F32))
    return pl.pallas_call(
        kern,
        grid=(t // tm, n_main + 1),
        in_specs=[
            pl.BlockSpec((tm, d), lambda i, j: (i, 0)),
            pl.BlockSpec((1, 1, d), lambda i, j: (i // tiles_per_batch, 0, 0)),
            pl.BlockSpec((1, 1, d), lambda i, j: (i // tiles_per_batch, 0, 0)),
            pl.BlockSpec((d, tn), lambda i, j: (0, j)),
            pl.BlockSpec((FOX_HEADS, d), lambda i, j: (0, 0)),
            pl.BlockSpec((1, FOX_HEAD_DIM), lambda i, j: (0, 0)),
            pl.BlockSpec((1, FOX_HEAD_DIM), lambda i, j: (0, 0)),
        ],
        out_specs=[
            pl.BlockSpec((tm, tn), lambda i, j: (i, jnp.minimum(j, n_main - 1))),
            pl.BlockSpec((tm, tn), lambda i, j: (i, 0)),
            pl.BlockSpec((FOX_HEADS, tm), lambda i, j: (0, i)),
        ],
        out_shape=[
            jax.ShapeDtypeStruct((t, n_main * tn), BF16),
            jax.ShapeDtypeStruct((t, tn), F32),
            jax.ShapeDtypeStruct((FOX_HEADS, t), F32),
        ],
        scratch_shapes=[pltpu.VMEM((tm, d), BF16)],
        compiler_params=pltpu.CompilerParams(
            dimension_semantics=("arbitrary", "arbitrary"),
            vmem_limit_bytes=_vmem_limit(pipelined, _nbytes((tm, d), BF16),
                                         3 * _nbytes((tm, d), F32)),
        ),
        name="inproj",
    )(x2d, scale1, shift1, w_pack, wft, qg, kg)


def _fox_prep_kernel(ft_ref, bias_ref, o_ref, *, blocks_per_seq):
    x = ft_ref[...] + bias_ref[...]
    lf = jnp.minimum(x, 0.0) - jnp.log1p(jnp.exp(-jnp.abs(x)))
    rows, width = lf.shape
    r = lax.broadcasted_iota(jnp.int32, (width, width), 0)
    c = lax.broadcasted_iota(jnp.int32, (width, width), 1)
    upper = (r <= c).astype(BF16)
    within = sum(_dot(p, upper) for p in _split3(lf))
    tot = jnp.broadcast_to(within[:, width - 1:width], (rows, width))
    rr = lax.broadcasted_iota(jnp.int32, (rows, rows), 0)
    cc = lax.broadcasted_iota(jnp.int32, (rows, rows), 1)
    prior = ((rr // blocks_per_seq == cc // blocks_per_seq) & (cc < rr)).astype(BF16)
    offset = sum(_dot(prior, p) for p in _split3(tot))
    o_ref[...] = within + offset


def _fox_prep(ft_rows, bias_rows, *, blocks_per_seq):
    return pl.pallas_call(
        functools.partial(_fox_prep_kernel, blocks_per_seq=blocks_per_seq),
        out_shape=jax.ShapeDtypeStruct(ft_rows.shape, F32),
        name="fox_prep",
    )(ft_rows, bias_rows)


def _fox_attn_kernel(q_ref, k_ref, v_ref, og_ref, cum_ref, o_ref, *, tq):
    seq = q_ref.shape[0]
    negcum = -cum_ref[0, 0]
    for qi in range(seq // tq):
        q0, kend = qi * tq, (qi + 1) * tq
        s = _dot_nt(q_ref[q0:kend, :], k_ref[0:kend, :]) + negcum[:, 0:kend]
        row = q0 + lax.broadcasted_iota(jnp.int32, (tq, kend), 0)
        col = lax.broadcasted_iota(jnp.int32, (tq, kend), 1)
        s = jnp.where(col <= row, s, -jnp.inf)
        m = jnp.max(s, axis=-1, keepdims=True)
        p = jnp.exp(s - m)
        l = jnp.sum(p, axis=-1, keepdims=True)
        o = _dot(p.astype(BF16), v_ref[0:kend, :]) / l
        gate = _sigmoid(og_ref[q0:kend, :].astype(F32))
        o_ref[q0:kend, :] = (o * gate).astype(BF16)


def _fox_attn(main, cum, *, batch, seq, tq=256):
    t = main.shape[0]
    hd = FOX_HEAD_DIM
    nh = FOX_HEADS
    blk = lambda off: pl.BlockSpec((seq, hd), lambda b, h: (b, off + h))
    pipelined = 5 * _nbytes((seq, hd), BF16) + _nbytes((8, seq), F32)
    return pl.pallas_call(
        functools.partial(_fox_attn_kernel, tq=tq),
        grid=(batch, nh),
        in_specs=[blk(0), blk(nh), blk(2 * nh), blk(3 * nh),
                  pl.BlockSpec((1, 1, 1, seq), lambda b, h: (h, b, 0, 0))],
        out_specs=pl.BlockSpec((seq, hd), lambda b, h: (b, h)),
        out_shape=jax.ShapeDtypeStruct((t, nh * hd), BF16),
        compiler_params=pltpu.CompilerParams(
            dimension_semantics=("arbitrary", "arbitrary"),
            vmem_limit_bytes=_vmem_limit(pipelined, 0, 6 * _nbytes((tq, seq), F32)),
        ),
        name="fox_attn",
    )(main, main, main, main, cum)


def _rwkv_prep_kernel(r_ref, k_ref, v_ref, small_ref, mu_r_ref, mu_k_ref, mu_v_ref, mu_s_ref,
                      w0_ref, w2_ref, a0_ref, a2_ref, g2_ref,
                      r_o, k_o, v_o, lw_o, a_o, g_o, prev_r, prev_k, prev_v, prev_s):
    prevs = (prev_r, prev_k, prev_v, prev_s)

    @pl.when(pl.program_id(1) == 0)
    def _():
        for p in prevs:
            p[...] = jnp.zeros_like(p)

    def shift_mix(x, prev_ref, mu):
        tm = x.shape[0]
        xp = pltpu.roll(x, 1, axis=0)
        row = lax.broadcasted_iota(jnp.int32, x.shape, 0)
        xp = jnp.where(row == 0, prev_ref[0:1, :], xp)
        prev_ref[0:1, :] = x[tm - 1:tm, :]
        return x + (xp - x) * mu

    r = shift_mix(r_ref[...].astype(F32), prev_r, mu_r_ref[...])
    k = shift_mix(k_ref[...].astype(F32), prev_k, mu_k_ref[...])
    v = shift_mix(v_ref[...].astype(F32), prev_v, mu_v_ref[...])
    sm = shift_mix(small_ref[...], prev_s, mu_s_ref[...])
    xw = sm[:, 0:LORA_PAD]
    xa = sm[:, LORA_PAD:2 * LORA_PAD]
    xg = sm[:, 2 * LORA_PAD:2 * LORA_PAD + GATE_LORA]

    z = w0_ref[...] + _dot(jnp.tanh(xw).astype(BF16), w2_ref[...])
    w_raw = -(jnp.maximum(-z, 0.0) + jnp.log1p(jnp.exp(-jnp.abs(z)))) - 0.5
    lw_o[...] = -jnp.exp(w_raw)
    a_o[...] = _sigmoid(a0_ref[...] + _dot(xa.astype(BF16), a2_ref[...]))
    g_o[...] = _dot(_sigmoid(xg).astype(BF16), g2_ref[...])
    r_o[...] = r
    k_o[...] = k
    v_o[...] = v


def _rwkv_prep(main, small, mu_r, mu_k, mu_v, mu_s, w0, w2p, a0, a2p, g2, *, batch, seq, tm=256):
    t = main.shape[0]
    width = RWKV_HEADS * RWKV_HEAD_DIM
    nsm = small.shape[1]
    col0 = (4 * FOX_HEADS * FOX_HEAD_DIM) // width
    steps = seq // tm
    act = lambda off: pl.BlockSpec((tm, width), lambda b, s: (b * steps + s, off))
    row_spec = lambda n: pl.BlockSpec((1, n), lambda b, s: (0, 0))
    full = lambda a: pl.BlockSpec(a.shape, lambda b, s: (0, 0))
    out_spec = pl.BlockSpec((tm, width), lambda b, s: (b * steps + s, 0))
    pipelined = (3 * _nbytes((tm, width), BF16) + _nbytes((tm, nsm), F32)
                 + 6 * _nbytes((tm, width), F32)
                 + _nbytes(w2p.shape, BF16) + _nbytes(a2p.shape, BF16) + _nbytes(g2.shape, BF16))
    return pl.pallas_call(
        _rwkv_prep_kernel,
        grid=(batch, steps),
        in_specs=[act(col0), act(col0 + 1), act(col0 + 2),
                  pl.BlockSpec((tm, nsm), lambda b, s: (b * steps + s, 0)),
                  row_spec(width), row_spec(width), row_spec(width), row_spec(nsm),
                  row_spec(width), full(w2p), row_spec(width), full(a2p), full(g2)],
        out_specs=[out_spec] * 6,
        out_shape=[jax.ShapeDtypeStruct((t, width), F32)] * 6,
        scratch_shapes=[pltpu.VMEM((8, width), F32)] * 3 + [pltpu.VMEM((8, nsm), F32)],
        compiler_params=pltpu.CompilerParams(
            dimension_semantics=("arbitrary", "arbitrary"),
            vmem_limit_bytes=_vmem_limit(pipelined, 0, 12 * _nbytes((tm, width), F32)),
        ),
        name="rwkv_prep",
    )(main, main, main, small, mu_r, mu_k, mu_v, mu_s, w0, w2p, a0, a2p, g2)


def _rwkv_rec_kernel(r_ref, k_ref, v_ref, lw_ref, a_ref, g_ref,
                     kk_ref, ka_ref, rk_ref, lng_ref, lnb_ref, y_ref, state, *, n_tiles):
    c = pl.program_id(1)

    @pl.when(c == 0)
    def _():
        state[...] = jnp.zeros_like(state)

    ch = r_ref.shape[0]
    hd = RWKV_HEAD_DIM
    lane = lax.broadcasted_iota(jnp.int32, (1, LANES), 1)
    m0 = lane < hd
    n2 = HEADS_PER_TILE * ch
    rr = lax.broadcasted_iota(jnp.int32, (n2, n2), 0)
    cc = lax.broadcasted_iota(jnp.int32, (n2, n2), 1)
    same = (rr // ch) == (cc // ch)
    strict = same & (rr > cc)
    incl = same & (rr >= cc)
    lr = lax.broadcasted_iota(jnp.int32, (ch, ch), 0)
    lc = lax.broadcasted_iota(jnp.int32, (ch, ch), 1)
    lower = (lc <= lr).astype(BF16)

    def head_sum(x):
        s0 = jnp.sum(jnp.where(m0, x, 0.0), axis=-1, keepdims=True)
        s1 = jnp.sum(jnp.where(m0, 0.0, x), axis=-1, keepdims=True)
        return jnp.where(m0, s0, s1)

    def stack(x):
        return jnp.concatenate([jnp.where(m0, x, 0.0), jnp.where(m0, 0.0, x)], axis=0)

    tiles = range(n_tiles)
    sls = [slice(p * LANES, (p + 1) * LANES) for p in tiles]

    def each(fn, *cols):
        return [fn(*args) for args in zip(*cols)]

    def load(ref):
        return [ref[:, sl] for sl in sls]

    r, k, v, lw, a, g = (load(ref) for ref in (r_ref, k_ref, v_ref, lw_ref, a_ref, g_ref))
    kkw, kaw, rkw, lng, lnb = (load(ref) for ref in (kk_ref, ka_ref, rk_ref, lng_ref, lnb_ref))

    kk = each(lambda k_, w_: k_ * w_, k, kkw)
    kk = each(lambda x: x / jnp.maximum(jnp.sqrt(head_sum(x * x)), 1e-12), kk)
    kr = each(lambda k_, a_, w_: k_ * (1.0 + (a_ - 1.0) * w_), k, a, kaw)
    cum = each(lambda x: sum(_dot(lower, part) for part in _split3(x)), lw)
    gam = each(jnp.exp, cum)
    ginv = each(lambda x: jnp.exp(-x), cum)
    g_end = each(lambda x: x[ch - 1:ch, :], gam)
    a_t = each(lambda kk_, c_, lw_: -kk_ * jnp.exp(c_ - lw_), kk, cum, lw)
    r_t = each(lambda r_, g_: r_ * g_, r, gam)
    b_h = each(lambda kk_, a_, gi_: kk_ * a_ * gi_, kk, a, ginv)
    k_h = each(lambda kr_, gi_: kr_ * gi_, kr, ginv)

    stack2 = lambda x, y: jnp.concatenate([stack(x), stack(y)], axis=0).astype(BF16)
    lhs = each(stack2, a_t, r_t)
    rhs = each(stack2, b_h, k_h)
    sc = each(_dot_nt, lhs, rhs)
    ab = each(lambda s: jnp.where(strict, s[0:n2, 0:n2], 0.0).astype(BF16), sc)
    ak = each(lambda s: jnp.where(strict, s[0:n2, n2:2 * n2], 0.0).astype(BF16), sc)
    rbk = each(lambda s: jnp.where(jnp.concatenate([incl, incl], axis=1), s[n2:2 * n2, :], 0.0).astype(BF16), sc)

    st = [state[p] for p in tiles]
    ah = each(lambda l_, s_: _dot_nt(l_, s_.astype(BF16)), lhs, st)
    vs = each(lambda v_: stack(v_).astype(BF16), v)
    x = each(lambda ah_, ak_, vs_: ah_[0:n2] + _dot(ak_, vs_), ah, ak, vs)
    pw = ab
    n_lvl = ch.bit_length() - 1
    for lvl in range(n_lvl):
        x = each(lambda x_, p_: x_ + _dot(p_, x_.astype(BF16)), x, pw)
        if lvl + 1 < n_lvl:
            pw = each(lambda p_: _dot(p_, p_).astype(BF16), pw)
    uv = each(lambda x_, vs_: jnp.concatenate([x_.astype(BF16), vs_], axis=0), x, vs)
    ys = each(lambda ah_, m_, uv_: ah_[n2:2 * n2] + _dot(m_, uv_), ah, rbk, uv)
    y = each(lambda ys_: ys_[0:ch] + ys_[ch:n2], ys)

    bk = each(lambda b_, k_, ge_: stack2(b_ * ge_, k_ * ge_), b_h, k_h, g_end)
    new_st = each(lambda s_, ge_, uv_, bk_: s_ * ge_ + _dot_tn(uv_, bk_), st, g_end, uv, bk)
    for p in tiles:
        state[p] = new_st[p]

    mu = each(lambda y_: head_sum(y_) * (1.0 / hd), y)
    yc = each(lambda y_, m_: y_ - m_, y, mu)
    var = each(lambda c_: head_sum(c_ * c_) * (1.0 / hd), yc)
    yn = each(lambda c_, v_, g_, b_: c_ * lax.rsqrt(v_ + RWKV_LNX_EPS) * g_ + b_, yc, var, lng, lnb)
    bonus = each(lambda r_, kr_, w_, v_: head_sum(r_ * kr_ * w_) * v_, r, kr, rkw, v)
    for p in tiles:
        y_ref[:, sls[p]] = ((yn[p] + bonus[p]) * g[p]).astype(BF16)


def _rwkv_rec(r, k, v, lw, a, g, kk, ka, rk, lng, lnb, *, batch, seq):
    t, width = r.shape
    ch = RWKV_CHUNK
    steps = seq // ch
    n_tiles = width // LANES
    act = pl.BlockSpec((ch, width), lambda b, c: (b * steps + c, 0))
    row = pl.BlockSpec((1, width), lambda b, c: (0, 0))
    pipelined = 6 * _nbytes((ch, width), F32) + _nbytes((ch, width), BF16)
    return pl.pallas_call(
        functools.partial(_rwkv_rec_kernel, n_tiles=n_tiles),
        grid=(batch, steps),
        in_specs=[act] * 6 + [row] * 5,
        out_specs=act,
        out_shape=jax.ShapeDtypeStruct((t, width), BF16),
        scratch_shapes=[pltpu.VMEM((n_tiles, LANES, LANES), F32)],
        compiler_params=pltpu.CompilerParams(
            dimension_semantics=("arbitrary", "arbitrary"),
            vmem_limit_bytes=_vmem_limit(pipelined, _nbytes((n_tiles, LANES, LANES), F32), 16 << 20),
        ),
        name="rwkv_rec",
    )(r, k, v, lw, a, g, kk, ka, rk, lng, lnb)


def _merge_kernel(o_ref, y_ref, ga_ref, gb_ref, wa_ref, wb_ref, m_ref):
    ua = _dot(o_ref[...], wa_ref[...])
    ub = _dot(y_ref[...], wb_ref[...])
    m = _sigmoid(ga_ref[...].astype(F32)) * ua + _sigmoid(gb_ref[...].astype(F32)) * ub
    m_ref[...] = m.astype(BF16)


def _merge(o, y, main, wa, wb, *, tm=1024, tn=512):
    t, kdim = o.shape
    d = wa.shape[1]
    gate_col0 = (main.shape[1] - 2 * d) // tn
    nj = d // tn
    pipelined = (2 * _nbytes((tm, kdim), BF16) + 2 * _nbytes((kdim, tn), BF16)
                 + 3 * _nbytes((tm, tn), BF16))
    return pl.pallas_call(
        _merge_kernel,
        grid=(t // tm, nj),
        in_specs=[
            pl.BlockSpec((tm, kdim), lambda i, j: (i, 0)),
            pl.BlockSpec((tm, kdim), lambda i, j: (i, 0)),
            pl.BlockSpec((tm, tn), lambda i, j: (i, gate_col0 + j)),
            pl.BlockSpec((tm, tn), lambda i, j: (i, gate_col0 + nj + j)),
            pl.BlockSpec((kdim, tn), lambda i, j: (0, j)),
            pl.BlockSpec((kdim, tn), lambda i, j: (0, j)),
        ],
        out_specs=pl.BlockSpec((tm, tn), lambda i, j: (i, j)),
        out_shape=jax.ShapeDtypeStruct((t, d), BF16),
        compiler_params=pltpu.CompilerParams(
            dimension_semantics=("arbitrary", "arbitrary"),
            vmem_limit_bytes=_vmem_limit(pipelined, 0, 6 * _nbytes((tm, tn), F32)),
        ),
        name="merge",
    )(o, y, main, main, wa, wb)


def _outproj_kernel(m_ref, w_ref, x_ref, gate_ref, lng_ref, lnb_ref, scale_ref, shift_ref,
                    x1_ref, h2_ref, acc):
    j = pl.program_id(1)
    tn = w_ref.shape[1]
    acc[:, pl.ds(pl.multiple_of(j * tn, tn), tn)] = _dot(m_ref[...], w_ref[...])

    @pl.when(j == pl.num_programs(1) - 1)
    def _():
        z = DEEPNORM_ALPHA * x_ref[...] + gate_ref[0] * acc[...]
        x1 = _layernorm(z) * lng_ref[...] + lnb_ref[...]
        x1_ref[...] = x1
        h2_ref[...] = (_layernorm(x1) * (1.0 + scale_ref[0]) + shift_ref[0]).astype(BF16)


def _outproj(m, w_out, x2d, gate1, ln_g, ln_b, scale2, shift2, *, seq, tm=512, tn=512):
    t, d = x2d.shape
    tiles_per_batch = seq // tm
    mod = pl.BlockSpec((1, 1, d), lambda i, j: (i // tiles_per_batch, 0, 0))
    row = pl.BlockSpec((1, d), lambda i, j: (0, 0))
    pipelined = (_nbytes((tm, d), BF16) + _nbytes((d, tn), BF16) + 2 * _nbytes((tm, d), F32)
                 + _nbytes((tm, d), BF16))
    return pl.pallas_call(
        _outproj_kernel,
        grid=(t // tm, d // tn),
        in_specs=[
            pl.BlockSpec((tm, d), lambda i, j: (i, 0)),
            pl.BlockSpec((d, tn), lambda i, j: (0, j)),
            pl.BlockSpec((tm, d), lambda i, j: (i, 0)),
            mod, row, row, mod, mod,
        ],
        out_specs=[pl.BlockSpec((tm, d), lambda i, j: (i, 0)),
                   pl.BlockSpec((tm, d), lambda i, j: (i, 0))],
        out_shape=[jax.ShapeDtypeStruct((t, d), F32), jax.ShapeDtypeStruct((t, d), BF16)],
        scratch_shapes=[pltpu.VMEM((tm, d), F32)],
        compiler_params=pltpu.CompilerParams(
            dimension_semantics=("arbitrary", "arbitrary"),
            vmem_limit_bytes=_vmem_limit(pipelined, _nbytes((tm, d), F32), 3 * _nbytes((tm, d), F32)),
        ),
        name="outproj",
    )(m, w_out, x2d, gate1, ln_g, ln_b, scale2, shift2)


def _ffn_kernel(h_ref, wg_ref, wu_ref, wd_ref, x1_ref, gate_ref, lng_ref, lnb_ref, o_ref, acc):
    f = pl.program_id(1)
    h = h_ref[...]
    g = _dot(h, wg_ref[...])
    u = _dot(h, wu_ref[...])
    act = (g * _sigmoid(g) * u).astype(BF16)
    part = _dot(act, wd_ref[...])

    @pl.when(f == 0)
    def _():
        acc[...] = part

    @pl.when(f > 0)
    def _():
        acc[...] += part

    @pl.when(f == pl.num_programs(1) - 1)
    def _():
        z = DEEPNORM_ALPHA * x1_ref[...] + gate_ref[0] * acc[...]
        o_ref[...] = _layernorm(z) * lng_ref[...] + lnb_ref[...]


def _ffn(h2, w_gu, w_down, x1, gate2, ln_g, ln_b, *, seq, tm=512, tf=512):
    t, d = x1.shape
    d_ff = w_down.shape[0]
    nf = d_ff // tf
    tiles_per_batch = seq // tm
    mod = pl.BlockSpec((1, 1, d), lambda i, f: (i // tiles_per_batch, 0, 0))
    row = pl.BlockSpec((1, d), lambda i, f: (0, 0))
    pipelined = (_nbytes((tm, d), BF16) + 3 * _nbytes((d, tf), BF16) + 2 * _nbytes((tm, d), F32))
    return pl.pallas_call(
        _ffn_kernel,
        grid=(t // tm, nf),
        in_specs=[
            pl.BlockSpec((tm, d), lambda i, f: (i, 0)),
            pl.BlockSpec((d, tf), lambda i, f: (0, f)),
            pl.BlockSpec((d, tf), lambda i, f: (0, nf + f)),
            pl.BlockSpec((tf, d), lambda i, f: (f, 0)),
            pl.BlockSpec((tm, d), lambda i, f: (i, 0)),
            mod, row, row,
        ],
        out_specs=pl.BlockSpec((tm, d), lambda i, f: (i, 0)),
        out_shape=jax.ShapeDtypeStruct((t, d), F32),
        scratch_shapes=[pltpu.VMEM((tm, d), F32)],
        compiler_params=pltpu.CompilerParams(
            dimension_semantics=("arbitrary", "arbitrary"),
            vmem_limit_bytes=_vmem_limit(pipelined, _nbytes((tm, d), F32),
                                         2 * _nbytes((tm, d), F32) + 4 * _nbytes((tm, tf), F32)),
        ),
        name="ffn",
    )(h2, w_gu, w_gu, w_down, x1, gate2, ln_g, ln_b)


def _pad_rows(w, rows):
    return jnp.zeros((rows, w.shape[1]), w.dtype).at[:w.shape[0]].set(w)


def kernel(x, c, w_ada, b_ada, w_in, b_fgate, q_norm_g, k_norm_g, rwkv_mu, rwkv_w0, rwkv_w2, rwkv_a0, rwkv_a2, rwkv_g2, rwkv_k_k, rwkv_k_a, rwkv_r_k, rwkv_lnx_g, rwkv_lnx_b, w_branch_a, w_branch_b, w_out, ln1_g, ln1_b, w_ffn_gu, w_ffn_down, ln2_g, ln2_b):
    batch, seq, d = x.shape
    t = batch * seq
    fox_w = FOX_HEADS * FOX_HEAD_DIM
    rw_w = RWKV_HEADS * RWKV_HEAD_DIM
    fox_cols = 4 * fox_w + FOX_HEADS
    lora0 = fox_cols + 3 * rw_w
    gate0 = lora0 + DECAY_LORA + AAA_LORA + GATE_LORA
    assert seq % RWKV_CHUNK == 0 and w_in.shape[1] == gate0 + 2 * d

    zpad = lambda n: jnp.zeros((d, n), w_in.dtype)
    w_pack = jnp.concatenate([
        w_in[:, :4 * fox_w],
        w_in[:, fox_cols:lora0],
        w_in[:, gate0:],
        w_in[:, lora0:lora0 + DECAY_LORA], zpad(LORA_PAD - DECAY_LORA),
        w_in[:, lora0 + DECAY_LORA:lora0 + DECAY_LORA + AAA_LORA], zpad(LORA_PAD - AAA_LORA),
        w_in[:, lora0 + DECAY_LORA + AAA_LORA:gate0],
    ], axis=1).astype(BF16)
    wft = w_in[:, 4 * fox_w:fox_cols].T.astype(BF16)
    mu_r = rwkv_mu[None, 0:rw_w]
    mu_k = rwkv_mu[None, rw_w:2 * rw_w]
    mu_v = rwkv_mu[None, 2 * rw_w:3 * rw_w]
    mu_l = rwkv_mu[3 * rw_w:]
    zrow = lambda n: jnp.zeros((n,), rwkv_mu.dtype)
    mu_s = jnp.concatenate([
        mu_l[:DECAY_LORA], zrow(LORA_PAD - DECAY_LORA),
        mu_l[DECAY_LORA:DECAY_LORA + AAA_LORA], zrow(LORA_PAD - AAA_LORA),
        mu_l[DECAY_LORA + AAA_LORA:]])[None]
    w2p = _pad_rows(rwkv_w2, LORA_PAD).astype(BF16)
    a2p = _pad_rows(rwkv_a2, LORA_PAD).astype(BF16)
    g2 = rwkv_g2.astype(BF16)
    wa = w_branch_a.astype(BF16)
    wb = w_branch_b.astype(BF16)
    wo = w_out.astype(BF16)
    wgu = w_ffn_gu.astype(BF16)
    wdn = w_ffn_down.astype(BF16)
    row = lambda v: v.reshape(1, -1)

    c_pad = jnp.zeros((8, d), c.dtype).at[:batch].set(c)
    mod = _ada(c_pad, w_ada, row(b_ada))[:batch].reshape(batch, 6, 1, d)
    shift1, scale1, gate1, shift2, scale2, gate2 = (mod[:, i] for i in range(6))

    x2d = x.reshape(t, d)
    main, small, ft = _inproj(x2d, scale1, shift1, w_pack, wft, row(q_norm_g), row(k_norm_g), seq=seq)

    blocks_per_seq = seq // LANES
    ft_rows = ft.reshape(FOX_HEADS * batch * blocks_per_seq, LANES)
    bias_rows = jnp.broadcast_to(b_fgate[:, None, None], (FOX_HEADS, batch * blocks_per_seq, LANES))
    cum = _fox_prep(ft_rows, bias_rows.reshape(ft_rows.shape), blocks_per_seq=blocks_per_seq)
    cum = cum.reshape(FOX_HEADS, batch, 1, seq)
    o = _fox_attn(main, cum, batch=batch, seq=seq)

    r, k, v, lw, a, g = _rwkv_prep(main, small, mu_r, mu_k, mu_v, mu_s, row(rwkv_w0), w2p,
                                   row(rwkv_a0), a2p, g2, batch=batch, seq=seq)
    y = _rwkv_rec(r, k, v, lw, a, g, row(rwkv_k_k), row(rwkv_k_a), row(rwkv_r_k),
                  row(rwkv_lnx_g), row(rwkv_lnx_b), batch=batch, seq=seq)

    merged = _merge(o, y, main, wa, wb)
    x1, h2 = _outproj(merged, wo, x2d, gate1, row(ln1_g), row(ln1_b), scale2, shift2, seq=seq)

    out = _ffn(h2, wgu, wdn, x1, gate2, row(ln2_g), row(ln2_b), seq=seq)
    return out.reshape(batch, seq, d)
```

```python
import functools

import jax
import jax.numpy as jnp
from jax import lax
from jax.experimental import pallas as pl
from jax.experimental.pallas import tpu as pltpu

F32 = jnp.float32
BF16 = jnp.bfloat16

LANES = 128
VMEM_CAP_BYTES = 60000 * 1024

FOX_HEADS = 8
FOX_HEAD_DIM = 128
RWKV_HEADS = 16
RWKV_HEAD_DIM = 64
DECAY_LORA = 96
AAA_LORA = 96
GATE_LORA = 256
LORA_PAD = 128
RWKV_LNX_EPS = 64e-5
DEPTH = 1
DEEPNORM_ALPHA = (2.0 * DEPTH) ** 0.25
LN_EPS = 1e-5
RMS_EPS = 1e-6
RWKV_CHUNK = 64
HEADS_PER_TILE = LANES // RWKV_HEAD_DIM


def _vmem_limit(pipelined_bytes, resident_bytes=0, temp_bytes=0):
    need = 2 * pipelined_bytes + resident_bytes + temp_bytes + (2 << 20)
    return int(min(VMEM_CAP_BYTES, need))


def _nbytes(shape, dtype):
    n = 1
    for s in shape:
        n *= s
    return n * jnp.dtype(dtype).itemsize


def _layernorm(x):
    mu = jnp.mean(x, axis=-1, keepdims=True)
    xc = x - mu
    var = jnp.mean(xc * xc, axis=-1, keepdims=True)
    return xc * lax.rsqrt(var + LN_EPS)


def _sigmoid(x):
    return 1.0 / (1.0 + jnp.exp(-x))


def _dot(a, b):
    return jnp.dot(a, b, preferred_element_type=F32)


def _dot_nt(a, b):
    return lax.dot_general(a, b, (((1,), (1,)), ((), ())), preferred_element_type=F32)


def _dot_tn(a, b):
    return lax.dot_general(a, b, (((0,), (0,)), ((), ())), preferred_element_type=F32)


def _split3(x):
    hi = x.astype(BF16)
    r1 = x - hi.astype(F32)
    mid = r1.astype(BF16)
    lo = (r1 - mid.astype(F32)).astype(BF16)
    return hi, mid, lo


def _ada_kernel(c_ref, w_ref, b_ref, o_ref):
    c = c_ref[...]
    s = (c * _sigmoid(c)).astype(BF16)
    o_ref[...] = _dot(s, w_ref[...].astype(BF16)) + b_ref[...]


def _ada(c_pad, w_ada, b_ada, *, tn=1024):
    rows, d = c_pad.shape
    n = w_ada.shape[1]
    return pl.pallas_call(
        _ada_kernel,
        grid=(n // tn,),
        in_specs=[
            pl.BlockSpec((rows, d), lambda j: (0, 0)),
            pl.BlockSpec((d, tn), lambda j: (0, j)),
            pl.BlockSpec((1, tn), lambda j: (0, j)),
        ],
        out_specs=pl.BlockSpec((rows, tn), lambda j: (0, j)),
        out_shape=jax.ShapeDtypeStruct((rows, n), F32),
        compiler_params=pltpu.CompilerParams(
            dimension_semantics=("arbitrary",),
            vmem_limit_bytes=_vmem_limit(_nbytes((d, tn), F32), temp_bytes=_nbytes((d, tn), BF16)),
        ),
        name="ada",
    )(c_pad, w_ada, b_ada)


def _inproj_kernel(x_ref, scale_ref, shift_ref, wfox_ref, wrw_ref, wgate_ref, wlora_ref, wft_ref,
                   qg_ref, kg_ref, main_ref, small_ref, ft_ref, h_scr, *, n_q, n_k, n_fox, n_rw, n_main):
    j = pl.program_id(1)
    tn = main_ref.shape[1]

    @pl.when(j == 0)
    def _():
        h = _layernorm(x_ref[...]) * (1.0 + scale_ref[0]) + shift_ref[0]
        h_scr[...] = h.astype(BF16)

    @pl.when(j < n_q + n_k)
    def _():
        acc = _dot(h_scr[...], wfox_ref[...])
        gain = jnp.where(j < n_q, qg_ref[...] * (FOX_HEAD_DIM ** -0.5), kg_ref[...])
        for hh in range(tn // FOX_HEAD_DIM):
            sl = slice(hh * FOX_HEAD_DIM, (hh + 1) * FOX_HEAD_DIM)
            a = acc[:, sl]
            ms = jnp.mean(a * a, axis=-1, keepdims=True)
            main_ref[:, sl] = (a * lax.rsqrt(ms + RMS_EPS) * gain).astype(BF16)

    @pl.when((j >= n_q + n_k) & (j < n_fox))
    def _():
        main_ref[...] = _dot(h_scr[...], wfox_ref[...]).astype(BF16)

    @pl.when((j >= n_fox) & (j < n_fox + n_rw))
    def _():
        main_ref[...] = _dot(h_scr[...], wrw_ref[...]).astype(BF16)

    @pl.when((j >= n_fox + n_rw) & (j < n_main))
    def _():
        main_ref[...] = _dot(h_scr[...], wgate_ref[...]).astype(BF16)

    @pl.when(j == n_main)
    def _():
        small_ref[...] = _dot(h_scr[...], wlora_ref[...])
        ft_ref[...] = _dot_nt(wft_ref[...], h_scr[...])


def _inproj(x2d, scale1, shift1, w_fox, w_rw, w_gate, w_lora, wft, qg, kg, *, seq, tm=1024, tn=512):
    t, d = x2d.shape
    fox_w = FOX_HEADS * FOX_HEAD_DIM
    n_fox, n_rw, n_gate = w_fox.shape[1] // tn, w_rw.shape[1] // tn, w_gate.shape[1] // tn
    n_main = n_fox + n_rw + n_gate
    assert w_lora.shape[1] == tn
    tiles_per_batch = seq // tm
    kern = functools.partial(_inproj_kernel, n_q=fox_w // tn, n_k=fox_w // tn,
                             n_fox=n_fox, n_rw=n_rw, n_main=n_main)
    pipelined = (_nbytes((tm, d), F32) + 4 * _nbytes((d, tn), BF16) + _nbytes((tm, tn), BF16)
                 + _nbytes((tm, tn), F32) + _nbytes((8, tm), F32))
    wspec = lambda lo, n: pl.BlockSpec((d, tn), lambda i, j: (0, jnp.clip(j - lo, 0, n - 1)))
    return pl.pallas_call(
        kern,
        grid=(t // tm, n_main + 1),
        in_specs=[
            pl.BlockSpec((tm, d), lambda i, j: (i, 0)),
            pl.BlockSpec((1, 1, d), lambda i, j: (i // tiles_per_batch, 0, 0)),
            pl.BlockSpec((1, 1, d), lambda i, j: (i // tiles_per_batch, 0, 0)),
            wspec(0, n_fox), wspec(n_fox, n_rw), wspec(n_fox + n_rw, n_gate), wspec(n_main, 1),
            pl.BlockSpec((FOX_HEADS, d), lambda i, j: (0, 0)),
            pl.BlockSpec((1, FOX_HEAD_DIM), lambda i, j: (0, 0)),
            pl.BlockSpec((1, FOX_HEAD_DIM), lambda i, j: (0, 0)),
        ],
        out_specs=[
            pl.BlockSpec((tm, tn), lambda i, j: (i, jnp.minimum(j, n_main - 1))),
            pl.BlockSpec((tm, tn), lambda i, j: (i, 0)),
            pl.BlockSpec((FOX_HEADS, tm), lambda i, j: (0, i)),
        ],
        out_shape=[
            jax.ShapeDtypeStruct((t, n_main * tn), BF16),
            jax.ShapeDtypeStruct((t, tn), F32),
            jax.ShapeDtypeStruct((FOX_HEADS, t), F32),
        ],
        scratch_shapes=[pltpu.VMEM((tm, d), BF16)],
        compiler_params=pltpu.CompilerParams(
            dimension_semantics=("arbitrary", "arbitrary"),
            vmem_limit_bytes=_vmem_limit(pipelined, _nbytes((tm, d), BF16),
                                         3 * _nbytes((tm, d), F32)),
        ),
        name="inproj",
    )(x2d, scale1, shift1, w_fox, w_rw, w_gate, w_lora, wft, qg, kg)


def _fox_prep_kernel(ft_ref, bias_ref, o_ref, *, blocks_per_seq):
    x = ft_ref[...] + bias_ref[...]
    lf = jnp.minimum(x, 0.0) - jnp.log1p(jnp.exp(-jnp.abs(x)))
    rows, width = lf.shape
    r = lax.broadcasted_iota(jnp.int32, (width, width), 0)
    c = lax.broadcasted_iota(jnp.int32, (width, width), 1)
    upper = (r <= c).astype(BF16)
    within = sum(_dot(p, upper) for p in _split3(lf))
    tot = jnp.broadcast_to(within[:, width - 1:width], (rows, width))
    rr = lax.broadcasted_iota(jnp.int32, (rows, rows), 0)
    cc = lax.broadcasted_iota(jnp.int32, (rows, rows), 1)
    prior = ((rr // blocks_per_seq == cc // blocks_per_seq) & (cc < rr)).astype(BF16)
    offset = sum(_dot(prior, p) for p in _split3(tot))
    o_ref[...] = within + offset


def _fox_prep(ft_rows, bias_rows, *, blocks_per_seq):
    return pl.pallas_call(
        functools.partial(_fox_prep_kernel, blocks_per_seq=blocks_per_seq),
        out_shape=jax.ShapeDtypeStruct(ft_rows.shape, F32),
        name="fox_prep",
    )(ft_rows, bias_rows)


def _fox_attn_kernel(q_ref, k_ref, v_ref, og_ref, cum_ref, o_ref, *, tq):
    seq = q_ref.shape[0]
    negcum = -cum_ref[0, 0]
    for qi in range(seq // tq):
        q0, kend = qi * tq, (qi + 1) * tq
        s = _dot_nt(q_ref[q0:kend, :], k_ref[0:kend, :]) + negcum[:, 0:kend]
        row = q0 + lax.broadcasted_iota(jnp.int32, (tq, kend), 0)
        col = lax.broadcasted_iota(jnp.int32, (tq, kend), 1)
        s = jnp.where(col <= row, s, -jnp.inf)
        m = jnp.max(s, axis=-1, keepdims=True)
        p = jnp.exp(s - m)
        l = jnp.sum(p, axis=-1, keepdims=True)
        o = _dot(p.astype(BF16), v_ref[0:kend, :]) / l
        gate = _sigmoid(og_ref[q0:kend, :].astype(F32))
        o_ref[q0:kend, :] = (o * gate).astype(BF16)


def _fox_attn(main, cum, *, batch, seq, tq=256):
    t = main.shape[0]
    hd = FOX_HEAD_DIM
    nh = FOX_HEADS
    blk = lambda off: pl.BlockSpec((seq, hd), lambda b, h: (b, off + h))
    pipelined = 5 * _nbytes((seq, hd), BF16) + _nbytes((8, seq), F32)
    return pl.pallas_call(
        functools.partial(_fox_attn_kernel, tq=tq),
        grid=(batch, nh),
        in_specs=[blk(0), blk(nh), blk(2 * nh), blk(3 * nh),
                  pl.BlockSpec((1, 1, 1, seq), lambda b, h: (h, b, 0, 0))],
        out_specs=pl.BlockSpec((seq, hd), lambda b, h: (b, h)),
        out_shape=jax.ShapeDtypeStruct((t, nh * hd), BF16),
        compiler_params=pltpu.CompilerParams(
            dimension_semantics=("arbitrary", "arbitrary"),
            vmem_limit_bytes=_vmem_limit(pipelined, 0, 6 * _nbytes((tq, seq), F32)),
        ),
        name="fox_attn",
    )(main, main, main, main, cum)


def _rwkv_prep_kernel(r_ref, k_ref, v_ref, small_ref, mu_r_ref, mu_k_ref, mu_v_ref, mu_s_ref,
                      w0_ref, w2_ref, a0_ref, a2_ref, g2_ref,
                      r_o, k_o, v_o, lw_o, a_o, g_o, prev_r, prev_k, prev_v, prev_s):
    prevs = (prev_r, prev_k, prev_v, prev_s)

    @pl.when(pl.program_id(1) == 0)
    def _():
        for p in prevs:
            p[...] = jnp.zeros_like(p)

    def shift_mix(x, prev_ref, mu):
        tm = x.shape[0]
        xp = pltpu.roll(x, 1, axis=0)
        row = lax.broadcasted_iota(jnp.int32, x.shape, 0)
        xp = jnp.where(row == 0, prev_ref[0:1, :], xp)
        prev_ref[0:1, :] = x[tm - 1:tm, :]
        return x + (xp - x) * mu

    r = shift_mix(r_ref[...].astype(F32), prev_r, mu_r_ref[...])
    k = shift_mix(k_ref[...].astype(F32), prev_k, mu_k_ref[...])
    v = shift_mix(v_ref[...].astype(F32), prev_v, mu_v_ref[...])
    sm = shift_mix(small_ref[...], prev_s, mu_s_ref[...])
    xw = sm[:, 0:LORA_PAD]
    xa = sm[:, LORA_PAD:2 * LORA_PAD]
    xg = sm[:, 2 * LORA_PAD:2 * LORA_PAD + GATE_LORA]

    z = w0_ref[...] + _dot(jnp.tanh(xw).astype(BF16), w2_ref[...])
    w_raw = -(jnp.maximum(-z, 0.0) + jnp.log1p(jnp.exp(-jnp.abs(z)))) - 0.5
    lw_o[...] = -jnp.exp(w_raw)
    a_o[...] = _sigmoid(a0_ref[...] + _dot(xa.astype(BF16), a2_ref[...])).astype(BF16)
    g_o[...] = _dot(_sigmoid(xg).astype(BF16), g2_ref[...]).astype(BF16)
    r_o[...] = r.astype(BF16)
    k_o[...] = k.astype(BF16)
    v_o[...] = v.astype(BF16)


def _rwkv_prep(main, small, mu_r, mu_k, mu_v, mu_s, w0, w2p, a0, a2p, g2, *, batch, seq, tm=256):
    t = main.shape[0]
    width = RWKV_HEADS * RWKV_HEAD_DIM
    nsm = small.shape[1]
    col0 = (4 * FOX_HEADS * FOX_HEAD_DIM) // width
    steps = seq // tm
    act = lambda off: pl.BlockSpec((tm, width), lambda b, s: (b * steps + s, off))
    row_spec = lambda n: pl.BlockSpec((1, n), lambda b, s: (0, 0))
    full = lambda a: pl.BlockSpec(a.shape, lambda b, s: (0, 0))
    out_spec = pl.BlockSpec((tm, width), lambda b, s: (b * steps + s, 0))
    out_dtypes = [BF16, BF16, BF16, F32, BF16, BF16]
    pipelined = (3 * _nbytes((tm, width), BF16) + _nbytes((tm, nsm), F32)
                 + sum(_nbytes((tm, width), dt) for dt in out_dtypes)
                 + _nbytes(w2p.shape, BF16) + _nbytes(a2p.shape, BF16) + _nbytes(g2.shape, BF16))
    return pl.pallas_call(
        _rwkv_prep_kernel,
        grid=(batch, steps),
        in_specs=[act(col0), act(col0 + 1), act(col0 + 2),
                  pl.BlockSpec((tm, nsm), lambda b, s: (b * steps + s, 0)),
                  row_spec(width), row_spec(width), row_spec(width), row_spec(nsm),
                  row_spec(width), full(w2p), row_spec(width), full(a2p), full(g2)],
        out_specs=[out_spec] * 6,
        out_shape=[jax.ShapeDtypeStruct((t, width), dt) for dt in out_dtypes],
        scratch_shapes=[pltpu.VMEM((8, width), F32)] * 3 + [pltpu.VMEM((8, nsm), F32)],
        compiler_params=pltpu.CompilerParams(
            dimension_semantics=("arbitrary", "arbitrary"),
            vmem_limit_bytes=_vmem_limit(pipelined, 0, 12 * _nbytes((tm, width), F32)),
        ),
        name="rwkv_prep",
    )(main, main, main, small, mu_r, mu_k, mu_v, mu_s, w0, w2p, a0, a2p, g2)


def _rwkv_rec_kernel(r_ref, k_ref, v_ref, lw_ref, a_ref, g_ref,
                     kk_ref, ka_ref, rk_ref, lng_ref, lnb_ref, y_ref, state, *, n_tiles):
    c = pl.program_id(1)

    @pl.when(c == 0)
    def _():
        state[...] = jnp.zeros_like(state)

    ch = r_ref.shape[0]
    hd = RWKV_HEAD_DIM
    lane = lax.broadcasted_iota(jnp.int32, (1, LANES), 1)
    m0 = lane < hd
    n2 = HEADS_PER_TILE * ch
    rr = lax.broadcasted_iota(jnp.int32, (n2, n2), 0)
    cc = lax.broadcasted_iota(jnp.int32, (n2, n2), 1)
    same = (rr // ch) == (cc // ch)
    strict = same & (rr > cc)
    incl = same & (rr >= cc)
    lr = lax.broadcasted_iota(jnp.int32, (ch, ch), 0)
    lc = lax.broadcasted_iota(jnp.int32, (ch, ch), 1)
    lower = (lc <= lr).astype(BF16)

    def head_sum(x):
        s0 = jnp.sum(jnp.where(m0, x, 0.0), axis=-1, keepdims=True)
        s1 = jnp.sum(jnp.where(m0, 0.0, x), axis=-1, keepdims=True)
        return jnp.where(m0, s0, s1)

    def stack(x):
        return jnp.concatenate([jnp.where(m0, x, 0.0), jnp.where(m0, 0.0, x)], axis=0)

    tiles = range(n_tiles)
    sls = [slice(p * LANES, (p + 1) * LANES) for p in tiles]

    def each(fn, *cols):
        return [fn(*args) for args in zip(*cols)]

    def load(ref):
        return [ref[:, sl].astype(F32) for sl in sls]

    r, k, v, lw, a, g = (load(ref) for ref in (r_ref, k_ref, v_ref, lw_ref, a_ref, g_ref))
    kkw, kaw, rkw, lng, lnb = (load(ref) for ref in (kk_ref, ka_ref, rk_ref, lng_ref, lnb_ref))

    kk = each(lambda k_, w_: k_ * w_, k, kkw)
    kk = each(lambda x: x / jnp.maximum(jnp.sqrt(head_sum(x * x)), 1e-12), kk)
    kr = each(lambda k_, a_, w_: k_ * (1.0 + (a_ - 1.0) * w_), k, a, kaw)
    cum = each(lambda x: sum(_dot(lower, part) for part in _split3(x)), lw)
    gam = each(jnp.exp, cum)
    ginv = each(lambda x: jnp.exp(-x), cum)
    g_end = each(lambda x: x[ch - 1:ch, :], gam)
    a_t = each(lambda kk_, c_, lw_: -kk_ * jnp.exp(c_ - lw_), kk, cum, lw)
    r_t = each(lambda r_, g_: r_ * g_, r, gam)
    b_h = each(lambda kk_, a_, gi_: kk_ * a_ * gi_, kk, a, ginv)
    k_h = each(lambda kr_, gi_: kr_ * gi_, kr, ginv)

    stack2 = lambda x, y: jnp.concatenate([stack(x), stack(y)], axis=0).astype(BF16)
    lhs = each(stack2, a_t, r_t)
    rhs = each(stack2, b_h, k_h)
    sc = each(_dot_nt, lhs, rhs)
    ab = each(lambda s: jnp.where(strict, s[0:n2, 0:n2], 0.0).astype(BF16), sc)
    ak = each(lambda s: jnp.where(strict, s[0:n2, n2:2 * n2], 0.0).astype(BF16), sc)
    rbk = each(lambda s: jnp.where(jnp.concatenate([incl, incl], axis=1), s[n2:2 * n2, :], 0.0).astype(BF16), sc)

    st = [state[p] for p in tiles]
    ah = each(lambda l_, s_: _dot_nt(l_, s_.astype(BF16)), lhs, st)
    vs = each(lambda v_: stack(v_).astype(BF16), v)
    x = each(lambda ah_, ak_, vs_: ah_[0:n2] + _dot(ak_, vs_), ah, ak, vs)
    pw = ab
    n_lvl = ch.bit_length() - 1
    for lvl in range(n_lvl):
        x = each(lambda x_, p_: x_ + _dot(p_, x_.astype(BF16)), x, pw)
        if lvl + 1 < n_lvl:
            pw = each(lambda p_: _dot(p_, p_).astype(BF16), pw)
    uv = each(lambda x_, vs_: jnp.concatenate([x_.astype(BF16), vs_], axis=0), x, vs)
    ys = each(lambda ah_, m_, uv_: ah_[n2:2 * n2] + _dot(m_, uv_), ah, rbk, uv)
    y = each(lambda ys_: ys_[0:ch] + ys_[ch:n2], ys)

    bk = each(lambda b_, k_, ge_: stack2(b_ * ge_, k_ * ge_), b_h, k_h, g_end)
    new_st = each(lambda s_, ge_, uv_, bk_: s_ * ge_ + _dot_tn(uv_, bk_), st, g_end, uv, bk)
    for p in tiles:
        state[p] = new_st[p]

    mu = each(lambda y_: head_sum(y_) * (1.0 / hd), y)
    yc = each(lambda y_, m_: y_ - m_, y, mu)
    var = each(lambda c_: head_sum(c_ * c_) * (1.0 / hd), yc)
    yn = each(lambda c_, v_, g_, b_: c_ * lax.rsqrt(v_ + RWKV_LNX_EPS) * g_ + b_, yc, var, lng, lnb)
    bonus = each(lambda r_, kr_, w_, v_: head_sum(r_ * kr_ * w_) * v_, r, kr, rkw, v)
    for p in tiles:
        y_ref[:, sls[p]] = ((yn[p] + bonus[p]) * g[p]).astype(BF16)


def _rwkv_rec(r, k, v, lw, a, g, kk, ka, rk, lng, lnb, *, batch, seq):
    t, width = r.shape
    ch = RWKV_CHUNK
    steps = seq // ch
    n_tiles = width // LANES
    act = pl.BlockSpec((ch, width), lambda b, c: (b * steps + c, 0))
    row = pl.BlockSpec((1, width), lambda b, c: (0, 0))
    pipelined = sum(_nbytes((ch, width), x.dtype) for x in (r, k, v, lw, a, g)) + _nbytes((ch, width), BF16)
    return pl.pallas_call(
        functools.partial(_rwkv_rec_kernel, n_tiles=n_tiles),
        grid=(batch, steps),
        in_specs=[act] * 6 + [row] * 5,
        out_specs=act,
        out_shape=jax.ShapeDtypeStruct((t, width), BF16),
        scratch_shapes=[pltpu.VMEM((n_tiles, LANES, LANES), F32)],
        compiler_params=pltpu.CompilerParams(
            dimension_semantics=("arbitrary", "arbitrary"),
            vmem_limit_bytes=_vmem_limit(pipelined, _nbytes((n_tiles, LANES, LANES), F32), 16 << 20),
        ),
        name="rwkv_rec",
    )(r, k, v, lw, a, g, kk, ka, rk, lng, lnb)


def _merge_kernel(o_ref, y_ref, ga_ref, gb_ref, wa_ref, wb_ref, m_ref):
    ua = _dot(o_ref[...], wa_ref[...])
    ub = _dot(y_ref[...], wb_ref[...])
    m = _sigmoid(ga_ref[...].astype(F32)) * ua + _sigmoid(gb_ref[...].astype(F32)) * ub
    m_ref[...] = m.astype(BF16)


def _merge(o, y, main, wa, wb, *, tm=1024, tn=512):
    t, kdim = o.shape
    d = wa.shape[1]
    gate_col0 = (main.shape[1] - 2 * d) // tn
    nj = d // tn
    pipelined = (2 * _nbytes((tm, kdim), BF16) + 2 * _nbytes((kdim, tn), BF16)
                 + 3 * _nbytes((tm, tn), BF16))
    return pl.pallas_call(
        _merge_kernel,
        grid=(t // tm, nj),
        in_specs=[
            pl.BlockSpec((tm, kdim), lambda i, j: (i, 0)),
            pl.BlockSpec((tm, kdim), lambda i, j: (i, 0)),
            pl.BlockSpec((tm, tn), lambda i, j: (i, gate_col0 + j)),
            pl.BlockSpec((tm, tn), lambda i, j: (i, gate_col0 + nj + j)),
            pl.BlockSpec((kdim, tn), lambda i, j: (0, j)),
            pl.BlockSpec((kdim, tn), lambda i, j: (0, j)),
        ],
        out_specs=pl.BlockSpec((tm, tn), lambda i, j: (i, j)),
        out_shape=jax.ShapeDtypeStruct((t, d), BF16),
        compiler_params=pltpu.CompilerParams(
            dimension_semantics=("arbitrary", "arbitrary"),
            vmem_limit_bytes=_vmem_limit(pipelined, 0, 6 * _nbytes((tm, tn), F32)),
        ),
        name="merge",
    )(o, y, main, main, wa, wb)


def _outproj_kernel(m_ref, w_ref, x_ref, gate_ref, lng_ref, lnb_ref, scale_ref, shift_ref,
                    x1_ref, h2_ref, acc):
    j = pl.program_id(1)
    tn = w_ref.shape[1]
    acc[:, pl.ds(pl.multiple_of(j * tn, tn), tn)] = _dot(m_ref[...], w_ref[...])

    @pl.when(j == pl.num_programs(1) - 1)
    def _():
        z = DEEPNORM_ALPHA * x_ref[...] + gate_ref[0] * acc[...]
        x1 = _layernorm(z) * lng_ref[...] + lnb_ref[...]
        x1_ref[...] = x1
        h2_ref[...] = (_layernorm(x1) * (1.0 + scale_ref[0]) + shift_ref[0]).astype(BF16)


def _outproj(m, w_out, x2d, gate1, ln_g, ln_b, scale2, shift2, *, seq, tm=512, tn=512):
    t, d = x2d.shape
    tiles_per_batch = seq // tm
    mod = pl.BlockSpec((1, 1, d), lambda i, j: (i // tiles_per_batch, 0, 0))
    row = pl.BlockSpec((1, d), lambda i, j: (0, 0))
    pipelined = (_nbytes((tm, d), BF16) + _nbytes((d, tn), BF16) + 2 * _nbytes((tm, d), F32)
                 + _nbytes((tm, d), BF16))
    return pl.pallas_call(
        _outproj_kernel,
        grid=(t // tm, d // tn),
        in_specs=[
            pl.BlockSpec((tm, d), lambda i, j: (i, 0)),
            pl.BlockSpec((d, tn), lambda i, j: (0, j)),
            pl.BlockSpec((tm, d), lambda i, j: (i, 0)),
            mod, row, row, mod, mod,
        ],
        out_specs=[pl.BlockSpec((tm, d), lambda i, j: (i, 0)),
                   pl.BlockSpec((tm, d), lambda i, j: (i, 0))],
        out_shape=[jax.ShapeDtypeStruct((t, d), F32), jax.ShapeDtypeStruct((t, d), BF16)],
        scratch_shapes=[pltpu.VMEM((tm, d), F32)],
        compiler_params=pltpu.CompilerParams(
            dimension_semantics=("arbitrary", "arbitrary"),
            vmem_limit_bytes=_vmem_limit(pipelined, _nbytes((tm, d), F32), 3 * _nbytes((tm, d), F32)),
        ),
        name="outproj",
    )(m, w_out, x2d, gate1, ln_g, ln_b, scale2, shift2)


def _ffn_kernel(h_ref, wg_ref, wu_ref, wd_ref, x1_ref, gate_ref, lng_ref, lnb_ref, o_ref, acc):
    f = pl.program_id(1)
    @pl.when(f == 0)
    def _():
        acc[...] = jnp.zeros_like(acc)

    h = h_ref[...]
    g = _dot(h, wg_ref[...])
    u = _dot(h, wu_ref[...])
    act = (g * _sigmoid(g) * u).astype(BF16)
    acc[...] += _dot(act, wd_ref[...])

    @pl.when(f == pl.num_programs(1) - 1)
    def _():
        z = DEEPNORM_ALPHA * x1_ref[...] + gate_ref[0] * acc[...]
        o_ref[...] = _layernorm(z) * lng_ref[...] + lnb_ref[...]


def _ffn(h2, w_gu, w_down, x1, gate2, ln_g, ln_b, *, seq, tm=512, tf=512):
    t, d = x1.shape
    d_ff = w_down.shape[0]
    nf = d_ff // tf
    tiles_per_batch = seq // tm
    mod = pl.BlockSpec((1, 1, d), lambda i, f: (i // tiles_per_batch, 0, 0))
    row = pl.BlockSpec((1, d), lambda i, f: (0, 0))
    pipelined = (_nbytes((tm, d), BF16) + 3 * _nbytes((d, tf), BF16) + 2 * _nbytes((tm, d), F32))
    return pl.pallas_call(
        _ffn_kernel,
        grid=(t // tm, nf),
        in_specs=[
            pl.BlockSpec((tm, d), lambda i, f: (i, 0)),
            pl.BlockSpec((d, tf), lambda i, f: (0, f)),
            pl.BlockSpec((d, tf), lambda i, f: (0, nf + f)),
            pl.BlockSpec((tf, d), lambda i, f: (f, 0)),
            pl.BlockSpec((tm, d), lambda i, f: (i, 0)),
            mod, row, row,
        ],
        out_specs=pl.BlockSpec((tm, d), lambda i, f: (i, 0)),
        out_shape=jax.ShapeDtypeStruct((t, d), F32),
        scratch_shapes=[pltpu.VMEM((tm, d), F32)],
        compiler_params=pltpu.CompilerParams(
            dimension_semantics=("arbitrary", "arbitrary"),
            vmem_limit_bytes=_vmem_limit(pipelined, _nbytes((tm, d), F32),
                                         2 * _nbytes((tm, d), F32) + 4 * _nbytes((tm, tf), F32)),
        ),
        name="ffn",
    )(h2, w_gu, w_gu, w_down, x1, gate2, ln_g, ln_b)


def _pad_rows(w, rows):
    return jnp.zeros((rows, w.shape[1]), w.dtype).at[:w.shape[0]].set(w)


def kernel(x, c, w_ada, b_ada, w_in, b_fgate, q_norm_g, k_norm_g, rwkv_mu, rwkv_w0, rwkv_w2, rwkv_a0, rwkv_a2, rwkv_g2, rwkv_k_k, rwkv_k_a, rwkv_r_k, rwkv_lnx_g, rwkv_lnx_b, w_branch_a, w_branch_b, w_out, ln1_g, ln1_b, w_ffn_gu, w_ffn_down, ln2_g, ln2_b):
    batch, seq, d = x.shape
    t = batch * seq
    fox_w = FOX_HEADS * FOX_HEAD_DIM
    rw_w = RWKV_HEADS * RWKV_HEAD_DIM
    fox_cols = 4 * fox_w + FOX_HEADS
    lora0 = fox_cols + 3 * rw_w
    gate0 = lora0 + DECAY_LORA + AAA_LORA + GATE_LORA
    assert seq % RWKV_CHUNK == 0 and w_in.shape[1] == gate0 + 2 * d

    zpad = lambda n: jnp.zeros((d, n), BF16)
    cols = lambda lo, hi: w_in[:, lo:hi].astype(BF16)
    w_fox = cols(0, 4 * fox_w)
    w_rw = cols(fox_cols, lora0)
    w_gate = cols(gate0, gate0 + 2 * d)
    xa0 = lora0 + DECAY_LORA
    xg0 = xa0 + AAA_LORA
    w_lora = jnp.concatenate([cols(lora0, xa0), zpad(LORA_PAD - DECAY_LORA),
                              cols(xa0, xg0), zpad(LORA_PAD - AAA_LORA), cols(xg0, gate0)], axis=1)
    wft = w_in[:, 4 * fox_w:fox_cols].T.astype(BF16)
    mu_r = rwkv_mu[None, 0:rw_w]
    mu_k = rwkv_mu[None, rw_w:2 * rw_w]
    mu_v = rwkv_mu[None, 2 * rw_w:3 * rw_w]
    mu_l = rwkv_mu[3 * rw_w:]
    zrow = lambda n: jnp.zeros((n,), rwkv_mu.dtype)
    mu_s = jnp.concatenate([
        mu_l[:DECAY_LORA], zrow(LORA_PAD - DECAY_LORA),
        mu_l[DECAY_LORA:DECAY_LORA + AAA_LORA], zrow(LORA_PAD - AAA_LORA),
        mu_l[DECAY_LORA + AAA_LORA:]])[None]
    w2p = _pad_rows(rwkv_w2, LORA_PAD).astype(BF16)
    a2p = _pad_rows(rwkv_a2, LORA_PAD).astype(BF16)
    g2 = rwkv_g2.astype(BF16)
    wa = w_branch_a.astype(BF16)
    wb = w_branch_b.astype(BF16)
    wo = w_out.astype(BF16)
    wgu = w_ffn_gu.astype(BF16)
    wdn = w_ffn_down.astype(BF16)
    row = lambda v: v.reshape(1, -1)

    c_pad = jnp.zeros((8, d), c.dtype).at[:batch].set(c)
    mod = _ada(c_pad, w_ada, row(b_ada))[:batch].reshape(batch, 6, 1, d)
    shift1, scale1, gate1, shift2, scale2, gate2 = (mod[:, i] for i in range(6))

    x2d = x.reshape(t, d)
    main, small, ft = _inproj(x2d, scale1, shift1, w_fox, w_rw, w_gate, w_lora, wft,
                              row(q_norm_g), row(k_norm_g), seq=seq)

    blocks_per_seq = seq // LANES
    ft_rows = ft.reshape(FOX_HEADS * batch * blocks_per_seq, LANES)
    bias_rows = jnp.broadcast_to(b_fgate[:, None, None], (FOX_HEADS, batch * blocks_per_seq, LANES))
    cum = _fox_prep(ft_rows, bias_rows.reshape(ft_rows.shape), blocks_per_seq=blocks_per_seq)
    cum = cum.reshape(FOX_HEADS, batch, 1, seq)
    o = _fox_attn(main, cum, batch=batch, seq=seq)

    r, k, v, lw, a, g = _rwkv_prep(main, small, mu_r, mu_k, mu_v, mu_s, row(rwkv_w0), w2p,
                                   row(rwkv_a0), a2p, g2, batch=batch, seq=seq)
    y = _rwkv_rec(r, k, v, lw, a, g, row(rwkv_k_k), row(rwkv_k_a), row(rwkv_r_k),
                  row(rwkv_lnx_g), row(rwkv_lnx_b), batch=batch, seq=seq)

    merged = _merge(o, y, main, wa, wb)
    x1, h2 = _outproj(merged, wo, x2d, gate1, row(ln1_g), row(ln1_b), scale2, shift2, seq=seq)

    out = _ffn(h2, wgu, wdn, x1, gate2, row(ln2_g), row(ln2_b), seq=seq)
    return out.reshape(batch, seq, d)
```

```python
import functools

import jax
import jax.numpy as jnp
from jax import lax
from jax.experimental import pallas as pl
from jax.experimental.pallas import tpu as pltpu

F32 = jnp.float32
BF16 = jnp.bfloat16

LANES = 128
VMEM_CAP_BYTES = 60000 * 1024

FOX_HEADS = 8
FOX_HEAD_DIM = 128
RWKV_HEADS = 16
RWKV_HEAD_DIM = 64
DECAY_LORA = 96
AAA_LORA = 96
GATE_LORA = 256
LORA_PAD = 128
INPROJ_TN = 512
RWKV_LNX_EPS = 64e-5
DEPTH = 1
DEEPNORM_ALPHA = (2.0 * DEPTH) ** 0.25
LN_EPS = 1e-5
RMS_EPS = 1e-6
RWKV_CHUNK = 64
HEADS_PER_TILE = LANES // RWKV_HEAD_DIM


def _vmem_limit(pipelined_bytes, resident_bytes=0, temp_bytes=0):
    need = 2 * pipelined_bytes + resident_bytes + temp_bytes + (2 << 20)
    return int(min(VMEM_CAP_BYTES, need))


def _nbytes(shape, dtype):
    n = 1
    for s in shape:
        n *= s
    return n * jnp.dtype(dtype).itemsize


def _layernorm(x):
    mu = jnp.mean(x, axis=-1, keepdims=True)
    xc = x - mu
    var = jnp.mean(xc * xc, axis=-1, keepdims=True)
    return xc * lax.rsqrt(var + LN_EPS)


def _sigmoid(x):
    return 1.0 / (1.0 + jnp.exp(-x))


def _dot(a, b):
    return jnp.dot(a, b, preferred_element_type=F32)


def _dot_nt(a, b):
    return lax.dot_general(a, b, (((1,), (1,)), ((), ())), preferred_element_type=F32)


def _dot_tn(a, b):
    return lax.dot_general(a, b, (((0,), (0,)), ((), ())), preferred_element_type=F32)


def _split3(x):
    hi = x.astype(BF16)
    r1 = x - hi.astype(F32)
    mid = r1.astype(BF16)
    lo = (r1 - mid.astype(F32)).astype(BF16)
    return hi, mid, lo


def _ada_kernel(c_ref, w_ref, b_ref, o_ref):
    c = c_ref[...]
    s = (c * _sigmoid(c)).astype(BF16)
    o_ref[...] = _dot(s, w_ref[...].astype(BF16)) + b_ref[...]


def _ada(c_pad, w_ada, b_ada, *, tn=1024):
    rows, d = c_pad.shape
    n = w_ada.shape[1]
    return pl.pallas_call(
        _ada_kernel,
        grid=(n // tn,),
        in_specs=[
            pl.BlockSpec((rows, d), lambda j: (0, 0)),
            pl.BlockSpec((d, tn), lambda j: (0, j)),
            pl.BlockSpec((1, tn), lambda j: (0, j)),
        ],
        out_specs=pl.BlockSpec((rows, tn), lambda j: (0, j)),
        out_shape=jax.ShapeDtypeStruct((rows, n), F32),
        compiler_params=pltpu.CompilerParams(
            dimension_semantics=("arbitrary",),
            vmem_limit_bytes=_vmem_limit(_nbytes((d, tn), F32), temp_bytes=_nbytes((d, tn), BF16)),
        ),
        name="ada",
    )(c_pad, w_ada, b_ada)


def _inproj_kernel(x_ref, scale_ref, shift_ref, wt_ref, wft_ref, qg_ref, kg_ref,
                   main_ref, small_ref, ft_ref, h_scr, *, n_q, n_k, n_main):
    j = pl.program_id(1)
    tn = main_ref.shape[1]

    @pl.when(j == 0)
    def _():
        h = _layernorm(x_ref[...]) * (1.0 + scale_ref[0]) + shift_ref[0]
        h_scr[...] = h.astype(BF16)

    def proj():
        return _dot_nt(h_scr[...], wt_ref[...].astype(BF16))

    @pl.when(j < n_q + n_k)
    def _():
        acc = proj()
        gain = jnp.where(j < n_q, qg_ref[...] * (FOX_HEAD_DIM ** -0.5), kg_ref[...])
        for hh in range(tn // FOX_HEAD_DIM):
            sl = slice(hh * FOX_HEAD_DIM, (hh + 1) * FOX_HEAD_DIM)
            a = acc[:, sl]
            ms = jnp.mean(a * a, axis=-1, keepdims=True)
            main_ref[:, sl] = (a * lax.rsqrt(ms + RMS_EPS) * gain).astype(BF16)

    @pl.when((j >= n_q + n_k) & (j < n_main))
    def _():
        main_ref[...] = proj().astype(BF16)

    @pl.when(j == n_main)
    def _():
        small_ref[...] = proj()
        ft_ref[...] = _dot_nt(wft_ref[...].astype(BF16), h_scr[...])


def _inproj(x2d, scale1, shift1, w_in_t, qg, kg, *, seq, group_rows, lora_row, fgate_row, tm=1024, tn=512):
    t, d = x2d.shape
    fox_w = FOX_HEADS * FOX_HEAD_DIM
    assert all(first % 8 == 0 and count % tn == 0 for first, count in group_rows)
    assert lora_row % 8 == 0 and fgate_row % 8 == 0 and lora_row + tn <= w_in_t.shape[0]
    n_main = sum(count // tn for _, count in group_rows)
    tiles_per_batch = seq // tm
    kern = functools.partial(_inproj_kernel, n_q=fox_w // tn, n_k=fox_w // tn, n_main=n_main)
    pipelined = (_nbytes((tm, d), F32) + _nbytes((tn, d), F32) + _nbytes((tm, tn), BF16)
                 + _nbytes((tm, tn), F32) + _nbytes((8, tm), F32))

    def w_rows(i, j):
        row, hi = jnp.int32(lora_row), n_main
        for first, count in reversed(group_rows):
            lo = hi - count // tn
            row = jnp.where(j < hi, first + tn * (j - lo), row)
            hi = lo
        return pl.multiple_of(row, 8), 0

    return pl.pallas_call(
        kern,
        grid=(t // tm, n_main + 1),
        in_specs=[
            pl.BlockSpec((tm, d), lambda i, j: (i, 0)),
            pl.BlockSpec((1, 1, d), lambda i, j: (i // tiles_per_batch, 0, 0)),
            pl.BlockSpec((1, 1, d), lambda i, j: (i // tiles_per_batch, 0, 0)),
            pl.BlockSpec((pl.Element(tn), pl.Element(d)), w_rows),
            pl.BlockSpec((pl.Element(FOX_HEADS), pl.Element(d)), lambda i, j: (fgate_row, 0)),
            pl.BlockSpec((1, FOX_HEAD_DIM), lambda i, j: (0, 0)),
            pl.BlockSpec((1, FOX_HEAD_DIM), lambda i, j: (0, 0)),
        ],
        out_specs=[
            pl.BlockSpec((tm, tn), lambda i, j: (i, jnp.minimum(j, n_main - 1))),
            pl.BlockSpec((tm, tn), lambda i, j: (i, 0)),
            pl.BlockSpec((FOX_HEADS, tm), lambda i, j: (0, i)),
        ],
        out_shape=[
            jax.ShapeDtypeStruct((t, n_main * tn), BF16),
            jax.ShapeDtypeStruct((t, tn), F32),
            jax.ShapeDtypeStruct((FOX_HEADS, t), F32),
        ],
        scratch_shapes=[pltpu.VMEM((tm, d), BF16)],
        compiler_params=pltpu.CompilerParams(
            dimension_semantics=("arbitrary", "arbitrary"),
            vmem_limit_bytes=_vmem_limit(pipelined, _nbytes((tm, d), BF16),
                                         3 * _nbytes((tm, d), F32)),
        ),
        name="inproj",
    )(x2d, scale1, shift1, w_in_t, w_in_t, qg, kg)


def _fox_prep_kernel(ft_ref, bias_ref, o_ref, *, blocks_per_seq):
    x = ft_ref[...] + bias_ref[...]
    lf = jnp.minimum(x, 0.0) - jnp.log1p(jnp.exp(-jnp.abs(x)))
    rows, width = lf.shape
    r = lax.broadcasted_iota(jnp.int32, (width, width), 0)
    c = lax.broadcasted_iota(jnp.int32, (width, width), 1)
    upper = (r <= c).astype(BF16)
    within = sum(_dot(p, upper) for p in _split3(lf))
    tot = jnp.broadcast_to(within[:, width - 1:width], (rows, width))
    rr = lax.broadcasted_iota(jnp.int32, (rows, rows), 0)
    cc = lax.broadcasted_iota(jnp.int32, (rows, rows), 1)
    prior = ((rr // blocks_per_seq == cc // blocks_per_seq) & (cc < rr)).astype(BF16)
    offset = sum(_dot(prior, p) for p in _split3(tot))
    o_ref[...] = within + offset


def _fox_prep(ft_rows, bias_rows, *, blocks_per_seq):
    return pl.pallas_call(
        functools.partial(_fox_prep_kernel, blocks_per_seq=blocks_per_seq),
        out_shape=jax.ShapeDtypeStruct(ft_rows.shape, F32),
        name="fox_prep",
    )(ft_rows, bias_rows)


def _fox_attn_kernel(q_ref, k_ref, v_ref, og_ref, cum_ref, o_ref, *, tq):
    seq = q_ref.shape[0]
    negcum = -cum_ref[0, 0]
    for qi in range(seq // tq):
        q0, kend = qi * tq, (qi + 1) * tq
        s = _dot_nt(q_ref[q0:kend, :], k_ref[0:kend, :]) + negcum[:, 0:kend]
        row = q0 + lax.broadcasted_iota(jnp.int32, (tq, kend), 0)
        col = lax.broadcasted_iota(jnp.int32, (tq, kend), 1)
        s = jnp.where(col <= row, s, -jnp.inf)
        m = jnp.max(s, axis=-1, keepdims=True)
        p = jnp.exp(s - m)
        l = jnp.sum(p, axis=-1, keepdims=True)
        o = _dot(p.astype(BF16), v_ref[0:kend, :]) / l
        gate = _sigmoid(og_ref[q0:kend, :].astype(F32))
        o_ref[q0:kend, :] = (o * gate).astype(BF16)


def _fox_attn(main, cum, *, batch, seq, tq=256):
    t = main.shape[0]
    hd = FOX_HEAD_DIM
    nh = FOX_HEADS
    blk = lambda off: pl.BlockSpec((seq, hd), lambda b, h: (b, off + h))
    pipelined = 5 * _nbytes((seq, hd), BF16) + _nbytes((8, seq), F32)
    return pl.pallas_call(
        functools.partial(_fox_attn_kernel, tq=tq),
        grid=(batch, nh),
        in_specs=[blk(0), blk(nh), blk(2 * nh), blk(3 * nh),
                  pl.BlockSpec((1, 1, 1, seq), lambda b, h: (h, b, 0, 0))],
        out_specs=pl.BlockSpec((seq, hd), lambda b, h: (b, h)),
        out_shape=jax.ShapeDtypeStruct((t, nh * hd), BF16),
        compiler_params=pltpu.CompilerParams(
            dimension_semantics=("arbitrary", "arbitrary"),
            vmem_limit_bytes=_vmem_limit(pipelined, 0, 6 * _nbytes((tq, seq), F32)),
        ),
        name="fox_attn",
    )(main, main, main, main, cum)


def _rwkv_prep_kernel(r_ref, k_ref, v_ref, small_ref, mu_r_ref, mu_k_ref, mu_v_ref, mu_s_ref,
                      w0_ref, w2_ref, a0_ref, a2_ref, g2_ref,
                      r_o, k_o, v_o, lw_o, a_o, g_o, prev_r, prev_k, prev_v, prev_s):
    prevs = (prev_r, prev_k, prev_v, prev_s)

    @pl.when(pl.program_id(1) == 0)
    def _():
        for p in prevs:
            p[...] = jnp.zeros_like(p)

    def shift_mix(x, prev_ref, mu):
        tm = x.shape[0]
        xp = pltpu.roll(x, 1, axis=0)
        row = lax.broadcasted_iota(jnp.int32, x.shape, 0)
        xp = jnp.where(row == 0, prev_ref[0:1, :], xp)
        prev_ref[0:1, :] = x[tm - 1:tm, :]
        return x + (xp - x) * mu

    r = shift_mix(r_ref[...].astype(F32), prev_r, mu_r_ref[...])
    k = shift_mix(k_ref[...].astype(F32), prev_k, mu_k_ref[...])
    v = shift_mix(v_ref[...].astype(F32), prev_v, mu_v_ref[...])
    sm = shift_mix(small_ref[...], prev_s, mu_s_ref[...])
    xw = sm[:, 0:LORA_PAD]
    xa = sm[:, DECAY_LORA:DECAY_LORA + LORA_PAD]
    xg = sm[:, DECAY_LORA + AAA_LORA:DECAY_LORA + AAA_LORA + GATE_LORA]

    z = w0_ref[...] + _dot(jnp.tanh(xw).astype(BF16), w2_ref[...])
    w_raw = -(jnp.maximum(-z, 0.0) + jnp.log1p(jnp.exp(-jnp.abs(z)))) - 0.5
    lw_o[...] = -jnp.exp(w_raw)
    a_o[...] = _sigmoid(a0_ref[...] + _dot(xa.astype(BF16), a2_ref[...])).astype(BF16)
    g_o[...] = _dot(_sigmoid(xg).astype(BF16), g2_ref[...]).astype(BF16)
    r_o[...] = r.astype(BF16)
    k_o[...] = k.astype(BF16)
    v_o[...] = v.astype(BF16)


def _rwkv_prep(main, small, mu_r, mu_k, mu_v, mu_s, w0, w2p, a0, a2p, g2, *, batch, seq, tm=256):
    t = main.shape[0]
    width = RWKV_HEADS * RWKV_HEAD_DIM
    nsm = small.shape[1]
    col0 = (4 * FOX_HEADS * FOX_HEAD_DIM) // width
    steps = seq // tm
    act = lambda off: pl.BlockSpec((tm, width), lambda b, s: (b * steps + s, off))
    row_spec = lambda n: pl.BlockSpec((1, n), lambda b, s: (0, 0))
    full = lambda a: pl.BlockSpec(a.shape, lambda b, s: (0, 0))
    out_spec = pl.BlockSpec((tm, width), lambda b, s: (b * steps + s, 0))
    out_dtypes = [BF16, BF16, BF16, F32, BF16, BF16]
    pipelined = (3 * _nbytes((tm, width), BF16) + _nbytes((tm, nsm), F32)
                 + sum(_nbytes((tm, width), dt) for dt in out_dtypes)
                 + _nbytes(w2p.shape, BF16) + _nbytes(a2p.shape, BF16) + _nbytes(g2.shape, BF16))
    return pl.pallas_call(
        _rwkv_prep_kernel,
        grid=(batch, steps),
        in_specs=[act(col0), act(col0 + 1), act(col0 + 2),
                  pl.BlockSpec((tm, nsm), lambda b, s: (b * steps + s, 0)),
                  row_spec(width), row_spec(width), row_spec(width), row_spec(nsm),
                  row_spec(width), full(w2p), row_spec(width), full(a2p), full(g2)],
        out_specs=[out_spec] * 6,
        out_shape=[jax.ShapeDtypeStruct((t, width), dt) for dt in out_dtypes],
        scratch_shapes=[pltpu.VMEM((8, width), F32)] * 3 + [pltpu.VMEM((8, nsm), F32)],
        compiler_params=pltpu.CompilerParams(
            dimension_semantics=("arbitrary", "arbitrary"),
            vmem_limit_bytes=_vmem_limit(pipelined, 0, 12 * _nbytes((tm, width), F32)),
        ),
        name="rwkv_prep",
    )(main, main, main, small, mu_r, mu_k, mu_v, mu_s, w0, w2p, a0, a2p, g2)


def _rwkv_rec_kernel(r_ref, k_ref, v_ref, lw_ref, a_ref, g_ref,
                     kk_ref, ka_ref, rk_ref, lng_ref, lnb_ref, y_ref, state, *, n_tiles):
    c = pl.program_id(1)

    @pl.when(c == 0)
    def _():
        state[...] = jnp.zeros_like(state)

    ch = r_ref.shape[0]
    hd = RWKV_HEAD_DIM
    lane = lax.broadcasted_iota(jnp.int32, (1, LANES), 1)
    m0 = lane < hd
    n2 = HEADS_PER_TILE * ch
    rr = lax.broadcasted_iota(jnp.int32, (n2, n2), 0)
    cc = lax.broadcasted_iota(jnp.int32, (n2, n2), 1)
    same = (rr // ch) == (cc // ch)
    strict = same & (rr > cc)
    incl = same & (rr >= cc)
    lr = lax.broadcasted_iota(jnp.int32, (ch, ch), 0)
    lc = lax.broadcasted_iota(jnp.int32, (ch, ch), 1)
    lower = (lc <= lr).astype(BF16)

    def head_sum(x):
        s0 = jnp.sum(jnp.where(m0, x, 0.0), axis=-1, keepdims=True)
        s1 = jnp.sum(jnp.where(m0, 0.0, x), axis=-1, keepdims=True)
        return jnp.where(m0, s0, s1)

    def stack(x):
        return jnp.concatenate([jnp.where(m0, x, 0.0), jnp.where(m0, 0.0, x)], axis=0)

    tiles = range(n_tiles)
    sls = [slice(p * LANES, (p + 1) * LANES) for p in tiles]

    def each(fn, *cols):
        return [fn(*args) for args in zip(*cols)]

    def load(ref):
        return [ref[:, sl].astype(F32) for sl in sls]

    r, k, v, lw, a, g = (load(ref) for ref in (r_ref, k_ref, v_ref, lw_ref, a_ref, g_ref))
    kkw, kaw, rkw, lng, lnb = (load(ref) for ref in (kk_ref, ka_ref, rk_ref, lng_ref, lnb_ref))

    kk = each(lambda k_, w_: k_ * w_, k, kkw)
    kk = each(lambda x: x / jnp.maximum(jnp.sqrt(head_sum(x * x)), 1e-12), kk)
    kr = each(lambda k_, a_, w_: k_ * (1.0 + (a_ - 1.0) * w_), k, a, kaw)
    cum = each(lambda x: sum(_dot(lower, part) for part in _split3(x)), lw)
    gam = each(jnp.exp, cum)
    ginv = each(lambda x: jnp.exp(-x), cum)
    g_end = each(lambda x: x[ch - 1:ch, :], gam)
    a_t = each(lambda kk_, c_, lw_: -kk_ * jnp.exp(c_ - lw_), kk, cum, lw)
    r_t = each(lambda r_, g_: r_ * g_, r, gam)
    b_h = each(lambda kk_, a_, gi_: kk_ * a_ * gi_, kk, a, ginv)
    k_h = each(lambda kr_, gi_: kr_ * gi_, kr, ginv)

    stack2 = lambda x, y: jnp.concatenate([stack(x), stack(y)], axis=0).astype(BF16)
    lhs = each(stack2, a_t, r_t)
    rhs = each(stack2, b_h, k_h)
    sc = each(_dot_nt, lhs, rhs)
    ab = each(lambda s: jnp.where(strict, s[0:n2, 0:n2], 0.0).astype(BF16), sc)
    ak = each(lambda s: jnp.where(strict, s[0:n2, n2:2 * n2], 0.0).astype(BF16), sc)
    rbk = each(lambda s: jnp.where(jnp.concatenate([incl, incl], axis=1), s[n2:2 * n2, :], 0.0).astype(BF16), sc)

    st = [state[p] for p in tiles]
    ah = each(lambda l_, s_: _dot_nt(l_, s_.astype(BF16)), lhs, st)
    vs = each(lambda v_: stack(v_).astype(BF16), v)
    x = each(lambda ah_, ak_, vs_: ah_[0:n2] + _dot(ak_, vs_), ah, ak, vs)
    pw = ab
    n_lvl = ch.bit_length() - 1
    for lvl in range(n_lvl):
        x = each(lambda x_, p_: x_ + _dot(p_, x_.astype(BF16)), x, pw)
        if lvl + 1 < n_lvl:
            pw = each(lambda p_: _dot(p_, p_).astype(BF16), pw)
    uv = each(lambda x_, vs_: jnp.concatenate([x_.astype(BF16), vs_], axis=0), x, vs)
    ys = each(lambda ah_, m_, uv_: ah_[n2:2 * n2] + _dot(m_, uv_), ah, rbk, uv)
    y = each(lambda ys_: ys_[0:ch] + ys_[ch:n2], ys)

    bk = each(lambda b_, k_, ge_: stack2(b_ * ge_, k_ * ge_), b_h, k_h, g_end)
    new_st = each(lambda s_, ge_, uv_, bk_: s_ * ge_ + _dot_tn(uv_, bk_), st, g_end, uv, bk)
    for p in tiles:
        state[p] = new_st[p]

    mu = each(lambda y_: head_sum(y_) * (1.0 / hd), y)
    yc = each(lambda y_, m_: y_ - m_, y, mu)
    var = each(lambda c_: head_sum(c_ * c_) * (1.0 / hd), yc)
    yn = each(lambda c_, v_, g_, b_: c_ * lax.rsqrt(v_ + RWKV_LNX_EPS) * g_ + b_, yc, var, lng, lnb)
    bonus = each(lambda r_, kr_, w_, v_: head_sum(r_ * kr_ * w_) * v_, r, kr, rkw, v)
    for p in tiles:
        y_ref[:, sls[p]] = ((yn[p] + bonus[p]) * g[p]).astype(BF16)


def _rwkv_rec(r, k, v, lw, a, g, kk, ka, rk, lng, lnb, *, batch, seq):
    t, width = r.shape
    ch = RWKV_CHUNK
    steps = seq // ch
    n_tiles = width // LANES
    act = pl.BlockSpec((ch, width), lambda b, c: (b * steps + c, 0))
    row = pl.BlockSpec((1, width), lambda b, c: (0, 0))
    pipelined = sum(_nbytes((ch, width), x.dtype) for x in (r, k, v, lw, a, g)) + _nbytes((ch, width), BF16)
    return pl.pallas_call(
        functools.partial(_rwkv_rec_kernel, n_tiles=n_tiles),
        grid=(batch, steps),
        in_specs=[act] * 6 + [row] * 5,
        out_specs=act,
        out_shape=jax.ShapeDtypeStruct((t, width), BF16),
        scratch_shapes=[pltpu.VMEM((n_tiles, LANES, LANES), F32)],
        compiler_params=pltpu.CompilerParams(
            dimension_semantics=("arbitrary", "arbitrary"),
            vmem_limit_bytes=_vmem_limit(pipelined, _nbytes((n_tiles, LANES, LANES), F32), 16 << 20),
        ),
        name="rwkv_rec",
    )(r, k, v, lw, a, g, kk, ka, rk, lng, lnb)


def _merge_kernel(o_ref, y_ref, ga_ref, gb_ref, wa_ref, wb_ref, m_ref):
    ua = _dot(o_ref[...], wa_ref[...])
    ub = _dot(y_ref[...], wb_ref[...])
    m = _sigmoid(ga_ref[...].astype(F32)) * ua + _sigmoid(gb_ref[...].astype(F32)) * ub
    m_ref[...] = m.astype(BF16)


def _merge(o, y, main, wa, wb, *, tm=1024, tn=512):
    t, kdim = o.shape
    d = wa.shape[1]
    gate_col0 = (main.shape[1] - 2 * d) // tn
    nj = d // tn
    pipelined = (2 * _nbytes((tm, kdim), BF16) + 2 * _nbytes((kdim, tn), BF16)
                 + 3 * _nbytes((tm, tn), BF16))
    return pl.pallas_call(
        _merge_kernel,
        grid=(t // tm, nj),
        in_specs=[
            pl.BlockSpec((tm, kdim), lambda i, j: (i, 0)),
            pl.BlockSpec((tm, kdim), lambda i, j: (i, 0)),
            pl.BlockSpec((tm, tn), lambda i, j: (i, gate_col0 + j)),
            pl.BlockSpec((tm, tn), lambda i, j: (i, gate_col0 + nj + j)),
            pl.BlockSpec((kdim, tn), lambda i, j: (0, j)),
            pl.BlockSpec((kdim, tn), lambda i, j: (0, j)),
        ],
        out_specs=pl.BlockSpec((tm, tn), lambda i, j: (i, j)),
        out_shape=jax.ShapeDtypeStruct((t, d), BF16),
        compiler_params=pltpu.CompilerParams(
            dimension_semantics=("arbitrary", "arbitrary"),
            vmem_limit_bytes=_vmem_limit(pipelined, 0, 6 * _nbytes((tm, tn), F32)),
        ),
        name="merge",
    )(o, y, main, main, wa, wb)


def _outproj_kernel(m_ref, w_ref, x_ref, gate_ref, lng_ref, lnb_ref, scale_ref, shift_ref,
                    x1_ref, h2_ref, acc):
    j = pl.program_id(1)
    tn = w_ref.shape[1]
    acc[:, pl.ds(pl.multiple_of(j * tn, tn), tn)] = _dot(m_ref[...], w_ref[...])

    @pl.when(j == pl.num_programs(1) - 1)
    def _():
        z = DEEPNORM_ALPHA * x_ref[...] + gate_ref[0] * acc[...]
        x1 = _layernorm(z) * lng_ref[...] + lnb_ref[...]
        x1_ref[...] = x1
        h2_ref[...] = (_layernorm(x1) * (1.0 + scale_ref[0]) + shift_ref[0]).astype(BF16)


def _outproj(m, w_out, x2d, gate1, ln_g, ln_b, scale2, shift2, *, seq, tm=512, tn=512):
    t, d = x2d.shape
    tiles_per_batch = seq // tm
    mod = pl.BlockSpec((1, 1, d), lambda i, j: (i // tiles_per_batch, 0, 0))
    row = pl.BlockSpec((1, d), lambda i, j: (0, 0))
    pipelined = (_nbytes((tm, d), BF16) + _nbytes((d, tn), BF16) + 2 * _nbytes((tm, d), F32)
                 + _nbytes((tm, d), BF16))
    return pl.pallas_call(
        _outproj_kernel,
        grid=(t // tm, d // tn),
        in_specs=[
            pl.BlockSpec((tm, d), lambda i, j: (i, 0)),
            pl.BlockSpec((d, tn), lambda i, j: (0, j)),
            pl.BlockSpec((tm, d), lambda i, j: (i, 0)),
            mod, row, row, mod, mod,
        ],
        out_specs=[pl.BlockSpec((tm, d), lambda i, j: (i, 0)),
                   pl.BlockSpec((tm, d), lambda i, j: (i, 0))],
        out_shape=[jax.ShapeDtypeStruct((t, d), F32), jax.ShapeDtypeStruct((t, d), BF16)],
        scratch_shapes=[pltpu.VMEM((tm, d), F32)],
        compiler_params=pltpu.CompilerParams(
            dimension_semantics=("arbitrary", "arbitrary"),
            vmem_limit_bytes=_vmem_limit(pipelined, _nbytes((tm, d), F32), 3 * _nbytes((tm, d), F32)),
        ),
        name="outproj",
    )(m, w_out, x2d, gate1, ln_g, ln_b, scale2, shift2)


def _ffn_kernel(h_ref, wg_ref, wu_ref, wd_ref, x1_ref, gate_ref, lng_ref, lnb_ref, o_ref, acc):
    f = pl.program_id(1)
    @pl.when(f == 0)
    def _():
        acc[...] = jnp.zeros_like(acc)

    h = h_ref[...]
    g = _dot(h, wg_ref[...])
    u = _dot(h, wu_ref[...])
    act = (g * _sigmoid(g) * u).astype(BF16)
    acc[...] += _dot(act, wd_ref[...])

    @pl.when(f == pl.num_programs(1) - 1)
    def _():
        z = DEEPNORM_ALPHA * x1_ref[...] + gate_ref[0] * acc[...]
        o_ref[...] = _layernorm(z) * lng_ref[...] + lnb_ref[...]


def _ffn(h2, w_gu, w_down, x1, gate2, ln_g, ln_b, *, seq, tm=512, tf=512):
    t, d = x1.shape
    d_ff = w_down.shape[0]
    nf = d_ff // tf
    tiles_per_batch = seq // tm
    mod = pl.BlockSpec((1, 1, d), lambda i, f: (i // tiles_per_batch, 0, 0))
    row = pl.BlockSpec((1, d), lambda i, f: (0, 0))
    pipelined = (_nbytes((tm, d), BF16) + 3 * _nbytes((d, tf), BF16) + 2 * _nbytes((tm, d), F32))
    return pl.pallas_call(
        _ffn_kernel,
        grid=(t // tm, nf),
        in_specs=[
            pl.BlockSpec((tm, d), lambda i, f: (i, 0)),
            pl.BlockSpec((d, tf), lambda i, f: (0, f)),
            pl.BlockSpec((d, tf), lambda i, f: (0, nf + f)),
            pl.BlockSpec((tf, d), lambda i, f: (f, 0)),
            pl.BlockSpec((tm, d), lambda i, f: (i, 0)),
            mod, row, row,
        ],
        out_specs=pl.BlockSpec((tm, d), lambda i, f: (i, 0)),
        out_shape=jax.ShapeDtypeStruct((t, d), F32),
        scratch_shapes=[pltpu.VMEM((tm, d), F32)],
        compiler_params=pltpu.CompilerParams(
            dimension_semantics=("arbitrary", "arbitrary"),
            vmem_limit_bytes=_vmem_limit(pipelined, _nbytes((tm, d), F32),
                                         2 * _nbytes((tm, d), F32) + 4 * _nbytes((tm, tf), F32)),
        ),
        name="ffn",
    )(h2, w_gu, w_gu, w_down, x1, gate2, ln_g, ln_b)


def _pad_rows(w, rows):
    return jnp.zeros((rows, w.shape[1]), w.dtype).at[:w.shape[0]].set(w)


def kernel(x, c, w_ada, b_ada, w_in, b_fgate, q_norm_g, k_norm_g, rwkv_mu, rwkv_w0, rwkv_w2, rwkv_a0, rwkv_a2, rwkv_g2, rwkv_k_k, rwkv_k_a, rwkv_r_k, rwkv_lnx_g, rwkv_lnx_b, w_branch_a, w_branch_b, w_out, ln1_g, ln1_b, w_ffn_gu, w_ffn_down, ln2_g, ln2_b):
    batch, seq, d = x.shape
    t = batch * seq
    fox_w = FOX_HEADS * FOX_HEAD_DIM
    rw_w = RWKV_HEADS * RWKV_HEAD_DIM
    fox_cols = 4 * fox_w + FOX_HEADS
    lora0 = fox_cols + 3 * rw_w
    gate0 = lora0 + DECAY_LORA + AAA_LORA + GATE_LORA
    assert seq % RWKV_CHUNK == 0 and w_in.shape[1] == gate0 + 2 * d

    lora_w = gate0 - lora0
    mu_r = rwkv_mu[None, 0:rw_w]
    mu_k = rwkv_mu[None, rw_w:2 * rw_w]
    mu_v = rwkv_mu[None, 2 * rw_w:3 * rw_w]
    mu_s = jnp.zeros((1, INPROJ_TN), rwkv_mu.dtype).at[0, :lora_w].set(rwkv_mu[3 * rw_w:])
    w2p = _pad_rows(rwkv_w2, LORA_PAD).astype(BF16)
    a2p = _pad_rows(rwkv_a2, LORA_PAD).astype(BF16)
    g2 = rwkv_g2.astype(BF16)
    wa = w_branch_a.astype(BF16)
    wb = w_branch_b.astype(BF16)
    wo = w_out.astype(BF16)
    wgu = w_ffn_gu.astype(BF16)
    wdn = w_ffn_down.astype(BF16)
    row = lambda v: v.reshape(1, -1)

    c_pad = jnp.zeros((8, d), c.dtype).at[:batch].set(c)
    mod = _ada(c_pad, w_ada, row(b_ada))[:batch].reshape(batch, 6, 1, d)
    shift1, scale1, gate1, shift2, scale2, gate2 = (mod[:, i] for i in range(6))

    x2d = x.reshape(t, d)
    main, small, ft = _inproj(x2d, scale1, shift1, w_in.T, row(q_norm_g), row(k_norm_g), seq=seq,
                              group_rows=[(0, 4 * fox_w), (fox_cols, 3 * rw_w), (gate0, 2 * d)],
                              lora_row=lora0, fgate_row=4 * fox_w, tn=INPROJ_TN)

    blocks_per_seq = seq // LANES
    ft_rows = ft.reshape(FOX_HEADS * batch * blocks_per_seq, LANES)
    bias_rows = jnp.broadcast_to(b_fgate[:, None, None], (FOX_HEADS, batch * blocks_per_seq, LANES))
    cum = _fox_prep(ft_rows, bias_rows.reshape(ft_rows.shape), blocks_per_seq=blocks_per_seq)
    cum = cum.reshape(FOX_HEADS, batch, 1, seq)
    o = _fox_attn(main, cum, batch=batch, seq=seq)

    r, k, v, lw, a, g = _rwkv_prep(main, small, mu_r, mu_k, mu_v, mu_s, row(rwkv_w0), w2p,
                                   row(rwkv_a0), a2p, g2, batch=batch, seq=seq)
    y = _rwkv_rec(r, k, v, lw, a, g, row(rwkv_k_k), row(rwkv_k_a), row(rwkv_r_k),
                  row(rwkv_lnx_g), row(rwkv_lnx_b), batch=batch, seq=seq)

    merged = _merge(o, y, main, wa, wb)
    x1, h2 = _outproj(merged, wo, x2d, gate1, row(ln1_g), row(ln1_b), scale2, shift2, seq=seq)

    out = _ffn(h2, wgu, wdn, x1, gate2, row(ln2_g), row(ln2_b), seq=seq)
    return out.reshape(batch, seq, d)
```

```python
import functools

import jax
import jax.numpy as jnp
from jax import lax
from jax.experimental import pallas as pl
from jax.experimental.pallas import tpu as pltpu

F32 = jnp.float32
BF16 = jnp.bfloat16

LANES = 128
VMEM_CAP_BYTES = 60000 * 1024

FOX_HEADS = 8
FOX_HEAD_DIM = 128
RWKV_HEADS = 16
RWKV_HEAD_DIM = 64
DECAY_LORA = 96
AAA_LORA = 96
GATE_LORA = 256
LORA_PAD = 128
INPROJ_TN = 512
RWKV_LNX_EPS = 64e-5
DEPTH = 1
DEEPNORM_ALPHA = (2.0 * DEPTH) ** 0.25
LN_EPS = 1e-5
RMS_EPS = 1e-6
RWKV_CHUNK = 64
HEADS_PER_TILE = LANES // RWKV_HEAD_DIM


def _vmem_limit(pipelined_bytes, resident_bytes=0, temp_bytes=0):
    need = 2 * pipelined_bytes + resident_bytes + temp_bytes + (2 << 20)
    return int(min(VMEM_CAP_BYTES, need))


def _nbytes(shape, dtype):
    n = 1
    for s in shape:
        n *= s
    return n * jnp.dtype(dtype).itemsize


def _layernorm(x):
    mu = jnp.mean(x, axis=-1, keepdims=True)
    xc = x - mu
    var = jnp.mean(xc * xc, axis=-1, keepdims=True)
    return xc * lax.rsqrt(var + LN_EPS)


def _sigmoid(x):
    return 1.0 / (1.0 + jnp.exp(-x))


def _dot(a, b):
    return jnp.dot(a, b, preferred_element_type=F32)


def _dot_nt(a, b):
    return lax.dot_general(a, b, (((1,), (1,)), ((), ())), preferred_element_type=F32)


def _dot_tn(a, b):
    return lax.dot_general(a, b, (((0,), (0,)), ((), ())), preferred_element_type=F32)


def _split3(x):
    hi = x.astype(BF16)
    r1 = x - hi.astype(F32)
    mid = r1.astype(BF16)
    lo = (r1 - mid.astype(F32)).astype(BF16)
    return hi, mid, lo


def _ada_kernel(c_ref, w_ref, b_ref, o_ref):
    c = c_ref[...]
    s = (c * _sigmoid(c)).astype(BF16)
    o_ref[...] = _dot(s, w_ref[...].astype(BF16)) + b_ref[...]


def _ada(c_pad, w_ada, b_ada, *, tn=1024):
    rows, d = c_pad.shape
    n = w_ada.shape[1]
    return pl.pallas_call(
        _ada_kernel,
        grid=(n // tn,),
        in_specs=[
            pl.BlockSpec((rows, d), lambda j: (0, 0)),
            pl.BlockSpec((d, tn), lambda j: (0, j)),
            pl.BlockSpec((1, tn), lambda j: (0, j)),
        ],
        out_specs=pl.BlockSpec((rows, tn), lambda j: (0, j)),
        out_shape=jax.ShapeDtypeStruct((rows, n), F32),
        compiler_params=pltpu.CompilerParams(
            dimension_semantics=("arbitrary",),
            vmem_limit_bytes=_vmem_limit(_nbytes((d, tn), F32), temp_bytes=_nbytes((d, tn), BF16)),
        ),
        name="ada",
    )(c_pad, w_ada, b_ada)


def _inproj_kernel(x_ref, scale_ref, shift_ref, wt_ref, wft_ref, qg_ref, kg_ref,
                   main_ref, small_ref, ft_ref, h_scr, *, n_q, n_k, n_main):
    j = pl.program_id(1)
    tn = main_ref.shape[1]

    @pl.when(j == 0)
    def _():
        h = _layernorm(x_ref[...]) * (1.0 + scale_ref[0]) + shift_ref[0]
        h_scr[...] = h.astype(BF16)

    def proj():
        return _dot_nt(h_scr[...], wt_ref[...].astype(BF16))

    @pl.when(j < n_q + n_k)
    def _():
        acc = proj()
        gain = jnp.where(j < n_q, qg_ref[...] * (FOX_HEAD_DIM ** -0.5), kg_ref[...])
        for hh in range(tn // FOX_HEAD_DIM):
            sl = slice(hh * FOX_HEAD_DIM, (hh + 1) * FOX_HEAD_DIM)
            a = acc[:, sl]
            ms = jnp.mean(a * a, axis=-1, keepdims=True)
            main_ref[:, sl] = (a * lax.rsqrt(ms + RMS_EPS) * gain).astype(BF16)

    @pl.when((j >= n_q + n_k) & (j < n_main))
    def _():
        main_ref[...] = proj().astype(BF16)

    @pl.when(j == n_main)
    def _():
        small_ref[...] = proj()
        ft_ref[...] = _dot_nt(wft_ref[...].astype(BF16), h_scr[...])


def _inproj(x2d, scale1, shift1, w_in_t, qg, kg, *, seq, group_rows, lora_row, fgate_row, tm=1024, tn=512):
    t, d = x2d.shape
    fox_w = FOX_HEADS * FOX_HEAD_DIM
    assert all(first % 8 == 0 and count % tn == 0 for first, count in group_rows)
    assert lora_row % 8 == 0 and fgate_row % 8 == 0 and lora_row + tn <= w_in_t.shape[0]
    n_main = sum(count // tn for _, count in group_rows)
    tiles_per_batch = seq // tm
    kern = functools.partial(_inproj_kernel, n_q=fox_w // tn, n_k=fox_w // tn, n_main=n_main)
    pipelined = (_nbytes((tm, d), F32) + _nbytes((tn, d), F32) + _nbytes((tm, tn), BF16)
                 + _nbytes((tm, tn), F32) + _nbytes((8, tm), F32))

    def w_rows(i, j):
        row, hi = jnp.int32(lora_row), n_main
        for first, count in reversed(group_rows):
            lo = hi - count // tn
            row = jnp.where(j < hi, first + tn * (j - lo), row)
            hi = lo
        return pl.multiple_of(row, 8), 0

    return pl.pallas_call(
        kern,
        grid=(t // tm, n_main + 1),
        in_specs=[
            pl.BlockSpec((tm, d), lambda i, j: (i, 0)),
            pl.BlockSpec((1, 1, d), lambda i, j: (i // tiles_per_batch, 0, 0)),
            pl.BlockSpec((1, 1, d), lambda i, j: (i // tiles_per_batch, 0, 0)),
            pl.BlockSpec((pl.Element(tn), pl.Element(d)), w_rows),
            pl.BlockSpec((pl.Element(FOX_HEADS), pl.Element(d)), lambda i, j: (fgate_row, 0)),
            pl.BlockSpec((1, FOX_HEAD_DIM), lambda i, j: (0, 0)),
            pl.BlockSpec((1, FOX_HEAD_DIM), lambda i, j: (0, 0)),
        ],
        out_specs=[
            pl.BlockSpec((tm, tn), lambda i, j: (i, jnp.minimum(j, n_main - 1))),
            pl.BlockSpec((tm, tn), lambda i, j: (i, 0)),
            pl.BlockSpec((FOX_HEADS, tm), lambda i, j: (0, i)),
        ],
        out_shape=[
            jax.ShapeDtypeStruct((t, n_main * tn), BF16),
            jax.ShapeDtypeStruct((t, tn), F32),
            jax.ShapeDtypeStruct((FOX_HEADS, t), F32),
        ],
        scratch_shapes=[pltpu.VMEM((tm, d), BF16)],
        compiler_params=pltpu.CompilerParams(
            dimension_semantics=("arbitrary", "arbitrary"),
            vmem_limit_bytes=_vmem_limit(pipelined, _nbytes((tm, d), BF16),
                                         3 * _nbytes((tm, d), F32)),
        ),
        name="inproj",
    )(x2d, scale1, shift1, w_in_t, w_in_t, qg, kg)


def _fox_prep_kernel(ft_ref, bias_ref, o_ref, *, blocks_per_seq):
    x = ft_ref[...] + bias_ref[...]
    lf = jnp.minimum(x, 0.0) - jnp.log1p(jnp.exp(-jnp.abs(x)))
    rows, width = lf.shape
    r = lax.broadcasted_iota(jnp.int32, (width, width), 0)
    c = lax.broadcasted_iota(jnp.int32, (width, width), 1)
    upper = (r <= c).astype(BF16)
    within = sum(_dot(p, upper) for p in _split3(lf))
    tot = jnp.broadcast_to(within[:, width - 1:width], (rows, width))
    rr = lax.broadcasted_iota(jnp.int32, (rows, rows), 0)
    cc = lax.broadcasted_iota(jnp.int32, (rows, rows), 1)
    prior = ((rr // blocks_per_seq == cc // blocks_per_seq) & (cc < rr)).astype(BF16)
    offset = sum(_dot(prior, p) for p in _split3(tot))
    o_ref[...] = within + offset


def _fox_prep(ft_rows, bias_rows, *, blocks_per_seq):
    return pl.pallas_call(
        functools.partial(_fox_prep_kernel, blocks_per_seq=blocks_per_seq),
        out_shape=jax.ShapeDtypeStruct(ft_rows.shape, F32),
        name="fox_prep",
    )(ft_rows, bias_rows)


def _fox_attn_kernel(q_ref, k_ref, v_ref, og_ref, cum_ref, o_ref, *, tq):
    seq = q_ref.shape[0]
    negcum = -cum_ref[0, 0]
    for qi in range(seq // tq):
        q0, kend = qi * tq, (qi + 1) * tq
        s = _dot_nt(q_ref[q0:kend, :], k_ref[0:kend, :]) + negcum[:, 0:kend]
        row = q0 + lax.broadcasted_iota(jnp.int32, (tq, kend), 0)
        col = lax.broadcasted_iota(jnp.int32, (tq, kend), 1)
        s = jnp.where(col <= row, s, -jnp.inf)
        m = jnp.max(s, axis=-1, keepdims=True)
        p = jnp.exp(s - m)
        l = jnp.sum(p, axis=-1, keepdims=True)
        o = _dot(p.astype(BF16), v_ref[0:kend, :]) / l
        gate = _sigmoid(og_ref[q0:kend, :].astype(F32))
        o_ref[q0:kend, :] = (o * gate).astype(BF16)


def _fox_attn(main, cum, *, batch, seq, tq=256):
    t = main.shape[0]
    hd = FOX_HEAD_DIM
    nh = FOX_HEADS
    blk = lambda off: pl.BlockSpec((seq, hd), lambda b, h: (b, off + h))
    pipelined = 5 * _nbytes((seq, hd), BF16) + _nbytes((8, seq), F32)
    return pl.pallas_call(
        functools.partial(_fox_attn_kernel, tq=tq),
        grid=(batch, nh),
        in_specs=[blk(0), blk(nh), blk(2 * nh), blk(3 * nh),
                  pl.BlockSpec((1, 1, 1, seq), lambda b, h: (h, b, 0, 0))],
        out_specs=pl.BlockSpec((seq, hd), lambda b, h: (b, h)),
        out_shape=jax.ShapeDtypeStruct((t, nh * hd), BF16),
        compiler_params=pltpu.CompilerParams(
            dimension_semantics=("arbitrary", "arbitrary"),
            vmem_limit_bytes=_vmem_limit(pipelined, 0, 6 * _nbytes((tq, seq), F32)),
        ),
        name="fox_attn",
    )(main, main, main, main, cum)


def _rwkv_prep_kernel(r_ref, k_ref, v_ref, small_ref, mu_r_ref, mu_k_ref, mu_v_ref, mu_s_ref,
                      w0_ref, w2_ref, a0_ref, a2_ref, g2_ref,
                      r_o, k_o, v_o, lw_o, a_o, g_o, prev_r, prev_k, prev_v, prev_s):
    prevs = (prev_r, prev_k, prev_v, prev_s)

    @pl.when(pl.program_id(1) == 0)
    def _():
        for p in prevs:
            p[...] = jnp.zeros_like(p)

    def shift_mix(x, prev_ref, mu):
        tm = x.shape[0]
        xp = pltpu.roll(x, 1, axis=0)
        row = lax.broadcasted_iota(jnp.int32, x.shape, 0)
        xp = jnp.where(row == 0, prev_ref[0:1, :], xp)
        prev_ref[0:1, :] = x[tm - 1:tm, :]
        return x + (xp - x) * mu

    r = shift_mix(r_ref[...].astype(F32), prev_r, mu_r_ref[...])
    k = shift_mix(k_ref[...].astype(F32), prev_k, mu_k_ref[...])
    v = shift_mix(v_ref[...].astype(F32), prev_v, mu_v_ref[...])
    sm = shift_mix(small_ref[...], prev_s, mu_s_ref[...])
    xw = sm[:, 0:LORA_PAD]
    xa = sm[:, DECAY_LORA:DECAY_LORA + LORA_PAD]
    xg = sm[:, DECAY_LORA + AAA_LORA:DECAY_LORA + AAA_LORA + GATE_LORA]

    z = w0_ref[...] + _dot(jnp.tanh(xw).astype(BF16), w2_ref[...])
    w_raw = -(jnp.maximum(-z, 0.0) + jnp.log1p(jnp.exp(-jnp.abs(z)))) - 0.5
    lw_o[...] = -jnp.exp(w_raw)
    a_o[...] = _sigmoid(a0_ref[...] + _dot(xa.astype(BF16), a2_ref[...])).astype(BF16)
    g_o[...] = _dot(_sigmoid(xg).astype(BF16), g2_ref[...]).astype(BF16)
    r_o[...] = r.astype(BF16)
    k_o[...] = k.astype(BF16)
    v_o[...] = v.astype(BF16)


def _rwkv_prep(main, small, mu_r, mu_k, mu_v, mu_s, w0, w2p, a0, a2p, g2, *, batch, seq, tm=256):
    t = main.shape[0]
    width = RWKV_HEADS * RWKV_HEAD_DIM
    nsm = small.shape[1]
    col0 = (4 * FOX_HEADS * FOX_HEAD_DIM) // width
    steps = seq // tm
    act = lambda off: pl.BlockSpec((tm, width), lambda b, s: (b * steps + s, off))
    row_spec = lambda n: pl.BlockSpec((1, n), lambda b, s: (0, 0))
    full = lambda a: pl.BlockSpec(a.shape, lambda b, s: (0, 0))
    out_spec = pl.BlockSpec((tm, width), lambda b, s: (b * steps + s, 0))
    out_dtypes = [BF16, BF16, BF16, F32, BF16, BF16]
    pipelined = (3 * _nbytes((tm, width), BF16) + _nbytes((tm, nsm), F32)
                 + sum(_nbytes((tm, width), dt) for dt in out_dtypes)
                 + _nbytes(w2p.shape, BF16) + _nbytes(a2p.shape, BF16) + _nbytes(g2.shape, BF16))
    return pl.pallas_call(
        _rwkv_prep_kernel,
        grid=(batch, steps),
        in_specs=[act(col0), act(col0 + 1), act(col0 + 2),
                  pl.BlockSpec((tm, nsm), lambda b, s: (b * steps + s, 0)),
                  row_spec(width), row_spec(width), row_spec(width), row_spec(nsm),
                  row_spec(width), full(w2p), row_spec(width), full(a2p), full(g2)],
        out_specs=[out_spec] * 6,
        out_shape=[jax.ShapeDtypeStruct((t, width), dt) for dt in out_dtypes],
        scratch_shapes=[pltpu.VMEM((8, width), F32)] * 3 + [pltpu.VMEM((8, nsm), F32)],
        compiler_params=pltpu.CompilerParams(
            dimension_semantics=("arbitrary", "arbitrary"),
            vmem_limit_bytes=_vmem_limit(pipelined, 0, 12 * _nbytes((tm, width), F32)),
        ),
        name="rwkv_prep",
    )(main, main, main, small, mu_r, mu_k, mu_v, mu_s, w0, w2p, a0, a2p, g2)


def _rwkv_rec_kernel(r_ref, k_ref, v_ref, lw_ref, a_ref, g_ref,
                     kk_ref, ka_ref, rk_ref, lng_ref, lnb_ref, y_ref, state, *, n_tiles):
    c = pl.program_id(1)

    @pl.when(c == 0)
    def _():
        state[...] = jnp.zeros_like(state)

    ch = r_ref.shape[0]
    hd = RWKV_HEAD_DIM
    lane = lax.broadcasted_iota(jnp.int32, (1, LANES), 1)
    m0 = lane < hd
    n2 = HEADS_PER_TILE * ch
    rr = lax.broadcasted_iota(jnp.int32, (n2, n2), 0)
    cc = lax.broadcasted_iota(jnp.int32, (n2, n2), 1)
    same = (rr // ch) == (cc // ch)
    strict = same & (rr > cc)
    incl = same & (rr >= cc)
    lr = lax.broadcasted_iota(jnp.int32, (ch, ch), 0)
    lc = lax.broadcasted_iota(jnp.int32, (ch, ch), 1)
    lower = (lc <= lr).astype(BF16)

    def head_sum(x):
        s0 = jnp.sum(jnp.where(m0, x, 0.0), axis=-1, keepdims=True)
        s1 = jnp.sum(jnp.where(m0, 0.0, x), axis=-1, keepdims=True)
        return jnp.where(m0, s0, s1)

    def stack(x):
        return jnp.concatenate([jnp.where(m0, x, 0.0), jnp.where(m0, 0.0, x)], axis=0)

    tiles = range(n_tiles)
    sls = [slice(p * LANES, (p + 1) * LANES) for p in tiles]

    def each(fn, *cols):
        return [fn(*args) for args in zip(*cols)]

    def load(ref):
        return [ref[:, sl].astype(F32) for sl in sls]

    r, k, v, lw, a, g = (load(ref) for ref in (r_ref, k_ref, v_ref, lw_ref, a_ref, g_ref))
    kkw, kaw, rkw, lng, lnb = (load(ref) for ref in (kk_ref, ka_ref, rk_ref, lng_ref, lnb_ref))

    kk = each(lambda k_, w_: k_ * w_, k, kkw)
    kk = each(lambda x: x / jnp.maximum(jnp.sqrt(head_sum(x * x)), 1e-12), kk)
    kr = each(lambda k_, a_, w_: k_ * (1.0 + (a_ - 1.0) * w_), k, a, kaw)
    cum = each(lambda x: sum(_dot(lower, part) for part in _split3(x)), lw)
    gam = each(jnp.exp, cum)
    ginv = each(lambda x: jnp.exp(-x), cum)
    g_end = each(lambda x: x[ch - 1:ch, :], gam)
    a_t = each(lambda kk_, c_, lw_: -kk_ * jnp.exp(c_ - lw_), kk, cum, lw)
    r_t = each(lambda r_, g_: r_ * g_, r, gam)
    b_h = each(lambda kk_, a_, gi_: kk_ * a_ * gi_, kk, a, ginv)
    k_h = each(lambda kr_, gi_: kr_ * gi_, kr, ginv)

    stack2 = lambda x, y: jnp.concatenate([stack(x), stack(y)], axis=0).astype(BF16)
    lhs = each(stack2, a_t, r_t)
    rhs = each(stack2, b_h, k_h)
    sc = each(_dot_nt, lhs, rhs)
    ab = each(lambda s: jnp.where(strict, s[0:n2, 0:n2], 0.0).astype(BF16), sc)
    ak = each(lambda s: jnp.where(strict, s[0:n2, n2:2 * n2], 0.0).astype(BF16), sc)
    rbk = each(lambda s: jnp.where(jnp.concatenate([incl, incl], axis=1), s[n2:2 * n2, :], 0.0).astype(BF16), sc)

    st = [state[p] for p in tiles]
    ah = each(lambda l_, s_: _dot_nt(l_, s_.astype(BF16)), lhs, st)
    vs = each(lambda v_: stack(v_).astype(BF16), v)
    x = each(lambda ah_, ak_, vs_: ah_[0:n2] + _dot(ak_, vs_), ah, ak, vs)
    pw = ab
    n_lvl = ch.bit_length() - 1
    for lvl in range(n_lvl):
        x = each(lambda x_, p_: x_ + _dot(p_, x_.astype(BF16)), x, pw)
        if lvl + 1 < n_lvl:
            pw = each(lambda p_: _dot(p_, p_).astype(BF16), pw)
    uv = each(lambda x_, vs_: jnp.concatenate([x_.astype(BF16), vs_], axis=0), x, vs)
    ys = each(lambda ah_, m_, uv_: ah_[n2:2 * n2] + _dot(m_, uv_), ah, rbk, uv)
    y = each(lambda ys_: ys_[0:ch] + ys_[ch:n2], ys)

    bk = each(lambda b_, k_, ge_: stack2(b_ * ge_, k_ * ge_), b_h, k_h, g_end)
    new_st = each(lambda s_, ge_, uv_, bk_: s_ * ge_ + _dot_tn(uv_, bk_), st, g_end, uv, bk)
    for p in tiles:
        state[p] = new_st[p]

    mu = each(lambda y_: head_sum(y_) * (1.0 / hd), y)
    yc = each(lambda y_, m_: y_ - m_, y, mu)
    var = each(lambda c_: head_sum(c_ * c_) * (1.0 / hd), yc)
    yn = each(lambda c_, v_, g_, b_: c_ * lax.rsqrt(v_ + RWKV_LNX_EPS) * g_ + b_, yc, var, lng, lnb)
    bonus = each(lambda r_, kr_, w_, v_: head_sum(r_ * kr_ * w_) * v_, r, kr, rkw, v)
    for p in tiles:
        y_ref[:, sls[p]] = ((yn[p] + bonus[p]) * g[p]).astype(BF16)


def _rwkv_rec(r, k, v, lw, a, g, kk, ka, rk, lng, lnb, *, batch, seq):
    t, width = r.shape
    ch = RWKV_CHUNK
    steps = seq // ch
    n_tiles = width // LANES
    act = pl.BlockSpec((ch, width), lambda b, c: (b * steps + c, 0))
    row = pl.BlockSpec((1, width), lambda b, c: (0, 0))
    pipelined = sum(_nbytes((ch, width), x.dtype) for x in (r, k, v, lw, a, g)) + _nbytes((ch, width), BF16)
    return pl.pallas_call(
        functools.partial(_rwkv_rec_kernel, n_tiles=n_tiles),
        grid=(batch, steps),
        in_specs=[act] * 6 + [row] * 5,
        out_specs=act,
        out_shape=jax.ShapeDtypeStruct((t, width), BF16),
        scratch_shapes=[pltpu.VMEM((n_tiles, LANES, LANES), F32)],
        compiler_params=pltpu.CompilerParams(
            dimension_semantics=("arbitrary", "arbitrary"),
            vmem_limit_bytes=_vmem_limit(pipelined, _nbytes((n_tiles, LANES, LANES), F32), 16 << 20),
        ),
        name="rwkv_rec",
    )(r, k, v, lw, a, g, kk, ka, rk, lng, lnb)


def _mixout_kernel(o_ref, y_ref, ga0_ref, ga1_ref, gb0_ref, gb1_ref, wa_ref, wb_ref, wo_ref, x_ref,
                   gate_ref, lng_ref, lnb_ref, scale_ref, shift_ref, x1_ref, h2_ref, *, n_sub):
    tm, d = x_ref.shape
    rows = tm // n_sub
    half = d // 2
    sl = [slice(s * rows, (s + 1) * rows) for s in range(n_sub)]

    def branches(s):
        return _dot(o_ref[sl[s], :], wa_ref[...]), _dot(y_ref[sl[s], :], wb_ref[...])

    def mixed(s, ua, ub):
        parts = []
        for c, (ga, gb) in enumerate(((ga0_ref, gb0_ref), (ga1_ref, gb1_ref))):
            cs = slice(c * half, (c + 1) * half)
            parts.append(_sigmoid(ga[sl[s], :].astype(F32)) * ua[:, cs]
                         + _sigmoid(gb[sl[s], :].astype(F32)) * ub[:, cs])
        return _dot(jnp.concatenate(parts, axis=1).astype(BF16), wo_ref[...])

    def norms(s, mix):
        z = DEEPNORM_ALPHA * x_ref[sl[s], :] + gate_ref[0] * mix
        x1 = _layernorm(z) * lng_ref[...] + lnb_ref[...]
        x1_ref[sl[s], :] = x1
        h2_ref[sl[s], :] = (_layernorm(x1) * (1.0 + scale_ref[0]) + shift_ref[0]).astype(BF16)

    u = branches(0)
    mix_prev = None
    for s in range(n_sub):
        u_next = branches(s + 1) if s + 1 < n_sub else None
        mix = mixed(s, *u)
        if mix_prev is not None:
            norms(s - 1, mix_prev)
        u, mix_prev = u_next, mix
    norms(n_sub - 1, mix_prev)


def _mixout(o, y, main, wa, wb, wo, x2d, gate1, ln_g, ln_b, scale2, shift2, *, seq, tm=256, n_sub=2):
    t, d = x2d.shape
    kdim = o.shape[1]
    half = d // 2
    gate_blk0 = (main.shape[1] - 2 * d) // half
    tiles_per_batch = seq // tm
    mod = pl.BlockSpec((1, 1, d), lambda i: (i // tiles_per_batch, 0, 0))
    row = pl.BlockSpec((1, d), lambda i: (0, 0))
    gate = lambda c: pl.BlockSpec((tm, half), lambda i: (i, gate_blk0 + c))
    resident = lambda w: pl.BlockSpec(w.shape, lambda i: (0, 0), pipeline_mode=pl.Buffered(1))
    pipelined = (2 * _nbytes((tm, kdim), BF16) + 4 * _nbytes((tm, half), BF16)
                 + 2 * _nbytes((tm, d), F32) + _nbytes((tm, d), BF16))
    weights = _nbytes(wa.shape, BF16) + _nbytes(wb.shape, BF16) + _nbytes(wo.shape, BF16)
    return pl.pallas_call(
        functools.partial(_mixout_kernel, n_sub=n_sub),
        grid=(t // tm,),
        in_specs=[
            pl.BlockSpec((tm, kdim), lambda i: (i, 0)),
            pl.BlockSpec((tm, kdim), lambda i: (i, 0)),
            gate(0), gate(1), gate(2), gate(3),
            resident(wa), resident(wb), resident(wo),
            pl.BlockSpec((tm, d), lambda i: (i, 0)),
            mod, row, row, mod, mod,
        ],
        out_specs=[pl.BlockSpec((tm, d), lambda i: (i, 0)),
                   pl.BlockSpec((tm, d), lambda i: (i, 0))],
        out_shape=[jax.ShapeDtypeStruct((t, d), F32), jax.ShapeDtypeStruct((t, d), BF16)],
        compiler_params=pltpu.CompilerParams(
            dimension_semantics=("arbitrary",),
            vmem_limit_bytes=_vmem_limit(pipelined, weights, 8 * _nbytes((tm, d), F32)),
        ),
        name="mixout",
    )(o, y, main, main, main, main, wa, wb, wo, x2d, gate1, ln_g, ln_b, scale2, shift2)


def _ffn_kernel(h_ref, wg_ref, wu_ref, wd_ref, x1_ref, gate_ref, lng_ref, lnb_ref, o_ref, acc):
    f = pl.program_id(1)
    @pl.when(f == 0)
    def _():
        acc[...] = jnp.zeros_like(acc)

    h = h_ref[...]
    g = _dot(h, wg_ref[...])
    u = _dot(h, wu_ref[...])
    act = (g * _sigmoid(g) * u).astype(BF16)
    acc[...] += _dot(act, wd_ref[...])

    @pl.when(f == pl.num_programs(1) - 1)
    def _():
        z = DEEPNORM_ALPHA * x1_ref[...] + gate_ref[0] * acc[...]
        o_ref[...] = _layernorm(z) * lng_ref[...] + lnb_ref[...]


def _ffn(h2, w_gu, w_down, x1, gate2, ln_g, ln_b, *, seq, tm=512, tf=512):
    t, d = x1.shape
    d_ff = w_down.shape[0]
    nf = d_ff // tf
    tiles_per_batch = seq // tm
    mod = pl.BlockSpec((1, 1, d), lambda i, f: (i // tiles_per_batch, 0, 0))
    row = pl.BlockSpec((1, d), lambda i, f: (0, 0))
    pipelined = (_nbytes((tm, d), BF16) + 3 * _nbytes((d, tf), BF16) + 2 * _nbytes((tm, d), F32))
    return pl.pallas_call(
        _ffn_kernel,
        grid=(t // tm, nf),
        in_specs=[
            pl.BlockSpec((tm, d), lambda i, f: (i, 0)),
            pl.BlockSpec((d, tf), lambda i, f: (0, f)),
            pl.BlockSpec((d, tf), lambda i, f: (0, nf + f)),
            pl.BlockSpec((tf, d), lambda i, f: (f, 0)),
            pl.BlockSpec((tm, d), lambda i, f: (i, 0)),
            mod, row, row,
        ],
        out_specs=pl.BlockSpec((tm, d), lambda i, f: (i, 0)),
        out_shape=jax.ShapeDtypeStruct((t, d), F32),
        scratch_shapes=[pltpu.VMEM((tm, d), F32)],
        compiler_params=pltpu.CompilerParams(
            dimension_semantics=("arbitrary", "arbitrary"),
            vmem_limit_bytes=_vmem_limit(pipelined, _nbytes((tm, d), F32),
                                         2 * _nbytes((tm, d), F32) + 4 * _nbytes((tm, tf), F32)),
        ),
        name="ffn",
    )(h2, w_gu, w_gu, w_down, x1, gate2, ln_g, ln_b)


def _pad_rows(w, rows):
    return jnp.zeros((rows, w.shape[1]), w.dtype).at[:w.shape[0]].set(w)


def kernel(x, c, w_ada, b_ada, w_in, b_fgate, q_norm_g, k_norm_g, rwkv_mu, rwkv_w0, rwkv_w2, rwkv_a0, rwkv_a2, rwkv_g2, rwkv_k_k, rwkv_k_a, rwkv_r_k, rwkv_lnx_g, rwkv_lnx_b, w_branch_a, w_branch_b, w_out, ln1_g, ln1_b, w_ffn_gu, w_ffn_down, ln2_g, ln2_b):
    batch, seq, d = x.shape
    t = batch * seq
    fox_w = FOX_HEADS * FOX_HEAD_DIM
    rw_w = RWKV_HEADS * RWKV_HEAD_DIM
    fox_cols = 4 * fox_w + FOX_HEADS
    lora0 = fox_cols + 3 * rw_w
    gate0 = lora0 + DECAY_LORA + AAA_LORA + GATE_LORA
    assert seq % RWKV_CHUNK == 0 and w_in.shape[1] == gate0 + 2 * d

    lora_w = gate0 - lora0
    mu_r = rwkv_mu[None, 0:rw_w]
    mu_k = rwkv_mu[None, rw_w:2 * rw_w]
    mu_v = rwkv_mu[None, 2 * rw_w:3 * rw_w]
    mu_s = jnp.zeros((1, INPROJ_TN), rwkv_mu.dtype).at[0, :lora_w].set(rwkv_mu[3 * rw_w:])
    w2p = _pad_rows(rwkv_w2, LORA_PAD).astype(BF16)
    a2p = _pad_rows(rwkv_a2, LORA_PAD).astype(BF16)
    g2 = rwkv_g2.astype(BF16)
    wa = w_branch_a.astype(BF16)
    wb = w_branch_b.astype(BF16)
    wo = w_out.astype(BF16)
    wgu = w_ffn_gu.astype(BF16)
    wdn = w_ffn_down.astype(BF16)
    row = lambda v: v.reshape(1, -1)

    c_pad = jnp.zeros((8, d), c.dtype).at[:batch].set(c)
    mod = _ada(c_pad, w_ada, row(b_ada))[:batch].reshape(batch, 6, 1, d)
    shift1, scale1, gate1, shift2, scale2, gate2 = (mod[:, i] for i in range(6))

    x2d = x.reshape(t, d)
    main, small, ft = _inproj(x2d, scale1, shift1, w_in.T, row(q_norm_g), row(k_norm_g), seq=seq,
                              group_rows=[(0, 4 * fox_w), (fox_cols, 3 * rw_w), (gate0, 2 * d)],
                              lora_row=lora0, fgate_row=4 * fox_w, tn=INPROJ_TN)

    blocks_per_seq = seq // LANES
    ft_rows = ft.reshape(FOX_HEADS * batch * blocks_per_seq, LANES)
    bias_rows = jnp.broadcast_to(b_fgate[:, None, None], (FOX_HEADS, batch * blocks_per_seq, LANES))
    cum = _fox_prep(ft_rows, bias_rows.reshape(ft_rows.shape), blocks_per_seq=blocks_per_seq)
    cum = cum.reshape(FOX_HEADS, batch, 1, seq)
    o = _fox_attn(main, cum, batch=batch, seq=seq)

    r, k, v, lw, a, g = _rwkv_prep(main, small, mu_r, mu_k, mu_v, mu_s, row(rwkv_w0), w2p,
                                   row(rwkv_a0), a2p, g2, batch=batch, seq=seq)
    y = _rwkv_rec(r, k, v, lw, a, g, row(rwkv_k_k), row(rwkv_k_a), row(rwkv_r_k),
                  row(rwkv_lnx_g), row(rwkv_lnx_b), batch=batch, seq=seq)

    x1, h2 = _mixout(o, y, main, wa, wb, wo, x2d, gate1, row(ln1_g), row(ln1_b), scale2, shift2, seq=seq)

    out = _ffn(h2, wgu, wdn, x1, gate2, row(ln2_g), row(ln2_b), seq=seq)
    return out.reshape(batch, seq, d)
```

```python
import functools

import jax
import jax.numpy as jnp
from jax import lax
from jax.experimental import pallas as pl
from jax.experimental.pallas import tpu as pltpu

F32 = jnp.float32
BF16 = jnp.bfloat16

LANES = 128
VMEM_CAP_BYTES = 60000 * 1024

FOX_HEADS = 8
FOX_HEAD_DIM = 128
RWKV_HEADS = 16
RWKV_HEAD_DIM = 64
DECAY_LORA = 96
AAA_LORA = 96
GATE_LORA = 256
LORA_PAD = 128
INPROJ_TN = 512
RWKV_LNX_EPS = 64e-5
DEPTH = 1
DEEPNORM_ALPHA = (2.0 * DEPTH) ** 0.25
LN_EPS = 1e-5
RMS_EPS = 1e-6
RWKV_CHUNK = 64
HEADS_PER_TILE = LANES // RWKV_HEAD_DIM


def _vmem_limit(pipelined_bytes, resident_bytes=0, temp_bytes=0):
    need = 2 * pipelined_bytes + resident_bytes + temp_bytes + (2 << 20)
    return int(min(VMEM_CAP_BYTES, need))


def _nbytes(shape, dtype):
    n = 1
    for s in shape:
        n *= s
    return n * jnp.dtype(dtype).itemsize


def _layernorm(x):
    mu = jnp.mean(x, axis=-1, keepdims=True)
    xc = x - mu
    var = jnp.mean(xc * xc, axis=-1, keepdims=True)
    return xc * lax.rsqrt(var + LN_EPS)


def _sigmoid(x):
    return 1.0 / (1.0 + jnp.exp(-x))


def _dot(a, b):
    return jnp.dot(a, b, preferred_element_type=F32)


def _dot_nt(a, b):
    return lax.dot_general(a, b, (((1,), (1,)), ((), ())), preferred_element_type=F32)


def _dot_tn(a, b):
    return lax.dot_general(a, b, (((0,), (0,)), ((), ())), preferred_element_type=F32)


def _split3(x):
    hi = x.astype(BF16)
    r1 = x - hi.astype(F32)
    mid = r1.astype(BF16)
    lo = (r1 - mid.astype(F32)).astype(BF16)
    return hi, mid, lo


def _ada_kernel(c_ref, w_ref, b_ref, o_ref):
    c = c_ref[...]
    s = (c * _sigmoid(c)).astype(BF16)
    o_ref[...] = _dot(s, w_ref[...].astype(BF16)) + b_ref[...]


def _ada(c_pad, w_ada, b_ada, *, tn=1024):
    rows, d = c_pad.shape
    n = w_ada.shape[1]
    return pl.pallas_call(
        _ada_kernel,
        grid=(n // tn,),
        in_specs=[
            pl.BlockSpec((rows, d), lambda j: (0, 0)),
            pl.BlockSpec((d, tn), lambda j: (0, j)),
            pl.BlockSpec((1, tn), lambda j: (0, j)),
        ],
        out_specs=pl.BlockSpec((rows, tn), lambda j: (0, j)),
        out_shape=jax.ShapeDtypeStruct((rows, n), F32),
        compiler_params=pltpu.CompilerParams(
            dimension_semantics=("arbitrary",),
            vmem_limit_bytes=_vmem_limit(_nbytes((d, tn), F32), temp_bytes=_nbytes((d, tn), BF16)),
        ),
        name="ada",
    )(c_pad, w_ada, b_ada)


def _inproj_kernel(x_ref, scale_ref, shift_ref, wt_ref, wft_ref, qg_ref, kg_ref,
                   main_ref, small_ref, ft_ref, h_scr, *, n_q, n_k, n_main):
    j = pl.program_id(1)
    tn = main_ref.shape[1]

    @pl.when(j == 0)
    def _():
        h = _layernorm(x_ref[...]) * (1.0 + scale_ref[0]) + shift_ref[0]
        h_scr[...] = h.astype(BF16)

    def proj():
        return _dot_nt(h_scr[...], wt_ref[...].astype(BF16))

    @pl.when(j < n_q + n_k)
    def _():
        acc = proj()
        gain = jnp.where(j < n_q, qg_ref[...] * (FOX_HEAD_DIM ** -0.5), kg_ref[...])
        for hh in range(tn // FOX_HEAD_DIM):
            sl = slice(hh * FOX_HEAD_DIM, (hh + 1) * FOX_HEAD_DIM)
            a = acc[:, sl]
            ms = jnp.mean(a * a, axis=-1, keepdims=True)
            main_ref[:, sl] = (a * lax.rsqrt(ms + RMS_EPS) * gain).astype(BF16)

    @pl.when((j >= n_q + n_k) & (j < n_main))
    def _():
        main_ref[...] = proj().astype(BF16)

    @pl.when(j == n_main)
    def _():
        small_ref[...] = proj()
        ft_ref[...] = _dot_nt(wft_ref[...].astype(BF16), h_scr[...])


def _inproj(x2d, scale1, shift1, w_in_t, qg, kg, *, seq, group_rows, lora_row, fgate_row, tm=1024, tn=512):
    t, d = x2d.shape
    fox_w = FOX_HEADS * FOX_HEAD_DIM
    assert all(first % 8 == 0 and count % tn == 0 for first, count in group_rows)
    assert lora_row % 8 == 0 and fgate_row % 8 == 0 and lora_row + tn <= w_in_t.shape[0]
    n_main = sum(count // tn for _, count in group_rows)
    tiles_per_batch = seq // tm
    kern = functools.partial(_inproj_kernel, n_q=fox_w // tn, n_k=fox_w // tn, n_main=n_main)
    pipelined = (_nbytes((tm, d), F32) + _nbytes((tn, d), F32) + _nbytes((tm, tn), BF16)
                 + _nbytes((tm, tn), F32) + _nbytes((8, tm), F32))

    def w_rows(i, j):
        row, hi = jnp.int32(lora_row), n_main
        for first, count in reversed(group_rows):
            lo = hi - count // tn
            row = jnp.where(j < hi, first + tn * (j - lo), row)
            hi = lo
        return pl.multiple_of(row, 8), 0

    return pl.pallas_call(
        kern,
        grid=(t // tm, n_main + 1),
        in_specs=[
            pl.BlockSpec((tm, d), lambda i, j: (i, 0)),
            pl.BlockSpec((1, 1, d), lambda i, j: (i // tiles_per_batch, 0, 0)),
            pl.BlockSpec((1, 1, d), lambda i, j: (i // tiles_per_batch, 0, 0)),
            pl.BlockSpec((pl.Element(tn), pl.Element(d)), w_rows),
            pl.BlockSpec((pl.Element(FOX_HEADS), pl.Element(d)), lambda i, j: (fgate_row, 0)),
            pl.BlockSpec((1, FOX_HEAD_DIM), lambda i, j: (0, 0)),
            pl.BlockSpec((1, FOX_HEAD_DIM), lambda i, j: (0, 0)),
        ],
        out_specs=[
            pl.BlockSpec((tm, tn), lambda i, j: (i, jnp.minimum(j, n_main - 1))),
            pl.BlockSpec((tm, tn), lambda i, j: (i, 0)),
            pl.BlockSpec((FOX_HEADS, tm), lambda i, j: (0, i)),
        ],
        out_shape=[
            jax.ShapeDtypeStruct((t, n_main * tn), BF16),
            jax.ShapeDtypeStruct((t, tn), F32),
            jax.ShapeDtypeStruct((FOX_HEADS, t), F32),
        ],
        scratch_shapes=[pltpu.VMEM((tm, d), BF16)],
        compiler_params=pltpu.CompilerParams(
            dimension_semantics=("arbitrary", "arbitrary"),
            vmem_limit_bytes=_vmem_limit(pipelined, _nbytes((tm, d), BF16),
                                         3 * _nbytes((tm, d), F32)),
        ),
        name="inproj",
    )(x2d, scale1, shift1, w_in_t, w_in_t, qg, kg)


def _fox_prep_kernel(ft_ref, bias_ref, o_ref, *, blocks_per_seq):
    x = ft_ref[...] + bias_ref[...]
    lf = jnp.minimum(x, 0.0) - jnp.log1p(jnp.exp(-jnp.abs(x)))
    rows, width = lf.shape
    r = lax.broadcasted_iota(jnp.int32, (width, width), 0)
    c = lax.broadcasted_iota(jnp.int32, (width, width), 1)
    upper = (r <= c).astype(BF16)
    within = sum(_dot(p, upper) for p in _split3(lf))
    tot = jnp.broadcast_to(within[:, width - 1:width], (rows, width))
    rr = lax.broadcasted_iota(jnp.int32, (rows, rows), 0)
    cc = lax.broadcasted_iota(jnp.int32, (rows, rows), 1)
    prior = ((rr // blocks_per_seq == cc // blocks_per_seq) & (cc < rr)).astype(BF16)
    offset = sum(_dot(prior, p) for p in _split3(tot))
    o_ref[...] = within + offset


def _fox_prep(ft_rows, bias_rows, *, blocks_per_seq):
    return pl.pallas_call(
        functools.partial(_fox_prep_kernel, blocks_per_seq=blocks_per_seq),
        out_shape=jax.ShapeDtypeStruct(ft_rows.shape, F32),
        name="fox_prep",
    )(ft_rows, bias_rows)


def _fox_attn_kernel(q_ref, k_ref, v_ref, og_ref, cum_ref, o_ref, *, tq):
    seq = q_ref.shape[0]
    negcum = -cum_ref[0, 0]
    for qi in range(seq // tq):
        q0, kend = qi * tq, (qi + 1) * tq
        s = _dot_nt(q_ref[q0:kend, :], k_ref[0:kend, :]) + negcum[:, 0:kend]
        row = q0 + lax.broadcasted_iota(jnp.int32, (tq, kend), 0)
        col = lax.broadcasted_iota(jnp.int32, (tq, kend), 1)
        s = jnp.where(col <= row, s, -jnp.inf)
        m = jnp.max(s, axis=-1, keepdims=True)
        p = jnp.exp(s - m)
        l = jnp.sum(p, axis=-1, keepdims=True)
        o = _dot(p.astype(BF16), v_ref[0:kend, :]) / l
        gate = _sigmoid(og_ref[q0:kend, :].astype(F32))
        o_ref[q0:kend, :] = (o * gate).astype(BF16)


def _fox_attn(main, cum, *, batch, seq, tq=256):
    t = main.shape[0]
    hd = FOX_HEAD_DIM
    nh = FOX_HEADS
    blk = lambda off: pl.BlockSpec((seq, hd), lambda b, h: (b, off + h))
    pipelined = 5 * _nbytes((seq, hd), BF16) + _nbytes((8, seq), F32)
    return pl.pallas_call(
        functools.partial(_fox_attn_kernel, tq=tq),
        grid=(batch, nh),
        in_specs=[blk(0), blk(nh), blk(2 * nh), blk(3 * nh),
                  pl.BlockSpec((1, 1, 1, seq), lambda b, h: (h, b, 0, 0))],
        out_specs=pl.BlockSpec((seq, hd), lambda b, h: (b, h)),
        out_shape=jax.ShapeDtypeStruct((t, nh * hd), BF16),
        compiler_params=pltpu.CompilerParams(
            dimension_semantics=("arbitrary", "arbitrary"),
            vmem_limit_bytes=_vmem_limit(pipelined, 0, 6 * _nbytes((tq, seq), F32)),
        ),
        name="fox_attn",
    )(main, main, main, main, cum)


def _rwkv_prep_kernel(r_ref, k_ref, v_ref, small_ref, mu_r_ref, mu_k_ref, mu_v_ref, mu_s_ref,
                      w0_ref, w2_ref, a0_ref, a2_ref, g2_ref,
                      r_o, k_o, v_o, lw_o, a_o, g_o, prev_r, prev_k, prev_v, prev_s):
    prevs = (prev_r, prev_k, prev_v, prev_s)

    @pl.when(pl.program_id(1) == 0)
    def _():
        for p in prevs:
            p[...] = jnp.zeros_like(p)

    def shift_mix(x, prev_ref, mu):
        tm = x.shape[0]
        xp = pltpu.roll(x, 1, axis=0)
        row = lax.broadcasted_iota(jnp.int32, x.shape, 0)
        xp = jnp.where(row == 0, prev_ref[0:1, :], xp)
        prev_ref[0:1, :] = x[tm - 1:tm, :]
        return x + (xp - x) * mu

    r = shift_mix(r_ref[...].astype(F32), prev_r, mu_r_ref[...])
    k = shift_mix(k_ref[...].astype(F32), prev_k, mu_k_ref[...])
    v = shift_mix(v_ref[...].astype(F32), prev_v, mu_v_ref[...])
    sm = shift_mix(small_ref[...], prev_s, mu_s_ref[...])
    xw = sm[:, 0:LORA_PAD]
    xa = sm[:, DECAY_LORA:DECAY_LORA + LORA_PAD]
    xg = sm[:, DECAY_LORA + AAA_LORA:DECAY_LORA + AAA_LORA + GATE_LORA]

    z = w0_ref[...] + _dot(jnp.tanh(xw).astype(BF16), w2_ref[...])
    w_raw = -(jnp.maximum(-z, 0.0) + jnp.log1p(jnp.exp(-jnp.abs(z)))) - 0.5
    lw_o[...] = -jnp.exp(w_raw)
    a_o[...] = _sigmoid(a0_ref[...] + _dot(xa.astype(BF16), a2_ref[...])).astype(BF16)
    g_o[...] = _dot(_sigmoid(xg).astype(BF16), g2_ref[...]).astype(BF16)
    r_o[...] = r.astype(BF16)
    k_o[...] = k.astype(BF16)
    v_o[...] = v.astype(BF16)


def _rwkv_prep(main, small, mu_r, mu_k, mu_v, mu_s, w0, w2p, a0, a2p, g2, *, batch, seq, tm=256):
    t = main.shape[0]
    width = RWKV_HEADS * RWKV_HEAD_DIM
    nsm = small.shape[1]
    col0 = (4 * FOX_HEADS * FOX_HEAD_DIM) // width
    steps = seq // tm
    act = lambda off: pl.BlockSpec((tm, width), lambda b, s: (b * steps + s, off))
    row_spec = lambda n: pl.BlockSpec((1, n), lambda b, s: (0, 0))
    full = lambda a: pl.BlockSpec(a.shape, lambda b, s: (0, 0))
    out_spec = pl.BlockSpec((tm, width), lambda b, s: (b * steps + s, 0))
    out_dtypes = [BF16, BF16, BF16, F32, BF16, BF16]
    pipelined = (3 * _nbytes((tm, width), BF16) + _nbytes((tm, nsm), F32)
                 + sum(_nbytes((tm, width), dt) for dt in out_dtypes)
                 + _nbytes(w2p.shape, BF16) + _nbytes(a2p.shape, BF16) + _nbytes(g2.shape, BF16))
    return pl.pallas_call(
        _rwkv_prep_kernel,
        grid=(batch, steps),
        in_specs=[act(col0), act(col0 + 1), act(col0 + 2),
                  pl.BlockSpec((tm, nsm), lambda b, s: (b * steps + s, 0)),
                  row_spec(width), row_spec(width), row_spec(width), row_spec(nsm),
                  row_spec(width), full(w2p), row_spec(width), full(a2p), full(g2)],
        out_specs=[out_spec] * 6,
        out_shape=[jax.ShapeDtypeStruct((t, width), dt) for dt in out_dtypes],
        scratch_shapes=[pltpu.VMEM((8, width), F32)] * 3 + [pltpu.VMEM((8, nsm), F32)],
        compiler_params=pltpu.CompilerParams(
            dimension_semantics=("arbitrary", "arbitrary"),
            vmem_limit_bytes=_vmem_limit(pipelined, 0, 12 * _nbytes((tm, width), F32)),
        ),
        name="rwkv_prep",
    )(main, main, main, small, mu_r, mu_k, mu_v, mu_s, w0, w2p, a0, a2p, g2)


def _rwkv_rec_kernel(r_ref, k_ref, v_ref, lw_ref, a_ref, g_ref,
                     kk_ref, ka_ref, rk_ref, lng_ref, lnb_ref, y_ref, state, *, n_tiles, n_seq):
    @pl.when(pl.program_id(1) == 0)
    def _():
        state[...] = jnp.zeros_like(state)

    ch = y_ref.shape[1]
    hd = RWKV_HEAD_DIM
    lane = lax.broadcasted_iota(jnp.int32, (1, LANES), 1)
    m0 = lane < hd
    n2 = HEADS_PER_TILE * ch
    rr = lax.broadcasted_iota(jnp.int32, (n2, n2), 0)
    cc = lax.broadcasted_iota(jnp.int32, (n2, n2), 1)
    same = (rr // ch) == (cc // ch)
    strict = same & (rr > cc)
    incl = same & (rr >= cc)
    lr = lax.broadcasted_iota(jnp.int32, (ch, ch), 0)
    lc = lax.broadcasted_iota(jnp.int32, (ch, ch), 1)
    lower = (lc <= lr).astype(BF16)

    def head_sum(x):
        s0 = jnp.sum(jnp.where(m0, x, 0.0), axis=-1, keepdims=True)
        s1 = jnp.sum(jnp.where(m0, 0.0, x), axis=-1, keepdims=True)
        return jnp.where(m0, s0, s1)

    def stack(x):
        return jnp.concatenate([jnp.where(m0, x, 0.0), jnp.where(m0, 0.0, x)], axis=0)

    sls = [slice(p * LANES, (p + 1) * LANES) for p in range(n_tiles)]
    tiles = range(n_seq * n_tiles)

    def each(fn, *cols):
        return [fn(*args) for args in zip(*cols)]

    r, k, v, lw, a, g = ([ref[q, :, sl].astype(F32) for q in range(n_seq) for sl in sls]
                         for ref in (r_ref, k_ref, v_ref, lw_ref, a_ref, g_ref))
    kkw, kaw, rkw, lng, lnb = ([ref[:, sl] for _ in range(n_seq) for sl in sls]
                               for ref in (kk_ref, ka_ref, rk_ref, lng_ref, lnb_ref))

    kk = each(lambda k_, w_: k_ * w_, k, kkw)
    kk = each(lambda x: x / jnp.maximum(jnp.sqrt(head_sum(x * x)), 1e-12), kk)
    kr = each(lambda k_, a_, w_: k_ * (1.0 + (a_ - 1.0) * w_), k, a, kaw)
    cum = each(lambda x: sum(_dot(lower, part) for part in _split3(x)[:2]), lw)
    gam = each(jnp.exp, cum)
    ginv = each(lambda x: jnp.exp(-x), cum)
    g_end = each(lambda x: x[ch - 1:ch, :], gam)
    a_t = each(lambda kk_, c_, lw_: -kk_ * jnp.exp(c_ - lw_), kk, cum, lw)
    r_t = each(lambda r_, g_: r_ * g_, r, gam)
    b_h = each(lambda kk_, a_, gi_: kk_ * a_ * gi_, kk, a, ginv)
    k_h = each(lambda kr_, gi_: kr_ * gi_, kr, ginv)

    stack2 = lambda x, y: jnp.concatenate([stack(x), stack(y)], axis=0).astype(BF16)
    lhs = each(stack2, a_t, r_t)
    rhs = each(stack2, b_h, k_h)
    sc = each(_dot_nt, lhs, rhs)
    ab = each(lambda s: jnp.where(strict, s[0:n2, 0:n2], 0.0).astype(BF16), sc)
    ak = each(lambda s: jnp.where(strict, s[0:n2, n2:2 * n2], 0.0).astype(BF16), sc)
    rbk = each(lambda s: jnp.where(jnp.concatenate([incl, incl], axis=1), s[n2:2 * n2, :], 0.0).astype(BF16), sc)

    st = [state[p] for p in tiles]
    ah = each(lambda l_, s_: _dot_nt(l_, s_.astype(BF16)), lhs, st)
    vs = each(lambda v_: stack(v_).astype(BF16), v)
    x = each(lambda ah_, ak_, vs_: ah_[0:n2] + _dot(ak_, vs_), ah, ak, vs)
    pw = ab
    n_lvl = ch.bit_length() - 1
    for lvl in range(n_lvl):
        x = each(lambda x_, p_: x_ + _dot(p_, x_.astype(BF16)), x, pw)
        if lvl + 1 < n_lvl:
            pw = each(lambda p_: _dot(p_, p_).astype(BF16), pw)
    uv = each(lambda x_, vs_: jnp.concatenate([x_.astype(BF16), vs_], axis=0), x, vs)
    ys = each(lambda ah_, m_, uv_: ah_[n2:2 * n2] + _dot(m_, uv_), ah, rbk, uv)
    y = each(lambda ys_: ys_[0:ch] + ys_[ch:n2], ys)

    bk = each(lambda b_, k_, ge_: stack2(b_ * ge_, k_ * ge_), b_h, k_h, g_end)
    new_st = each(lambda s_, ge_, uv_, bk_: s_ * ge_ + _dot_tn(uv_, bk_), st, g_end, uv, bk)
    for p in tiles:
        state[p] = new_st[p]

    mu = each(lambda y_: head_sum(y_) * (1.0 / hd), y)
    yc = each(lambda y_, m_: y_ - m_, y, mu)
    var = each(lambda c_: head_sum(c_ * c_) * (1.0 / hd), yc)
    yn = each(lambda c_, v_, g_, b_: c_ * lax.rsqrt(v_ + RWKV_LNX_EPS) * g_ + b_, yc, var, lng, lnb)
    bonus = each(lambda r_, kr_, w_, v_: head_sum(r_ * kr_ * w_) * v_, r, kr, rkw, v)
    for i in tiles:
        y_ref[i // n_tiles, :, sls[i % n_tiles]] = ((yn[i] + bonus[i]) * g[i]).astype(BF16)


def _rwkv_rec(r, k, v, lw, a, g, kk, ka, rk, lng, lnb, *, batch, seq, n_seq=2):
    t, width = r.shape
    ch = RWKV_CHUNK
    steps = seq // ch
    n_tiles = width // LANES
    assert batch % n_seq == 0
    grouped = (batch // n_seq, n_seq, seq, width)
    act = pl.BlockSpec((None, n_seq, ch, width), lambda b, c: (b, 0, c, 0))
    row = pl.BlockSpec((1, width), lambda b, c: (0, 0))
    pipelined = n_seq * (sum(_nbytes((ch, width), x.dtype) for x in (r, k, v, lw, a, g))
                         + _nbytes((ch, width), BF16))
    state_shape = (n_seq * n_tiles, LANES, LANES)
    y = pl.pallas_call(
        functools.partial(_rwkv_rec_kernel, n_tiles=n_tiles, n_seq=n_seq),
        grid=(batch // n_seq, steps),
        in_specs=[act] * 6 + [row] * 5,
        out_specs=act,
        out_shape=jax.ShapeDtypeStruct(grouped, BF16),
        scratch_shapes=[pltpu.VMEM(state_shape, F32)],
        compiler_params=pltpu.CompilerParams(
            dimension_semantics=("arbitrary", "arbitrary"),
            vmem_limit_bytes=_vmem_limit(pipelined, _nbytes(state_shape, F32), 24 << 20),
        ),
        name="rwkv_rec",
    )(*(x.reshape(grouped) for x in (r, k, v, lw, a, g)), kk, ka, rk, lng, lnb)
    return y.reshape(t, width)


def _mixout_kernel(o_ref, y_ref, ga0_ref, ga1_ref, gb0_ref, gb1_ref, wa_ref, wb_ref, wo_ref, x_ref,
                   gate_ref, lng_ref, lnb_ref, scale_ref, shift_ref, x1_ref, h2_ref, *, n_sub):
    tm, d = x_ref.shape
    rows = tm // n_sub
    half = d // 2
    sl = [slice(s * rows, (s + 1) * rows) for s in range(n_sub)]

    def branches(s):
        return _dot(o_ref[sl[s], :], wa_ref[...]), _dot(y_ref[sl[s], :], wb_ref[...])

    def mixed(s, ua, ub):
        parts = []
        for c, (ga, gb) in enumerate(((ga0_ref, gb0_ref), (ga1_ref, gb1_ref))):
            cs = slice(c * half, (c + 1) * half)
            parts.append(_sigmoid(ga[sl[s], :].astype(F32)) * ua[:, cs]
                         + _sigmoid(gb[sl[s], :].astype(F32)) * ub[:, cs])
        return _dot(jnp.concatenate(parts, axis=1).astype(BF16), wo_ref[...])

    def norms(s, mix):
        z = DEEPNORM_ALPHA * x_ref[sl[s], :] + gate_ref[0] * mix
        x1 = _layernorm(z) * lng_ref[...] + lnb_ref[...]
        x1_ref[sl[s], :] = x1
        h2_ref[sl[s], :] = (_layernorm(x1) * (1.0 + scale_ref[0]) + shift_ref[0]).astype(BF16)

    u = branches(0)
    mix_prev = None
    for s in range(n_sub):
        u_next = branches(s + 1) if s + 1 < n_sub else None
        mix = mixed(s, *u)
        if mix_prev is not None:
            norms(s - 1, mix_prev)
        u, mix_prev = u_next, mix
    norms(n_sub - 1, mix_prev)


def _mixout(o, y, main, wa, wb, wo, x2d, gate1, ln_g, ln_b, scale2, shift2, *, seq, tm=256, n_sub=2):
    t, d = x2d.shape
    kdim = o.shape[1]
    half = d // 2
    gate_blk0 = (main.shape[1] - 2 * d) // half
    tiles_per_batch = seq // tm
    mod = pl.BlockSpec((1, 1, d), lambda i: (i // tiles_per_batch, 0, 0))
    row = pl.BlockSpec((1, d), lambda i: (0, 0))
    gate = lambda c: pl.BlockSpec((tm, half), lambda i: (i, gate_blk0 + c))
    resident = lambda w: pl.BlockSpec(w.shape, lambda i: (0, 0), pipeline_mode=pl.Buffered(1))
    pipelined = (2 * _nbytes((tm, kdim), BF16) + 4 * _nbytes((tm, half), BF16)
                 + 2 * _nbytes((tm, d), F32) + _nbytes((tm, d), BF16))
    weights = _nbytes(wa.shape, BF16) + _nbytes(wb.shape, BF16) + _nbytes(wo.shape, BF16)
    return pl.pallas_call(
        functools.partial(_mixout_kernel, n_sub=n_sub),
        grid=(t // tm,),
        in_specs=[
            pl.BlockSpec((tm, kdim), lambda i: (i, 0)),
            pl.BlockSpec((tm, kdim), lambda i: (i, 0)),
            gate(0), gate(1), gate(2), gate(3),
            resident(wa), resident(wb), resident(wo),
            pl.BlockSpec((tm, d), lambda i: (i, 0)),
            mod, row, row, mod, mod,
        ],
        out_specs=[pl.BlockSpec((tm, d), lambda i: (i, 0)),
                   pl.BlockSpec((tm, d), lambda i: (i, 0))],
        out_shape=[jax.ShapeDtypeStruct((t, d), F32), jax.ShapeDtypeStruct((t, d), BF16)],
        compiler_params=pltpu.CompilerParams(
            dimension_semantics=("arbitrary",),
            vmem_limit_bytes=_vmem_limit(pipelined, weights, 8 * _nbytes((tm, d), F32)),
        ),
        name="mixout",
    )(o, y, main, main, main, main, wa, wb, wo, x2d, gate1, ln_g, ln_b, scale2, shift2)


def _ffn_kernel(h_ref, wg_ref, wu_ref, wd_ref, x1_ref, gate_ref, lng_ref, lnb_ref, o_ref, acc):
    f = pl.program_id(1)
    @pl.when(f == 0)
    def _():
        acc[...] = jnp.zeros_like(acc)

    h = h_ref[...]
    g = _dot(h, wg_ref[...])
    u = _dot(h, wu_ref[...])
    act = (g * _sigmoid(g) * u).astype(BF16)
    acc[...] += _dot(act, wd_ref[...])

    @pl.when(f == pl.num_programs(1) - 1)
    def _():
        z = DEEPNORM_ALPHA * x1_ref[...] + gate_ref[0] * acc[...]
        o_ref[...] = _layernorm(z) * lng_ref[...] + lnb_ref[...]


def _ffn(h2, w_gu, w_down, x1, gate2, ln_g, ln_b, *, seq, tm=512, tf=512):
    t, d = x1.shape
    d_ff = w_down.shape[0]
    nf = d_ff // tf
    tiles_per_batch = seq // tm
    mod = pl.BlockSpec((1, 1, d), lambda i, f: (i // tiles_per_batch, 0, 0))
    row = pl.BlockSpec((1, d), lambda i, f: (0, 0))
    pipelined = (_nbytes((tm, d), BF16) + 3 * _nbytes((d, tf), BF16) + 2 * _nbytes((tm, d), F32))
    return pl.pallas_call(
        _ffn_kernel,
        grid=(t // tm, nf),
        in_specs=[
            pl.BlockSpec((tm, d), lambda i, f: (i, 0)),
            pl.BlockSpec((d, tf), lambda i, f: (0, f)),
            pl.BlockSpec((d, tf), lambda i, f: (0, nf + f)),
            pl.BlockSpec((tf, d), lambda i, f: (f, 0)),
            pl.BlockSpec((tm, d), lambda i, f: (i, 0)),
            mod, row, row,
        ],
        out_specs=pl.BlockSpec((tm, d), lambda i, f: (i, 0)),
        out_shape=jax.ShapeDtypeStruct((t, d), F32),
        scratch_shapes=[pltpu.VMEM((tm, d), F32)],
        compiler_params=pltpu.CompilerParams(
            dimension_semantics=("arbitrary", "arbitrary"),
            vmem_limit_bytes=_vmem_limit(pipelined, _nbytes((tm, d), F32),
                                         2 * _nbytes((tm, d), F32) + 4 * _nbytes((tm, tf), F32)),
        ),
        name="ffn",
    )(h2, w_gu, w_gu, w_down, x1, gate2, ln_g, ln_b)


def _pad_rows(w, rows):
    return jnp.zeros((rows, w.shape[1]), w.dtype).at[:w.shape[0]].set(w)


def kernel(x, c, w_ada, b_ada, w_in, b_fgate, q_norm_g, k_norm_g, rwkv_mu, rwkv_w0, rwkv_w2, rwkv_a0, rwkv_a2, rwkv_g2, rwkv_k_k, rwkv_k_a, rwkv_r_k, rwkv_lnx_g, rwkv_lnx_b, w_branch_a, w_branch_b, w_out, ln1_g, ln1_b, w_ffn_gu, w_ffn_down, ln2_g, ln2_b):
    batch, seq, d = x.shape
    t = batch * seq
    fox_w = FOX_HEADS * FOX_HEAD_DIM
    rw_w = RWKV_HEADS * RWKV_HEAD_DIM
    fox_cols = 4 * fox_w + FOX_HEADS
    lora0 = fox_cols + 3 * rw_w
    gate0 = lora0 + DECAY_LORA + AAA_LORA + GATE_LORA
    assert seq % RWKV_CHUNK == 0 and w_in.shape[1] == gate0 + 2 * d

    lora_w = gate0 - lora0
    mu_r = rwkv_mu[None, 0:rw_w]
    mu_k = rwkv_mu[None, rw_w:2 * rw_w]
    mu_v = rwkv_mu[None, 2 * rw_w:3 * rw_w]
    mu_s = jnp.zeros((1, INPROJ_TN), rwkv_mu.dtype).at[0, :lora_w].set(rwkv_mu[3 * rw_w:])
    w2p = _pad_rows(rwkv_w2, LORA_PAD).astype(BF16)
    a2p = _pad_rows(rwkv_a2, LORA_PAD).astype(BF16)
    g2 = rwkv_g2.astype(BF16)
    wa = w_branch_a.astype(BF16)
    wb = w_branch_b.astype(BF16)
    wo = w_out.astype(BF16)
    wgu = w_ffn_gu.astype(BF16)
    wdn = w_ffn_down.astype(BF16)
    row = lambda v: v.reshape(1, -1)

    c_pad = jnp.zeros((8, d), c.dtype).at[:batch].set(c)
    mod = _ada(c_pad, w_ada, row(b_ada))[:batch].reshape(batch, 6, 1, d)
    shift1, scale1, gate1, shift2, scale2, gate2 = (mod[:, i] for i in range(6))

    x2d = x.reshape(t, d)
    main, small, ft = _inproj(x2d, scale1, shift1, w_in.T, row(q_norm_g), row(k_norm_g), seq=seq,
                              group_rows=[(0, 4 * fox_w), (fox_cols, 3 * rw_w), (gate0, 2 * d)],
                              lora_row=lora0, fgate_row=4 * fox_w, tn=INPROJ_TN)

    blocks_per_seq = seq // LANES
    ft_rows = ft.reshape(FOX_HEADS * batch * blocks_per_seq, LANES)
    bias_rows = jnp.broadcast_to(b_fgate[:, None, None], (FOX_HEADS, batch * blocks_per_seq, LANES))
    cum = _fox_prep(ft_rows, bias_rows.reshape(ft_rows.shape), blocks_per_seq=blocks_per_seq)
    cum = cum.reshape(FOX_HEADS, batch, 1, seq)
    o = _fox_attn(main, cum, batch=batch, seq=seq)

    r, k, v, lw, a, g = _rwkv_prep(main, small, mu_r, mu_k, mu_v, mu_s, row(rwkv_w0), w2p,
                                   row(rwkv_a0), a2p, g2, batch=batch, seq=seq)
    y = _rwkv_rec(r, k, v, lw, a, g, row(rwkv_k_k), row(rwkv_k_a), row(rwkv_r_k),
                  row(rwkv_lnx_g), row(rwkv_lnx_b), batch=batch, seq=seq)

    x1, h2 = _mixout(o, y, main, wa, wb, wo, x2d, gate1, row(ln1_g), row(ln1_b), scale2, shift2, seq=seq)

    out = _ffn(h2, wgu, wdn, x1, gate2, row(ln2_g), row(ln2_b), seq=seq)
    return out.reshape(batch, seq, d)
```

```python
import functools

import jax
import jax.numpy as jnp
from jax import lax
from jax.experimental import pallas as pl
from jax.experimental.pallas import tpu as pltpu

F32 = jnp.float32
BF16 = jnp.bfloat16

LANES = 128
VMEM_CAP_BYTES = 60000 * 1024

FOX_HEADS = 8
FOX_HEAD_DIM = 128
RWKV_HEADS = 16
RWKV_HEAD_DIM = 64
DECAY_LORA = 96
AAA_LORA = 96
GATE_LORA = 256
LORA_PAD = 128
INPROJ_TN = 512
RWKV_LNX_EPS = 64e-5
DEPTH = 1
DEEPNORM_ALPHA = (2.0 * DEPTH) ** 0.25
LN_EPS = 1e-5
RMS_EPS = 1e-6
RWKV_CHUNK = 64
HEADS_PER_TILE = LANES // RWKV_HEAD_DIM


def _vmem_limit(pipelined_bytes, resident_bytes=0, temp_bytes=0):
    need = 2 * pipelined_bytes + resident_bytes + temp_bytes + (2 << 20)
    return int(min(VMEM_CAP_BYTES, need))


def _nbytes(shape, dtype):
    n = 1
    for s in shape:
        n *= s
    return n * jnp.dtype(dtype).itemsize


def _layernorm(x):
    mu = jnp.mean(x, axis=-1, keepdims=True)
    xc = x - mu
    var = jnp.mean(xc * xc, axis=-1, keepdims=True)
    return xc * lax.rsqrt(var + LN_EPS)


def _sigmoid(x):
    return 1.0 / (1.0 + jnp.exp(-x))


def _dot(a, b):
    return jnp.dot(a, b, preferred_element_type=F32)


def _dot_nt(a, b):
    return lax.dot_general(a, b, (((1,), (1,)), ((), ())), preferred_element_type=F32)


def _dot_tn(a, b):
    return lax.dot_general(a, b, (((0,), (0,)), ((), ())), preferred_element_type=F32)


def _split3(x):
    hi = x.astype(BF16)
    r1 = x - hi.astype(F32)
    mid = r1.astype(BF16)
    lo = (r1 - mid.astype(F32)).astype(BF16)
    return hi, mid, lo


def _ada_kernel(c_ref, w_ref, b_ref, o_ref):
    c = c_ref[...]
    s = (c * _sigmoid(c)).astype(BF16)
    o_ref[...] = _dot(s, w_ref[...].astype(BF16)) + b_ref[...]


def _ada(c_pad, w_ada, b_ada, *, tn=1024):
    rows, d = c_pad.shape
    n = w_ada.shape[1]
    return pl.pallas_call(
        _ada_kernel,
        grid=(n // tn,),
        in_specs=[
            pl.BlockSpec((rows, d), lambda j: (0, 0)),
            pl.BlockSpec((d, tn), lambda j: (0, j)),
            pl.BlockSpec((1, tn), lambda j: (0, j)),
        ],
        out_specs=pl.BlockSpec((rows, tn), lambda j: (0, j)),
        out_shape=jax.ShapeDtypeStruct((rows, n), F32),
        compiler_params=pltpu.CompilerParams(
            dimension_semantics=("arbitrary",),
            vmem_limit_bytes=_vmem_limit(_nbytes((d, tn), F32), temp_bytes=_nbytes((d, tn), BF16)),
        ),
        name="ada",
    )(c_pad, w_ada, b_ada)


def _inproj_kernel(x_ref, scale_ref, shift_ref, wt_ref, wft_ref, qg_ref, kg_ref,
                   main_ref, small_ref, ft_ref, h_scr, *, n_q, n_k, n_main):
    j = pl.program_id(1)
    tn = main_ref.shape[1]

    @pl.when(j == 0)
    def _():
        h = _layernorm(x_ref[...]) * (1.0 + scale_ref[0]) + shift_ref[0]
        h_scr[...] = h.astype(BF16)

    def proj():
        return _dot_nt(h_scr[...], wt_ref[...].astype(BF16))

    @pl.when(j < n_q + n_k)
    def _():
        acc = proj()
        gain = jnp.where(j < n_q, qg_ref[...] * (FOX_HEAD_DIM ** -0.5), kg_ref[...])
        for hh in range(tn // FOX_HEAD_DIM):
            sl = slice(hh * FOX_HEAD_DIM, (hh + 1) * FOX_HEAD_DIM)
            a = acc[:, sl]
            ms = jnp.mean(a * a, axis=-1, keepdims=True)
            main_ref[:, sl] = (a * lax.rsqrt(ms + RMS_EPS) * gain).astype(BF16)

    @pl.when((j >= n_q + n_k) & (j < n_main))
    def _():
        main_ref[...] = proj().astype(BF16)

    @pl.when(j == n_main)
    def _():
        small_ref[...] = proj()
        ft_ref[...] = _dot_nt(wft_ref[...].astype(BF16), h_scr[...])


def _inproj(x2d, scale1, shift1, w_in_t, qg, kg, *, seq, group_rows, lora_row, fgate_row, tm=1024, tn=512):
    t, d = x2d.shape
    fox_w = FOX_HEADS * FOX_HEAD_DIM
    assert all(first % 8 == 0 and count % tn == 0 for first, count in group_rows)
    assert lora_row % 8 == 0 and fgate_row % 8 == 0 and lora_row + tn <= w_in_t.shape[0]
    n_main = sum(count // tn for _, count in group_rows)
    tiles_per_batch = seq // tm
    kern = functools.partial(_inproj_kernel, n_q=fox_w // tn, n_k=fox_w // tn, n_main=n_main)
    pipelined = (_nbytes((tm, d), F32) + _nbytes((tn, d), F32) + _nbytes((tm, tn), BF16)
                 + _nbytes((tm, tn), F32) + _nbytes((8, tm), F32))

    def w_rows(i, j):
        row, hi = jnp.int32(lora_row), n_main
        for first, count in reversed(group_rows):
            lo = hi - count // tn
            row = jnp.where(j < hi, first + tn * (j - lo), row)
            hi = lo
        return pl.multiple_of(row, 8), 0

    return pl.pallas_call(
        kern,
        grid=(t // tm, n_main + 1),
        in_specs=[
            pl.BlockSpec((tm, d), lambda i, j: (i, 0)),
            pl.BlockSpec((1, 1, d), lambda i, j: (i // tiles_per_batch, 0, 0)),
            pl.BlockSpec((1, 1, d), lambda i, j: (i // tiles_per_batch, 0, 0)),
            pl.BlockSpec((pl.Element(tn), pl.Element(d)), w_rows),
            pl.BlockSpec((pl.Element(FOX_HEADS), pl.Element(d)), lambda i, j: (fgate_row, 0)),
            pl.BlockSpec((1, FOX_HEAD_DIM), lambda i, j: (0, 0)),
            pl.BlockSpec((1, FOX_HEAD_DIM), lambda i, j: (0, 0)),
        ],
        out_specs=[
            pl.BlockSpec((tm, tn), lambda i, j: (i, jnp.minimum(j, n_main - 1))),
            pl.BlockSpec((tm, tn), lambda i, j: (i, 0)),
            pl.BlockSpec((FOX_HEADS, tm), lambda i, j: (0, i)),
        ],
        out_shape=[
            jax.ShapeDtypeStruct((t, n_main * tn), BF16),
            jax.ShapeDtypeStruct((t, tn), F32),
            jax.ShapeDtypeStruct((FOX_HEADS, t), F32),
        ],
        scratch_shapes=[pltpu.VMEM((tm, d), BF16)],
        compiler_params=pltpu.CompilerParams(
            dimension_semantics=("arbitrary", "arbitrary"),
            vmem_limit_bytes=_vmem_limit(pipelined, _nbytes((tm, d), BF16),
                                         3 * _nbytes((tm, d), F32)),
        ),
        name="inproj",
    )(x2d, scale1, shift1, w_in_t, w_in_t, qg, kg)


def _fox_prep_kernel(ft_ref, bias_ref, o_ref, *, blocks_per_seq):
    x = ft_ref[...] + bias_ref[...]
    lf = jnp.minimum(x, 0.0) - jnp.log1p(jnp.exp(-jnp.abs(x)))
    rows, width = lf.shape
    r = lax.broadcasted_iota(jnp.int32, (width, width), 0)
    c = lax.broadcasted_iota(jnp.int32, (width, width), 1)
    upper = (r <= c).astype(BF16)
    within = sum(_dot(p, upper) for p in _split3(lf))
    tot = jnp.broadcast_to(within[:, width - 1:width], (rows, width))
    rr = lax.broadcasted_iota(jnp.int32, (rows, rows), 0)
    cc = lax.broadcasted_iota(jnp.int32, (rows, rows), 1)
    prior = ((rr // blocks_per_seq == cc // blocks_per_seq) & (cc < rr)).astype(BF16)
    offset = sum(_dot(prior, p) for p in _split3(tot))
    o_ref[...] = within + offset


def _fox_prep(ft_rows, bias_rows, *, blocks_per_seq):
    return pl.pallas_call(
        functools.partial(_fox_prep_kernel, blocks_per_seq=blocks_per_seq),
        out_shape=jax.ShapeDtypeStruct(ft_rows.shape, F32),
        name="fox_prep",
    )(ft_rows, bias_rows)


def _fox_attn_kernel(q_ref, k_ref, v_ref, og_ref, cum_ref, o_ref, *, tq):
    seq = q_ref.shape[0]
    negcum = -cum_ref[0, 0]
    for qi in range(seq // tq):
        q0, kend = qi * tq, (qi + 1) * tq
        s = _dot_nt(q_ref[q0:kend, :], k_ref[0:kend, :]) + negcum[:, 0:kend]
        row = q0 + lax.broadcasted_iota(jnp.int32, (tq, kend), 0)
        col = lax.broadcasted_iota(jnp.int32, (tq, kend), 1)
        s = jnp.where(col <= row, s, -jnp.inf)
        m = jnp.max(s, axis=-1, keepdims=True)
        p = jnp.exp(s - m)
        l = jnp.sum(p, axis=-1, keepdims=True)
        o = _dot(p.astype(BF16), v_ref[0:kend, :]) / l
        gate = _sigmoid(og_ref[q0:kend, :].astype(F32))
        o_ref[q0:kend, :] = (o * gate).astype(BF16)


def _fox_attn(main, cum, *, batch, seq, tq=256):
    t = main.shape[0]
    hd = FOX_HEAD_DIM
    nh = FOX_HEADS
    blk = lambda off: pl.BlockSpec((seq, hd), lambda b, h: (b, off + h))
    pipelined = 5 * _nbytes((seq, hd), BF16) + _nbytes((8, seq), F32)
    return pl.pallas_call(
        functools.partial(_fox_attn_kernel, tq=tq),
        grid=(batch, nh),
        in_specs=[blk(0), blk(nh), blk(2 * nh), blk(3 * nh),
                  pl.BlockSpec((1, 1, 1, seq), lambda b, h: (h, b, 0, 0))],
        out_specs=pl.BlockSpec((seq, hd), lambda b, h: (b, h)),
        out_shape=jax.ShapeDtypeStruct((t, nh * hd), BF16),
        compiler_params=pltpu.CompilerParams(
            dimension_semantics=("arbitrary", "arbitrary"),
            vmem_limit_bytes=_vmem_limit(pipelined, 0, 6 * _nbytes((tq, seq), F32)),
        ),
        name="fox_attn",
    )(main, main, main, main, cum)


def _rwkv_prep_kernel(r_ref, k_ref, v_ref, small_ref, mu_r_ref, mu_k_ref, mu_v_ref, mu_s_ref,
                      w0_ref, w2_ref, a0_ref, a2_ref, g2_ref,
                      r_o, k_o, v_o, lw_o, a_o, g_o, prev_r, prev_k, prev_v, prev_s):
    prevs = (prev_r, prev_k, prev_v, prev_s)

    @pl.when(pl.program_id(1) == 0)
    def _():
        for p in prevs:
            p[...] = jnp.zeros_like(p)

    def shift_mix(x, prev_ref, mu):
        tm = x.shape[0]
        xp = pltpu.roll(x, 1, axis=0)
        row = lax.broadcasted_iota(jnp.int32, x.shape, 0)
        xp = jnp.where(row == 0, prev_ref[0:1, :], xp)
        prev_ref[0:1, :] = x[tm - 1:tm, :]
        return x + (xp - x) * mu

    r = shift_mix(r_ref[...].astype(F32), prev_r, mu_r_ref[...])
    k = shift_mix(k_ref[...].astype(F32), prev_k, mu_k_ref[...])
    v = shift_mix(v_ref[...].astype(F32), prev_v, mu_v_ref[...])
    sm = shift_mix(small_ref[...], prev_s, mu_s_ref[...])
    xw = sm[:, 0:LORA_PAD]
    xa = sm[:, DECAY_LORA:DECAY_LORA + LORA_PAD]
    xg = sm[:, DECAY_LORA + AAA_LORA:DECAY_LORA + AAA_LORA + GATE_LORA]

    z = w0_ref[...] + _dot(jnp.tanh(xw).astype(BF16), w2_ref[...])
    w_raw = -(jnp.maximum(-z, 0.0) + jnp.log1p(jnp.exp(-jnp.abs(z)))) - 0.5
    lw_o[...] = -jnp.exp(w_raw)
    a_o[...] = _sigmoid(a0_ref[...] + _dot(xa.astype(BF16), a2_ref[...])).astype(BF16)
    g_o[...] = _dot(_sigmoid(xg).astype(BF16), g2_ref[...]).astype(BF16)
    r_o[...] = r.astype(BF16)
    k_o[...] = k.astype(BF16)
    v_o[...] = v.astype(BF16)


def _rwkv_prep(main, small, mu_r, mu_k, mu_v, mu_s, w0, w2p, a0, a2p, g2, *, batch, seq, tm=256):
    t = main.shape[0]
    width = RWKV_HEADS * RWKV_HEAD_DIM
    nsm = small.shape[1]
    col0 = (4 * FOX_HEADS * FOX_HEAD_DIM) // width
    steps = seq // tm
    act = lambda off: pl.BlockSpec((tm, width), lambda b, s: (b * steps + s, off))
    row_spec = lambda n: pl.BlockSpec((1, n), lambda b, s: (0, 0))
    full = lambda a: pl.BlockSpec(a.shape, lambda b, s: (0, 0))
    out_spec = pl.BlockSpec((tm, width), lambda b, s: (b * steps + s, 0))
    out_dtypes = [BF16, BF16, BF16, F32, BF16, BF16]
    pipelined = (3 * _nbytes((tm, width), BF16) + _nbytes((tm, nsm), F32)
                 + sum(_nbytes((tm, width), dt) for dt in out_dtypes)
                 + _nbytes(w2p.shape, BF16) + _nbytes(a2p.shape, BF16) + _nbytes(g2.shape, BF16))
    return pl.pallas_call(
        _rwkv_prep_kernel,
        grid=(batch, steps),
        in_specs=[act(col0), act(col0 + 1), act(col0 + 2),
                  pl.BlockSpec((tm, nsm), lambda b, s: (b * steps + s, 0)),
                  row_spec(width), row_spec(width), row_spec(width), row_spec(nsm),
                  row_spec(width), full(w2p), row_spec(width), full(a2p), full(g2)],
        out_specs=[out_spec] * 6,
        out_shape=[jax.ShapeDtypeStruct((t, width), dt) for dt in out_dtypes],
        scratch_shapes=[pltpu.VMEM((8, width), F32)] * 3 + [pltpu.VMEM((8, nsm), F32)],
        compiler_params=pltpu.CompilerParams(
            dimension_semantics=("arbitrary", "arbitrary"),
            vmem_limit_bytes=_vmem_limit(pipelined, 0, 12 * _nbytes((tm, width), F32)),
        ),
        name="rwkv_prep",
    )(main, main, main, small, mu_r, mu_k, mu_v, mu_s, w0, w2p, a0, a2p, g2)


def _rwkv_rec_kernel(r_ref, k_ref, v_ref, lw_ref, a_ref, g_ref,
                     kk_ref, ka_ref, rk_ref, lng_ref, lnb_ref, *rest, n_tiles, n_seq, n_cast):
    cast_src, y_ref, cast_dst, state = rest[:n_cast], rest[n_cast], rest[n_cast + 1:-1], rest[-1]
    for src, dst in zip(cast_src, cast_dst):
        dst[...] = src[...].astype(BF16)

    @pl.when(pl.program_id(1) == 0)
    def _():
        state[...] = jnp.zeros_like(state)

    ch = y_ref.shape[1]
    hd = RWKV_HEAD_DIM
    lane = lax.broadcasted_iota(jnp.int32, (1, LANES), 1)
    m0 = lane < hd
    n2 = HEADS_PER_TILE * ch
    rr = lax.broadcasted_iota(jnp.int32, (n2, n2), 0)
    cc = lax.broadcasted_iota(jnp.int32, (n2, n2), 1)
    same = (rr // ch) == (cc // ch)
    strict = same & (rr > cc)
    incl = same & (rr >= cc)
    lr = lax.broadcasted_iota(jnp.int32, (ch, ch), 0)
    lc = lax.broadcasted_iota(jnp.int32, (ch, ch), 1)
    lower = (lc <= lr).astype(BF16)

    def head_sum(x):
        s0 = jnp.sum(jnp.where(m0, x, 0.0), axis=-1, keepdims=True)
        s1 = jnp.sum(jnp.where(m0, 0.0, x), axis=-1, keepdims=True)
        return jnp.where(m0, s0, s1)

    def stack(x):
        return jnp.concatenate([jnp.where(m0, x, 0.0), jnp.where(m0, 0.0, x)], axis=0)

    sls = [slice(p * LANES, (p + 1) * LANES) for p in range(n_tiles)]
    tiles = range(n_seq * n_tiles)

    def each(fn, *cols):
        return [fn(*args) for args in zip(*cols)]

    r, k, v, lw, a, g = ([ref[q, :, sl].astype(F32) for q in range(n_seq) for sl in sls]
                         for ref in (r_ref, k_ref, v_ref, lw_ref, a_ref, g_ref))
    kkw, kaw, rkw, lng, lnb = ([ref[:, sl] for _ in range(n_seq) for sl in sls]
                               for ref in (kk_ref, ka_ref, rk_ref, lng_ref, lnb_ref))

    kk = each(lambda k_, w_: k_ * w_, k, kkw)
    kk = each(lambda x: x / jnp.maximum(jnp.sqrt(head_sum(x * x)), 1e-12), kk)
    kr = each(lambda k_, a_, w_: k_ * (1.0 + (a_ - 1.0) * w_), k, a, kaw)
    cum = each(lambda x: sum(_dot(lower, part) for part in _split3(x)[:2]), lw)
    gam = each(jnp.exp, cum)
    ginv = each(lambda x: jnp.exp(-x), cum)
    g_end = each(lambda x: x[ch - 1:ch, :], gam)
    a_t = each(lambda kk_, c_, lw_: -kk_ * jnp.exp(c_ - lw_), kk, cum, lw)
    r_t = each(lambda r_, g_: r_ * g_, r, gam)
    b_h = each(lambda kk_, a_, gi_: kk_ * a_ * gi_, kk, a, ginv)
    k_h = each(lambda kr_, gi_: kr_ * gi_, kr, ginv)

    stack2 = lambda x, y: jnp.concatenate([stack(x), stack(y)], axis=0).astype(BF16)
    lhs = each(stack2, a_t, r_t)
    rhs = each(stack2, b_h, k_h)
    sc = each(_dot_nt, lhs, rhs)
    ab = each(lambda s: jnp.where(strict, s[0:n2, 0:n2], 0.0).astype(BF16), sc)
    ak = each(lambda s: jnp.where(strict, s[0:n2, n2:2 * n2], 0.0).astype(BF16), sc)
    rbk = each(lambda s: jnp.where(jnp.concatenate([incl, incl], axis=1), s[n2:2 * n2, :], 0.0).astype(BF16), sc)

    st = [state[p] for p in tiles]
    ah = each(lambda l_, s_: _dot_nt(l_, s_.astype(BF16)), lhs, st)
    vs = each(lambda v_: stack(v_).astype(BF16), v)
    x = each(lambda ah_, ak_, vs_: ah_[0:n2] + _dot(ak_, vs_), ah, ak, vs)
    pw = ab
    n_lvl = ch.bit_length() - 1
    for lvl in range(n_lvl):
        x = each(lambda x_, p_: x_ + _dot(p_, x_.astype(BF16)), x, pw)
        if lvl + 1 < n_lvl:
            pw = each(lambda p_: _dot(p_, p_).astype(BF16), pw)
    uv = each(lambda x_, vs_: jnp.concatenate([x_.astype(BF16), vs_], axis=0), x, vs)
    ys = each(lambda ah_, m_, uv_: ah_[n2:2 * n2] + _dot(m_, uv_), ah, rbk, uv)
    y = each(lambda ys_: ys_[0:ch] + ys_[ch:n2], ys)

    bk = each(lambda b_, k_, ge_: stack2(b_ * ge_, k_ * ge_), b_h, k_h, g_end)
    new_st = each(lambda s_, ge_, uv_, bk_: s_ * ge_ + _dot_tn(uv_, bk_), st, g_end, uv, bk)
    for p in tiles:
        state[p] = new_st[p]

    mu = each(lambda y_: head_sum(y_) * (1.0 / hd), y)
    yc = each(lambda y_, m_: y_ - m_, y, mu)
    var = each(lambda c_: head_sum(c_ * c_) * (1.0 / hd), yc)
    yn = each(lambda c_, v_, g_, b_: c_ * lax.rsqrt(v_ + RWKV_LNX_EPS) * g_ + b_, yc, var, lng, lnb)
    bonus = each(lambda r_, kr_, w_, v_: head_sum(r_ * kr_ * w_) * v_, r, kr, rkw, v)
    for i in tiles:
        y_ref[i // n_tiles, :, sls[i % n_tiles]] = ((yn[i] + bonus[i]) * g[i]).astype(BF16)


def _slab_spec(w, n_steps, step_of):
    rep = 1
    while w.shape[0] % (n_steps // rep * 16):
        rep *= 2
        assert rep <= n_steps
    return pl.BlockSpec((w.shape[0] // (n_steps // rep), w.shape[1]),
                        lambda *idx: (step_of(*idx) // rep, 0))


def _rwkv_rec(r, k, v, lw, a, g, kk, ka, rk, lng, lnb, cast_weights, *, batch, seq, n_seq=2):
    t, width = r.shape
    ch = RWKV_CHUNK
    steps = seq // ch
    n_tiles = width // LANES
    assert batch % n_seq == 0
    groups = batch // n_seq
    grouped = (groups, n_seq, seq, width)
    act = pl.BlockSpec((None, n_seq, ch, width), lambda b, c: (b, 0, c, 0))
    row = pl.BlockSpec((1, width), lambda b, c: (0, 0))
    slabs = [_slab_spec(w, groups * steps, lambda b, c: b * steps + c) for w in cast_weights]
    pipelined = n_seq * (sum(_nbytes((ch, width), x.dtype) for x in (r, k, v, lw, a, g))
                         + _nbytes((ch, width), BF16))
    pipelined += sum(_nbytes(s.block_shape, F32) + _nbytes(s.block_shape, BF16) for s in slabs)
    state_shape = (n_seq * n_tiles, LANES, LANES)
    y, *copies = pl.pallas_call(
        functools.partial(_rwkv_rec_kernel, n_tiles=n_tiles, n_seq=n_seq, n_cast=len(slabs)),
        grid=(groups, steps),
        in_specs=[act] * 6 + [row] * 5 + slabs,
        out_specs=[act] + slabs,
        out_shape=[jax.ShapeDtypeStruct(grouped, BF16)]
        + [jax.ShapeDtypeStruct(w.shape, BF16) for w in cast_weights],
        scratch_shapes=[pltpu.VMEM(state_shape, F32)],
        compiler_params=pltpu.CompilerParams(
            dimension_semantics=("arbitrary", "arbitrary"),
            vmem_limit_bytes=_vmem_limit(pipelined, _nbytes(state_shape, F32), 24 << 20),
        ),
        name="rwkv_rec",
    )(*(x.reshape(grouped) for x in (r, k, v, lw, a, g)), kk, ka, rk, lng, lnb, *cast_weights)
    return y.reshape(t, width), copies


def _mixout_kernel(o_ref, y_ref, ga0_ref, ga1_ref, gb0_ref, gb1_ref, wa_ref, wb_ref, wo_ref, x_ref,
                   gate_ref, lng_ref, lnb_ref, scale_ref, shift_ref, x1_ref, h2_ref, *, n_sub):
    tm, d = x_ref.shape
    rows = tm // n_sub
    half = d // 2
    sl = [slice(s * rows, (s + 1) * rows) for s in range(n_sub)]

    def branches(s):
        return _dot(o_ref[sl[s], :], wa_ref[...]), _dot(y_ref[sl[s], :], wb_ref[...])

    def mixed(s, ua, ub):
        parts = []
        for c, (ga, gb) in enumerate(((ga0_ref, gb0_ref), (ga1_ref, gb1_ref))):
            cs = slice(c * half, (c + 1) * half)
            parts.append(_sigmoid(ga[sl[s], :].astype(F32)) * ua[:, cs]
                         + _sigmoid(gb[sl[s], :].astype(F32)) * ub[:, cs])
        return _dot(jnp.concatenate(parts, axis=1).astype(BF16), wo_ref[...])

    def norms(s, mix):
        z = DEEPNORM_ALPHA * x_ref[sl[s], :] + gate_ref[0] * mix
        x1 = _layernorm(z) * lng_ref[...] + lnb_ref[...]
        x1_ref[sl[s], :] = x1
        h2_ref[sl[s], :] = (_layernorm(x1) * (1.0 + scale_ref[0]) + shift_ref[0]).astype(BF16)

    u = branches(0)
    mix_prev = None
    for s in range(n_sub):
        u_next = branches(s + 1) if s + 1 < n_sub else None
        mix = mixed(s, *u)
        if mix_prev is not None:
            norms(s - 1, mix_prev)
        u, mix_prev = u_next, mix
    norms(n_sub - 1, mix_prev)


def _mixout(o, y, main, wa, wb, wo, x2d, gate1, ln_g, ln_b, scale2, shift2, *, seq, tm=256, n_sub=2):
    t, d = x2d.shape
    kdim = o.shape[1]
    half = d // 2
    gate_blk0 = (main.shape[1] - 2 * d) // half
    tiles_per_batch = seq // tm
    mod = pl.BlockSpec((1, 1, d), lambda i: (i // tiles_per_batch, 0, 0))
    row = pl.BlockSpec((1, d), lambda i: (0, 0))
    gate = lambda c: pl.BlockSpec((tm, half), lambda i: (i, gate_blk0 + c))
    resident = lambda w: pl.BlockSpec(w.shape, lambda i: (0, 0), pipeline_mode=pl.Buffered(1))
    pipelined = (2 * _nbytes((tm, kdim), BF16) + 4 * _nbytes((tm, half), BF16)
                 + 2 * _nbytes((tm, d), F32) + _nbytes((tm, d), BF16))
    weights = _nbytes(wa.shape, BF16) + _nbytes(wb.shape, BF16) + _nbytes(wo.shape, BF16)
    return pl.pallas_call(
        functools.partial(_mixout_kernel, n_sub=n_sub),
        grid=(t // tm,),
        in_specs=[
            pl.BlockSpec((tm, kdim), lambda i: (i, 0)),
            pl.BlockSpec((tm, kdim), lambda i: (i, 0)),
            gate(0), gate(1), gate(2), gate(3),
            resident(wa), resident(wb), resident(wo),
            pl.BlockSpec((tm, d), lambda i: (i, 0)),
            mod, row, row, mod, mod,
        ],
        out_specs=[pl.BlockSpec((tm, d), lambda i: (i, 0)),
                   pl.BlockSpec((tm, d), lambda i: (i, 0))],
        out_shape=[jax.ShapeDtypeStruct((t, d), F32), jax.ShapeDtypeStruct((t, d), BF16)],
        compiler_params=pltpu.CompilerParams(
            dimension_semantics=("arbitrary",),
            vmem_limit_bytes=_vmem_limit(pipelined, weights, 8 * _nbytes((tm, d), F32)),
        ),
        name="mixout",
    )(o, y, main, main, main, main, wa, wb, wo, x2d, gate1, ln_g, ln_b, scale2, shift2)


def _ffn_kernel(h_ref, wg_ref, wu_ref, wd_ref, x1_ref, gate_ref, lng_ref, lnb_ref, o_ref, acc):
    f = pl.program_id(1)
    @pl.when(f == 0)
    def _():
        acc[...] = jnp.zeros_like(acc)

    h = h_ref[...]
    g = _dot(h, wg_ref[...])
    u = _dot(h, wu_ref[...])
    act = (g * _sigmoid(g) * u).astype(BF16)
    acc[...] += _dot(act, wd_ref[...])

    @pl.when(f == pl.num_programs(1) - 1)
    def _():
        z = DEEPNORM_ALPHA * x1_ref[...] + gate_ref[0] * acc[...]
        o_ref[...] = _layernorm(z) * lng_ref[...] + lnb_ref[...]


def _ffn(h2, w_gu, w_down, x1, gate2, ln_g, ln_b, *, seq, tm=512, tf=512):
    t, d = x1.shape
    d_ff = w_down.shape[0]
    nf = d_ff // tf
    tiles_per_batch = seq // tm
    mod = pl.BlockSpec((1, 1, d), lambda i, f: (i // tiles_per_batch, 0, 0))
    row = pl.BlockSpec((1, d), lambda i, f: (0, 0))
    pipelined = (_nbytes((tm, d), BF16) + 3 * _nbytes((d, tf), BF16) + 2 * _nbytes((tm, d), F32))
    return pl.pallas_call(
        _ffn_kernel,
        grid=(t // tm, nf),
        in_specs=[
            pl.BlockSpec((tm, d), lambda i, f: (i, 0)),
            pl.BlockSpec((d, tf), lambda i, f: (0, f)),
            pl.BlockSpec((d, tf), lambda i, f: (0, nf + f)),
            pl.BlockSpec((tf, d), lambda i, f: (f, 0)),
            pl.BlockSpec((tm, d), lambda i, f: (i, 0)),
            mod, row, row,
        ],
        out_specs=pl.BlockSpec((tm, d), lambda i, f: (i, 0)),
        out_shape=jax.ShapeDtypeStruct((t, d), F32),
        scratch_shapes=[pltpu.VMEM((tm, d), F32)],
        compiler_params=pltpu.CompilerParams(
            dimension_semantics=("arbitrary", "arbitrary"),
            vmem_limit_bytes=_vmem_limit(pipelined, _nbytes((tm, d), F32),
                                         2 * _nbytes((tm, d), F32) + 4 * _nbytes((tm, tf), F32)),
        ),
        name="ffn",
    )(h2, w_gu, w_gu, w_down, x1, gate2, ln_g, ln_b)


def _pad_rows(w, rows):
    return jnp.zeros((rows, w.shape[1]), w.dtype).at[:w.shape[0]].set(w)


def kernel(x, c, w_ada, b_ada, w_in, b_fgate, q_norm_g, k_norm_g, rwkv_mu, rwkv_w0, rwkv_w2, rwkv_a0, rwkv_a2, rwkv_g2, rwkv_k_k, rwkv_k_a, rwkv_r_k, rwkv_lnx_g, rwkv_lnx_b, w_branch_a, w_branch_b, w_out, ln1_g, ln1_b, w_ffn_gu, w_ffn_down, ln2_g, ln2_b):
    batch, seq, d = x.shape
    t = batch * seq
    fox_w = FOX_HEADS * FOX_HEAD_DIM
    rw_w = RWKV_HEADS * RWKV_HEAD_DIM
    fox_cols = 4 * fox_w + FOX_HEADS
    lora0 = fox_cols + 3 * rw_w
    gate0 = lora0 + DECAY_LORA + AAA_LORA + GATE_LORA
    assert seq % RWKV_CHUNK == 0 and w_in.shape[1] == gate0 + 2 * d

    lora_w = gate0 - lora0
    mu_r = rwkv_mu[None, 0:rw_w]
    mu_k = rwkv_mu[None, rw_w:2 * rw_w]
    mu_v = rwkv_mu[None, 2 * rw_w:3 * rw_w]
    mu_s = jnp.zeros((1, INPROJ_TN), rwkv_mu.dtype).at[0, :lora_w].set(rwkv_mu[3 * rw_w:])
    w2p = _pad_rows(rwkv_w2, LORA_PAD).astype(BF16)
    a2p = _pad_rows(rwkv_a2, LORA_PAD).astype(BF16)
    g2 = rwkv_g2.astype(BF16)
    row = lambda v: v.reshape(1, -1)

    c_pad = jnp.zeros((8, d), c.dtype).at[:batch].set(c)
    mod = _ada(c_pad, w_ada, row(b_ada))[:batch].reshape(batch, 6, 1, d)
    shift1, scale1, gate1, shift2, scale2, gate2 = (mod[:, i] for i in range(6))

    x2d = x.reshape(t, d)
    main, small, ft = _inproj(x2d, scale1, shift1, w_in.T, row(q_norm_g), row(k_norm_g), seq=seq,
                              group_rows=[(0, 4 * fox_w), (fox_cols, 3 * rw_w), (gate0, 2 * d)],
                              lora_row=lora0, fgate_row=4 * fox_w, tn=INPROJ_TN)

    blocks_per_seq = seq // LANES
    ft_rows = ft.reshape(FOX_HEADS * batch * blocks_per_seq, LANES)
    bias_rows = jnp.broadcast_to(b_fgate[:, None, None], (FOX_HEADS, batch * blocks_per_seq, LANES))
    cum = _fox_prep(ft_rows, bias_rows.reshape(ft_rows.shape), blocks_per_seq=blocks_per_seq)
    cum = cum.reshape(FOX_HEADS, batch, 1, seq)
    o = _fox_attn(main, cum, batch=batch, seq=seq)

    r, k, v, lw, a, g = _rwkv_prep(main, small, mu_r, mu_k, mu_v, mu_s, row(rwkv_w0), w2p,
                                   row(rwkv_a0), a2p, g2, batch=batch, seq=seq)
    y, (wa, wb, wo, wgu, wdn) = _rwkv_rec(
        r, k, v, lw, a, g, row(rwkv_k_k), row(rwkv_k_a), row(rwkv_r_k), row(rwkv_lnx_g), row(rwkv_lnx_b),
        [w_branch_a, w_branch_b, w_out, w_ffn_gu, w_ffn_down], batch=batch, seq=seq)

    x1, h2 = _mixout(o, y, main, wa, wb, wo, x2d, gate1, row(ln1_g), row(ln1_b), scale2, shift2, seq=seq)

    out = _ffn(h2, wgu, wdn, x1, gate2, row(ln2_g), row(ln2_b), seq=seq)
    return out.reshape(batch, seq, d)
```

```python
import functools

import jax
import jax.numpy as jnp
from jax import lax
from jax.experimental import pallas as pl
from jax.experimental.pallas import tpu as pltpu

F32 = jnp.float32
BF16 = jnp.bfloat16

LANES = 128
VMEM_CAP_BYTES = 60000 * 1024

FOX_HEADS = 8
FOX_HEAD_DIM = 128
RWKV_HEADS = 16
RWKV_HEAD_DIM = 64
DECAY_LORA = 96
AAA_LORA = 96
GATE_LORA = 256
LORA_PAD = 128
INPROJ_TN = 512
RWKV_LNX_EPS = 64e-5
DEPTH = 1
DEEPNORM_ALPHA = (2.0 * DEPTH) ** 0.25
LN_EPS = 1e-5
RMS_EPS = 1e-6
RWKV_CHUNK = 64
HEADS_PER_TILE = LANES // RWKV_HEAD_DIM


def _vmem_limit(pipelined_bytes, resident_bytes=0, temp_bytes=0):
    need = 2 * pipelined_bytes + resident_bytes + temp_bytes + (2 << 20)
    return int(min(VMEM_CAP_BYTES, need))


def _nbytes(shape, dtype):
    n = 1
    for s in shape:
        n *= s
    return n * jnp.dtype(dtype).itemsize


def _layernorm(x):
    mu = jnp.mean(x, axis=-1, keepdims=True)
    xc = x - mu
    var = jnp.mean(xc * xc, axis=-1, keepdims=True)
    return xc * lax.rsqrt(var + LN_EPS)


def _sigmoid(x):
    return 1.0 / (1.0 + jnp.exp(-x))


def _dot(a, b):
    return jnp.dot(a, b, preferred_element_type=F32)


def _dot_nt(a, b):
    return lax.dot_general(a, b, (((1,), (1,)), ((), ())), preferred_element_type=F32)


def _dot_tn(a, b):
    return lax.dot_general(a, b, (((0,), (0,)), ((), ())), preferred_element_type=F32)


def _split3(x):
    hi = x.astype(BF16)
    r1 = x - hi.astype(F32)
    mid = r1.astype(BF16)
    lo = (r1 - mid.astype(F32)).astype(BF16)
    return hi, mid, lo


def _ada_kernel(c_ref, w_ref, b_ref, o_ref):
    c = c_ref[...]
    s = (c * _sigmoid(c)).astype(BF16)
    o_ref[...] = _dot(s, w_ref[...].astype(BF16)) + b_ref[...]


def _ada(c_pad, w_ada, b_ada, *, tn=1024):
    rows, d = c_pad.shape
    n = w_ada.shape[1]
    return pl.pallas_call(
        _ada_kernel,
        grid=(n // tn,),
        in_specs=[
            pl.BlockSpec((rows, d), lambda j: (0, 0)),
            pl.BlockSpec((d, tn), lambda j: (0, j)),
            pl.BlockSpec((1, tn), lambda j: (0, j)),
        ],
        out_specs=pl.BlockSpec((rows, tn), lambda j: (0, j)),
        out_shape=jax.ShapeDtypeStruct((rows, n), F32),
        compiler_params=pltpu.CompilerParams(
            dimension_semantics=("arbitrary",),
            vmem_limit_bytes=_vmem_limit(_nbytes((d, tn), F32), temp_bytes=_nbytes((d, tn), BF16)),
        ),
        name="ada",
    )(c_pad, w_ada, b_ada)


def _inproj_kernel(x_ref, scale_ref, shift_ref, wt_ref, wft_ref, qg_ref, kg_ref,
                   main_ref, small_ref, ft_ref, h_scr, *, n_q, n_k, n_main):
    j = pl.program_id(1)
    tn = main_ref.shape[1]

    @pl.when(j == 0)
    def _():
        h = _layernorm(x_ref[...]) * (1.0 + scale_ref[0]) + shift_ref[0]
        h_scr[...] = h.astype(BF16)

    def proj():
        return _dot_nt(h_scr[...], wt_ref[...].astype(BF16))

    @pl.when(j < n_q + n_k)
    def _():
        acc = proj()
        gain = jnp.where(j < n_q, qg_ref[...] * (FOX_HEAD_DIM ** -0.5), kg_ref[...])
        for hh in range(tn // FOX_HEAD_DIM):
            sl = slice(hh * FOX_HEAD_DIM, (hh + 1) * FOX_HEAD_DIM)
            a = acc[:, sl]
            ms = jnp.mean(a * a, axis=-1, keepdims=True)
            main_ref[:, sl] = (a * lax.rsqrt(ms + RMS_EPS) * gain).astype(BF16)

    @pl.when((j >= n_q + n_k) & (j < n_main))
    def _():
        main_ref[...] = proj().astype(BF16)

    @pl.when(j == n_main)
    def _():
        small_ref[...] = proj()
        ft_ref[...] = _dot_nt(wft_ref[...].astype(BF16), h_scr[...])


def _inproj(x2d, scale1, shift1, w_in_t, qg, kg, *, seq, group_rows, lora_row, fgate_row, tm=1024, tn=512):
    t, d = x2d.shape
    fox_w = FOX_HEADS * FOX_HEAD_DIM
    assert all(first % 8 == 0 and count % tn == 0 for first, count in group_rows)
    assert lora_row % 8 == 0 and fgate_row % 8 == 0 and lora_row + tn <= w_in_t.shape[0]
    n_main = sum(count // tn for _, count in group_rows)
    tiles_per_batch = seq // tm
    kern = functools.partial(_inproj_kernel, n_q=fox_w // tn, n_k=fox_w // tn, n_main=n_main)
    pipelined = (_nbytes((tm, d), F32) + _nbytes((tn, d), F32) + _nbytes((tm, tn), BF16)
                 + _nbytes((tm, tn), F32) + _nbytes((8, tm), F32))

    def w_rows(i, j):
        row, hi = jnp.int32(lora_row), n_main
        for first, count in reversed(group_rows):
            lo = hi - count // tn
            row = jnp.where(j < hi, first + tn * (j - lo), row)
            hi = lo
        return pl.multiple_of(row, 8), 0

    return pl.pallas_call(
        kern,
        grid=(t // tm, n_main + 1),
        in_specs=[
            pl.BlockSpec((tm, d), lambda i, j: (i, 0)),
            pl.BlockSpec((1, 1, d), lambda i, j: (i // tiles_per_batch, 0, 0)),
            pl.BlockSpec((1, 1, d), lambda i, j: (i // tiles_per_batch, 0, 0)),
            pl.BlockSpec((pl.Element(tn), pl.Element(d)), w_rows),
            pl.BlockSpec((pl.Element(FOX_HEADS), pl.Element(d)), lambda i, j: (fgate_row, 0)),
            pl.BlockSpec((1, FOX_HEAD_DIM), lambda i, j: (0, 0)),
            pl.BlockSpec((1, FOX_HEAD_DIM), lambda i, j: (0, 0)),
        ],
        out_specs=[
            pl.BlockSpec((tm, tn), lambda i, j: (i, jnp.minimum(j, n_main - 1))),
            pl.BlockSpec((tm, tn), lambda i, j: (i, 0)),
            pl.BlockSpec((FOX_HEADS, tm), lambda i, j: (0, i)),
        ],
        out_shape=[
            jax.ShapeDtypeStruct((t, n_main * tn), BF16),
            jax.ShapeDtypeStruct((t, tn), F32),
            jax.ShapeDtypeStruct((FOX_HEADS, t), F32),
        ],
        scratch_shapes=[pltpu.VMEM((tm, d), BF16)],
        compiler_params=pltpu.CompilerParams(
            dimension_semantics=("arbitrary", "arbitrary"),
            vmem_limit_bytes=_vmem_limit(pipelined, _nbytes((tm, d), BF16),
                                         3 * _nbytes((tm, d), F32)),
        ),
        name="inproj",
    )(x2d, scale1, shift1, w_in_t, w_in_t, qg, kg)


def _fox_prep_kernel(ft_ref, bias_ref, o_ref, *, blocks_per_seq):
    x = ft_ref[...] + bias_ref[...]
    lf = jnp.minimum(x, 0.0) - jnp.log1p(jnp.exp(-jnp.abs(x)))
    rows, width = lf.shape
    r = lax.broadcasted_iota(jnp.int32, (width, width), 0)
    c = lax.broadcasted_iota(jnp.int32, (width, width), 1)
    upper = (r <= c).astype(BF16)
    within = sum(_dot(p, upper) for p in _split3(lf))
    tot = jnp.broadcast_to(within[:, width - 1:width], (rows, width))
    rr = lax.broadcasted_iota(jnp.int32, (rows, rows), 0)
    cc = lax.broadcasted_iota(jnp.int32, (rows, rows), 1)
    prior = ((rr // blocks_per_seq == cc // blocks_per_seq) & (cc < rr)).astype(BF16)
    offset = sum(_dot(prior, p) for p in _split3(tot))
    o_ref[...] = within + offset


def _fox_prep(ft_rows, bias_rows, *, blocks_per_seq):
    return pl.pallas_call(
        functools.partial(_fox_prep_kernel, blocks_per_seq=blocks_per_seq),
        out_shape=jax.ShapeDtypeStruct(ft_rows.shape, F32),
        name="fox_prep",
    )(ft_rows, bias_rows)


def _fox_attn_kernel(q_ref, k_ref, v_ref, og_ref, cum_ref, o_ref, *, tq):
    seq = q_ref.shape[0]
    negcum = -cum_ref[0, 0]
    for qi in range(seq // tq):
        q0, kend = qi * tq, (qi + 1) * tq
        s = _dot_nt(q_ref[q0:kend, :], k_ref[0:kend, :]) + negcum[:, 0:kend]
        row = q0 + lax.broadcasted_iota(jnp.int32, (tq, kend), 0)
        col = lax.broadcasted_iota(jnp.int32, (tq, kend), 1)
        s = jnp.where(col <= row, s, -jnp.inf)
        m = jnp.max(s, axis=-1, keepdims=True)
        p = jnp.exp(s - m)
        l = jnp.sum(p, axis=-1, keepdims=True)
        o = _dot(p.astype(BF16), v_ref[0:kend, :]) / l
        gate = _sigmoid(og_ref[q0:kend, :].astype(F32))
        o_ref[q0:kend, :] = (o * gate).astype(BF16)


def _fox_attn(main, cum, *, batch, seq, tq=256):
    t = main.shape[0]
    hd = FOX_HEAD_DIM
    nh = FOX_HEADS
    blk = lambda off: pl.BlockSpec((seq, hd), lambda b, h: (b, off + h))
    pipelined = 5 * _nbytes((seq, hd), BF16) + _nbytes((8, seq), F32)
    return pl.pallas_call(
        functools.partial(_fox_attn_kernel, tq=tq),
        grid=(batch, nh),
        in_specs=[blk(0), blk(nh), blk(2 * nh), blk(3 * nh),
                  pl.BlockSpec((1, 1, 1, seq), lambda b, h: (h, b, 0, 0))],
        out_specs=pl.BlockSpec((seq, hd), lambda b, h: (b, h)),
        out_shape=jax.ShapeDtypeStruct((t, nh * hd), BF16),
        compiler_params=pltpu.CompilerParams(
            dimension_semantics=("arbitrary", "arbitrary"),
            vmem_limit_bytes=_vmem_limit(pipelined, 0, 6 * _nbytes((tq, seq), F32)),
        ),
        name="fox_attn",
    )(main, main, main, main, cum)


def _rwkv_inputs(raw_refs, small_ref, mu_refs, mu_s_ref, w0_ref, w2_ref, a0_ref, a2_ref, g2_ref,
                 prev_refs, prev_s, n_seq):
    def shift_mix(x, prev_ref, q, mu):
        rows = x.shape[0]
        xp = pltpu.roll(x, 1, axis=0)
        row = lax.broadcasted_iota(jnp.int32, x.shape, 0)
        xp = jnp.where(row == 0, prev_ref[q, 0:1, :], xp)
        prev_ref[q, 0:1, :] = x[rows - 1:rows, :]
        return x + (xp - x) * mu

    seqs = range(n_seq)
    r, k, v = ([shift_mix(ref[q].astype(F32), prev, q, mu[...]) for q in seqs]
               for ref, prev, mu in zip(raw_refs, prev_refs, mu_refs))
    sm = jnp.concatenate([shift_mix(small_ref[q], prev_s, q, mu_s_ref[...]) for q in seqs], axis=0)
    xw = sm[:, 0:LORA_PAD]
    xa = sm[:, DECAY_LORA:DECAY_LORA + LORA_PAD]
    xg = sm[:, DECAY_LORA + AAA_LORA:DECAY_LORA + AAA_LORA + GATE_LORA]

    z = w0_ref[...] + _dot(jnp.tanh(xw).astype(BF16), w2_ref[...])
    w_raw = -(jnp.maximum(-z, 0.0) + jnp.log1p(jnp.exp(-jnp.abs(z)))) - 0.5
    lw = -jnp.exp(w_raw)
    a = _sigmoid(a0_ref[...] + _dot(xa.astype(BF16), a2_ref[...]))
    g = _dot(_sigmoid(xg).astype(BF16), g2_ref[...])
    ch = lw.shape[0] // n_seq
    per_seq = lambda x: [x[q * ch:(q + 1) * ch, :] for q in seqs]
    return r, k, v, per_seq(lw), per_seq(a), per_seq(g)


def _rwkv_kernel(r_ref, k_ref, v_ref, small_ref, mu_r_ref, mu_k_ref, mu_v_ref, mu_s_ref,
                 w0_ref, w2_ref, a0_ref, a2_ref, g2_ref,
                 kk_ref, ka_ref, rk_ref, lng_ref, lnb_ref, *rest, n_tiles, n_seq, n_cast):
    cast_src, y_ref, cast_dst = rest[:n_cast], rest[n_cast], rest[n_cast + 1:2 * n_cast + 1]
    state, prev_r, prev_k, prev_v, prev_s = rest[2 * n_cast + 1:]
    for src, dst in zip(cast_src, cast_dst):
        dst[...] = src[...].astype(BF16)

    @pl.when(pl.program_id(1) == 0)
    def _():
        for ref in (state, prev_r, prev_k, prev_v, prev_s):
            ref[...] = jnp.zeros_like(ref)

    ch = y_ref.shape[1]
    hd = RWKV_HEAD_DIM
    lane = lax.broadcasted_iota(jnp.int32, (1, LANES), 1)
    m0 = lane < hd
    n2 = HEADS_PER_TILE * ch
    rr = lax.broadcasted_iota(jnp.int32, (n2, n2), 0)
    cc = lax.broadcasted_iota(jnp.int32, (n2, n2), 1)
    same = (rr // ch) == (cc // ch)
    strict = same & (rr > cc)
    incl = same & (rr >= cc)
    lr = lax.broadcasted_iota(jnp.int32, (ch, ch), 0)
    lc = lax.broadcasted_iota(jnp.int32, (ch, ch), 1)
    lower = (lc <= lr).astype(BF16)

    def head_sum(x):
        s0 = jnp.sum(jnp.where(m0, x, 0.0), axis=-1, keepdims=True)
        s1 = jnp.sum(jnp.where(m0, 0.0, x), axis=-1, keepdims=True)
        return jnp.where(m0, s0, s1)

    def stack(x):
        return jnp.concatenate([jnp.where(m0, x, 0.0), jnp.where(m0, 0.0, x)], axis=0)

    sls = [slice(p * LANES, (p + 1) * LANES) for p in range(n_tiles)]
    tiles = range(n_seq * n_tiles)

    def each(fn, *cols):
        return [fn(*args) for args in zip(*cols)]

    per_chain = lambda ref: [ref[:, sl] for _ in range(n_seq) for sl in sls]
    kkw, kaw, rkw, lng, lnb = (per_chain(ref) for ref in (kk_ref, ka_ref, rk_ref, lng_ref, lnb_ref))

    full = _rwkv_inputs((r_ref, k_ref, v_ref), small_ref, (mu_r_ref, mu_k_ref, mu_v_ref), mu_s_ref,
                        w0_ref, w2_ref, a0_ref, a2_ref, g2_ref, (prev_r, prev_k, prev_v), prev_s, n_seq)
    r, k, v, lw, a, g = ([x[q][:, sl] for q in range(n_seq) for sl in sls] for x in full)

    kk = each(lambda k_, w_: k_ * w_, k, kkw)
    kk = each(lambda x: x / jnp.maximum(jnp.sqrt(head_sum(x * x)), 1e-12), kk)
    kr = each(lambda k_, a_, w_: k_ * (1.0 + (a_ - 1.0) * w_), k, a, kaw)
    cum = each(lambda x: sum(_dot(lower, part) for part in _split3(x)[:2]), lw)
    gam = each(jnp.exp, cum)
    ginv = each(lambda x: jnp.exp(-x), cum)
    g_end = each(lambda x: x[ch - 1:ch, :], gam)
    a_t = each(lambda kk_, c_, lw_: -kk_ * jnp.exp(c_ - lw_), kk, cum, lw)
    r_t = each(lambda r_, g_: r_ * g_, r, gam)
    b_h = each(lambda kk_, a_, gi_: kk_ * a_ * gi_, kk, a, ginv)
    k_h = each(lambda kr_, gi_: kr_ * gi_, kr, ginv)

    stack2 = lambda x, y: jnp.concatenate([stack(x), stack(y)], axis=0).astype(BF16)
    lhs = each(stack2, a_t, r_t)
    rhs = each(stack2, b_h, k_h)
    sc = each(_dot_nt, lhs, rhs)
    ab = each(lambda s: jnp.where(strict, s[0:n2, 0:n2], 0.0).astype(BF16), sc)
    ak = each(lambda s: jnp.where(strict, s[0:n2, n2:2 * n2], 0.0).astype(BF16), sc)
    incl2 = jnp.concatenate([incl, incl], axis=1)
    rbk = each(lambda s: jnp.where(incl2, s[n2:2 * n2, :], 0.0).astype(BF16), sc)

    st = [state[p] for p in tiles]
    ah = each(lambda l_, s_: _dot_nt(l_, s_.astype(BF16)), lhs, st)
    vs = each(lambda v_: stack(v_).astype(BF16), v)
    x = each(lambda ah_, ak_, vs_: ah_[0:n2] + _dot(ak_, vs_), ah, ak, vs)
    pw = ab
    n_lvl = ch.bit_length() - 1
    for lvl in range(n_lvl):
        x = each(lambda x_, p_: x_ + _dot(p_, x_.astype(BF16)), x, pw)
        if lvl + 1 < n_lvl:
            pw = each(lambda p_: _dot(p_, p_).astype(BF16), pw)
    uv = each(lambda x_, vs_: jnp.concatenate([x_.astype(BF16), vs_], axis=0), x, vs)
    ys = each(lambda ah_, m_, uv_: ah_[n2:2 * n2] + _dot(m_, uv_), ah, rbk, uv)
    y = each(lambda ys_: ys_[0:ch] + ys_[ch:n2], ys)

    bk = each(lambda b_, k_, ge_: stack2(b_ * ge_, k_ * ge_), b_h, k_h, g_end)
    new_st = each(lambda s_, ge_, uv_, bk_: s_ * ge_ + _dot_tn(uv_, bk_), st, g_end, uv, bk)
    for p in tiles:
        state[p] = new_st[p]

    mu = each(lambda y_: head_sum(y_) * (1.0 / hd), y)
    yc = each(lambda y_, m_: y_ - m_, y, mu)
    var = each(lambda c_: head_sum(c_ * c_) * (1.0 / hd), yc)
    yn = each(lambda c_, v_, g_, b_: c_ * lax.rsqrt(v_ + RWKV_LNX_EPS) * g_ + b_, yc, var, lng, lnb)
    bonus = each(lambda r_, kr_, w_, v_: head_sum(r_ * kr_ * w_) * v_, r, kr, rkw, v)
    for i in tiles:
        y_ref[i // n_tiles, :, sls[i % n_tiles]] = ((yn[i] + bonus[i]) * g[i]).astype(BF16)


def _slab_spec(w, n_steps, step_of):
    rep = 1
    while w.shape[0] % (n_steps // rep * 16):
        rep *= 2
        assert rep <= n_steps
    return pl.BlockSpec((w.shape[0] // (n_steps // rep), w.shape[1]),
                        lambda *idx: (step_of(*idx) // rep, 0))


def _rwkv(main, small, mus, mu_s, w0, w2p, a0, a2p, g2, head_rows, cast_weights, *, batch, seq, n_seq=2):
    t = main.shape[0]
    width = RWKV_HEADS * RWKV_HEAD_DIM
    nsm = small.shape[1]
    col0 = (4 * FOX_HEADS * FOX_HEAD_DIM) // width
    ch = RWKV_CHUNK
    steps = seq // ch
    n_tiles = width // LANES
    assert batch % n_seq == 0
    groups = batch // n_seq
    grouped = lambda x: x.reshape(groups, n_seq, seq, x.shape[-1])
    act = lambda cols, blk: pl.BlockSpec((None, n_seq, ch, cols), lambda b, c: (b, 0, c, blk))
    row = lambda n: pl.BlockSpec((1, n), lambda b, c: (0, 0))
    whole = lambda w: pl.BlockSpec(w.shape, lambda b, c: (0, 0))
    slabs = [_slab_spec(w, groups * steps, lambda b, c: b * steps + c) for w in cast_weights]
    pipelined = n_seq * (4 * _nbytes((ch, width), BF16) + _nbytes((ch, nsm), F32))
    pipelined += sum(_nbytes(s.block_shape, F32) + _nbytes(s.block_shape, BF16) for s in slabs)
    pipelined += sum(_nbytes(w.shape, BF16) for w in (w2p, a2p, g2))
    scratch = [(n_seq * n_tiles, LANES, LANES)] + [(n_seq, 8, width)] * 3 + [(n_seq, 8, nsm)]
    y, *copies = pl.pallas_call(
        functools.partial(_rwkv_kernel, n_tiles=n_tiles, n_seq=n_seq, n_cast=len(slabs)),
        grid=(groups, steps),
        in_specs=[act(width, col0), act(width, col0 + 1), act(width, col0 + 2), act(nsm, 0),
                  row(width), row(width), row(width), row(nsm),
                  row(width), whole(w2p), row(width), whole(a2p), whole(g2)]
        + [row(width)] * 5 + slabs,
        out_specs=[act(width, 0)] + slabs,
        out_shape=[jax.ShapeDtypeStruct((groups, n_seq, seq, width), BF16)]
        + [jax.ShapeDtypeStruct(w.shape, BF16) for w in cast_weights],
        scratch_shapes=[pltpu.VMEM(s, F32) for s in scratch],
        compiler_params=pltpu.CompilerParams(
            dimension_semantics=("arbitrary", "arbitrary"),
            vmem_limit_bytes=_vmem_limit(pipelined, sum(_nbytes(s, F32) for s in scratch), 24 << 20),
        ),
        name="rwkv",
    )(grouped(main), grouped(main), grouped(main), grouped(small), *mus, mu_s, w0, w2p, a0, a2p, g2,
      *head_rows, *cast_weights)
    return y.reshape(t, width), copies


def _mixout_kernel(o_ref, y_ref, ga0_ref, ga1_ref, gb0_ref, gb1_ref, wa_ref, wb_ref, wo_ref, x_ref,
                   gate_ref, lng_ref, lnb_ref, scale_ref, shift_ref, x1_ref, h2_ref, *, n_sub):
    tm, d = x_ref.shape
    rows = tm // n_sub
    half = d // 2
    sl = [slice(s * rows, (s + 1) * rows) for s in range(n_sub)]

    def branches(s):
        return _dot(o_ref[sl[s], :], wa_ref[...]), _dot(y_ref[sl[s], :], wb_ref[...])

    def mixed(s, ua, ub):
        parts = []
        for c, (ga, gb) in enumerate(((ga0_ref, gb0_ref), (ga1_ref, gb1_ref))):
            cs = slice(c * half, (c + 1) * half)
            parts.append(_sigmoid(ga[sl[s], :].astype(F32)) * ua[:, cs]
                         + _sigmoid(gb[sl[s], :].astype(F32)) * ub[:, cs])
        return _dot(jnp.concatenate(parts, axis=1).astype(BF16), wo_ref[...])

    def norms(s, mix):
        z = DEEPNORM_ALPHA * x_ref[sl[s], :] + gate_ref[0] * mix
        x1 = _layernorm(z) * lng_ref[...] + lnb_ref[...]
        x1_ref[sl[s], :] = x1
        h2_ref[sl[s], :] = (_layernorm(x1) * (1.0 + scale_ref[0]) + shift_ref[0]).astype(BF16)

    u = branches(0)
    mix_prev = None
    for s in range(n_sub):
        u_next = branches(s + 1) if s + 1 < n_sub else None
        mix = mixed(s, *u)
        if mix_prev is not None:
            norms(s - 1, mix_prev)
        u, mix_prev = u_next, mix
    norms(n_sub - 1, mix_prev)


def _mixout(o, y, main, wa, wb, wo, x2d, gate1, ln_g, ln_b, scale2, shift2, *, seq, tm=256, n_sub=2):
    t, d = x2d.shape
    kdim = o.shape[1]
    half = d // 2
    gate_blk0 = (main.shape[1] - 2 * d) // half
    tiles_per_batch = seq // tm
    mod = pl.BlockSpec((1, 1, d), lambda i: (i // tiles_per_batch, 0, 0))
    row = pl.BlockSpec((1, d), lambda i: (0, 0))
    gate = lambda c: pl.BlockSpec((tm, half), lambda i: (i, gate_blk0 + c))
    resident = lambda w: pl.BlockSpec(w.shape, lambda i: (0, 0), pipeline_mode=pl.Buffered(1))
    pipelined = (2 * _nbytes((tm, kdim), BF16) + 4 * _nbytes((tm, half), BF16)
                 + 2 * _nbytes((tm, d), F32) + _nbytes((tm, d), BF16))
    weights = _nbytes(wa.shape, BF16) + _nbytes(wb.shape, BF16) + _nbytes(wo.shape, BF16)
    return pl.pallas_call(
        functools.partial(_mixout_kernel, n_sub=n_sub),
        grid=(t // tm,),
        in_specs=[
            pl.BlockSpec((tm, kdim), lambda i: (i, 0)),
            pl.BlockSpec((tm, kdim), lambda i: (i, 0)),
            gate(0), gate(1), gate(2), gate(3),
            resident(wa), resident(wb), resident(wo),
            pl.BlockSpec((tm, d), lambda i: (i, 0)),
            mod, row, row, mod, mod,
        ],
        out_specs=[pl.BlockSpec((tm, d), lambda i: (i, 0)),
                   pl.BlockSpec((tm, d), lambda i: (i, 0))],
        out_shape=[jax.ShapeDtypeStruct((t, d), F32), jax.ShapeDtypeStruct((t, d), BF16)],
        compiler_params=pltpu.CompilerParams(
            dimension_semantics=("arbitrary",),
            vmem_limit_bytes=_vmem_limit(pipelined, weights, 8 * _nbytes((tm, d), F32)),
        ),
        name="mixout",
    )(o, y, main, main, main, main, wa, wb, wo, x2d, gate1, ln_g, ln_b, scale2, shift2)


def _ffn_kernel(h_ref, wg_ref, wu_ref, wd_ref, x1_ref, gate_ref, lng_ref, lnb_ref, o_ref, acc):
    f = pl.program_id(1)
    @pl.when(f == 0)
    def _():
        acc[...] = jnp.zeros_like(acc)

    h = h_ref[...]
    g = _dot(h, wg_ref[...])
    u = _dot(h, wu_ref[...])
    act = (g * _sigmoid(g) * u).astype(BF16)
    acc[...] += _dot(act, wd_ref[...])

    @pl.when(f == pl.num_programs(1) - 1)
    def _():
        z = DEEPNORM_ALPHA * x1_ref[...] + gate_ref[0] * acc[...]
        o_ref[...] = _layernorm(z) * lng_ref[...] + lnb_ref[...]


def _ffn(h2, w_gu, w_down, x1, gate2, ln_g, ln_b, *, seq, tm=512, tf=512):
    t, d = x1.shape
    d_ff = w_down.shape[0]
    nf = d_ff // tf
    tiles_per_batch = seq // tm
    mod = pl.BlockSpec((1, 1, d), lambda i, f: (i // tiles_per_batch, 0, 0))
    row = pl.BlockSpec((1, d), lambda i, f: (0, 0))
    pipelined = (_nbytes((tm, d), BF16) + 3 * _nbytes((d, tf), BF16) + 2 * _nbytes((tm, d), F32))
    return pl.pallas_call(
        _ffn_kernel,
        grid=(t // tm, nf),
        in_specs=[
            pl.BlockSpec((tm, d), lambda i, f: (i, 0)),
            pl.BlockSpec((d, tf), lambda i, f: (0, f)),
            pl.BlockSpec((d, tf), lambda i, f: (0, nf + f)),
            pl.BlockSpec((tf, d), lambda i, f: (f, 0)),
            pl.BlockSpec((tm, d), lambda i, f: (i, 0)),
            mod, row, row,
        ],
        out_specs=pl.BlockSpec((tm, d), lambda i, f: (i, 0)),
        out_shape=jax.ShapeDtypeStruct((t, d), F32),
        scratch_shapes=[pltpu.VMEM((tm, d), F32)],
        compiler_params=pltpu.CompilerParams(
            dimension_semantics=("arbitrary", "arbitrary"),
            vmem_limit_bytes=_vmem_limit(pipelined, _nbytes((tm, d), F32),
                                         2 * _nbytes((tm, d), F32) + 4 * _nbytes((tm, tf), F32)),
        ),
        name="ffn",
    )(h2, w_gu, w_gu, w_down, x1, gate2, ln_g, ln_b)


def _pad_rows(w, rows):
    return jnp.zeros((rows, w.shape[1]), w.dtype).at[:w.shape[0]].set(w)


def kernel(x, c, w_ada, b_ada, w_in, b_fgate, q_norm_g, k_norm_g, rwkv_mu, rwkv_w0, rwkv_w2, rwkv_a0, rwkv_a2, rwkv_g2, rwkv_k_k, rwkv_k_a, rwkv_r_k, rwkv_lnx_g, rwkv_lnx_b, w_branch_a, w_branch_b, w_out, ln1_g, ln1_b, w_ffn_gu, w_ffn_down, ln2_g, ln2_b):
    batch, seq, d = x.shape
    t = batch * seq
    fox_w = FOX_HEADS * FOX_HEAD_DIM
    rw_w = RWKV_HEADS * RWKV_HEAD_DIM
    fox_cols = 4 * fox_w + FOX_HEADS
    lora0 = fox_cols + 3 * rw_w
    gate0 = lora0 + DECAY_LORA + AAA_LORA + GATE_LORA
    assert seq % RWKV_CHUNK == 0 and w_in.shape[1] == gate0 + 2 * d

    lora_w = gate0 - lora0
    mu_r = rwkv_mu[None, 0:rw_w]
    mu_k = rwkv_mu[None, rw_w:2 * rw_w]
    mu_v = rwkv_mu[None, 2 * rw_w:3 * rw_w]
    mu_s = jnp.zeros((1, INPROJ_TN), rwkv_mu.dtype).at[0, :lora_w].set(rwkv_mu[3 * rw_w:])
    w2p = _pad_rows(rwkv_w2, LORA_PAD).astype(BF16)
    a2p = _pad_rows(rwkv_a2, LORA_PAD).astype(BF16)
    g2 = rwkv_g2.astype(BF16)
    row = lambda v: v.reshape(1, -1)

    c_pad = jnp.zeros((8, d), c.dtype).at[:batch].set(c)
    mod = _ada(c_pad, w_ada, row(b_ada))[:batch].reshape(batch, 6, 1, d)
    shift1, scale1, gate1, shift2, scale2, gate2 = (mod[:, i] for i in range(6))

    x2d = x.reshape(t, d)
    main, small, ft = _inproj(x2d, scale1, shift1, w_in.T, row(q_norm_g), row(k_norm_g), seq=seq,
                              group_rows=[(0, 4 * fox_w), (fox_cols, 3 * rw_w), (gate0, 2 * d)],
                              lora_row=lora0, fgate_row=4 * fox_w, tn=INPROJ_TN)

    blocks_per_seq = seq // LANES
    ft_rows = ft.reshape(FOX_HEADS * batch * blocks_per_seq, LANES)
    bias_rows = jnp.broadcast_to(b_fgate[:, None, None], (FOX_HEADS, batch * blocks_per_seq, LANES))
    cum = _fox_prep(ft_rows, bias_rows.reshape(ft_rows.shape), blocks_per_seq=blocks_per_seq)
    cum = cum.reshape(FOX_HEADS, batch, 1, seq)
    o = _fox_attn(main, cum, batch=batch, seq=seq)

    y, (wa, wb, wo, wgu, wdn) = _rwkv(
        main, small, (mu_r, mu_k, mu_v), mu_s, row(rwkv_w0), w2p, row(rwkv_a0), a2p, g2,
        [row(p) for p in (rwkv_k_k, rwkv_k_a, rwkv_r_k, rwkv_lnx_g, rwkv_lnx_b)],
        [w_branch_a, w_branch_b, w_out, w_ffn_gu, w_ffn_down], batch=batch, seq=seq)

    x1, h2 = _mixout(o, y, main, wa, wb, wo, x2d, gate1, row(ln1_g), row(ln1_b), scale2, shift2, seq=seq)

    out = _ffn(h2, wgu, wdn, x1, gate2, row(ln2_g), row(ln2_b), seq=seq)
    return out.reshape(batch, seq, d)
```

```python
import functools

import jax
import jax.numpy as jnp
from jax import lax
from jax.experimental import pallas as pl
from jax.experimental.pallas import tpu as pltpu

F32 = jnp.float32
BF16 = jnp.bfloat16

LANES = 128
VMEM_CAP_BYTES = 60000 * 1024

FOX_HEADS = 8
FOX_HEAD_DIM = 128
RWKV_HEADS = 16
RWKV_HEAD_DIM = 64
DECAY_LORA = 96
AAA_LORA = 96
GATE_LORA = 256
LORA_PAD = 128
INPROJ_TN = 512
RWKV_LNX_EPS = 64e-5
DEPTH = 1
DEEPNORM_ALPHA = (2.0 * DEPTH) ** 0.25
LN_EPS = 1e-5
RMS_EPS = 1e-6
RWKV_CHUNK = 64
HEADS_PER_TILE = LANES // RWKV_HEAD_DIM


def _vmem_limit(pipelined_bytes, resident_bytes=0, temp_bytes=0):
    need = 2 * pipelined_bytes + resident_bytes + temp_bytes + (2 << 20)
    return int(min(VMEM_CAP_BYTES, need))


def _nbytes(shape, dtype):
    n = 1
    for s in shape:
        n *= s
    return n * jnp.dtype(dtype).itemsize


def _layernorm(x):
    mu = jnp.mean(x, axis=-1, keepdims=True)
    xc = x - mu
    var = jnp.mean(xc * xc, axis=-1, keepdims=True)
    return xc * lax.rsqrt(var + LN_EPS)


def _sigmoid(x):
    return 1.0 / (1.0 + jnp.exp(-x))


def _dot(a, b):
    return jnp.dot(a, b, preferred_element_type=F32)


def _dot_nt(a, b):
    return lax.dot_general(a, b, (((1,), (1,)), ((), ())), preferred_element_type=F32)


def _dot_tn(a, b):
    return lax.dot_general(a, b, (((0,), (0,)), ((), ())), preferred_element_type=F32)


def _split3(x):
    hi = x.astype(BF16)
    r1 = x - hi.astype(F32)
    mid = r1.astype(BF16)
    lo = (r1 - mid.astype(F32)).astype(BF16)
    return hi, mid, lo


def _ada_kernel(c_ref, w_ref, b_ref, o_ref):
    c = c_ref[...]
    s = (c * _sigmoid(c)).astype(BF16)
    o_ref[...] = _dot(s, w_ref[...].astype(BF16)) + b_ref[...]


def _ada(c_pad, w_ada, b_ada, *, tn=1024):
    rows, d = c_pad.shape
    n = w_ada.shape[1]
    return pl.pallas_call(
        _ada_kernel,
        grid=(n // tn,),
        in_specs=[
            pl.BlockSpec((rows, d), lambda j: (0, 0)),
            pl.BlockSpec((d, tn), lambda j: (0, j)),
            pl.BlockSpec((1, tn), lambda j: (0, j)),
        ],
        out_specs=pl.BlockSpec((rows, tn), lambda j: (0, j)),
        out_shape=jax.ShapeDtypeStruct((rows, n), F32),
        compiler_params=pltpu.CompilerParams(
            dimension_semantics=("arbitrary",),
            vmem_limit_bytes=_vmem_limit(_nbytes((d, tn), F32), temp_bytes=_nbytes((d, tn), BF16)),
        ),
        name="ada",
    )(c_pad, w_ada, b_ada)


def _inproj_kernel(x_ref, scale_ref, shift_ref, wt_ref, wft_ref, qg_ref, kg_ref,
                   main_ref, small_ref, ft_ref, h_scr, *, n_q, n_k, n_main):
    j = pl.program_id(1)
    tn = main_ref.shape[1]

    @pl.when(j == 0)
    def _():
        h = _layernorm(x_ref[...]) * (1.0 + scale_ref[0]) + shift_ref[0]
        h_scr[...] = h.astype(BF16)

    def proj():
        return _dot_nt(h_scr[...], wt_ref[...].astype(BF16))

    @pl.when(j < n_q + n_k)
    def _():
        acc = proj()
        gain = jnp.where(j < n_q, qg_ref[...] * (FOX_HEAD_DIM ** -0.5), kg_ref[...])
        for hh in range(tn // FOX_HEAD_DIM):
            sl = slice(hh * FOX_HEAD_DIM, (hh + 1) * FOX_HEAD_DIM)
            a = acc[:, sl]
            ms = jnp.mean(a * a, axis=-1, keepdims=True)
            main_ref[:, sl] = (a * lax.rsqrt(ms + RMS_EPS) * gain).astype(BF16)

    @pl.when((j >= n_q + n_k) & (j < n_main))
    def _():
        main_ref[...] = proj().astype(BF16)

    @pl.when(j == n_main)
    def _():
        small_ref[...] = proj()
        ft_ref[...] = _dot_nt(wft_ref[...].astype(BF16), h_scr[...])


def _inproj(x2d, scale1, shift1, w_in_t, qg, kg, *, seq, group_rows, lora_row, fgate_row, tm=1024, tn=512):
    t, d = x2d.shape
    fox_w = FOX_HEADS * FOX_HEAD_DIM
    assert all(first % 8 == 0 and count % tn == 0 for first, count in group_rows)
    assert lora_row % 8 == 0 and fgate_row % 8 == 0 and lora_row + tn <= w_in_t.shape[0]
    n_main = sum(count // tn for _, count in group_rows)
    tiles_per_batch = seq // tm
    kern = functools.partial(_inproj_kernel, n_q=fox_w // tn, n_k=fox_w // tn, n_main=n_main)
    pipelined = (_nbytes((tm, d), F32) + _nbytes((tn, d), F32) + _nbytes((tm, tn), BF16)
                 + _nbytes((tm, tn), F32) + _nbytes((8, tm), F32))

    def w_rows(i, j):
        row, hi = jnp.int32(lora_row), n_main
        for first, count in reversed(group_rows):
            lo = hi - count // tn
            row = jnp.where(j < hi, first + tn * (j - lo), row)
            hi = lo
        return pl.multiple_of(row, 8), 0

    return pl.pallas_call(
        kern,
        grid=(t // tm, n_main + 1),
        in_specs=[
            pl.BlockSpec((tm, d), lambda i, j: (i, 0)),
            pl.BlockSpec((1, 1, d), lambda i, j: (i // tiles_per_batch, 0, 0)),
            pl.BlockSpec((1, 1, d), lambda i, j: (i // tiles_per_batch, 0, 0)),
            pl.BlockSpec((pl.Element(tn), pl.Element(d)), w_rows),
            pl.BlockSpec((pl.Element(FOX_HEADS), pl.Element(d)), lambda i, j: (fgate_row, 0)),
            pl.BlockSpec((1, FOX_HEAD_DIM), lambda i, j: (0, 0)),
            pl.BlockSpec((1, FOX_HEAD_DIM), lambda i, j: (0, 0)),
        ],
        out_specs=[
            pl.BlockSpec((tm, tn), lambda i, j: (i, jnp.minimum(j, n_main - 1))),
            pl.BlockSpec((tm, tn), lambda i, j: (i, 0)),
            pl.BlockSpec((FOX_HEADS, tm), lambda i, j: (0, i)),
        ],
        out_shape=[
            jax.ShapeDtypeStruct((t, n_main * tn), BF16),
            jax.ShapeDtypeStruct((t, tn), F32),
            jax.ShapeDtypeStruct((FOX_HEADS, t), F32),
        ],
        scratch_shapes=[pltpu.VMEM((tm, d), BF16)],
        compiler_params=pltpu.CompilerParams(
            dimension_semantics=("arbitrary", "arbitrary"),
            vmem_limit_bytes=_vmem_limit(pipelined, _nbytes((tm, d), BF16),
                                         3 * _nbytes((tm, d), F32)),
        ),
        name="inproj",
    )(x2d, scale1, shift1, w_in_t, w_in_t, qg, kg)


def _fox_prep_kernel(ft_ref, bias_ref, o_ref, *, blocks_per_seq):
    x = ft_ref[...] + bias_ref[...]
    lf = jnp.minimum(x, 0.0) - jnp.log1p(jnp.exp(-jnp.abs(x)))
    rows, width = lf.shape
    r = lax.broadcasted_iota(jnp.int32, (width, width), 0)
    c = lax.broadcasted_iota(jnp.int32, (width, width), 1)
    upper = (r <= c).astype(BF16)
    within = sum(_dot(p, upper) for p in _split3(lf))
    tot = jnp.broadcast_to(within[:, width - 1:width], (rows, width))
    rr = lax.broadcasted_iota(jnp.int32, (rows, rows), 0)
    cc = lax.broadcasted_iota(jnp.int32, (rows, rows), 1)
    prior = ((rr // blocks_per_seq == cc // blocks_per_seq) & (cc < rr)).astype(BF16)
    offset = sum(_dot(prior, p) for p in _split3(tot))
    o_ref[...] = within + offset


def _fox_prep(ft_rows, bias_rows, *, blocks_per_seq):
    return pl.pallas_call(
        functools.partial(_fox_prep_kernel, blocks_per_seq=blocks_per_seq),
        out_shape=jax.ShapeDtypeStruct(ft_rows.shape, F32),
        name="fox_prep",
    )(ft_rows, bias_rows)


def _fox_attn_kernel(q_ref, k_ref, v_ref, og_ref, cum_ref, o_ref, *, tq):
    seq = q_ref.shape[0]
    n_q = seq // tq
    negcum = -cum_ref[0, 0]
    row = lax.broadcasted_iota(jnp.int32, (tq, tq), 0)
    col = lax.broadcasted_iota(jnp.int32, (tq, tq), 1)
    causal = col <= row

    def logits(qi):
        q0, kend = qi * tq, (qi + 1) * tq
        q = q_ref[q0:kend, :]
        diag = _dot_nt(q, k_ref[q0:kend, :]) + negcum[:, q0:kend]
        diag = jnp.where(causal, diag, -jnp.inf)
        past = _dot_nt(q, k_ref[0:q0, :]) + negcum[:, 0:q0] if qi else None
        return past, diag

    def attend(qi, past, diag):
        q0, kend = qi * tq, (qi + 1) * tq
        m = jnp.max(diag, axis=-1, keepdims=True)
        if past is not None:
            m = jnp.maximum(m, jnp.max(past, axis=-1, keepdims=True))
        p = jnp.exp(diag - m)
        l = jnp.sum(p, axis=-1, keepdims=True)
        o = _dot(p.astype(BF16), v_ref[q0:kend, :])
        if past is not None:
            p = jnp.exp(past - m)
            l = l + jnp.sum(p, axis=-1, keepdims=True)
            o = o + _dot(p.astype(BF16), v_ref[0:q0, :])
        gate = _sigmoid(og_ref[q0:kend, :].astype(F32))
        o_ref[q0:kend, :] = (o / l * gate).astype(BF16)

    nxt = logits(0)
    for qi in range(n_q):
        cur, nxt = nxt, (logits(qi + 1) if qi + 1 < n_q else None)
        attend(qi, *cur)


def _fox_attn(main, cum, *, batch, seq, tq=256):
    t = main.shape[0]
    hd = FOX_HEAD_DIM
    nh = FOX_HEADS
    blk = lambda off: pl.BlockSpec((seq, hd), lambda b, h: (b, off + h))
    pipelined = 5 * _nbytes((seq, hd), BF16) + _nbytes((8, seq), F32)
    return pl.pallas_call(
        functools.partial(_fox_attn_kernel, tq=tq),
        grid=(batch, nh),
        in_specs=[blk(0), blk(nh), blk(2 * nh), blk(3 * nh),
                  pl.BlockSpec((1, 1, 1, seq), lambda b, h: (h, b, 0, 0))],
        out_specs=pl.BlockSpec((seq, hd), lambda b, h: (b, h)),
        out_shape=jax.ShapeDtypeStruct((t, nh * hd), BF16),
        compiler_params=pltpu.CompilerParams(
            dimension_semantics=("arbitrary", "arbitrary"),
            vmem_limit_bytes=_vmem_limit(pipelined, 0, 6 * _nbytes((tq, seq), F32)),
        ),
        name="fox_attn",
    )(main, main, main, main, cum)


def _rwkv_inputs(raw_refs, small_ref, mu_refs, mu_s_ref, w0_ref, w2_ref, a0_ref, a2_ref, g2_ref,
                 prev_refs, prev_s, n_seq):
    def shift_mix(x, prev_ref, q, mu):
        rows = x.shape[0]
        xp = pltpu.roll(x, 1, axis=0)
        row = lax.broadcasted_iota(jnp.int32, x.shape, 0)
        xp = jnp.where(row == 0, prev_ref[q, 0:1, :], xp)
        prev_ref[q, 0:1, :] = x[rows - 1:rows, :]
        return x + (xp - x) * mu

    seqs = range(n_seq)
    r, k, v = ([shift_mix(ref[q].astype(F32), prev, q, mu[...]) for q in seqs]
               for ref, prev, mu in zip(raw_refs, prev_refs, mu_refs))
    sm = jnp.concatenate([shift_mix(small_ref[q], prev_s, q, mu_s_ref[...]) for q in seqs], axis=0)
    xw = sm[:, 0:LORA_PAD]
    xa = sm[:, DECAY_LORA:DECAY_LORA + LORA_PAD]
    xg = sm[:, DECAY_LORA + AAA_LORA:DECAY_LORA + AAA_LORA + GATE_LORA]

    z = w0_ref[...] + _dot(jnp.tanh(xw).astype(BF16), w2_ref[...])
    w_raw = -(jnp.maximum(-z, 0.0) + jnp.log1p(jnp.exp(-jnp.abs(z)))) - 0.5
    lw = -jnp.exp(w_raw)
    a = _sigmoid(a0_ref[...] + _dot(xa.astype(BF16), a2_ref[...]))
    g = _dot(_sigmoid(xg).astype(BF16), g2_ref[...])
    ch = lw.shape[0] // n_seq
    per_seq = lambda x: [x[q * ch:(q + 1) * ch, :] for q in seqs]
    return r, k, v, per_seq(lw), per_seq(a), per_seq(g)


def _rwkv_kernel(r_ref, k_ref, v_ref, small_ref, mu_r_ref, mu_k_ref, mu_v_ref, mu_s_ref,
                 w0_ref, w2_ref, a0_ref, a2_ref, g2_ref,
                 kk_ref, ka_ref, rk_ref, lng_ref, lnb_ref, *rest, n_tiles, n_seq, n_cast):
    cast_src, y_ref, cast_dst = rest[:n_cast], rest[n_cast], rest[n_cast + 1:2 * n_cast + 1]
    state, prev_r, prev_k, prev_v, prev_s = rest[2 * n_cast + 1:]
    for src, dst in zip(cast_src, cast_dst):
        dst[...] = src[...].astype(BF16)

    @pl.when(pl.program_id(1) == 0)
    def _():
        for ref in (state, prev_r, prev_k, prev_v, prev_s):
            ref[...] = jnp.zeros_like(ref)

    ch = y_ref.shape[1]
    hd = RWKV_HEAD_DIM
    lane = lax.broadcasted_iota(jnp.int32, (1, LANES), 1)
    m0 = lane < hd
    n2 = HEADS_PER_TILE * ch
    rr = lax.broadcasted_iota(jnp.int32, (n2, n2), 0)
    cc = lax.broadcasted_iota(jnp.int32, (n2, n2), 1)
    same = (rr // ch) == (cc // ch)
    strict = same & (rr > cc)
    incl = same & (rr >= cc)
    lr = lax.broadcasted_iota(jnp.int32, (ch, ch), 0)
    lc = lax.broadcasted_iota(jnp.int32, (ch, ch), 1)
    lower = (lc <= lr).astype(BF16)

    def head_sum(x):
        s0 = jnp.sum(jnp.where(m0, x, 0.0), axis=-1, keepdims=True)
        s1 = jnp.sum(jnp.where(m0, 0.0, x), axis=-1, keepdims=True)
        return jnp.where(m0, s0, s1)

    def stack(x):
        return jnp.concatenate([jnp.where(m0, x, 0.0), jnp.where(m0, 0.0, x)], axis=0)

    sls = [slice(p * LANES, (p + 1) * LANES) for p in range(n_tiles)]
    tiles = range(n_seq * n_tiles)

    def each(fn, *cols):
        return [fn(*args) for args in zip(*cols)]

    per_chain = lambda ref: [ref[:, sl] for _ in range(n_seq) for sl in sls]
    kkw, kaw, rkw, lng, lnb = (per_chain(ref) for ref in (kk_ref, ka_ref, rk_ref, lng_ref, lnb_ref))

    full = _rwkv_inputs((r_ref, k_ref, v_ref), small_ref, (mu_r_ref, mu_k_ref, mu_v_ref), mu_s_ref,
                        w0_ref, w2_ref, a0_ref, a2_ref, g2_ref, (prev_r, prev_k, prev_v), prev_s, n_seq)
    r, k, v, lw, a, g = ([x[q][:, sl] for q in range(n_seq) for sl in sls] for x in full)

    kk = each(lambda k_, w_: k_ * w_, k, kkw)
    kk = each(lambda x: x / jnp.maximum(jnp.sqrt(head_sum(x * x)), 1e-12), kk)
    kr = each(lambda k_, a_, w_: k_ * (1.0 + (a_ - 1.0) * w_), k, a, kaw)
    cum = each(lambda x: sum(_dot(lower, part) for part in _split3(x)[:2]), lw)
    gam = each(jnp.exp, cum)
    ginv = each(lambda x: jnp.exp(-x), cum)
    g_end = each(lambda x: x[ch - 1:ch, :], gam)
    a_t = each(lambda kk_, c_, lw_: -kk_ * jnp.exp(c_ - lw_), kk, cum, lw)
    r_t = each(lambda r_, g_: r_ * g_, r, gam)
    b_h = each(lambda kk_, a_, gi_: kk_ * a_ * gi_, kk, a, ginv)
    k_h = each(lambda kr_, gi_: kr_ * gi_, kr, ginv)

    stack2 = lambda x, y: jnp.concatenate([stack(x), stack(y)], axis=0).astype(BF16)
    lhs = each(stack2, a_t, r_t)
    rhs = each(stack2, b_h, k_h)
    sc = each(_dot_nt, lhs, rhs)
    ab = each(lambda s: jnp.where(strict, s[0:n2, 0:n2], 0.0).astype(BF16), sc)
    ak = each(lambda s: jnp.where(strict, s[0:n2, n2:2 * n2], 0.0).astype(BF16), sc)
    incl2 = jnp.concatenate([incl, incl], axis=1)
    rbk = each(lambda s: jnp.where(incl2, s[n2:2 * n2, :], 0.0).astype(BF16), sc)

    st = [state[p] for p in tiles]
    ah = each(lambda l_, s_: _dot_nt(l_, s_.astype(BF16)), lhs, st)
    vs = each(lambda v_: stack(v_).astype(BF16), v)
    x = each(lambda ah_, ak_, vs_: ah_[0:n2] + _dot(ak_, vs_), ah, ak, vs)
    pw = ab
    n_lvl = ch.bit_length() - 1
    for lvl in range(n_lvl):
        x = each(lambda x_, p_: x_ + _dot(p_, x_.astype(BF16)), x, pw)
        if lvl + 1 < n_lvl:
            pw = each(lambda p_: _dot(p_, p_).astype(BF16), pw)
    uv = each(lambda x_, vs_: jnp.concatenate([x_.astype(BF16), vs_], axis=0), x, vs)
    ys = each(lambda ah_, m_, uv_: ah_[n2:2 * n2] + _dot(m_, uv_), ah, rbk, uv)
    y = each(lambda ys_: ys_[0:ch] + ys_[ch:n2], ys)

    bk = each(lambda b_, k_, ge_: stack2(b_ * ge_, k_ * ge_), b_h, k_h, g_end)
    new_st = each(lambda s_, ge_, uv_, bk_: s_ * ge_ + _dot_tn(uv_, bk_), st, g_end, uv, bk)
    for p in tiles:
        state[p] = new_st[p]

    mu = each(lambda y_: head_sum(y_) * (1.0 / hd), y)
    yc = each(lambda y_, m_: y_ - m_, y, mu)
    var = each(lambda c_: head_sum(c_ * c_) * (1.0 / hd), yc)
    yn = each(lambda c_, v_, g_, b_: c_ * lax.rsqrt(v_ + RWKV_LNX_EPS) * g_ + b_, yc, var, lng, lnb)
    bonus = each(lambda r_, kr_, w_, v_: head_sum(r_ * kr_ * w_) * v_, r, kr, rkw, v)
    for i in tiles:
        y_ref[i // n_tiles, :, sls[i % n_tiles]] = ((yn[i] + bonus[i]) * g[i]).astype(BF16)


def _slab_spec(w, n_steps, step_of):
    rep = 1
    while w.shape[0] % (n_steps // rep * 16):
        rep *= 2
        assert rep <= n_steps
    return pl.BlockSpec((w.shape[0] // (n_steps // rep), w.shape[1]),
                        lambda *idx: (step_of(*idx) // rep, 0))


def _rwkv(main, small, mus, mu_s, w0, w2p, a0, a2p, g2, head_rows, cast_weights, *, batch, seq, n_seq=2):
    t = main.shape[0]
    width = RWKV_HEADS * RWKV_HEAD_DIM
    nsm = small.shape[1]
    col0 = (4 * FOX_HEADS * FOX_HEAD_DIM) // width
    ch = RWKV_CHUNK
    steps = seq // ch
    n_tiles = width // LANES
    assert batch % n_seq == 0
    groups = batch // n_seq
    grouped = lambda x: x.reshape(groups, n_seq, seq, x.shape[-1])
    act = lambda cols, blk: pl.BlockSpec((None, n_seq, ch, cols), lambda b, c: (b, 0, c, blk))
    row = lambda n: pl.BlockSpec((1, n), lambda b, c: (0, 0))
    whole = lambda w: pl.BlockSpec(w.shape, lambda b, c: (0, 0))
    slabs = [_slab_spec(w, groups * steps, lambda b, c: b * steps + c) for w in cast_weights]
    pipelined = n_seq * (4 * _nbytes((ch, width), BF16) + _nbytes((ch, nsm), F32))
    pipelined += sum(_nbytes(s.block_shape, F32) + _nbytes(s.block_shape, BF16) for s in slabs)
    pipelined += sum(_nbytes(w.shape, BF16) for w in (w2p, a2p, g2))
    scratch = [(n_seq * n_tiles, LANES, LANES)] + [(n_seq, 8, width)] * 3 + [(n_seq, 8, nsm)]
    y, *copies = pl.pallas_call(
        functools.partial(_rwkv_kernel, n_tiles=n_tiles, n_seq=n_seq, n_cast=len(slabs)),
        grid=(groups, steps),
        in_specs=[act(width, col0), act(width, col0 + 1), act(width, col0 + 2), act(nsm, 0),
                  row(width), row(width), row(width), row(nsm),
                  row(width), whole(w2p), row(width), whole(a2p), whole(g2)]
        + [row(width)] * 5 + slabs,
        out_specs=[act(width, 0)] + slabs,
        out_shape=[jax.ShapeDtypeStruct((groups, n_seq, seq, width), BF16)]
        + [jax.ShapeDtypeStruct(w.shape, BF16) for w in cast_weights],
        scratch_shapes=[pltpu.VMEM(s, F32) for s in scratch],
        compiler_params=pltpu.CompilerParams(
            dimension_semantics=("arbitrary", "arbitrary"),
            vmem_limit_bytes=_vmem_limit(pipelined, sum(_nbytes(s, F32) for s in scratch), 24 << 20),
        ),
        name="rwkv",
    )(grouped(main), grouped(main), grouped(main), grouped(small), *mus, mu_s, w0, w2p, a0, a2p, g2,
      *head_rows, *cast_weights)
    return y.reshape(t, width), copies


def _mixout_kernel(o_ref, y_ref, ga0_ref, ga1_ref, gb0_ref, gb1_ref, wa_ref, wb_ref, wo_ref, x_ref,
                   gate_ref, lng_ref, lnb_ref, scale_ref, shift_ref, x1_ref, h2_ref, *, n_sub):
    tm, d = x_ref.shape
    rows = tm // n_sub
    half = d // 2
    sl = [slice(s * rows, (s + 1) * rows) for s in range(n_sub)]

    def branches(s):
        return _dot(o_ref[sl[s], :], wa_ref[...]), _dot(y_ref[sl[s], :], wb_ref[...])

    def mixed(s, ua, ub):
        parts = []
        for c, (ga, gb) in enumerate(((ga0_ref, gb0_ref), (ga1_ref, gb1_ref))):
            cs = slice(c * half, (c + 1) * half)
            parts.append(_sigmoid(ga[sl[s], :].astype(F32)) * ua[:, cs]
                         + _sigmoid(gb[sl[s], :].astype(F32)) * ub[:, cs])
        return _dot(jnp.concatenate(parts, axis=1).astype(BF16), wo_ref[...])

    def norms(s, mix):
        z = DEEPNORM_ALPHA * x_ref[sl[s], :] + gate_ref[0] * mix
        x1 = _layernorm(z) * lng_ref[...] + lnb_ref[...]
        x1_ref[sl[s], :] = x1
        h2_ref[sl[s], :] = (_layernorm(x1) * (1.0 + scale_ref[0]) + shift_ref[0]).astype(BF16)

    u = branches(0)
    mix_prev = None
    for s in range(n_sub):
        u_next = branches(s + 1) if s + 1 < n_sub else None
        mix = mixed(s, *u)
        if mix_prev is not None:
            norms(s - 1, mix_prev)
        u, mix_prev = u_next, mix
    norms(n_sub - 1, mix_prev)


def _mixout(o, y, main, wa, wb, wo, x2d, gate1, ln_g, ln_b, scale2, shift2, *, seq, tm=256, n_sub=2):
    t, d = x2d.shape
    kdim = o.shape[1]
    half = d // 2
    gate_blk0 = (main.shape[1] - 2 * d) // half
    tiles_per_batch = seq // tm
    mod = pl.BlockSpec((1, 1, d), lambda i: (i // tiles_per_batch, 0, 0))
    row = pl.BlockSpec((1, d), lambda i: (0, 0))
    gate = lambda c: pl.BlockSpec((tm, half), lambda i: (i, gate_blk0 + c))
    resident = lambda w: pl.BlockSpec(w.shape, lambda i: (0, 0), pipeline_mode=pl.Buffered(1))
    pipelined = (2 * _nbytes((tm, kdim), BF16) + 4 * _nbytes((tm, half), BF16)
                 + 2 * _nbytes((tm, d), F32) + _nbytes((tm, d), BF16))
    weights = _nbytes(wa.shape, BF16) + _nbytes(wb.shape, BF16) + _nbytes(wo.shape, BF16)
    return pl.pallas_call(
        functools.partial(_mixout_kernel, n_sub=n_sub),
        grid=(t // tm,),
        in_specs=[
            pl.BlockSpec((tm, kdim), lambda i: (i, 0)),
            pl.BlockSpec((tm, kdim), lambda i: (i, 0)),
            gate(0), gate(1), gate(2), gate(3),
            resident(wa), resident(wb), resident(wo),
            pl.BlockSpec((tm, d), lambda i: (i, 0)),
            mod, row, row, mod, mod,
        ],
        out_specs=[pl.BlockSpec((tm, d), lambda i: (i, 0)),
                   pl.BlockSpec((tm, d), lambda i: (i, 0))],
        out_shape=[jax.ShapeDtypeStruct((t, d), F32), jax.ShapeDtypeStruct((t, d), BF16)],
        compiler_params=pltpu.CompilerParams(
            dimension_semantics=("arbitrary",),
            vmem_limit_bytes=_vmem_limit(pipelined, weights, 8 * _nbytes((tm, d), F32)),
        ),
        name="mixout",
    )(o, y, main, main, main, main, wa, wb, wo, x2d, gate1, ln_g, ln_b, scale2, shift2)


def _ffn_kernel(h_ref, wg_ref, wu_ref, wd_ref, x1_ref, gate_ref, lng_ref, lnb_ref, o_ref, acc):
    f = pl.program_id(1)
    @pl.when(f == 0)
    def _():
        acc[...] = jnp.zeros_like(acc)

    h = h_ref[...]
    g = _dot(h, wg_ref[...])
    u = _dot(h, wu_ref[...])
    act = (g * _sigmoid(g) * u).astype(BF16)
    acc[...] += _dot(act, wd_ref[...])

    @pl.when(f == pl.num_programs(1) - 1)
    def _():
        z = DEEPNORM_ALPHA * x1_ref[...] + gate_ref[0] * acc[...]
        o_ref[...] = _layernorm(z) * lng_ref[...] + lnb_ref[...]


def _ffn(h2, w_gu, w_down, x1, gate2, ln_g, ln_b, *, seq, tm=512, tf=512):
    t, d = x1.shape
    d_ff = w_down.shape[0]
    nf = d_ff // tf
    tiles_per_batch = seq // tm
    mod = pl.BlockSpec((1, 1, d), lambda i, f: (i // tiles_per_batch, 0, 0))
    row = pl.BlockSpec((1, d), lambda i, f: (0, 0))
    pipelined = (_nbytes((tm, d), BF16) + 3 * _nbytes((d, tf), BF16) + 2 * _nbytes((tm, d), F32))
    return pl.pallas_call(
        _ffn_kernel,
        grid=(t // tm, nf),
        in_specs=[
            pl.BlockSpec((tm, d), lambda i, f: (i, 0)),
            pl.BlockSpec((d, tf), lambda i, f: (0, f)),
            pl.BlockSpec((d, tf), lambda i, f: (0, nf + f)),
            pl.BlockSpec((tf, d), lambda i, f: (f, 0)),
            pl.BlockSpec((tm, d), lambda i, f: (i, 0)),
            mod, row, row,
        ],
        out_specs=pl.BlockSpec((tm, d), lambda i, f: (i, 0)),
        out_shape=jax.ShapeDtypeStruct((t, d), F32),
        scratch_shapes=[pltpu.VMEM((tm, d), F32)],
        compiler_params=pltpu.CompilerParams(
            dimension_semantics=("arbitrary", "arbitrary"),
            vmem_limit_bytes=_vmem_limit(pipelined, _nbytes((tm, d), F32),
                                         2 * _nbytes((tm, d), F32) + 4 * _nbytes((tm, tf), F32)),
        ),
        name="ffn",
    )(h2, w_gu, w_gu, w_down, x1, gate2, ln_g, ln_b)


def _pad_rows(w, rows):
    return jnp.zeros((rows, w.shape[1]), w.dtype).at[:w.shape[0]].set(w)


def kernel(x, c, w_ada, b_ada, w_in, b_fgate, q_norm_g, k_norm_g, rwkv_mu, rwkv_w0, rwkv_w2, rwkv_a0, rwkv_a2, rwkv_g2, rwkv_k_k, rwkv_k_a, rwkv_r_k, rwkv_lnx_g, rwkv_lnx_b, w_branch_a, w_branch_b, w_out, ln1_g, ln1_b, w_ffn_gu, w_ffn_down, ln2_g, ln2_b):
    batch, seq, d = x.shape
    t = batch * seq
    fox_w = FOX_HEADS * FOX_HEAD_DIM
    rw_w = RWKV_HEADS * RWKV_HEAD_DIM
    fox_cols = 4 * fox_w + FOX_HEADS
    lora0 = fox_cols + 3 * rw_w
    gate0 = lora0 + DECAY_LORA + AAA_LORA + GATE_LORA
    assert seq % RWKV_CHUNK == 0 and w_in.shape[1] == gate0 + 2 * d

    lora_w = gate0 - lora0
    mu_r = rwkv_mu[None, 0:rw_w]
    mu_k = rwkv_mu[None, rw_w:2 * rw_w]
    mu_v = rwkv_mu[None, 2 * rw_w:3 * rw_w]
    mu_s = jnp.zeros((1, INPROJ_TN), rwkv_mu.dtype).at[0, :lora_w].set(rwkv_mu[3 * rw_w:])
    w2p = _pad_rows(rwkv_w2, LORA_PAD).astype(BF16)
    a2p = _pad_rows(rwkv_a2, LORA_PAD).astype(BF16)
    g2 = rwkv_g2.astype(BF16)
    row = lambda v: v.reshape(1, -1)

    c_pad = jnp.zeros((8, d), c.dtype).at[:batch].set(c)
    mod = _ada(c_pad, w_ada, row(b_ada))[:batch].reshape(batch, 6, 1, d)
    shift1, scale1, gate1, shift2, scale2, gate2 = (mod[:, i] for i in range(6))

    x2d = x.reshape(t, d)
    main, small, ft = _inproj(x2d, scale1, shift1, w_in.T, row(q_norm_g), row(k_norm_g), seq=seq,
                              group_rows=[(0, 4 * fox_w), (fox_cols, 3 * rw_w), (gate0, 2 * d)],
                              lora_row=lora0, fgate_row=4 * fox_w, tn=INPROJ_TN)

    blocks_per_seq = seq // LANES
    ft_rows = ft.reshape(FOX_HEADS * batch * blocks_per_seq, LANES)
    bias_rows = jnp.broadcast_to(b_fgate[:, None, None], (FOX_HEADS, batch * blocks_per_seq, LANES))
    cum = _fox_prep(ft_rows, bias_rows.reshape(ft_rows.shape), blocks_per_seq=blocks_per_seq)
    cum = cum.reshape(FOX_HEADS, batch, 1, seq)
    o = _fox_attn(main, cum, batch=batch, seq=seq)

    y, (wa, wb, wo, wgu, wdn) = _rwkv(
        main, small, (mu_r, mu_k, mu_v), mu_s, row(rwkv_w0), w2p, row(rwkv_a0), a2p, g2,
        [row(p) for p in (rwkv_k_k, rwkv_k_a, rwkv_r_k, rwkv_lnx_g, rwkv_lnx_b)],
        [w_branch_a, w_branch_b, w_out, w_ffn_gu, w_ffn_down], batch=batch, seq=seq)

    x1, h2 = _mixout(o, y, main, wa, wb, wo, x2d, gate1, row(ln1_g), row(ln1_b), scale2, shift2, seq=seq)

    out = _ffn(h2, wgu, wdn, x1, gate2, row(ln2_g), row(ln2_b), seq=seq)
    return out.reshape(batch, seq, d)
```

```python
import functools

import jax
import jax.numpy as jnp
from jax import lax
from jax.experimental import pallas as pl
from jax.experimental.pallas import tpu as pltpu

F32 = jnp.float32
BF16 = jnp.bfloat16

LANES = 128
VMEM_CAP_BYTES = 60000 * 1024

FOX_HEADS = 8
FOX_HEAD_DIM = 128
RWKV_HEADS = 16
RWKV_HEAD_DIM = 64
DECAY_LORA = 96
AAA_LORA = 96
GATE_LORA = 256
LORA_PAD = 128
INPROJ_TN = 512
RWKV_LNX_EPS = 64e-5
DEPTH = 1
DEEPNORM_ALPHA = (2.0 * DEPTH) ** 0.25
LN_EPS = 1e-5
RMS_EPS = 1e-6
RWKV_CHUNK = 64
HEADS_PER_TILE = LANES // RWKV_HEAD_DIM


def _vmem_limit(pipelined_bytes, resident_bytes=0, temp_bytes=0):
    need = 2 * pipelined_bytes + resident_bytes + temp_bytes + (2 << 20)
    return int(min(VMEM_CAP_BYTES, need))


def _nbytes(shape, dtype):
    n = 1
    for s in shape:
        n *= s
    return n * jnp.dtype(dtype).itemsize


def _layernorm(x):
    mu = jnp.mean(x, axis=-1, keepdims=True)
    xc = x - mu
    var = jnp.mean(xc * xc, axis=-1, keepdims=True)
    return xc * lax.rsqrt(var + LN_EPS)


def _sigmoid(x):
    return 1.0 / (1.0 + jnp.exp(-x))


def _dot(a, b):
    return jnp.dot(a, b, preferred_element_type=F32)


def _dot_nt(a, b):
    return lax.dot_general(a, b, (((1,), (1,)), ((), ())), preferred_element_type=F32)


def _dot_tn(a, b):
    return lax.dot_general(a, b, (((0,), (0,)), ((), ())), preferred_element_type=F32)


def _split3(x):
    hi = x.astype(BF16)
    r1 = x - hi.astype(F32)
    mid = r1.astype(BF16)
    lo = (r1 - mid.astype(F32)).astype(BF16)
    return hi, mid, lo


def _ada_kernel(c_ref, w_ref, b_ref, o_ref):
    c = c_ref[...]
    s = (c * _sigmoid(c)).astype(BF16)
    o_ref[...] = _dot(s, w_ref[...].astype(BF16)) + b_ref[...]


def _ada(c_pad, w_ada, b_ada, *, tn=1024):
    rows, d = c_pad.shape
    n = w_ada.shape[1]
    return pl.pallas_call(
        _ada_kernel,
        grid=(n // tn,),
        in_specs=[
            pl.BlockSpec((rows, d), lambda j: (0, 0)),
            pl.BlockSpec((d, tn), lambda j: (0, j)),
            pl.BlockSpec((1, tn), lambda j: (0, j)),
        ],
        out_specs=pl.BlockSpec((rows, tn), lambda j: (0, j)),
        out_shape=jax.ShapeDtypeStruct((rows, n), F32),
        compiler_params=pltpu.CompilerParams(
            dimension_semantics=("arbitrary",),
            vmem_limit_bytes=_vmem_limit(_nbytes((d, tn), F32), temp_bytes=_nbytes((d, tn), BF16)),
        ),
        name="ada",
    )(c_pad, w_ada, b_ada)


def _inproj_kernel(x_ref, scale_ref, shift_ref, wt_ref, wft_ref, qg_ref, kg_ref,
                   main_ref, small_ref, ft_ref, h_scr, *, n_q, n_k, n_main):
    j = pl.program_id(1)
    tn = main_ref.shape[1]

    @pl.when(j == 0)
    def _():
        h = _layernorm(x_ref[...]) * (1.0 + scale_ref[0]) + shift_ref[0]
        h_scr[...] = h.astype(BF16)

    def proj():
        return _dot_nt(h_scr[...], wt_ref[...].astype(BF16))

    @pl.when(j < n_q + n_k)
    def _():
        acc = proj()
        gain = jnp.where(j < n_q, qg_ref[...] * (FOX_HEAD_DIM ** -0.5), kg_ref[...])
        for hh in range(tn // FOX_HEAD_DIM):
            sl = slice(hh * FOX_HEAD_DIM, (hh + 1) * FOX_HEAD_DIM)
            a = acc[:, sl]
            ms = jnp.mean(a * a, axis=-1, keepdims=True)
            main_ref[:, sl] = (a * lax.rsqrt(ms + RMS_EPS) * gain).astype(BF16)

    @pl.when((j >= n_q + n_k) & (j < n_main))
    def _():
        main_ref[...] = proj().astype(BF16)

    @pl.when(j == n_main)
    def _():
        small_ref[...] = proj()
        ft_ref[...] = _dot_nt(wft_ref[...].astype(BF16), h_scr[...])


def _inproj(x2d, scale1, shift1, w_in_t, qg, kg, *, seq, group_rows, lora_row, fgate_row, tm=1024, tn=512):
    t, d = x2d.shape
    fox_w = FOX_HEADS * FOX_HEAD_DIM
    assert all(first % 8 == 0 and count % tn == 0 for first, count in group_rows)
    assert lora_row % 8 == 0 and fgate_row % 8 == 0 and lora_row + tn <= w_in_t.shape[0]
    n_main = sum(count // tn for _, count in group_rows)
    tiles_per_batch = seq // tm
    kern = functools.partial(_inproj_kernel, n_q=fox_w // tn, n_k=fox_w // tn, n_main=n_main)
    pipelined = (_nbytes((tm, d), F32) + _nbytes((tn, d), F32) + _nbytes((tm, tn), BF16)
                 + _nbytes((tm, tn), F32) + _nbytes((8, tm), F32))

    def w_rows(i, j):
        row, hi = jnp.int32(lora_row), n_main
        for first, count in reversed(group_rows):
            lo = hi - count // tn
            row = jnp.where(j < hi, first + tn * (j - lo), row)
            hi = lo
        return pl.multiple_of(row, 8), 0

    return pl.pallas_call(
        kern,
        grid=(t // tm, n_main + 1),
        in_specs=[
            pl.BlockSpec((tm, d), lambda i, j: (i, 0)),
            pl.BlockSpec((1, 1, d), lambda i, j: (i // tiles_per_batch, 0, 0)),
            pl.BlockSpec((1, 1, d), lambda i, j: (i // tiles_per_batch, 0, 0)),
            pl.BlockSpec((pl.Element(tn), pl.Element(d)), w_rows),
            pl.BlockSpec((pl.Element(FOX_HEADS), pl.Element(d)), lambda i, j: (fgate_row, 0)),
            pl.BlockSpec((1, FOX_HEAD_DIM), lambda i, j: (0, 0)),
            pl.BlockSpec((1, FOX_HEAD_DIM), lambda i, j: (0, 0)),
        ],
        out_specs=[
            pl.BlockSpec((tm, tn), lambda i, j: (i, jnp.minimum(j, n_main - 1))),
            pl.BlockSpec((tm, tn), lambda i, j: (i, 0)),
            pl.BlockSpec((FOX_HEADS, tm), lambda i, j: (0, i)),
        ],
        out_shape=[
            jax.ShapeDtypeStruct((t, n_main * tn), BF16),
            jax.ShapeDtypeStruct((t, tn), F32),
            jax.ShapeDtypeStruct((FOX_HEADS, t), F32),
        ],
        scratch_shapes=[pltpu.VMEM((tm, d), BF16)],
        compiler_params=pltpu.CompilerParams(
            dimension_semantics=("arbitrary", "arbitrary"),
            vmem_limit_bytes=_vmem_limit(pipelined, _nbytes((tm, d), BF16),
                                         3 * _nbytes((tm, d), F32)),
        ),
        name="inproj",
    )(x2d, scale1, shift1, w_in_t, w_in_t, qg, kg)


def _fox_prep_kernel(ft_ref, bias_ref, o_ref, *, blocks_per_seq):
    x = ft_ref[...] + bias_ref[...]
    lf = jnp.minimum(x, 0.0) - jnp.log1p(jnp.exp(-jnp.abs(x)))
    rows, width = lf.shape
    r = lax.broadcasted_iota(jnp.int32, (width, width), 0)
    c = lax.broadcasted_iota(jnp.int32, (width, width), 1)
    upper = (r <= c).astype(BF16)
    within = sum(_dot(p, upper) for p in _split3(lf))
    tot = jnp.broadcast_to(within[:, width - 1:width], (rows, width))
    rr = lax.broadcasted_iota(jnp.int32, (rows, rows), 0)
    cc = lax.broadcasted_iota(jnp.int32, (rows, rows), 1)
    prior = ((rr // blocks_per_seq == cc // blocks_per_seq) & (cc < rr)).astype(BF16)
    offset = sum(_dot(prior, p) for p in _split3(tot))
    o_ref[...] = within + offset


def _fox_prep(ft_rows, bias_rows, *, blocks_per_seq):
    return pl.pallas_call(
        functools.partial(_fox_prep_kernel, blocks_per_seq=blocks_per_seq),
        out_shape=jax.ShapeDtypeStruct(ft_rows.shape, F32),
        name="fox_prep",
    )(ft_rows, bias_rows)


def _fox_attn_kernel(q_ref, k_ref, v_ref, og_ref, cum_ref, o_ref, *, tq):
    seq = q_ref.shape[0]
    n_q = seq // tq
    negcum = -cum_ref[0, 0]
    row = lax.broadcasted_iota(jnp.int32, (tq, tq), 0)
    col = lax.broadcasted_iota(jnp.int32, (tq, tq), 1)
    causal = col <= row

    def logits(qi):
        q0, kend = qi * tq, (qi + 1) * tq
        q = q_ref[q0:kend, :]
        diag = _dot_nt(q, k_ref[q0:kend, :]) + negcum[:, q0:kend]
        diag = jnp.where(causal, diag, -jnp.inf)
        past = _dot_nt(q, k_ref[0:q0, :]) + negcum[:, 0:q0] if qi else None
        return past, diag

    def attend(qi, past, diag):
        q0, kend = qi * tq, (qi + 1) * tq
        m = jnp.max(diag, axis=-1, keepdims=True)
        if past is not None:
            m = jnp.maximum(m, jnp.max(past, axis=-1, keepdims=True))
        p = jnp.exp(diag - m)
        l = jnp.sum(p, axis=-1, keepdims=True)
        o = _dot(p.astype(BF16), v_ref[q0:kend, :])
        if past is not None:
            p = jnp.exp(past - m)
            l = l + jnp.sum(p, axis=-1, keepdims=True)
            o = o + _dot(p.astype(BF16), v_ref[0:q0, :])
        gate = _sigmoid(og_ref[q0:kend, :].astype(F32))
        o_ref[q0:kend, :] = (o / l * gate).astype(BF16)

    nxt = logits(0)
    for qi in range(n_q):
        cur, nxt = nxt, (logits(qi + 1) if qi + 1 < n_q else None)
        attend(qi, *cur)


def _fox_attn(main, cum, *, batch, seq, tq=256):
    t = main.shape[0]
    hd = FOX_HEAD_DIM
    nh = FOX_HEADS
    blk = lambda off: pl.BlockSpec((seq, hd), lambda b, h: (b, off + h))
    pipelined = 5 * _nbytes((seq, hd), BF16) + _nbytes((8, seq), F32)
    return pl.pallas_call(
        functools.partial(_fox_attn_kernel, tq=tq),
        grid=(batch, nh),
        in_specs=[blk(0), blk(nh), blk(2 * nh), blk(3 * nh),
                  pl.BlockSpec((1, 1, 1, seq), lambda b, h: (h, b, 0, 0))],
        out_specs=pl.BlockSpec((seq, hd), lambda b, h: (b, h)),
        out_shape=jax.ShapeDtypeStruct((t, nh * hd), BF16),
        compiler_params=pltpu.CompilerParams(
            dimension_semantics=("arbitrary", "arbitrary"),
            vmem_limit_bytes=_vmem_limit(pipelined, 0, 6 * _nbytes((tq, seq), F32)),
        ),
        name="fox_attn",
    )(main, main, main, main, cum)


def _run_stages(gen, side=None):
    result = {}
    live = [g for g in (gen, side) if g is not None]
    while live:
        for g in list(live):
            try:
                next(g)
            except StopIteration as stop:
                result[g] = stop.value
                live.remove(g)
    return result[gen if side is None else side]


def _rwkv_inputs(raw_refs, small_ref, mu_refs, mu_s_ref, w0_ref, w2_ref, a0_ref, a2_ref, g2_ref,
                 prev_refs, prev_s, seqs):
    def shift_mix(x, prev_ref, q, mu):
        rows = x.shape[0]
        xp = pltpu.roll(x, 1, axis=0)
        row = lax.broadcasted_iota(jnp.int32, x.shape, 0)
        xp = jnp.where(row == 0, prev_ref[q, 0:1, :], xp)
        prev_ref[q, 0:1, :] = x[rows - 1:rows, :]
        return x + (xp - x) * mu

    r, k, v = ([shift_mix(ref[q].astype(F32), prev, q, mu[...]) for q in seqs]
               for ref, prev, mu in zip(raw_refs, prev_refs, mu_refs))
    sm = jnp.concatenate([shift_mix(small_ref[q], prev_s, q, mu_s_ref[...]) for q in seqs], axis=0)
    xw = sm[:, 0:LORA_PAD]
    xa = sm[:, DECAY_LORA:DECAY_LORA + LORA_PAD]
    xg = sm[:, DECAY_LORA + AAA_LORA:DECAY_LORA + AAA_LORA + GATE_LORA]

    z = w0_ref[...] + _dot(jnp.tanh(xw).astype(BF16), w2_ref[...])
    w_raw = -(jnp.maximum(-z, 0.0) + jnp.log1p(jnp.exp(-jnp.abs(z)))) - 0.5
    lw = -jnp.exp(w_raw)
    a = _sigmoid(a0_ref[...] + _dot(xa.astype(BF16), a2_ref[...]))
    g = _dot(_sigmoid(xg).astype(BF16), g2_ref[...])
    ch = lw.shape[0] // len(seqs)
    per_seq = lambda x: [x[j * ch:(j + 1) * ch, :] for j in range(len(seqs))]
    return r, k, v, per_seq(lw), per_seq(a), per_seq(g)


def _rwkv_kernel(r_ref, k_ref, v_ref, small_ref, mu_r_ref, mu_k_ref, mu_v_ref, mu_s_ref,
                 w0_ref, w2_ref, a0_ref, a2_ref, g2_ref,
                 kk_ref, ka_ref, rk_ref, lng_ref, lnb_ref, *rest, n_tiles, n_seq, seq_group, n_cast):
    cast_src, y_ref, cast_dst = rest[:n_cast], rest[n_cast], rest[n_cast + 1:2 * n_cast + 1]
    state, prev_r, prev_k, prev_v, prev_s = rest[2 * n_cast + 1:]
    for src, dst in zip(cast_src, cast_dst):
        dst[...] = src[...].astype(BF16)

    @pl.when(pl.program_id(1) == 0)
    def _():
        for ref in (state, prev_r, prev_k, prev_v, prev_s):
            ref[...] = jnp.zeros_like(ref)

    ch = y_ref.shape[1]
    hd = RWKV_HEAD_DIM
    lane = lax.broadcasted_iota(jnp.int32, (1, LANES), 1)
    m0 = lane < hd
    n2 = HEADS_PER_TILE * ch
    rr = lax.broadcasted_iota(jnp.int32, (n2, n2), 0)
    cc = lax.broadcasted_iota(jnp.int32, (n2, n2), 1)
    same = (rr // ch) == (cc // ch)
    strict = same & (rr > cc)
    incl = same & (rr >= cc)
    lr = lax.broadcasted_iota(jnp.int32, (ch, ch), 0)
    lc = lax.broadcasted_iota(jnp.int32, (ch, ch), 1)
    lower = (lc <= lr).astype(BF16)

    def head_sum(x):
        s0 = jnp.sum(jnp.where(m0, x, 0.0), axis=-1, keepdims=True)
        s1 = jnp.sum(jnp.where(m0, 0.0, x), axis=-1, keepdims=True)
        return jnp.where(m0, s0, s1)

    def stack(x):
        return jnp.concatenate([jnp.where(m0, x, 0.0), jnp.where(m0, 0.0, x)], axis=0)

    sls = [slice(p * LANES, (p + 1) * LANES) for p in range(n_tiles)]

    def each(fn, *cols):
        return [fn(*args) for args in zip(*cols)]

    stack2 = lambda x, y: jnp.concatenate([stack(x), stack(y)], axis=0).astype(BF16)
    incl2 = jnp.concatenate([incl, incl], axis=1)

    def prepare(seqs):
        per_chain = lambda ref: [ref[:, sl] for _ in seqs for sl in sls]
        full = _rwkv_inputs((r_ref, k_ref, v_ref), small_ref, (mu_r_ref, mu_k_ref, mu_v_ref), mu_s_ref,
                            w0_ref, w2_ref, a0_ref, a2_ref, g2_ref, (prev_r, prev_k, prev_v), prev_s, seqs)
        r, k, v, lw, a, g = ([x[j][:, sl] for j in range(len(seqs)) for sl in sls] for x in full)
        yield
        kk = each(lambda k_, w_: k_ * w_, k, per_chain(kk_ref))
        kk = each(lambda x: x / jnp.maximum(jnp.sqrt(head_sum(x * x)), 1e-12), kk)
        kr = each(lambda k_, a_, w_: k_ * (1.0 + (a_ - 1.0) * w_), k, a, per_chain(ka_ref))
        yield
        cum = each(lambda x: sum(_dot(lower, part) for part in _split3(x)[:2]), lw)
        yield
        gam = each(jnp.exp, cum)
        ginv = each(lambda x: jnp.exp(-x), cum)
        g_end = each(lambda x: x[ch - 1:ch, :], gam)
        yield
        a_t = each(lambda kk_, c_, lw_: -kk_ * jnp.exp(c_ - lw_), kk, cum, lw)
        r_t = each(lambda r_, g_: r_ * g_, r, gam)
        lhs = each(stack2, a_t, r_t)
        yield
        b_h = each(lambda kk_, a_, gi_: kk_ * a_ * gi_, kk, a, ginv)
        k_h = each(lambda kr_, gi_: kr_ * gi_, kr, ginv)
        rhs = each(stack2, b_h, k_h)
        yield
        vs = each(lambda v_: stack(v_).astype(BF16), v)
        bk = each(lambda b_, k_, ge_: stack2(b_ * ge_, k_ * ge_), b_h, k_h, g_end)
        yield
        bonus = each(lambda r_, kr_, w_, v_: head_sum(r_ * kr_ * w_) * v_, r, kr, per_chain(rk_ref), v)
        return lhs, rhs, vs, bk, g_end, bonus, g

    def advance(seqs, prepared):
        lhs, rhs, vs, bk, g_end, bonus, g = prepared
        per_chain = lambda ref: [ref[:, sl] for _ in seqs for sl in sls]
        slots = [q * n_tiles + p for q in seqs for p in range(n_tiles)]
        sc = each(_dot_nt, lhs, rhs)
        yield
        ab = each(lambda s: jnp.where(strict, s[0:n2, 0:n2], 0.0).astype(BF16), sc)
        ak = each(lambda s: jnp.where(strict, s[0:n2, n2:2 * n2], 0.0).astype(BF16), sc)
        rbk = each(lambda s: jnp.where(incl2, s[n2:2 * n2, :], 0.0).astype(BF16), sc)
        yield
        st = [state[i] for i in slots]
        ah = each(lambda l_, s_: _dot_nt(l_, s_.astype(BF16)), lhs, st)
        yield
        x = each(lambda ah_, ak_, vs_: ah_[0:n2] + _dot(ak_, vs_), ah, ak, vs)
        yield
        pw = ab
        n_lvl = ch.bit_length() - 1
        for lvl in range(n_lvl):
            x = each(lambda x_, p_: x_ + _dot(p_, x_.astype(BF16)), x, pw)
            yield
            if lvl + 1 < n_lvl:
                pw = each(lambda p_: _dot(p_, p_).astype(BF16), pw)
                yield
        uv = each(lambda x_, vs_: jnp.concatenate([x_.astype(BF16), vs_], axis=0), x, vs)
        ys = each(lambda ah_, m_, uv_: ah_[n2:2 * n2] + _dot(m_, uv_), ah, rbk, uv)
        yield
        y = each(lambda ys_: ys_[0:ch] + ys_[ch:n2], ys)
        new_st = each(lambda s_, ge_, uv_, bk_: s_ * ge_ + _dot_tn(uv_, bk_), st, g_end, uv, bk)
        for i, s_new in zip(slots, new_st):
            state[i] = s_new
        yield
        mu = each(lambda y_: head_sum(y_) * (1.0 / hd), y)
        yc = each(lambda y_, m_: y_ - m_, y, mu)
        var = each(lambda c_: head_sum(c_ * c_) * (1.0 / hd), yc)
        yield
        yn = each(lambda c_, v_, g_, b_: c_ * lax.rsqrt(v_ + RWKV_LNX_EPS) * g_ + b_,
                  yc, var, per_chain(lng_ref), per_chain(lnb_ref))
        for j, i in enumerate(slots):
            y_ref[i // n_tiles, :, sls[i % n_tiles]] = ((yn[j] + bonus[j]) * g[j]).astype(BF16)

    groups = [tuple(range(s, s + seq_group)) for s in range(0, n_seq, seq_group)]
    prepared = _run_stages(prepare(groups[0]))
    for gi, seqs in enumerate(groups):
        nxt = prepare(groups[gi + 1]) if gi + 1 < len(groups) else None
        prepared = _run_stages(advance(seqs, prepared), nxt)


def _slab_spec(w, n_steps, step_of):
    rep = 1
    while w.shape[0] % (n_steps // rep * 16):
        rep *= 2
        assert rep <= n_steps
    return pl.BlockSpec((w.shape[0] // (n_steps // rep), w.shape[1]),
                        lambda *idx: (step_of(*idx) // rep, 0))


def _rwkv(main, small, mus, mu_s, w0, w2p, a0, a2p, g2, head_rows, cast_weights, *, batch, seq,
          n_seq=4, seq_group=2):
    t = main.shape[0]
    width = RWKV_HEADS * RWKV_HEAD_DIM
    nsm = small.shape[1]
    col0 = (4 * FOX_HEADS * FOX_HEAD_DIM) // width
    ch = RWKV_CHUNK
    steps = seq // ch
    n_tiles = width // LANES
    assert batch % n_seq == 0
    groups = batch // n_seq
    grouped = lambda x: x.reshape(groups, n_seq, seq, x.shape[-1])
    act = lambda cols, blk: pl.BlockSpec((None, n_seq, ch, cols), lambda b, c: (b, 0, c, blk))
    row = lambda n: pl.BlockSpec((1, n), lambda b, c: (0, 0))
    whole = lambda w: pl.BlockSpec(w.shape, lambda b, c: (0, 0))
    slabs = [_slab_spec(w, groups * steps, lambda b, c: b * steps + c) for w in cast_weights]
    pipelined = n_seq * (4 * _nbytes((ch, width), BF16) + _nbytes((ch, nsm), F32))
    pipelined += sum(_nbytes(s.block_shape, F32) + _nbytes(s.block_shape, BF16) for s in slabs)
    pipelined += sum(_nbytes(w.shape, BF16) for w in (w2p, a2p, g2))
    scratch = [(n_seq * n_tiles, LANES, LANES)] + [(n_seq, 8, width)] * 3 + [(n_seq, 8, nsm)]
    y, *copies = pl.pallas_call(
        functools.partial(_rwkv_kernel, n_tiles=n_tiles, n_seq=n_seq, seq_group=seq_group,
                          n_cast=len(slabs)),
        grid=(groups, steps),
        in_specs=[act(width, col0), act(width, col0 + 1), act(width, col0 + 2), act(nsm, 0),
                  row(width), row(width), row(width), row(nsm),
                  row(width), whole(w2p), row(width), whole(a2p), whole(g2)]
        + [row(width)] * 5 + slabs,
        out_specs=[act(width, 0)] + slabs,
        out_shape=[jax.ShapeDtypeStruct((groups, n_seq, seq, width), BF16)]
        + [jax.ShapeDtypeStruct(w.shape, BF16) for w in cast_weights],
        scratch_shapes=[pltpu.VMEM(s, F32) for s in scratch],
        compiler_params=pltpu.CompilerParams(
            dimension_semantics=("arbitrary", "arbitrary"),
            vmem_limit_bytes=_vmem_limit(pipelined, sum(_nbytes(s, F32) for s in scratch), 24 << 20),
        ),
        name="rwkv",
    )(grouped(main), grouped(main), grouped(main), grouped(small), *mus, mu_s, w0, w2p, a0, a2p, g2,
      *head_rows, *cast_weights)
    return y.reshape(t, width), copies


def _mixout_kernel(o_ref, y_ref, ga0_ref, ga1_ref, gb0_ref, gb1_ref, wa_ref, wb_ref, wo_ref, x_ref,
                   gate_ref, lng_ref, lnb_ref, scale_ref, shift_ref, x1_ref, h2_ref, *, n_sub):
    tm, d = x_ref.shape
    rows = tm // n_sub
    half = d // 2
    sl = [slice(s * rows, (s + 1) * rows) for s in range(n_sub)]

    def branches(s):
        return _dot(o_ref[sl[s], :], wa_ref[...]), _dot(y_ref[sl[s], :], wb_ref[...])

    def mixed(s, ua, ub):
        parts = []
        for c, (ga, gb) in enumerate(((ga0_ref, gb0_ref), (ga1_ref, gb1_ref))):
            cs = slice(c * half, (c + 1) * half)
            parts.append(_sigmoid(ga[sl[s], :].astype(F32)) * ua[:, cs]
                         + _sigmoid(gb[sl[s], :].astype(F32)) * ub[:, cs])
        return _dot(jnp.concatenate(parts, axis=1).astype(BF16), wo_ref[...])

    def norms(s, mix):
        z = DEEPNORM_ALPHA * x_ref[sl[s], :] + gate_ref[0] * mix
        x1 = _layernorm(z) * lng_ref[...] + lnb_ref[...]
        x1_ref[sl[s], :] = x1
        h2_ref[sl[s], :] = (_layernorm(x1) * (1.0 + scale_ref[0]) + shift_ref[0]).astype(BF16)

    u = branches(0)
    mix_prev = None
    for s in range(n_sub):
        u_next = branches(s + 1) if s + 1 < n_sub else None
        mix = mixed(s, *u)
        if mix_prev is not None:
            norms(s - 1, mix_prev)
        u, mix_prev = u_next, mix
    norms(n_sub - 1, mix_prev)


def _mixout(o, y, main, wa, wb, wo, x2d, gate1, ln_g, ln_b, scale2, shift2, *, seq, tm=256, n_sub=2):
    t, d = x2d.shape
    kdim = o.shape[1]
    half = d // 2
    gate_blk0 = (main.shape[1] - 2 * d) // half
    tiles_per_batch = seq // tm
    mod = pl.BlockSpec((1, 1, d), lambda i: (i // tiles_per_batch, 0, 0))
    row = pl.BlockSpec((1, d), lambda i: (0, 0))
    gate = lambda c: pl.BlockSpec((tm, half), lambda i: (i, gate_blk0 + c))
    resident = lambda w: pl.BlockSpec(w.shape, lambda i: (0, 0), pipeline_mode=pl.Buffered(1))
    pipelined = (2 * _nbytes((tm, kdim), BF16) + 4 * _nbytes((tm, half), BF16)
                 + 2 * _nbytes((tm, d), F32) + _nbytes((tm, d), BF16))
    weights = _nbytes(wa.shape, BF16) + _nbytes(wb.shape, BF16) + _nbytes(wo.shape, BF16)
    return pl.pallas_call(
        functools.partial(_mixout_kernel, n_sub=n_sub),
        grid=(t // tm,),
        in_specs=[
            pl.BlockSpec((tm, kdim), lambda i: (i, 0)),
            pl.BlockSpec((tm, kdim), lambda i: (i, 0)),
            gate(0), gate(1), gate(2), gate(3),
            resident(wa), resident(wb), resident(wo),
            pl.BlockSpec((tm, d), lambda i: (i, 0)),
            mod, row, row, mod, mod,
        ],
        out_specs=[pl.BlockSpec((tm, d), lambda i: (i, 0)),
                   pl.BlockSpec((tm, d), lambda i: (i, 0))],
        out_shape=[jax.ShapeDtypeStruct((t, d), F32), jax.ShapeDtypeStruct((t, d), BF16)],
        compiler_params=pltpu.CompilerParams(
            dimension_semantics=("arbitrary",),
            vmem_limit_bytes=_vmem_limit(pipelined, weights, 8 * _nbytes((tm, d), F32)),
        ),
        name="mixout",
    )(o, y, main, main, main, main, wa, wb, wo, x2d, gate1, ln_g, ln_b, scale2, shift2)


def _ffn_kernel(h_ref, wg_ref, wu_ref, wd_ref, x1_ref, gate_ref, lng_ref, lnb_ref, o_ref, acc):
    f = pl.program_id(1)
    @pl.when(f == 0)
    def _():
        acc[...] = jnp.zeros_like(acc)

    h = h_ref[...]
    g = _dot(h, wg_ref[...])
    u = _dot(h, wu_ref[...])
    act = (g * _sigmoid(g) * u).astype(BF16)
    acc[...] += _dot(act, wd_ref[...])

    @pl.when(f == pl.num_programs(1) - 1)
    def _():
        z = DEEPNORM_ALPHA * x1_ref[...] + gate_ref[0] * acc[...]
        o_ref[...] = _layernorm(z) * lng_ref[...] + lnb_ref[...]


def _ffn(h2, w_gu, w_down, x1, gate2, ln_g, ln_b, *, seq, tm=512, tf=512):
    t, d = x1.shape
    d_ff = w_down.shape[0]
    nf = d_ff // tf
    tiles_per_batch = seq // tm
    mod = pl.BlockSpec((1, 1, d), lambda i, f: (i // tiles_per_batch, 0, 0))
    row = pl.BlockSpec((1, d), lambda i, f: (0, 0))
    pipelined = (_nbytes((tm, d), BF16) + 3 * _nbytes((d, tf), BF16) + 2 * _nbytes((tm, d), F32))
    return pl.pallas_call(
        _ffn_kernel,
        grid=(t // tm, nf),
        in_specs=[
            pl.BlockSpec((tm, d), lambda i, f: (i, 0)),
            pl.BlockSpec((d, tf), lambda i, f: (0, f)),
            pl.BlockSpec((d, tf), lambda i, f: (0, nf + f)),
            pl.BlockSpec((tf, d), lambda i, f: (f, 0)),
            pl.BlockSpec((tm, d), lambda i, f: (i, 0)),
            mod, row, row,
        ],
        out_specs=pl.BlockSpec((tm, d), lambda i, f: (i, 0)),
        out_shape=jax.ShapeDtypeStruct((t, d), F32),
        scratch_shapes=[pltpu.VMEM((tm, d), F32)],
        compiler_params=pltpu.CompilerParams(
            dimension_semantics=("arbitrary", "arbitrary"),
            vmem_limit_bytes=_vmem_limit(pipelined, _nbytes((tm, d), F32),
                                         2 * _nbytes((tm, d), F32) + 4 * _nbytes((tm, tf), F32)),
        ),
        name="ffn",
    )(h2, w_gu, w_gu, w_down, x1, gate2, ln_g, ln_b)


def _pad_rows(w, rows):
    return jnp.zeros((rows, w.shape[1]), w.dtype).at[:w.shape[0]].set(w)


def kernel(x, c, w_ada, b_ada, w_in, b_fgate, q_norm_g, k_norm_g, rwkv_mu, rwkv_w0, rwkv_w2, rwkv_a0, rwkv_a2, rwkv_g2, rwkv_k_k, rwkv_k_a, rwkv_r_k, rwkv_lnx_g, rwkv_lnx_b, w_branch_a, w_branch_b, w_out, ln1_g, ln1_b, w_ffn_gu, w_ffn_down, ln2_g, ln2_b):
    batch, seq, d = x.shape
    t = batch * seq
    fox_w = FOX_HEADS * FOX_HEAD_DIM
    rw_w = RWKV_HEADS * RWKV_HEAD_DIM
    fox_cols = 4 * fox_w + FOX_HEADS
    lora0 = fox_cols + 3 * rw_w
    gate0 = lora0 + DECAY_LORA + AAA_LORA + GATE_LORA
    assert seq % RWKV_CHUNK == 0 and w_in.shape[1] == gate0 + 2 * d

    lora_w = gate0 - lora0
    mu_r = rwkv_mu[None, 0:rw_w]
    mu_k = rwkv_mu[None, rw_w:2 * rw_w]
    mu_v = rwkv_mu[None, 2 * rw_w:3 * rw_w]
    mu_s = jnp.zeros((1, INPROJ_TN), rwkv_mu.dtype).at[0, :lora_w].set(rwkv_mu[3 * rw_w:])
    w2p = _pad_rows(rwkv_w2, LORA_PAD).astype(BF16)
    a2p = _pad_rows(rwkv_a2, LORA_PAD).astype(BF16)
    g2 = rwkv_g2.astype(BF16)
    row = lambda v: v.reshape(1, -1)

    c_pad = jnp.zeros((8, d), c.dtype).at[:batch].set(c)
    mod = _ada(c_pad, w_ada, row(b_ada))[:batch].reshape(batch, 6, 1, d)
    shift1, scale1, gate1, shift2, scale2, gate2 = (mod[:, i] for i in range(6))

    x2d = x.reshape(t, d)
    main, small, ft = _inproj(x2d, scale1, shift1, w_in.T, row(q_norm_g), row(k_norm_g), seq=seq,
                              group_rows=[(0, 4 * fox_w), (fox_cols, 3 * rw_w), (gate0, 2 * d)],
                              lora_row=lora0, fgate_row=4 * fox_w, tn=INPROJ_TN)

    blocks_per_seq = seq // LANES
    ft_rows = ft.reshape(FOX_HEADS * batch * blocks_per_seq, LANES)
    bias_rows = jnp.broadcast_to(b_fgate[:, None, None], (FOX_HEADS, batch * blocks_per_seq, LANES))
    cum = _fox_prep(ft_rows, bias_rows.reshape(ft_rows.shape), blocks_per_seq=blocks_per_seq)
    cum = cum.reshape(FOX_HEADS, batch, 1, seq)
    o = _fox_attn(main, cum, batch=batch, seq=seq)

    y, (wa, wb, wo, wgu, wdn) = _rwkv(
        main, small, (mu_r, mu_k, mu_v), mu_s, row(rwkv_w0), w2p, row(rwkv_a0), a2p, g2,
        [row(p) for p in (rwkv_k_k, rwkv_k_a, rwkv_r_k, rwkv_lnx_g, rwkv_lnx_b)],
        [w_branch_a, w_branch_b, w_out, w_ffn_gu, w_ffn_down], batch=batch, seq=seq)

    x1, h2 = _mixout(o, y, main, wa, wb, wo, x2d, gate1, row(ln1_g), row(ln1_b), scale2, shift2, seq=seq)

    out = _ffn(h2, wgu, wdn, x1, gate2, row(ln2_g), row(ln2_b), seq=seq)
    return out.reshape(batch, seq, d)
```

```python
import functools

import jax
import jax.numpy as jnp
from jax import lax
from jax.experimental import pallas as pl
from jax.experimental.pallas import tpu as pltpu

F32 = jnp.float32
BF16 = jnp.bfloat16

LANES = 128
VMEM_CAP_BYTES = 60000 * 1024

FOX_HEADS = 8
FOX_HEAD_DIM = 128
RWKV_HEADS = 16
RWKV_HEAD_DIM = 64
DECAY_LORA = 96
AAA_LORA = 96
GATE_LORA = 256
LORA_PAD = 128
INPROJ_TN = 512
RWKV_LNX_EPS = 64e-5
DEPTH = 1
DEEPNORM_ALPHA = (2.0 * DEPTH) ** 0.25
LN_EPS = 1e-5
RMS_EPS = 1e-6
RWKV_CHUNK = 64
HEADS_PER_TILE = LANES // RWKV_HEAD_DIM


def _vmem_limit(pipelined_bytes, resident_bytes=0, temp_bytes=0):
    need = 2 * pipelined_bytes + resident_bytes + temp_bytes + (2 << 20)
    return int(min(VMEM_CAP_BYTES, need))


def _nbytes(shape, dtype):
    n = 1
    for s in shape:
        n *= s
    return n * jnp.dtype(dtype).itemsize


def _layernorm(x):
    mu = jnp.mean(x, axis=-1, keepdims=True)
    xc = x - mu
    var = jnp.mean(xc * xc, axis=-1, keepdims=True)
    return xc * lax.rsqrt(var + LN_EPS)


def _sigmoid(x):
    return 1.0 / (1.0 + jnp.exp(-x))


def _dot(a, b):
    return jnp.dot(a, b, preferred_element_type=F32)


def _dot_nt(a, b):
    return lax.dot_general(a, b, (((1,), (1,)), ((), ())), preferred_element_type=F32)


def _dot_tn(a, b):
    return lax.dot_general(a, b, (((0,), (0,)), ((), ())), preferred_element_type=F32)


def _split3(x):
    hi = x.astype(BF16)
    r1 = x - hi.astype(F32)
    mid = r1.astype(BF16)
    lo = (r1 - mid.astype(F32)).astype(BF16)
    return hi, mid, lo


def _ada_kernel(c_ref, w_ref, b_ref, o_ref):
    c = c_ref[...]
    s = (c * _sigmoid(c)).astype(BF16)
    o_ref[...] = _dot(s, w_ref[...].astype(BF16)) + b_ref[...]


def _ada(c_pad, w_ada, b_ada, n, *, tn=1024):
    rows, d = c_pad.shape
    return pl.pallas_call(
        _ada_kernel,
        grid=(n // tn,),
        in_specs=[
            pl.BlockSpec((rows, d), lambda j: (0, 0)),
            pl.BlockSpec((d, tn), lambda j: (0, j)),
            pl.BlockSpec((1, tn), lambda j: (0, j)),
        ],
        out_specs=pl.BlockSpec((rows, tn), lambda j: (0, j)),
        out_shape=jax.ShapeDtypeStruct((rows, n), F32),
        compiler_params=pltpu.CompilerParams(
            dimension_semantics=("arbitrary",),
            vmem_limit_bytes=_vmem_limit(_nbytes((d, tn), F32), temp_bytes=_nbytes((d, tn), BF16)),
        ),
        name="ada",
    )(c_pad, w_ada, b_ada)


def _inproj_kernel(x_ref, scale_ref, shift_ref, wt_ref, wft_ref, qg_ref, kg_ref,
                   main_ref, small_ref, ft_ref, h_scr, *, n_q, n_k, n_main):
    j = pl.program_id(1)
    tn = main_ref.shape[1]

    @pl.when(j == 0)
    def _():
        h = _layernorm(x_ref[...]) * (1.0 + scale_ref[0]) + shift_ref[0]
        h_scr[...] = h.astype(BF16)

    def proj():
        return _dot_nt(h_scr[...], wt_ref[...].astype(BF16))

    @pl.when(j < n_q + n_k)
    def _():
        acc = proj()
        gain = jnp.where(j < n_q, qg_ref[...] * (FOX_HEAD_DIM ** -0.5), kg_ref[...])
        for hh in range(tn // FOX_HEAD_DIM):
            sl = slice(hh * FOX_HEAD_DIM, (hh + 1) * FOX_HEAD_DIM)
            a = acc[:, sl]
            ms = jnp.mean(a * a, axis=-1, keepdims=True)
            main_ref[:, sl] = (a * lax.rsqrt(ms + RMS_EPS) * gain).astype(BF16)

    @pl.when((j >= n_q + n_k) & (j < n_main))
    def _():
        main_ref[...] = proj().astype(BF16)

    @pl.when(j == n_main)
    def _():
        small_ref[...] = proj()
        ft_ref[...] = _dot_nt(wft_ref[...].astype(BF16), h_scr[...])


def _inproj(x2d, scale1, shift1, w_in_t, qg, kg, *, seq, group_rows, lora_row, fgate_row, tm=1024, tn=512):
    t, d = x2d.shape
    fox_w = FOX_HEADS * FOX_HEAD_DIM
    assert all(first % 8 == 0 and count % tn == 0 for first, count in group_rows)
    assert lora_row % 8 == 0 and fgate_row % 8 == 0 and lora_row + tn <= w_in_t.shape[0]
    n_main = sum(count // tn for _, count in group_rows)
    tiles_per_batch = seq // tm
    kern = functools.partial(_inproj_kernel, n_q=fox_w // tn, n_k=fox_w // tn, n_main=n_main)
    pipelined = (_nbytes((tm, d), F32) + _nbytes((tn, d), F32) + _nbytes((tm, tn), BF16)
                 + _nbytes((tm, tn), F32) + _nbytes((8, tm), F32))

    def w_rows(i, j):
        row, hi = jnp.int32(lora_row), n_main
        for first, count in reversed(group_rows):
            lo = hi - count // tn
            row = jnp.where(j < hi, first + tn * (j - lo), row)
            hi = lo
        return pl.multiple_of(row, 8), 0

    return pl.pallas_call(
        kern,
        grid=(t // tm, n_main + 1),
        in_specs=[
            pl.BlockSpec((tm, d), lambda i, j: (i, 0)),
            pl.BlockSpec((1, 1, d), lambda i, j: (i // tiles_per_batch, 0, 0)),
            pl.BlockSpec((1, 1, d), lambda i, j: (i // tiles_per_batch, 0, 0)),
            pl.BlockSpec((pl.Element(tn), pl.Element(d)), w_rows),
            pl.BlockSpec((pl.Element(FOX_HEADS), pl.Element(d)), lambda i, j: (fgate_row, 0)),
            pl.BlockSpec((1, FOX_HEAD_DIM), lambda i, j: (0, 0)),
            pl.BlockSpec((1, FOX_HEAD_DIM), lambda i, j: (0, 0)),
        ],
        out_specs=[
            pl.BlockSpec((tm, tn), lambda i, j: (i, jnp.minimum(j, n_main - 1))),
            pl.BlockSpec((tm, tn), lambda i, j: (i, 0)),
            pl.BlockSpec((FOX_HEADS, tm), lambda i, j: (0, i)),
        ],
        out_shape=[
            jax.ShapeDtypeStruct((t, n_main * tn), BF16),
            jax.ShapeDtypeStruct((t, tn), F32),
            jax.ShapeDtypeStruct((FOX_HEADS, t), F32),
        ],
        scratch_shapes=[pltpu.VMEM((tm, d), BF16)],
        compiler_params=pltpu.CompilerParams(
            dimension_semantics=("arbitrary", "arbitrary"),
            vmem_limit_bytes=_vmem_limit(pipelined, _nbytes((tm, d), BF16),
                                         3 * _nbytes((tm, d), F32)),
        ),
        name="inproj",
    )(x2d, scale1, shift1, w_in_t, w_in_t, qg, kg)


def _fox_prep_kernel(ft_ref, bias_ref, o_ref, *, blocks_per_seq):
    x = ft_ref[...] + bias_ref[...]
    lf = jnp.minimum(x, 0.0) - jnp.log1p(jnp.exp(-jnp.abs(x)))
    rows, width = lf.shape
    r = lax.broadcasted_iota(jnp.int32, (width, width), 0)
    c = lax.broadcasted_iota(jnp.int32, (width, width), 1)
    upper = (r <= c).astype(BF16)
    within = sum(_dot(p, upper) for p in _split3(lf))
    tot = jnp.broadcast_to(within[:, width - 1:width], (rows, width))
    rr = lax.broadcasted_iota(jnp.int32, (rows, rows), 0)
    cc = lax.broadcasted_iota(jnp.int32, (rows, rows), 1)
    prior = ((rr // blocks_per_seq == cc // blocks_per_seq) & (cc < rr)).astype(BF16)
    offset = sum(_dot(prior, p) for p in _split3(tot))
    o_ref[...] = within + offset


def _fox_prep(ft_rows, bias_rows, *, blocks_per_seq):
    return pl.pallas_call(
        functools.partial(_fox_prep_kernel, blocks_per_seq=blocks_per_seq),
        out_shape=jax.ShapeDtypeStruct(ft_rows.shape, F32),
        name="fox_prep",
    )(ft_rows, bias_rows)


def _fox_attn_kernel(q_ref, k_ref, v_ref, og_ref, cum_ref, c_ref, wada_ref, bada_ref, o_ref, mod_ref, *, tq):
    _ada_kernel(c_ref, wada_ref, bada_ref, mod_ref)

    seq = q_ref.shape[0]
    n_q = seq // tq
    negcum = -cum_ref[0, 0]
    row = lax.broadcasted_iota(jnp.int32, (tq, tq), 0)
    col = lax.broadcasted_iota(jnp.int32, (tq, tq), 1)
    causal = col <= row

    def logits(qi):
        q0, kend = qi * tq, (qi + 1) * tq
        q = q_ref[q0:kend, :]
        diag = _dot_nt(q, k_ref[q0:kend, :]) + negcum[:, q0:kend]
        diag = jnp.where(causal, diag, -jnp.inf)
        past = _dot_nt(q, k_ref[0:q0, :]) + negcum[:, 0:q0] if qi else None
        return past, diag

    def attend(qi, past, diag):
        q0, kend = qi * tq, (qi + 1) * tq
        m = jnp.max(diag, axis=-1, keepdims=True)
        if past is not None:
            m = jnp.maximum(m, jnp.max(past, axis=-1, keepdims=True))
        p = jnp.exp(diag - m)
        l = jnp.sum(p, axis=-1, keepdims=True)
        o = _dot(p.astype(BF16), v_ref[q0:kend, :])
        if past is not None:
            p = jnp.exp(past - m)
            l = l + jnp.sum(p, axis=-1, keepdims=True)
            o = o + _dot(p.astype(BF16), v_ref[0:q0, :])
        gate = _sigmoid(og_ref[q0:kend, :].astype(F32))
        o_ref[q0:kend, :] = (o / l * gate).astype(BF16)

    nxt = logits(0)
    for qi in range(n_q):
        cur, nxt = nxt, (logits(qi + 1) if qi + 1 < n_q else None)
        attend(qi, *cur)


def _fox_attn(main, cum, c_pad, w_ada, b_ada, ada_col0, *, batch, seq, tq=256):
    t = main.shape[0]
    hd = FOX_HEAD_DIM
    nh = FOX_HEADS
    rows, d = c_pad.shape
    n_ada = w_ada.shape[1] - ada_col0
    tn = n_ada // (batch * nh)
    assert n_ada % (batch * nh) == 0 and tn % LANES == 0 and ada_col0 % tn == 0
    slab = lambda r: pl.BlockSpec((r, tn), lambda b, h: (0, ada_col0 // tn + b * nh + h))
    blk = lambda off: pl.BlockSpec((seq, hd), lambda b, h: (b, off + h))
    pipelined = 5 * _nbytes((seq, hd), BF16) + _nbytes((8, seq), F32) + _nbytes((d, tn), F32)
    return pl.pallas_call(
        functools.partial(_fox_attn_kernel, tq=tq),
        grid=(batch, nh),
        in_specs=[blk(0), blk(nh), blk(2 * nh), blk(3 * nh),
                  pl.BlockSpec((1, 1, 1, seq), lambda b, h: (h, b, 0, 0)),
                  pl.BlockSpec((rows, d), lambda b, h: (0, 0)), slab(d), slab(1)],
        out_specs=[pl.BlockSpec((seq, hd), lambda b, h: (b, h)),
                   pl.BlockSpec((rows, tn), lambda b, h: (0, b * nh + h))],
        out_shape=[jax.ShapeDtypeStruct((t, nh * hd), BF16), jax.ShapeDtypeStruct((rows, n_ada), F32)],
        compiler_params=pltpu.CompilerParams(
            dimension_semantics=("arbitrary", "arbitrary"),
            vmem_limit_bytes=_vmem_limit(pipelined, 0, 6 * _nbytes((tq, seq), F32)),
        ),
        name="fox_attn",
    )(main, main, main, main, cum, c_pad, w_ada, b_ada)


def _run_stages(gen, side=None):
    result = {}
    live = [g for g in (gen, side) if g is not None]
    while live:
        for g in list(live):
            try:
                next(g)
            except StopIteration as stop:
                result[g] = stop.value
                live.remove(g)
    return result[gen if side is None else side]


def _rwkv_inputs(raw_refs, small_ref, mu_refs, mu_s_ref, w0_ref, w2_ref, a0_ref, a2_ref, g2_ref,
                 prev_refs, prev_s, seqs):
    def shift_mix(x, prev_ref, q, mu):
        rows = x.shape[0]
        xp = pltpu.roll(x, 1, axis=0)
        row = lax.broadcasted_iota(jnp.int32, x.shape, 0)
        xp = jnp.where(row == 0, prev_ref[q, 0:1, :], xp)
        prev_ref[q, 0:1, :] = x[rows - 1:rows, :]
        return x + (xp - x) * mu

    r, k, v = ([shift_mix(ref[q].astype(F32), prev, q, mu[...]) for q in seqs]
               for ref, prev, mu in zip(raw_refs, prev_refs, mu_refs))
    sm = jnp.concatenate([shift_mix(small_ref[q], prev_s, q, mu_s_ref[...]) for q in seqs], axis=0)
    xw = sm[:, 0:LORA_PAD]
    xa = sm[:, DECAY_LORA:DECAY_LORA + LORA_PAD]
    xg = sm[:, DECAY_LORA + AAA_LORA:DECAY_LORA + AAA_LORA + GATE_LORA]

    z = w0_ref[...] + _dot(jnp.tanh(xw).astype(BF16), w2_ref[...])
    w_raw = -(jnp.maximum(-z, 0.0) + jnp.log1p(jnp.exp(-jnp.abs(z)))) - 0.5
    lw = -jnp.exp(w_raw)
    a = _sigmoid(a0_ref[...] + _dot(xa.astype(BF16), a2_ref[...]))
    g = _dot(_sigmoid(xg).astype(BF16), g2_ref[...])
    ch = lw.shape[0] // len(seqs)
    per_seq = lambda x: [x[j * ch:(j + 1) * ch, :] for j in range(len(seqs))]
    return r, k, v, per_seq(lw), per_seq(a), per_seq(g)


def _rwkv_kernel(r_ref, k_ref, v_ref, small_ref, mu_r_ref, mu_k_ref, mu_v_ref, mu_s_ref,
                 w0_ref, w2_ref, a0_ref, a2_ref, g2_ref,
                 kk_ref, ka_ref, rk_ref, lng_ref, lnb_ref, *rest, n_tiles, seq_groups, n_cast):
    cast_src, y_ref, cast_dst = rest[:n_cast], rest[n_cast], rest[n_cast + 1:2 * n_cast + 1]
    state, prev_r, prev_k, prev_v, prev_s = rest[2 * n_cast + 1:]
    for src, dst in zip(cast_src, cast_dst):
        dst[...] = src[...].astype(BF16)

    @pl.when(pl.program_id(1) == 0)
    def _():
        for ref in (state, prev_r, prev_k, prev_v, prev_s):
            ref[...] = jnp.zeros_like(ref)

    ch = y_ref.shape[1]
    hd = RWKV_HEAD_DIM
    lane = lax.broadcasted_iota(jnp.int32, (1, LANES), 1)
    m0 = lane < hd
    n2 = HEADS_PER_TILE * ch
    rr = lax.broadcasted_iota(jnp.int32, (n2, n2), 0)
    cc = lax.broadcasted_iota(jnp.int32, (n2, n2), 1)
    same = (rr // ch) == (cc // ch)
    strict = same & (rr > cc)
    incl = same & (rr >= cc)
    lr = lax.broadcasted_iota(jnp.int32, (ch, ch), 0)
    lc = lax.broadcasted_iota(jnp.int32, (ch, ch), 1)
    lower = (lc <= lr).astype(BF16)

    def head_sum(x):
        s0 = jnp.sum(jnp.where(m0, x, 0.0), axis=-1, keepdims=True)
        s1 = jnp.sum(jnp.where(m0, 0.0, x), axis=-1, keepdims=True)
        return jnp.where(m0, s0, s1)

    def stack(x):
        return jnp.concatenate([jnp.where(m0, x, 0.0), jnp.where(m0, 0.0, x)], axis=0)

    sls = [slice(p * LANES, (p + 1) * LANES) for p in range(n_tiles)]

    def each(fn, *cols):
        return [fn(*args) for args in zip(*cols)]

    stack2 = lambda x, y: jnp.concatenate([stack(x), stack(y)], axis=0).astype(BF16)
    incl2 = jnp.concatenate([incl, incl], axis=1)

    def prepare(seqs):
        per_chain = lambda ref: [ref[:, sl] for _ in seqs for sl in sls]
        full = _rwkv_inputs((r_ref, k_ref, v_ref), small_ref, (mu_r_ref, mu_k_ref, mu_v_ref), mu_s_ref,
                            w0_ref, w2_ref, a0_ref, a2_ref, g2_ref, (prev_r, prev_k, prev_v), prev_s, seqs)
        r, k, v, lw, a, g = ([x[j][:, sl] for j in range(len(seqs)) for sl in sls] for x in full)
        yield
        kk = each(lambda k_, w_: k_ * w_, k, per_chain(kk_ref))
        kk = each(lambda x: x / jnp.maximum(jnp.sqrt(head_sum(x * x)), 1e-12), kk)
        kr = each(lambda k_, a_, w_: k_ * (1.0 + (a_ - 1.0) * w_), k, a, per_chain(ka_ref))
        yield
        cum = each(lambda x: sum(_dot(lower, part) for part in _split3(x)[:2]), lw)
        yield
        gam = each(jnp.exp, cum)
        ginv = each(lambda x: jnp.exp(-x), cum)
        g_end = each(lambda x: x[ch - 1:ch, :], gam)
        yield
        a_t = each(lambda kk_, c_, lw_: -kk_ * jnp.exp(c_ - lw_), kk, cum, lw)
        r_t = each(lambda r_, g_: r_ * g_, r, gam)
        lhs = each(stack2, a_t, r_t)
        yield
        b_h = each(lambda kk_, a_, gi_: kk_ * a_ * gi_, kk, a, ginv)
        k_h = each(lambda kr_, gi_: kr_ * gi_, kr, ginv)
        rhs = each(stack2, b_h, k_h)
        yield
        vs = each(lambda v_: stack(v_).astype(BF16), v)
        bk = each(lambda b_, k_, ge_: stack2(b_ * ge_, k_ * ge_), b_h, k_h, g_end)
        yield
        bonus = each(lambda r_, kr_, w_, v_: head_sum(r_ * kr_ * w_) * v_, r, kr, per_chain(rk_ref), v)
        return lhs, rhs, vs, bk, g_end, bonus, g

    def advance(seqs, prepared):
        lhs, rhs, vs, bk, g_end, bonus, g = prepared
        per_chain = lambda ref: [ref[:, sl] for _ in seqs for sl in sls]
        slots = [q * n_tiles + p for q in seqs for p in range(n_tiles)]
        sc = each(_dot_nt, lhs, rhs)
        yield
        ab = each(lambda s: jnp.where(strict, s[0:n2, 0:n2], 0.0).astype(BF16), sc)
        ak = each(lambda s: jnp.where(strict, s[0:n2, n2:2 * n2], 0.0).astype(BF16), sc)
        rbk = each(lambda s: jnp.where(incl2, s[n2:2 * n2, :], 0.0).astype(BF16), sc)
        yield
        st = [state[i] for i in slots]
        ah = each(lambda l_, s_: _dot_nt(l_, s_.astype(BF16)), lhs, st)
        yield
        x = each(lambda ah_, ak_, vs_: ah_[0:n2] + _dot(ak_, vs_), ah, ak, vs)
        yield
        pw = ab
        n_lvl = ch.bit_length() - 1
        for lvl in range(n_lvl):
            x = each(lambda x_, p_: x_ + _dot(p_, x_.astype(BF16)), x, pw)
            yield
            if lvl + 1 < n_lvl:
                pw = each(lambda p_: _dot(p_, p_).astype(BF16), pw)
                yield
        uv = each(lambda x_, vs_: jnp.concatenate([x_.astype(BF16), vs_], axis=0), x, vs)
        ys = each(lambda ah_, m_, uv_: ah_[n2:2 * n2] + _dot(m_, uv_), ah, rbk, uv)
        yield
        y = each(lambda ys_: ys_[0:ch] + ys_[ch:n2], ys)
        new_st = each(lambda s_, ge_, uv_, bk_: s_ * ge_ + _dot_tn(uv_, bk_), st, g_end, uv, bk)
        for i, s_new in zip(slots, new_st):
            state[i] = s_new
        yield
        mu = each(lambda y_: head_sum(y_) * (1.0 / hd), y)
        yc = each(lambda y_, m_: y_ - m_, y, mu)
        var = each(lambda c_: head_sum(c_ * c_) * (1.0 / hd), yc)
        yield
        yn = each(lambda c_, v_, g_, b_: c_ * lax.rsqrt(v_ + RWKV_LNX_EPS) * g_ + b_,
                  yc, var, per_chain(lng_ref), per_chain(lnb_ref))
        for j, i in enumerate(slots):
            y_ref[i // n_tiles, :, sls[i % n_tiles]] = ((yn[j] + bonus[j]) * g[j]).astype(BF16)

    starts = [sum(seq_groups[:i]) for i in range(len(seq_groups))]
    groups = [tuple(range(s, s + n)) for s, n in zip(starts, seq_groups)]
    prepared = _run_stages(prepare(groups[0]))
    for gi, seqs in enumerate(groups):
        nxt = prepare(groups[gi + 1]) if gi + 1 < len(groups) else None
        prepared = _run_stages(advance(seqs, prepared), nxt)


def _slab_spec(w, n_steps, step_of):
    rep = 1
    while w.shape[0] % (n_steps // rep * 16):
        rep *= 2
        assert rep <= n_steps
    return pl.BlockSpec((w.shape[0] // (n_steps // rep), w.shape[1]),
                        lambda *idx: (step_of(*idx) // rep, 0))


def _rwkv(main, small, mus, mu_s, w0, w2p, a0, a2p, g2, head_rows, cast_weights, *, batch, seq,
          seq_groups=(2, 2)):
    n_seq = sum(seq_groups)
    t = main.shape[0]
    width = RWKV_HEADS * RWKV_HEAD_DIM
    nsm = small.shape[1]
    col0 = (4 * FOX_HEADS * FOX_HEAD_DIM) // width
    ch = RWKV_CHUNK
    steps = seq // ch
    n_tiles = width // LANES
    assert batch % n_seq == 0
    groups = batch // n_seq
    grouped = lambda x: x.reshape(groups, n_seq, seq, x.shape[-1])
    act = lambda cols, blk: pl.BlockSpec((None, n_seq, ch, cols), lambda b, c: (b, 0, c, blk))
    row = lambda n: pl.BlockSpec((1, n), lambda b, c: (0, 0))
    whole = lambda w: pl.BlockSpec(w.shape, lambda b, c: (0, 0))
    slabs = [_slab_spec(w, groups * steps, lambda b, c: b * steps + c) for w in cast_weights]
    pipelined = n_seq * (4 * _nbytes((ch, width), BF16) + _nbytes((ch, nsm), F32))
    pipelined += sum(_nbytes(s.block_shape, F32) + _nbytes(s.block_shape, BF16) for s in slabs)
    pipelined += sum(_nbytes(w.shape, BF16) for w in (w2p, a2p, g2))
    scratch = [(n_seq * n_tiles, LANES, LANES)] + [(n_seq, 8, width)] * 3 + [(n_seq, 8, nsm)]
    y, *copies = pl.pallas_call(
        functools.partial(_rwkv_kernel, n_tiles=n_tiles, seq_groups=seq_groups, n_cast=len(slabs)),
        grid=(groups, steps),
        in_specs=[act(width, col0), act(width, col0 + 1), act(width, col0 + 2), act(nsm, 0),
                  row(width), row(width), row(width), row(nsm),
                  row(width), whole(w2p), row(width), whole(a2p), whole(g2)]
        + [row(width)] * 5 + slabs,
        out_specs=[act(width, 0)] + slabs,
        out_shape=[jax.ShapeDtypeStruct((groups, n_seq, seq, width), BF16)]
        + [jax.ShapeDtypeStruct(w.shape, BF16) for w in cast_weights],
        scratch_shapes=[pltpu.VMEM(s, F32) for s in scratch],
        compiler_params=pltpu.CompilerParams(
            dimension_semantics=("arbitrary", "arbitrary"),
            vmem_limit_bytes=_vmem_limit(pipelined, sum(_nbytes(s, F32) for s in scratch), 24 << 20),
        ),
        name="rwkv",
    )(grouped(main), grouped(main), grouped(main), grouped(small), *mus, mu_s, w0, w2p, a0, a2p, g2,
      *head_rows, *cast_weights)
    return y.reshape(t, width), copies


def _mixout_kernel(o_ref, y_ref, ga0_ref, ga1_ref, gb0_ref, gb1_ref, wa_ref, wb_ref, wo_ref, x_ref,
                   gate_ref, lng_ref, lnb_ref, scale_ref, shift_ref, x1_ref, h2_ref, *, n_sub):
    tm, d = x_ref.shape
    rows = tm // n_sub
    half = d // 2
    sl = [slice(s * rows, (s + 1) * rows) for s in range(n_sub)]

    def branches(s):
        return _dot(o_ref[sl[s], :], wa_ref[...]), _dot(y_ref[sl[s], :], wb_ref[...])

    def mixed(s, ua, ub):
        parts = []
        for c, (ga, gb) in enumerate(((ga0_ref, gb0_ref), (ga1_ref, gb1_ref))):
            cs = slice(c * half, (c + 1) * half)
            parts.append(_sigmoid(ga[sl[s], :].astype(F32)) * ua[:, cs]
                         + _sigmoid(gb[sl[s], :].astype(F32)) * ub[:, cs])
        return _dot(jnp.concatenate(parts, axis=1).astype(BF16), wo_ref[...])

    def norms(s, mix):
        z = DEEPNORM_ALPHA * x_ref[sl[s], :] + gate_ref[0] * mix
        x1 = _layernorm(z) * lng_ref[...] + lnb_ref[...]
        x1_ref[sl[s], :] = x1
        h2_ref[sl[s], :] = (_layernorm(x1) * (1.0 + scale_ref[0]) + shift_ref[0]).astype(BF16)

    u = branches(0)
    mix_prev = None
    for s in range(n_sub):
        u_next = branches(s + 1) if s + 1 < n_sub else None
        mix = mixed(s, *u)
        if mix_prev is not None:
            norms(s - 1, mix_prev)
        u, mix_prev = u_next, mix
    norms(n_sub - 1, mix_prev)


def _mixout(o, y, main, wa, wb, wo, x2d, gate1, ln_g, ln_b, scale2, shift2, *, seq, tm=256, n_sub=2):
    t, d = x2d.shape
    kdim = o.shape[1]
    half = d // 2
    gate_blk0 = (main.shape[1] - 2 * d) // half
    tiles_per_batch = seq // tm
    mod = pl.BlockSpec((1, 1, d), lambda i: (i // tiles_per_batch, 0, 0))
    row = pl.BlockSpec((1, d), lambda i: (0, 0))
    gate = lambda c: pl.BlockSpec((tm, half), lambda i: (i, gate_blk0 + c))
    resident = lambda w: pl.BlockSpec(w.shape, lambda i: (0, 0), pipeline_mode=pl.Buffered(1))
    pipelined = (2 * _nbytes((tm, kdim), BF16) + 4 * _nbytes((tm, half), BF16)
                 + 2 * _nbytes((tm, d), F32) + _nbytes((tm, d), BF16))
    weights = _nbytes(wa.shape, BF16) + _nbytes(wb.shape, BF16) + _nbytes(wo.shape, BF16)
    return pl.pallas_call(
        functools.partial(_mixout_kernel, n_sub=n_sub),
        grid=(t // tm,),
        in_specs=[
            pl.BlockSpec((tm, kdim), lambda i: (i, 0)),
            pl.BlockSpec((tm, kdim), lambda i: (i, 0)),
            gate(0), gate(1), gate(2), gate(3),
            resident(wa), resident(wb), resident(wo),
            pl.BlockSpec((tm, d), lambda i: (i, 0)),
            mod, row, row, mod, mod,
        ],
        out_specs=[pl.BlockSpec((tm, d), lambda i: (i, 0)),
                   pl.BlockSpec((tm, d), lambda i: (i, 0))],
        out_shape=[jax.ShapeDtypeStruct((t, d), F32), jax.ShapeDtypeStruct((t, d), BF16)],
        compiler_params=pltpu.CompilerParams(
            dimension_semantics=("arbitrary",),
            vmem_limit_bytes=_vmem_limit(pipelined, weights, 8 * _nbytes((tm, d), F32)),
        ),
        name="mixout",
    )(o, y, main, main, main, main, wa, wb, wo, x2d, gate1, ln_g, ln_b, scale2, shift2)


def _ffn_kernel(h_ref, wg_ref, wu_ref, wd_ref, x1_ref, gate_ref, lng_ref, lnb_ref, o_ref, acc, *, nf):
    f = pl.program_id(1)

    @pl.when(f == 0)
    def _():
        acc[...] = jnp.zeros_like(acc)

    @pl.when(f < nf)
    def _():
        h = h_ref[...]
        g = _dot(h, wg_ref[...])
        u = _dot(h, wu_ref[...])
        act = (g * _sigmoid(g) * u).astype(BF16)
        acc[...] += _dot(act, wd_ref[...])

    @pl.when(f >= nf)
    def _():
        rows = o_ref.shape[0]
        ffn = acc[pl.ds(pl.multiple_of((f - nf) * rows, rows), rows), :]
        z = DEEPNORM_ALPHA * x1_ref[...] + gate_ref[0] * ffn
        o_ref[...] = _layernorm(z) * lng_ref[...] + lnb_ref[...]


def _ffn(h2, w_gu, w_down, x1, gate2, ln_g, ln_b, *, seq, tm=1024, tf=512, n_slab=2):
    t, d = x1.shape
    d_ff = w_down.shape[0]
    nf = d_ff // tf
    ts = tm // n_slab
    tiles_per_batch = seq // tm
    w_tile = lambda f: jnp.minimum(f, nf - 1)
    slab = pl.BlockSpec((ts, d), lambda i, f: (i * n_slab + jnp.clip(f - nf, 0, n_slab - 1), 0))
    mod = pl.BlockSpec((1, 1, d), lambda i, f: (i // tiles_per_batch, 0, 0))
    row = pl.BlockSpec((1, d), lambda i, f: (0, 0))
    pipelined = (_nbytes((tm, d), BF16) + 3 * _nbytes((d, tf), BF16) + 2 * _nbytes((ts, d), F32))
    return pl.pallas_call(
        functools.partial(_ffn_kernel, nf=nf),
        grid=(t // tm, nf + n_slab),
        in_specs=[
            pl.BlockSpec((tm, d), lambda i, f: (i, 0)),
            pl.BlockSpec((d, tf), lambda i, f: (0, w_tile(f))),
            pl.BlockSpec((d, tf), lambda i, f: (0, nf + w_tile(f))),
            pl.BlockSpec((tf, d), lambda i, f: (w_tile(f), 0)),
            slab, mod, row, row,
        ],
        out_specs=slab,
        out_shape=jax.ShapeDtypeStruct((t, d), F32),
        scratch_shapes=[pltpu.VMEM((tm, d), F32)],
        compiler_params=pltpu.CompilerParams(
            dimension_semantics=("arbitrary", "arbitrary"),
            vmem_limit_bytes=_vmem_limit(pipelined, _nbytes((tm, d), F32),
                                         _nbytes((tm, d), F32) + 4 * _nbytes((tm, tf), F32)),
        ),
        name="ffn",
    )(h2, w_gu, w_gu, w_down, x1, gate2, ln_g, ln_b)


def _pad_rows(w, rows):
    return jnp.zeros((rows, w.shape[1]), w.dtype).at[:w.shape[0]].set(w)


def kernel(x, c, w_ada, b_ada, w_in, b_fgate, q_norm_g, k_norm_g, rwkv_mu, rwkv_w0, rwkv_w2, rwkv_a0, rwkv_a2, rwkv_g2, rwkv_k_k, rwkv_k_a, rwkv_r_k, rwkv_lnx_g, rwkv_lnx_b, w_branch_a, w_branch_b, w_out, ln1_g, ln1_b, w_ffn_gu, w_ffn_down, ln2_g, ln2_b):
    batch, seq, d = x.shape
    t = batch * seq
    fox_w = FOX_HEADS * FOX_HEAD_DIM
    rw_w = RWKV_HEADS * RWKV_HEAD_DIM
    fox_cols = 4 * fox_w + FOX_HEADS
    lora0 = fox_cols + 3 * rw_w
    gate0 = lora0 + DECAY_LORA + AAA_LORA + GATE_LORA
    assert seq % RWKV_CHUNK == 0 and w_in.shape[1] == gate0 + 2 * d

    lora_w = gate0 - lora0
    mu_r = rwkv_mu[None, 0:rw_w]
    mu_k = rwkv_mu[None, rw_w:2 * rw_w]
    mu_v = rwkv_mu[None, 2 * rw_w:3 * rw_w]
    mu_s = jnp.zeros((1, INPROJ_TN), rwkv_mu.dtype).at[0, :lora_w].set(rwkv_mu[3 * rw_w:])
    w2p = _pad_rows(rwkv_w2, LORA_PAD).astype(BF16)
    a2p = _pad_rows(rwkv_a2, LORA_PAD).astype(BF16)
    g2 = rwkv_g2.astype(BF16)
    row = lambda v: v.reshape(1, -1)

    c_pad = jnp.zeros((8, d), c.dtype).at[:batch].set(c)
    mod_head = _ada(c_pad, w_ada, row(b_ada), 2 * d)[:batch].reshape(batch, 2, 1, d)
    shift1, scale1 = mod_head[:, 0], mod_head[:, 1]

    x2d = x.reshape(t, d)
    main, small, ft = _inproj(x2d, scale1, shift1, w_in.T, row(q_norm_g), row(k_norm_g), seq=seq,
                              group_rows=[(0, 4 * fox_w), (fox_cols, 3 * rw_w), (gate0, 2 * d)],
                              lora_row=lora0, fgate_row=4 * fox_w, tn=INPROJ_TN)

    blocks_per_seq = seq // LANES
    ft_rows = ft.reshape(FOX_HEADS * batch * blocks_per_seq, LANES)
    bias_rows = jnp.broadcast_to(b_fgate[:, None, None], (FOX_HEADS, batch * blocks_per_seq, LANES))
    cum = _fox_prep(ft_rows, bias_rows.reshape(ft_rows.shape), blocks_per_seq=blocks_per_seq)
    cum = cum.reshape(FOX_HEADS, batch, 1, seq)
    o, mod_tail = _fox_attn(main, cum, c_pad, w_ada, row(b_ada), 2 * d, batch=batch, seq=seq)
    gate1, shift2, scale2, gate2 = (mod_tail[:batch].reshape(batch, 4, 1, d)[:, i] for i in range(4))

    y, (wa, wb, wo, wgu, wdn) = _rwkv(
        main, small, (mu_r, mu_k, mu_v), mu_s, row(rwkv_w0), w2p, row(rwkv_a0), a2p, g2,
        [row(p) for p in (rwkv_k_k, rwkv_k_a, rwkv_r_k, rwkv_lnx_g, rwkv_lnx_b)],
        [w_branch_a, w_branch_b, w_out, w_ffn_gu, w_ffn_down], batch=batch, seq=seq)

    x1, h2 = _mixout(o, y, main, wa, wb, wo, x2d, gate1, row(ln1_g), row(ln1_b), scale2, shift2, seq=seq)

    out = _ffn(h2, wgu, wdn, x1, gate2, row(ln2_g), row(ln2_b), seq=seq)
    return out.reshape(batch, seq, d)
```

```python
import functools

import jax
import jax.numpy as jnp
from jax import lax
from jax.experimental import pallas as pl
from jax.experimental.pallas import tpu as pltpu

F32 = jnp.float32
BF16 = jnp.bfloat16

LANES = 128
VMEM_CAP_BYTES = 60000 * 1024

FOX_HEADS = 8
FOX_HEAD_DIM = 128
RWKV_HEADS = 16
RWKV_HEAD_DIM = 64
DECAY_LORA = 96
AAA_LORA = 96
GATE_LORA = 256
LORA_PAD = 128
INPROJ_TN = 512
RWKV_LNX_EPS = 64e-5
DEPTH = 1
DEEPNORM_ALPHA = (2.0 * DEPTH) ** 0.25
LN_EPS = 1e-5
RMS_EPS = 1e-6
RWKV_CHUNK = 64
HEADS_PER_TILE = LANES // RWKV_HEAD_DIM


def _vmem_limit(pipelined_bytes, resident_bytes=0, temp_bytes=0):
    need = 2 * pipelined_bytes + resident_bytes + temp_bytes + (2 << 20)
    return int(min(VMEM_CAP_BYTES, need))


def _nbytes(shape, dtype):
    n = 1
    for s in shape:
        n *= s
    return n * jnp.dtype(dtype).itemsize


def _layernorm(x):
    mu = jnp.mean(x, axis=-1, keepdims=True)
    xc = x - mu
    var = jnp.mean(xc * xc, axis=-1, keepdims=True)
    return xc * lax.rsqrt(var + LN_EPS)


def _sigmoid(x):
    return 1.0 / (1.0 + jnp.exp(-x))


def _dot(a, b):
    return jnp.dot(a, b, preferred_element_type=F32)


def _dot_nt(a, b):
    return lax.dot_general(a, b, (((1,), (1,)), ((), ())), preferred_element_type=F32)


def _dot_tn(a, b):
    return lax.dot_general(a, b, (((0,), (0,)), ((), ())), preferred_element_type=F32)


def _split3(x):
    hi = x.astype(BF16)
    r1 = x - hi.astype(F32)
    mid = r1.astype(BF16)
    lo = (r1 - mid.astype(F32)).astype(BF16)
    return hi, mid, lo


def _ada_kernel(c_ref, w_ref, b_ref, o_ref):
    c = c_ref[...]
    s = (c * _sigmoid(c)).astype(BF16)
    o_ref[...] = _dot(s, w_ref[...].astype(BF16)) + b_ref[...]


def _ada(c_pad, w_ada, b_ada, n, *, tn=1024):
    rows, d = c_pad.shape
    return pl.pallas_call(
        _ada_kernel,
        grid=(n // tn,),
        in_specs=[
            pl.BlockSpec((rows, d), lambda j: (0, 0)),
            pl.BlockSpec((d, tn), lambda j: (0, j)),
            pl.BlockSpec((1, tn), lambda j: (0, j)),
        ],
        out_specs=pl.BlockSpec((rows, tn), lambda j: (0, j)),
        out_shape=jax.ShapeDtypeStruct((rows, n), F32),
        compiler_params=pltpu.CompilerParams(
            dimension_semantics=("arbitrary",),
            vmem_limit_bytes=_vmem_limit(_nbytes((d, tn), F32), temp_bytes=_nbytes((d, tn), BF16)),
        ),
        name="ada",
    )(c_pad, w_ada, b_ada)


def _inproj_kernel(x_ref, scale_ref, shift_ref, wt_ref, wft_ref, qg_ref, kg_ref,
                   main_ref, small_ref, ft_ref, h_scr, *, n_q, n_k, n_main):
    j = pl.program_id(1)
    tn = main_ref.shape[1]

    @pl.when(j == 0)
    def _():
        h = _layernorm(x_ref[...]) * (1.0 + scale_ref[0]) + shift_ref[0]
        h_scr[...] = h.astype(BF16)

    def proj():
        w = wt_ref[...].astype(BF16)
        half = h_scr.shape[0] // 2
        return jnp.concatenate([_dot_nt(h_scr[0:half, :], w), _dot_nt(h_scr[half:, :], w)], axis=0)

    @pl.when(j < n_q + n_k)
    def _():
        acc = proj()
        gain = jnp.where(j < n_q, qg_ref[...] * (FOX_HEAD_DIM ** -0.5), kg_ref[...])
        for hh in range(tn // FOX_HEAD_DIM):
            sl = slice(hh * FOX_HEAD_DIM, (hh + 1) * FOX_HEAD_DIM)
            a = acc[:, sl]
            ms = jnp.mean(a * a, axis=-1, keepdims=True)
            main_ref[:, sl] = (a * lax.rsqrt(ms + RMS_EPS) * gain).astype(BF16)

    @pl.when((j >= n_q + n_k) & (j < n_main))
    def _():
        main_ref[...] = proj().astype(BF16)

    @pl.when(j == n_main)
    def _():
        small_ref[...] = proj()
        ft_ref[...] = _dot_nt(wft_ref[...].astype(BF16), h_scr[...])


def _inproj(x2d, scale1, shift1, w_in_t, qg, kg, *, seq, group_rows, lora_row, fgate_row, tm=1024, tn=512):
    t, d = x2d.shape
    fox_w = FOX_HEADS * FOX_HEAD_DIM
    assert all(first % 8 == 0 and count % tn == 0 for first, count in group_rows)
    assert lora_row % 8 == 0 and fgate_row % 8 == 0 and lora_row + tn <= w_in_t.shape[0]
    n_main = sum(count // tn for _, count in group_rows)
    tiles_per_batch = seq // tm
    kern = functools.partial(_inproj_kernel, n_q=fox_w // tn, n_k=fox_w // tn, n_main=n_main)
    pipelined = (_nbytes((tm, d), F32) + _nbytes((tn, d), F32) + _nbytes((tm, tn), BF16)
                 + _nbytes((tm, tn), F32) + _nbytes((8, tm), F32))

    def w_rows(i, j):
        row, hi = jnp.int32(lora_row), n_main
        for first, count in reversed(group_rows):
            lo = hi - count // tn
            row = jnp.where(j < hi, first + tn * (j - lo), row)
            hi = lo
        return pl.multiple_of(row, 8), 0

    return pl.pallas_call(
        kern,
        grid=(t // tm, n_main + 1),
        in_specs=[
            pl.BlockSpec((tm, d), lambda i, j: (i, 0)),
            pl.BlockSpec((1, 1, d), lambda i, j: (i // tiles_per_batch, 0, 0)),
            pl.BlockSpec((1, 1, d), lambda i, j: (i // tiles_per_batch, 0, 0)),
            pl.BlockSpec((pl.Element(tn), pl.Element(d)), w_rows),
            pl.BlockSpec((pl.Element(FOX_HEADS), pl.Element(d)), lambda i, j: (fgate_row, 0)),
            pl.BlockSpec((1, FOX_HEAD_DIM), lambda i, j: (0, 0)),
            pl.BlockSpec((1, FOX_HEAD_DIM), lambda i, j: (0, 0)),
        ],
        out_specs=[
            pl.BlockSpec((tm, tn), lambda i, j: (i, jnp.minimum(j, n_main - 1))),
            pl.BlockSpec((tm, tn), lambda i, j: (i, 0)),
            pl.BlockSpec((FOX_HEADS, tm), lambda i, j: (0, i)),
        ],
        out_shape=[
            jax.ShapeDtypeStruct((t, n_main * tn), BF16),
            jax.ShapeDtypeStruct((t, tn), F32),
            jax.ShapeDtypeStruct((FOX_HEADS, t), F32),
        ],
        scratch_shapes=[pltpu.VMEM((tm, d), BF16)],
        compiler_params=pltpu.CompilerParams(
            dimension_semantics=("arbitrary", "arbitrary"),
            vmem_limit_bytes=_vmem_limit(pipelined, _nbytes((tm, d), BF16),
                                         3 * _nbytes((tm, d), F32)),
        ),
        name="inproj",
    )(x2d, scale1, shift1, w_in_t, w_in_t, qg, kg)


def _fox_prep_kernel(ft_ref, bias_ref, o_ref, *, blocks_per_seq):
    x = ft_ref[...] + bias_ref[...]
    lf = jnp.minimum(x, 0.0) - jnp.log1p(jnp.exp(-jnp.abs(x)))
    rows, width = lf.shape
    r = lax.broadcasted_iota(jnp.int32, (width, width), 0)
    c = lax.broadcasted_iota(jnp.int32, (width, width), 1)
    upper = (r <= c).astype(BF16)
    within = sum(_dot(p, upper) for p in _split3(lf))
    tot = jnp.broadcast_to(within[:, width - 1:width], (rows, width))
    rr = lax.broadcasted_iota(jnp.int32, (rows, rows), 0)
    cc = lax.broadcasted_iota(jnp.int32, (rows, rows), 1)
    prior = ((rr // blocks_per_seq == cc // blocks_per_seq) & (cc < rr)).astype(BF16)
    offset = sum(_dot(prior, p) for p in _split3(tot))
    o_ref[...] = within + offset


def _fox_prep(ft_rows, bias_rows, *, blocks_per_seq):
    return pl.pallas_call(
        functools.partial(_fox_prep_kernel, blocks_per_seq=blocks_per_seq),
        out_shape=jax.ShapeDtypeStruct(ft_rows.shape, F32),
        name="fox_prep",
    )(ft_rows, bias_rows)


def _fox_attn_kernel(q_ref, k_ref, v_ref, og_ref, cum_ref, c_ref, wada_ref, bada_ref, o_ref, mod_ref, *, tq):
    _ada_kernel(c_ref, wada_ref, bada_ref, mod_ref)

    seq = q_ref.shape[0]
    n_q = seq // tq
    negcum = -cum_ref[0, 0]
    row = lax.broadcasted_iota(jnp.int32, (tq, tq), 0)
    col = lax.broadcasted_iota(jnp.int32, (tq, tq), 1)
    causal = col <= row

    def logits(qi):
        q0, kend = qi * tq, (qi + 1) * tq
        q = q_ref[q0:kend, :]
        diag = _dot_nt(q, k_ref[q0:kend, :]) + negcum[:, q0:kend]
        diag = jnp.where(causal, diag, -jnp.inf)
        past = _dot_nt(q, k_ref[0:q0, :]) + negcum[:, 0:q0] if qi else None
        return past, diag

    def attend(qi, past, diag):
        q0, kend = qi * tq, (qi + 1) * tq
        m = jnp.max(diag, axis=-1, keepdims=True)
        if past is not None:
            m = jnp.maximum(m, jnp.max(past, axis=-1, keepdims=True))
        p = jnp.exp(diag - m)
        l = jnp.sum(p, axis=-1, keepdims=True)
        o = _dot(p.astype(BF16), v_ref[q0:kend, :])
        if past is not None:
            p = jnp.exp(past - m)
            l = l + jnp.sum(p, axis=-1, keepdims=True)
            o = o + _dot(p.astype(BF16), v_ref[0:q0, :])
        gate = _sigmoid(og_ref[q0:kend, :].astype(F32))
        o_ref[q0:kend, :] = (o / l * gate).astype(BF16)

    nxt = logits(0)
    for qi in range(n_q):
        cur, nxt = nxt, (logits(qi + 1) if qi + 1 < n_q else None)
        attend(qi, *cur)


def _fox_attn(main, cum, c_pad, w_ada, b_ada, ada_col0, *, batch, seq, tq=256):
    t = main.shape[0]
    hd = FOX_HEAD_DIM
    nh = FOX_HEADS
    rows, d = c_pad.shape
    n_ada = w_ada.shape[1] - ada_col0
    tn = n_ada // (batch * nh)
    assert n_ada % (batch * nh) == 0 and tn % LANES == 0 and ada_col0 % tn == 0
    slab = lambda r: pl.BlockSpec((r, tn), lambda b, h: (0, ada_col0 // tn + b * nh + h))
    blk = lambda off: pl.BlockSpec((seq, hd), lambda b, h: (b, off + h))
    pipelined = 5 * _nbytes((seq, hd), BF16) + _nbytes((8, seq), F32) + _nbytes((d, tn), F32)
    return pl.pallas_call(
        functools.partial(_fox_attn_kernel, tq=tq),
        grid=(batch, nh),
        in_specs=[blk(0), blk(nh), blk(2 * nh), blk(3 * nh),
                  pl.BlockSpec((1, 1, 1, seq), lambda b, h: (h, b, 0, 0)),
                  pl.BlockSpec((rows, d), lambda b, h: (0, 0)), slab(d), slab(1)],
        out_specs=[pl.BlockSpec((seq, hd), lambda b, h: (b, h)),
                   pl.BlockSpec((rows, tn), lambda b, h: (0, b * nh + h))],
        out_shape=[jax.ShapeDtypeStruct((t, nh * hd), BF16), jax.ShapeDtypeStruct((rows, n_ada), F32)],
        compiler_params=pltpu.CompilerParams(
            dimension_semantics=("arbitrary", "arbitrary"),
            vmem_limit_bytes=_vmem_limit(pipelined, 0, 6 * _nbytes((tq, seq), F32)),
        ),
        name="fox_attn",
    )(main, main, main, main, cum, c_pad, w_ada, b_ada)


def _run_stages(gen, side=None):
    result = {}
    live = [g for g in (gen, side) if g is not None]
    while live:
        for g in list(live):
            try:
                next(g)
            except StopIteration as stop:
                result[g] = stop.value
                live.remove(g)
    return result[gen if side is None else side]


def _rwkv_inputs(raw_refs, small_ref, mu_refs, mu_s_ref, w0_ref, w2_ref, a0_ref, a2_ref, g2_ref,
                 prev_refs, prev_s, seqs):
    def shift_mix(x, prev_ref, q, mu):
        rows = x.shape[0]
        xp = pltpu.roll(x, 1, axis=0)
        row = lax.broadcasted_iota(jnp.int32, x.shape, 0)
        xp = jnp.where(row == 0, prev_ref[q, 0:1, :], xp)
        prev_ref[q, 0:1, :] = x[rows - 1:rows, :]
        return x + (xp - x) * mu

    r, k, v = ([shift_mix(ref[q].astype(F32), prev, q, mu[...]) for q in seqs]
               for ref, prev, mu in zip(raw_refs, prev_refs, mu_refs))
    sm = jnp.concatenate([shift_mix(small_ref[q], prev_s, q, mu_s_ref[...]) for q in seqs], axis=0)
    xw = sm[:, 0:LORA_PAD]
    xa = sm[:, DECAY_LORA:DECAY_LORA + LORA_PAD]
    xg = sm[:, DECAY_LORA + AAA_LORA:DECAY_LORA + AAA_LORA + GATE_LORA]

    z = w0_ref[...] + _dot(jnp.tanh(xw).astype(BF16), w2_ref[...])
    w_raw = -(jnp.maximum(-z, 0.0) + jnp.log1p(jnp.exp(-jnp.abs(z)))) - 0.5
    lw = -jnp.exp(w_raw)
    a = _sigmoid(a0_ref[...] + _dot(xa.astype(BF16), a2_ref[...]))
    g = _dot(_sigmoid(xg).astype(BF16), g2_ref[...])
    ch = lw.shape[0] // len(seqs)
    per_seq = lambda x: [x[j * ch:(j + 1) * ch, :] for j in range(len(seqs))]
    return r, k, v, per_seq(lw), per_seq(a), per_seq(g)


def _rwkv_kernel(r_ref, k_ref, v_ref, small_ref, mu_r_ref, mu_k_ref, mu_v_ref, mu_s_ref,
                 w0_ref, w2_ref, a0_ref, a2_ref, g2_ref,
                 kk_ref, ka_ref, rk_ref, lng_ref, lnb_ref, *rest, n_tiles, seq_groups, n_cast):
    cast_src, y_ref, cast_dst = rest[:n_cast], rest[n_cast], rest[n_cast + 1:2 * n_cast + 1]
    state, prev_r, prev_k, prev_v, prev_s = rest[2 * n_cast + 1:]
    for src, dst in zip(cast_src, cast_dst):
        dst[...] = src[...].astype(BF16)

    @pl.when(pl.program_id(1) == 0)
    def _():
        for ref in (state, prev_r, prev_k, prev_v, prev_s):
            ref[...] = jnp.zeros_like(ref)

    ch = y_ref.shape[1]
    hd = RWKV_HEAD_DIM
    lane = lax.broadcasted_iota(jnp.int32, (1, LANES), 1)
    m0 = lane < hd
    n2 = HEADS_PER_TILE * ch
    rr = lax.broadcasted_iota(jnp.int32, (n2, n2), 0)
    cc = lax.broadcasted_iota(jnp.int32, (n2, n2), 1)
    same = (rr // ch) == (cc // ch)
    strict = same & (rr > cc)
    incl = same & (rr >= cc)
    lr = lax.broadcasted_iota(jnp.int32, (ch, ch), 0)
    lc = lax.broadcasted_iota(jnp.int32, (ch, ch), 1)
    lower = (lc <= lr).astype(BF16)

    def head_sum(x):
        s0 = jnp.sum(jnp.where(m0, x, 0.0), axis=-1, keepdims=True)
        s1 = jnp.sum(jnp.where(m0, 0.0, x), axis=-1, keepdims=True)
        return jnp.where(m0, s0, s1)

    def stack(x):
        return jnp.concatenate([jnp.where(m0, x, 0.0), jnp.where(m0, 0.0, x)], axis=0)

    sls = [slice(p * LANES, (p + 1) * LANES) for p in range(n_tiles)]

    def each(fn, *cols):
        return [fn(*args) for args in zip(*cols)]

    stack2 = lambda x, y: jnp.concatenate([stack(x), stack(y)], axis=0).astype(BF16)
    incl2 = jnp.concatenate([incl, incl], axis=1)

    def prepare(seqs):
        per_chain = lambda ref: [ref[:, sl] for _ in seqs for sl in sls]
        full = _rwkv_inputs((r_ref, k_ref, v_ref), small_ref, (mu_r_ref, mu_k_ref, mu_v_ref), mu_s_ref,
                            w0_ref, w2_ref, a0_ref, a2_ref, g2_ref, (prev_r, prev_k, prev_v), prev_s, seqs)
        r, k, v, lw, a, g = ([x[j][:, sl] for j in range(len(seqs)) for sl in sls] for x in full)
        yield
        kk = each(lambda k_, w_: k_ * w_, k, per_chain(kk_ref))
        kk = each(lambda x: x / jnp.maximum(jnp.sqrt(head_sum(x * x)), 1e-12), kk)
        kr = each(lambda k_, a_, w_: k_ * (1.0 + (a_ - 1.0) * w_), k, a, per_chain(ka_ref))
        yield
        cum = each(lambda x: sum(_dot(lower, part) for part in _split3(x)[:2]), lw)
        yield
        gam = each(jnp.exp, cum)
        ginv = each(lambda x: jnp.exp(-x), cum)
        g_end = each(lambda x: x[ch - 1:ch, :], gam)
        yield
        a_t = each(lambda kk_, c_, lw_: -kk_ * jnp.exp(c_ - lw_), kk, cum, lw)
        r_t = each(lambda r_, g_: r_ * g_, r, gam)
        lhs = each(stack2, a_t, r_t)
        yield
        b_h = each(lambda kk_, a_, gi_: kk_ * a_ * gi_, kk, a, ginv)
        k_h = each(lambda kr_, gi_: kr_ * gi_, kr, ginv)
        rhs = each(stack2, b_h, k_h)
        yield
        vs = each(lambda v_: stack(v_).astype(BF16), v)
        bk = each(lambda b_, k_, ge_: stack2(b_ * ge_, k_ * ge_), b_h, k_h, g_end)
        yield
        bonus = each(lambda r_, kr_, w_, v_: head_sum(r_ * kr_ * w_) * v_, r, kr, per_chain(rk_ref), v)
        return lhs, rhs, vs, bk, g_end, bonus, g

    def advance(seqs, prepared):
        lhs, rhs, vs, bk, g_end, bonus, g = prepared
        per_chain = lambda ref: [ref[:, sl] for _ in seqs for sl in sls]
        slots = [q * n_tiles + p for q in seqs for p in range(n_tiles)]
        sc = each(_dot_nt, lhs, rhs)
        yield
        ab = each(lambda s: jnp.where(strict, s[0:n2, 0:n2], 0.0).astype(BF16), sc)
        ak = each(lambda s: jnp.where(strict, s[0:n2, n2:2 * n2], 0.0).astype(BF16), sc)
        rbk = each(lambda s: jnp.where(incl2, s[n2:2 * n2, :], 0.0).astype(BF16), sc)
        yield
        st = [state[i] for i in slots]
        ah = each(lambda l_, s_: _dot_nt(l_, s_.astype(BF16)), lhs, st)
        yield
        x = each(lambda ah_, ak_, vs_: ah_[0:n2] + _dot(ak_, vs_), ah, ak, vs)
        yield
        pw = ab
        n_lvl = ch.bit_length() - 1
        for lvl in range(n_lvl):
            x = each(lambda x_, p_: x_ + _dot(p_, x_.astype(BF16)), x, pw)
            yield
            if lvl + 1 < n_lvl:
                pw = each(lambda p_: _dot(p_, p_).astype(BF16), pw)
                yield
        uv = each(lambda x_, vs_: jnp.concatenate([x_.astype(BF16), vs_], axis=0), x, vs)
        ys = each(lambda ah_, m_, uv_: ah_[n2:2 * n2] + _dot(m_, uv_), ah, rbk, uv)
        yield
        y = each(lambda ys_: ys_[0:ch] + ys_[ch:n2], ys)
        new_st = each(lambda s_, ge_, uv_, bk_: s_ * ge_ + _dot_tn(uv_, bk_), st, g_end, uv, bk)
        for i, s_new in zip(slots, new_st):
            state[i] = s_new
        yield
        mu = each(lambda y_: head_sum(y_) * (1.0 / hd), y)
        yc = each(lambda y_, m_: y_ - m_, y, mu)
        var = each(lambda c_: head_sum(c_ * c_) * (1.0 / hd), yc)
        yield
        yn = each(lambda c_, v_, g_, b_: c_ * lax.rsqrt(v_ + RWKV_LNX_EPS) * g_ + b_,
                  yc, var, per_chain(lng_ref), per_chain(lnb_ref))
        for j, i in enumerate(slots):
            y_ref[i // n_tiles, :, sls[i % n_tiles]] = ((yn[j] + bonus[j]) * g[j]).astype(BF16)

    starts = [sum(seq_groups[:i]) for i in range(len(seq_groups))]
    groups = [tuple(range(s, s + n)) for s, n in zip(starts, seq_groups)]
    prepared = _run_stages(prepare(groups[0]))
    for gi, seqs in enumerate(groups):
        nxt = prepare(groups[gi + 1]) if gi + 1 < len(groups) else None
        prepared = _run_stages(advance(seqs, prepared), nxt)


def _slab_spec(w, n_steps, step_of):
    rep = 1
    while w.shape[0] % (n_steps // rep * 16):
        rep *= 2
        assert rep <= n_steps
    return pl.BlockSpec((w.shape[0] // (n_steps // rep), w.shape[1]),
                        lambda *idx: (step_of(*idx) // rep, 0))


def _rwkv(main, small, mus, mu_s, w0, w2p, a0, a2p, g2, head_rows, cast_weights, *, batch, seq,
          seq_groups=(2, 2)):
    n_seq = sum(seq_groups)
    t = main.shape[0]
    width = RWKV_HEADS * RWKV_HEAD_DIM
    nsm = small.shape[1]
    col0 = (4 * FOX_HEADS * FOX_HEAD_DIM) // width
    ch = RWKV_CHUNK
    steps = seq // ch
    n_tiles = width // LANES
    assert batch % n_seq == 0
    groups = batch // n_seq
    grouped = lambda x: x.reshape(groups, n_seq, seq, x.shape[-1])
    act = lambda cols, blk: pl.BlockSpec((None, n_seq, ch, cols), lambda b, c: (b, 0, c, blk))
    row = lambda n: pl.BlockSpec((1, n), lambda b, c: (0, 0))
    whole = lambda w: pl.BlockSpec(w.shape, lambda b, c: (0, 0))
    slabs = [_slab_spec(w, groups * steps, lambda b, c: b * steps + c) for w in cast_weights]
    pipelined = n_seq * (4 * _nbytes((ch, width), BF16) + _nbytes((ch, nsm), F32))
    pipelined += sum(_nbytes(s.block_shape, F32) + _nbytes(s.block_shape, BF16) for s in slabs)
    pipelined += sum(_nbytes(w.shape, BF16) for w in (w2p, a2p, g2))
    scratch = [(n_seq * n_tiles, LANES, LANES)] + [(n_seq, 8, width)] * 3 + [(n_seq, 8, nsm)]
    y, *copies = pl.pallas_call(
        functools.partial(_rwkv_kernel, n_tiles=n_tiles, seq_groups=seq_groups, n_cast=len(slabs)),
        grid=(groups, steps),
        in_specs=[act(width, col0), act(width, col0 + 1), act(width, col0 + 2), act(nsm, 0),
                  row(width), row(width), row(width), row(nsm),
                  row(width), whole(w2p), row(width), whole(a2p), whole(g2)]
        + [row(width)] * 5 + slabs,
        out_specs=[act(width, 0)] + slabs,
        out_shape=[jax.ShapeDtypeStruct((groups, n_seq, seq, width), BF16)]
        + [jax.ShapeDtypeStruct(w.shape, BF16) for w in cast_weights],
        scratch_shapes=[pltpu.VMEM(s, F32) for s in scratch],
        compiler_params=pltpu.CompilerParams(
            dimension_semantics=("arbitrary", "arbitrary"),
            vmem_limit_bytes=_vmem_limit(pipelined, sum(_nbytes(s, F32) for s in scratch), 24 << 20),
        ),
        name="rwkv",
    )(grouped(main), grouped(main), grouped(main), grouped(small), *mus, mu_s, w0, w2p, a0, a2p, g2,
      *head_rows, *cast_weights)
    return y.reshape(t, width), copies


def _mixout_kernel(o_ref, y_ref, ga0_ref, ga1_ref, gb0_ref, gb1_ref, wa_ref, wb_ref, wo_ref, x_ref,
                   gate_ref, lng_ref, lnb_ref, scale_ref, shift_ref, x1_ref, h2_ref, *, n_sub):
    tm, d = x_ref.shape
    rows = tm // n_sub
    half = d // 2
    sl = [slice(s * rows, (s + 1) * rows) for s in range(n_sub)]

    def branches(s):
        return _dot(o_ref[sl[s], :], wa_ref[...]), _dot(y_ref[sl[s], :], wb_ref[...])

    def mixed(s, ua, ub):
        parts = []
        for c, (ga, gb) in enumerate(((ga0_ref, gb0_ref), (ga1_ref, gb1_ref))):
            cs = slice(c * half, (c + 1) * half)
            parts.append(_sigmoid(ga[sl[s], :].astype(F32)) * ua[:, cs]
                         + _sigmoid(gb[sl[s], :].astype(F32)) * ub[:, cs])
        return _dot(jnp.concatenate(parts, axis=1).astype(BF16), wo_ref[...])

    def norms(s, mix):
        z = DEEPNORM_ALPHA * x_ref[sl[s], :] + gate_ref[0] * mix
        x1 = _layernorm(z) * lng_ref[...] + lnb_ref[...]
        x1_ref[sl[s], :] = x1
        h2_ref[sl[s], :] = (_layernorm(x1) * (1.0 + scale_ref[0]) + shift_ref[0]).astype(BF16)

    u = branches(0)
    mix_prev = None
    for s in range(n_sub):
        u_next = branches(s + 1) if s + 1 < n_sub else None
        mix = mixed(s, *u)
        if mix_prev is not None:
            norms(s - 1, mix_prev)
        u, mix_prev = u_next, mix
    norms(n_sub - 1, mix_prev)


def _mixout(o, y, main, wa, wb, wo, x2d, gate1, ln_g, ln_b, scale2, shift2, *, seq, tm=256, n_sub=2):
    t, d = x2d.shape
    kdim = o.shape[1]
    half = d // 2
    gate_blk0 = (main.shape[1] - 2 * d) // half
    tiles_per_batch = seq // tm
    mod = pl.BlockSpec((1, 1, d), lambda i: (i // tiles_per_batch, 0, 0))
    row = pl.BlockSpec((1, d), lambda i: (0, 0))
    gate = lambda c: pl.BlockSpec((tm, half), lambda i: (i, gate_blk0 + c))
    resident = lambda w: pl.BlockSpec(w.shape, lambda i: (0, 0), pipeline_mode=pl.Buffered(1))
    pipelined = (2 * _nbytes((tm, kdim), BF16) + 4 * _nbytes((tm, half), BF16)
                 + 2 * _nbytes((tm, d), F32) + _nbytes((tm, d), BF16))
    weights = _nbytes(wa.shape, BF16) + _nbytes(wb.shape, BF16) + _nbytes(wo.shape, BF16)
    return pl.pallas_call(
        functools.partial(_mixout_kernel, n_sub=n_sub),
        grid=(t // tm,),
        in_specs=[
            pl.BlockSpec((tm, kdim), lambda i: (i, 0)),
            pl.BlockSpec((tm, kdim), lambda i: (i, 0)),
            gate(0), gate(1), gate(2), gate(3),
            resident(wa), resident(wb), resident(wo),
            pl.BlockSpec((tm, d), lambda i: (i, 0)),
            mod, row, row, mod, mod,
        ],
        out_specs=[pl.BlockSpec((tm, d), lambda i: (i, 0)),
                   pl.BlockSpec((tm, d), lambda i: (i, 0))],
        out_shape=[jax.ShapeDtypeStruct((t, d), F32), jax.ShapeDtypeStruct((t, d), BF16)],
        compiler_params=pltpu.CompilerParams(
            dimension_semantics=("arbitrary",),
            vmem_limit_bytes=_vmem_limit(pipelined, weights, 8 * _nbytes((tm, d), F32)),
        ),
        name="mixout",
    )(o, y, main, main, main, main, wa, wb, wo, x2d, gate1, ln_g, ln_b, scale2, shift2)


def _ffn_kernel(h_ref, wg_ref, wu_ref, wd_ref, x1_ref, gate_ref, lng_ref, lnb_ref, o_ref, acc):
    f = pl.program_id(1)
    @pl.when(f == 0)
    def _():
        acc[...] = jnp.zeros_like(acc)

    h = h_ref[...]
    g = _dot(h, wg_ref[...])
    u = _dot(h, wu_ref[...])
    act = (g * _sigmoid(g) * u).astype(BF16)
    acc[...] += _dot(act, wd_ref[...])

    @pl.when(f == pl.num_programs(1) - 1)
    def _():
        z = DEEPNORM_ALPHA * x1_ref[...] + gate_ref[0] * acc[...]
        o_ref[...] = _layernorm(z) * lng_ref[...] + lnb_ref[...]


def _ffn(h2, w_gu, w_down, x1, gate2, ln_g, ln_b, *, seq, tm=512, tf=512):
    t, d = x1.shape
    d_ff = w_down.shape[0]
    nf = d_ff // tf
    tiles_per_batch = seq // tm
    mod = pl.BlockSpec((1, 1, d), lambda i, f: (i // tiles_per_batch, 0, 0))
    row = pl.BlockSpec((1, d), lambda i, f: (0, 0))
    pipelined = (_nbytes((tm, d), BF16) + 3 * _nbytes((d, tf), BF16) + 2 * _nbytes((tm, d), F32))
    return pl.pallas_call(
        _ffn_kernel,
        grid=(t // tm, nf),
        in_specs=[
            pl.BlockSpec((tm, d), lambda i, f: (i, 0)),
            pl.BlockSpec((d, tf), lambda i, f: (0, f)),
            pl.BlockSpec((d, tf), lambda i, f: (0, nf + f)),
            pl.BlockSpec((tf, d), lambda i, f: (f, 0)),
            pl.BlockSpec((tm, d), lambda i, f: (i, 0)),
            mod, row, row,
        ],
        out_specs=pl.BlockSpec((tm, d), lambda i, f: (i, 0)),
        out_shape=jax.ShapeDtypeStruct((t, d), F32),
        scratch_shapes=[pltpu.VMEM((tm, d), F32)],
        compiler_params=pltpu.CompilerParams(
            dimension_semantics=("arbitrary", "arbitrary"),
            vmem_limit_bytes=_vmem_limit(pipelined, _nbytes((tm, d), F32),
                                         2 * _nbytes((tm, d), F32) + 4 * _nbytes((tm, tf), F32)),
        ),
        name="ffn",
    )(h2, w_gu, w_gu, w_down, x1, gate2, ln_g, ln_b)


def _pad_rows(w, rows):
    return jnp.zeros((rows, w.shape[1]), w.dtype).at[:w.shape[0]].set(w)


def kernel(x, c, w_ada, b_ada, w_in, b_fgate, q_norm_g, k_norm_g, rwkv_mu, rwkv_w0, rwkv_w2, rwkv_a0, rwkv_a2, rwkv_g2, rwkv_k_k, rwkv_k_a, rwkv_r_k, rwkv_lnx_g, rwkv_lnx_b, w_branch_a, w_branch_b, w_out, ln1_g, ln1_b, w_ffn_gu, w_ffn_down, ln2_g, ln2_b):
    batch, seq, d = x.shape
    t = batch * seq
    fox_w = FOX_HEADS * FOX_HEAD_DIM
    rw_w = RWKV_HEADS * RWKV_HEAD_DIM
    fox_cols = 4 * fox_w + FOX_HEADS
    lora0 = fox_cols + 3 * rw_w
    gate0 = lora0 + DECAY_LORA + AAA_LORA + GATE_LORA
    assert seq % RWKV_CHUNK == 0 and w_in.shape[1] == gate0 + 2 * d

    lora_w = gate0 - lora0
    mu_r = rwkv_mu[None, 0:rw_w]
    mu_k = rwkv_mu[None, rw_w:2 * rw_w]
    mu_v = rwkv_mu[None, 2 * rw_w:3 * rw_w]
    mu_s = jnp.zeros((1, INPROJ_TN), rwkv_mu.dtype).at[0, :lora_w].set(rwkv_mu[3 * rw_w:])
    w2p = _pad_rows(rwkv_w2, LORA_PAD).astype(BF16)
    a2p = _pad_rows(rwkv_a2, LORA_PAD).astype(BF16)
    g2 = rwkv_g2.astype(BF16)
    row = lambda v: v.reshape(1, -1)

    c_pad = jnp.zeros((8, d), c.dtype).at[:batch].set(c)
    mod_head = _ada(c_pad, w_ada, row(b_ada), 2 * d)[:batch].reshape(batch, 2, 1, d)
    shift1, scale1 = mod_head[:, 0], mod_head[:, 1]

    x2d = x.reshape(t, d)
    main, small, ft = _inproj(x2d, scale1, shift1, w_in.T, row(q_norm_g), row(k_norm_g), seq=seq,
                              group_rows=[(0, 4 * fox_w), (fox_cols, 3 * rw_w), (gate0, 2 * d)],
                              lora_row=lora0, fgate_row=4 * fox_w, tn=INPROJ_TN)

    blocks_per_seq = seq // LANES
    ft_rows = ft.reshape(FOX_HEADS * batch * blocks_per_seq, LANES)
    bias_rows = jnp.broadcast_to(b_fgate[:, None, None], (FOX_HEADS, batch * blocks_per_seq, LANES))
    cum = _fox_prep(ft_rows, bias_rows.reshape(ft_rows.shape), blocks_per_seq=blocks_per_seq)
    cum = cum.reshape(FOX_HEADS, batch, 1, seq)
    o, mod_tail = _fox_attn(main, cum, c_pad, w_ada, row(b_ada), 2 * d, batch=batch, seq=seq)
    gate1, shift2, scale2, gate2 = (mod_tail[:batch].reshape(batch, 4, 1, d)[:, i] for i in range(4))

    y, (wa, wb, wo, wgu, wdn) = _rwkv(
        main, small, (mu_r, mu_k, mu_v), mu_s, row(rwkv_w0), w2p, row(rwkv_a0), a2p, g2,
        [row(p) for p in (rwkv_k_k, rwkv_k_a, rwkv_r_k, rwkv_lnx_g, rwkv_lnx_b)],
        [w_branch_a, w_branch_b, w_out, w_ffn_gu, w_ffn_down], batch=batch, seq=seq)

    x1, h2 = _mixout(o, y, main, wa, wb, wo, x2d, gate1, row(ln1_g), row(ln1_b), scale2, shift2, seq=seq)

    out = _ffn(h2, wgu, wdn, x1, gate2, row(ln2_g), row(ln2_b), seq=seq)
    return out.reshape(batch, seq, d)
```

```python
import functools

import jax
import jax.numpy as jnp
from jax import lax
from jax.experimental import pallas as pl
from jax.experimental.pallas import tpu as pltpu

F32 = jnp.float32
BF16 = jnp.bfloat16

LANES = 128
VMEM_CAP_BYTES = 60000 * 1024

FOX_HEADS = 8
FOX_HEAD_DIM = 128
RWKV_HEADS = 16
RWKV_HEAD_DIM = 64
DECAY_LORA = 96
AAA_LORA = 96
GATE_LORA = 256
LORA_PAD = 128
INPROJ_TN = 512
RWKV_LNX_EPS = 64e-5
DEPTH = 1
DEEPNORM_ALPHA = (2.0 * DEPTH) ** 0.25
LN_EPS = 1e-5
RMS_EPS = 1e-6
RWKV_CHUNK = 64
HEADS_PER_TILE = LANES // RWKV_HEAD_DIM


def _vmem_limit(pipelined_bytes, resident_bytes=0, temp_bytes=0):
    need = 2 * pipelined_bytes + resident_bytes + temp_bytes + (2 << 20)
    return int(min(VMEM_CAP_BYTES, need))


def _nbytes(shape, dtype):
    n = 1
    for s in shape:
        n *= s
    return n * jnp.dtype(dtype).itemsize


def _layernorm(x):
    mu = jnp.mean(x, axis=-1, keepdims=True)
    xc = x - mu
    var = jnp.mean(xc * xc, axis=-1, keepdims=True)
    return xc * lax.rsqrt(var + LN_EPS)


def _sigmoid(x):
    return 1.0 / (1.0 + jnp.exp(-x))


def _dot(a, b):
    return jnp.dot(a, b, preferred_element_type=F32)


def _dot_nt(a, b):
    return lax.dot_general(a, b, (((1,), (1,)), ((), ())), preferred_element_type=F32)


def _dot_tn(a, b):
    return lax.dot_general(a, b, (((0,), (0,)), ((), ())), preferred_element_type=F32)


def _split3(x):
    hi = x.astype(BF16)
    r1 = x - hi.astype(F32)
    mid = r1.astype(BF16)
    lo = (r1 - mid.astype(F32)).astype(BF16)
    return hi, mid, lo


def _ada_kernel(c_ref, w_ref, b_ref, o_ref):
    c = c_ref[...]
    s = (c * _sigmoid(c)).astype(BF16)
    o_ref[...] = _dot(s, w_ref[...].astype(BF16)) + b_ref[...]


def _ada(c_pad, w_ada, b_ada, n, *, tn=1024):
    rows, d = c_pad.shape
    return pl.pallas_call(
        _ada_kernel,
        grid=(n // tn,),
        in_specs=[
            pl.BlockSpec((rows, d), lambda j: (0, 0)),
            pl.BlockSpec((d, tn), lambda j: (0, j)),
            pl.BlockSpec((1, tn), lambda j: (0, j)),
        ],
        out_specs=pl.BlockSpec((rows, tn), lambda j: (0, j)),
        out_shape=jax.ShapeDtypeStruct((rows, n), F32),
        compiler_params=pltpu.CompilerParams(
            dimension_semantics=("arbitrary",),
            vmem_limit_bytes=_vmem_limit(_nbytes((d, tn), F32), temp_bytes=_nbytes((d, tn), BF16)),
        ),
        name="ada",
    )(c_pad, w_ada, b_ada)


def _inproj_kernel(x_ref, scale_ref, shift_ref, wt_ref, wft_ref, qg_ref, kg_ref,
                   main_ref, small_ref, ft_ref, h_scr, *, n_q, n_k, n_main):
    j = pl.program_id(1)
    tn = wt_ref.shape[0]
    wide = main_ref.shape[1] // tn
    col0 = pl.multiple_of((j % wide) * tn, tn)

    @pl.when(j == 0)
    def _():
        h = _layernorm(x_ref[...]) * (1.0 + scale_ref[0]) + shift_ref[0]
        h_scr[...] = h.astype(BF16)

    def proj():
        w = wt_ref[...].astype(BF16)
        half = h_scr.shape[0] // 2
        return jnp.concatenate([_dot_nt(h_scr[0:half, :], w), _dot_nt(h_scr[half:, :], w)], axis=0)

    @pl.when(j < n_q + n_k)
    def _():
        acc = proj()
        gain = jnp.where(j < n_q, qg_ref[...] * (FOX_HEAD_DIM ** -0.5), kg_ref[...])
        for hh in range(tn // FOX_HEAD_DIM):
            a = acc[:, hh * FOX_HEAD_DIM:(hh + 1) * FOX_HEAD_DIM]
            ms = jnp.mean(a * a, axis=-1, keepdims=True)
            main_ref[:, pl.ds(col0 + hh * FOX_HEAD_DIM, FOX_HEAD_DIM)] = (
                a * lax.rsqrt(ms + RMS_EPS) * gain).astype(BF16)

    @pl.when((j >= n_q + n_k) & (j < n_main))
    def _():
        main_ref[:, pl.ds(col0, tn)] = proj().astype(BF16)

    @pl.when(j == n_main)
    def _():
        small_ref[...] = proj()
        ft_ref[...] = _dot_nt(wft_ref[...].astype(BF16), h_scr[...])


def _inproj(x2d, scale1, shift1, w_in_t, qg, kg, *, seq, group_rows, lora_row, fgate_row,
            tm=1024, tn=512, wide=2):
    t, d = x2d.shape
    fox_w = FOX_HEADS * FOX_HEAD_DIM
    assert all(first % 8 == 0 and count % tn == 0 for first, count in group_rows)
    assert lora_row % 8 == 0 and fgate_row % 8 == 0 and lora_row + tn <= w_in_t.shape[0]
    n_main = sum(count // tn for _, count in group_rows)
    tiles_per_batch = seq // tm
    kern = functools.partial(_inproj_kernel, n_q=fox_w // tn, n_k=fox_w // tn, n_main=n_main)
    assert n_main % wide == 0
    pipelined = (_nbytes((tm, d), F32) + _nbytes((tn, d), F32) + _nbytes((tm, wide * tn), BF16)
                 + _nbytes((tm, tn), F32) + _nbytes((8, tm), F32))

    def w_rows(i, j):
        row, hi = jnp.int32(lora_row), n_main
        for first, count in reversed(group_rows):
            lo = hi - count // tn
            row = jnp.where(j < hi, first + tn * (j - lo), row)
            hi = lo
        return pl.multiple_of(row, 8), 0

    return pl.pallas_call(
        kern,
        grid=(t // tm, n_main + 1),
        in_specs=[
            pl.BlockSpec((tm, d), lambda i, j: (i, 0)),
            pl.BlockSpec((1, 1, d), lambda i, j: (i // tiles_per_batch, 0, 0)),
            pl.BlockSpec((1, 1, d), lambda i, j: (i // tiles_per_batch, 0, 0)),
            pl.BlockSpec((pl.Element(tn), pl.Element(d)), w_rows),
            pl.BlockSpec((pl.Element(FOX_HEADS), pl.Element(d)), lambda i, j: (fgate_row, 0)),
            pl.BlockSpec((1, FOX_HEAD_DIM), lambda i, j: (0, 0)),
            pl.BlockSpec((1, FOX_HEAD_DIM), lambda i, j: (0, 0)),
        ],
        out_specs=[
            pl.BlockSpec((tm, wide * tn), lambda i, j: (i, jnp.minimum(j, n_main - 1) // wide)),
            pl.BlockSpec((tm, tn), lambda i, j: (i, 0)),
            pl.BlockSpec((FOX_HEADS, tm), lambda i, j: (0, i)),
        ],
        out_shape=[
            jax.ShapeDtypeStruct((t, n_main * tn), BF16),
            jax.ShapeDtypeStruct((t, tn), F32),
            jax.ShapeDtypeStruct((FOX_HEADS, t), F32),
        ],
        scratch_shapes=[pltpu.VMEM((tm, d), BF16)],
        compiler_params=pltpu.CompilerParams(
            dimension_semantics=("arbitrary", "arbitrary"),
            vmem_limit_bytes=_vmem_limit(pipelined, _nbytes((tm, d), BF16),
                                         3 * _nbytes((tm, d), F32)),
        ),
        name="inproj",
    )(x2d, scale1, shift1, w_in_t, w_in_t, qg, kg)


def _fox_prep_kernel(ft_ref, bias_ref, o_ref, *, blocks_per_seq):
    x = ft_ref[...] + bias_ref[...]
    lf = jnp.minimum(x, 0.0) - jnp.log1p(jnp.exp(-jnp.abs(x)))
    rows, width = lf.shape
    r = lax.broadcasted_iota(jnp.int32, (width, width), 0)
    c = lax.broadcasted_iota(jnp.int32, (width, width), 1)
    upper = (r <= c).astype(BF16)
    within = sum(_dot(p, upper) for p in _split3(lf))
    tot = jnp.broadcast_to(within[:, width - 1:width], (rows, width))
    rr = lax.broadcasted_iota(jnp.int32, (rows, rows), 0)
    cc = lax.broadcasted_iota(jnp.int32, (rows, rows), 1)
    prior = ((rr // blocks_per_seq == cc // blocks_per_seq) & (cc < rr)).astype(BF16)
    offset = sum(_dot(prior, p) for p in _split3(tot))
    o_ref[...] = within + offset


def _fox_prep(ft_rows, bias_rows, *, blocks_per_seq):
    return pl.pallas_call(
        functools.partial(_fox_prep_kernel, blocks_per_seq=blocks_per_seq),
        out_shape=jax.ShapeDtypeStruct(ft_rows.shape, F32),
        name="fox_prep",
    )(ft_rows, bias_rows)


def _fox_attn_kernel(q_ref, k_ref, v_ref, og_ref, cum_ref, c_ref, wada_ref, bada_ref, o_ref, mod_ref, *, tq):
    _ada_kernel(c_ref, wada_ref, bada_ref, mod_ref)

    seq = q_ref.shape[0]
    n_q = seq // tq
    negcum = -cum_ref[0, 0]
    row = lax.broadcasted_iota(jnp.int32, (tq, tq), 0)
    col = lax.broadcasted_iota(jnp.int32, (tq, tq), 1)
    causal = col <= row

    def logits(qi):
        q0, kend = qi * tq, (qi + 1) * tq
        q = q_ref[q0:kend, :]
        diag = _dot_nt(q, k_ref[q0:kend, :]) + negcum[:, q0:kend]
        diag = jnp.where(causal, diag, -jnp.inf)
        past = _dot_nt(q, k_ref[0:q0, :]) + negcum[:, 0:q0] if qi else None
        return past, diag

    def attend(qi, past, diag):
        q0, kend = qi * tq, (qi + 1) * tq
        m = jnp.max(diag, axis=-1, keepdims=True)
        if past is not None:
            m = jnp.maximum(m, jnp.max(past, axis=-1, keepdims=True))
        p = jnp.exp(diag - m)
        l = jnp.sum(p, axis=-1, keepdims=True)
        o = _dot(p.astype(BF16), v_ref[q0:kend, :])
        if past is not None:
            p = jnp.exp(past - m)
            l = l + jnp.sum(p, axis=-1, keepdims=True)
            o = o + _dot(p.astype(BF16), v_ref[0:q0, :])
        gate = _sigmoid(og_ref[q0:kend, :].astype(F32))
        o_ref[q0:kend, :] = (o / l * gate).astype(BF16)

    nxt = logits(0)
    for qi in range(n_q):
        cur, nxt = nxt, (logits(qi + 1) if qi + 1 < n_q else None)
        attend(qi, *cur)


def _fox_attn(main, cum, c_pad, w_ada, b_ada, ada_col0, *, batch, seq, tq=256):
    t = main.shape[0]
    hd = FOX_HEAD_DIM
    nh = FOX_HEADS
    rows, d = c_pad.shape
    n_ada = w_ada.shape[1] - ada_col0
    tn = n_ada // (batch * nh)
    assert n_ada % (batch * nh) == 0 and tn % LANES == 0 and ada_col0 % tn == 0
    slab = lambda r: pl.BlockSpec((r, tn), lambda b, h: (0, ada_col0 // tn + b * nh + h))
    blk = lambda off: pl.BlockSpec((seq, hd), lambda b, h: (b, off + h))
    pipelined = 5 * _nbytes((seq, hd), BF16) + _nbytes((8, seq), F32) + _nbytes((d, tn), F32)
    return pl.pallas_call(
        functools.partial(_fox_attn_kernel, tq=tq),
        grid=(batch, nh),
        in_specs=[blk(0), blk(nh), blk(2 * nh), blk(3 * nh),
                  pl.BlockSpec((1, 1, 1, seq), lambda b, h: (h, b, 0, 0)),
                  pl.BlockSpec((rows, d), lambda b, h: (0, 0)), slab(d), slab(1)],
        out_specs=[pl.BlockSpec((seq, hd), lambda b, h: (b, h)),
                   pl.BlockSpec((rows, tn), lambda b, h: (0, b * nh + h))],
        out_shape=[jax.ShapeDtypeStruct((t, nh * hd), BF16), jax.ShapeDtypeStruct((rows, n_ada), F32)],
        compiler_params=pltpu.CompilerParams(
            dimension_semantics=("arbitrary", "arbitrary"),
            vmem_limit_bytes=_vmem_limit(pipelined, 0, 6 * _nbytes((tq, seq), F32)),
        ),
        name="fox_attn",
    )(main, main, main, main, cum, c_pad, w_ada, b_ada)


def _run_stages(gen, side=None):
    result = {}
    live = [g for g in (gen, side) if g is not None]
    while live:
        for g in list(live):
            try:
                next(g)
            except StopIteration as stop:
                result[g] = stop.value
                live.remove(g)
    return result[gen if side is None else side]


def _rwkv_inputs(raw_refs, small_ref, mu_refs, mu_s_ref, w0_ref, w2_ref, a0_ref, a2_ref, g2_ref,
                 prev_refs, prev_s, seqs):
    def shift_mix(x, prev_ref, q, mu):
        rows = x.shape[0]
        xp = pltpu.roll(x, 1, axis=0)
        row = lax.broadcasted_iota(jnp.int32, x.shape, 0)
        xp = jnp.where(row == 0, prev_ref[q, 0:1, :], xp)
        prev_ref[q, 0:1, :] = x[rows - 1:rows, :]
        return x + (xp - x) * mu

    r, k, v = ([shift_mix(ref[q].astype(F32), prev, q, mu[...]) for q in seqs]
               for ref, prev, mu in zip(raw_refs, prev_refs, mu_refs))
    sm = jnp.concatenate([shift_mix(small_ref[q], prev_s, q, mu_s_ref[...]) for q in seqs], axis=0)
    xw = sm[:, 0:LORA_PAD]
    xa = sm[:, DECAY_LORA:DECAY_LORA + LORA_PAD]
    xg = sm[:, DECAY_LORA + AAA_LORA:DECAY_LORA + AAA_LORA + GATE_LORA]

    z = w0_ref[...] + _dot(jnp.tanh(xw).astype(BF16), w2_ref[...])
    w_raw = -(jnp.maximum(-z, 0.0) + jnp.log1p(jnp.exp(-jnp.abs(z)))) - 0.5
    lw = -jnp.exp(w_raw)
    a = _sigmoid(a0_ref[...] + _dot(xa.astype(BF16), a2_ref[...]))
    g = _dot(_sigmoid(xg).astype(BF16), g2_ref[...])
    ch = lw.shape[0] // len(seqs)
    per_seq = lambda x: [x[j * ch:(j + 1) * ch, :] for j in range(len(seqs))]
    return r, k, v, per_seq(lw), per_seq(a), per_seq(g)


def _rwkv_kernel(r_ref, k_ref, v_ref, small_ref, mu_r_ref, mu_k_ref, mu_v_ref, mu_s_ref,
                 w0_ref, w2_ref, a0_ref, a2_ref, g2_ref,
                 kk_ref, ka_ref, rk_ref, lng_ref, lnb_ref, *rest, n_tiles, seq_groups, n_cast):
    cast_src, y_ref, cast_dst = rest[:n_cast], rest[n_cast], rest[n_cast + 1:2 * n_cast + 1]
    state, prev_r, prev_k, prev_v, prev_s = rest[2 * n_cast + 1:]
    for src, dst in zip(cast_src, cast_dst):
        dst[...] = src[...].astype(BF16)

    @pl.when(pl.program_id(1) == 0)
    def _():
        for ref in (state, prev_r, prev_k, prev_v, prev_s):
            ref[...] = jnp.zeros_like(ref)

    ch = y_ref.shape[1]
    hd = RWKV_HEAD_DIM
    lane = lax.broadcasted_iota(jnp.int32, (1, LANES), 1)
    m0 = lane < hd
    n2 = HEADS_PER_TILE * ch
    rr = lax.broadcasted_iota(jnp.int32, (n2, n2), 0)
    cc = lax.broadcasted_iota(jnp.int32, (n2, n2), 1)
    same = (rr // ch) == (cc // ch)
    strict = same & (rr > cc)
    incl = same & (rr >= cc)
    lr = lax.broadcasted_iota(jnp.int32, (ch, ch), 0)
    lc = lax.broadcasted_iota(jnp.int32, (ch, ch), 1)
    lower = (lc <= lr).astype(BF16)

    def head_sum(x):
        s0 = jnp.sum(jnp.where(m0, x, 0.0), axis=-1, keepdims=True)
        s1 = jnp.sum(jnp.where(m0, 0.0, x), axis=-1, keepdims=True)
        return jnp.where(m0, s0, s1)

    def stack(x):
        return jnp.concatenate([jnp.where(m0, x, 0.0), jnp.where(m0, 0.0, x)], axis=0)

    sls = [slice(p * LANES, (p + 1) * LANES) for p in range(n_tiles)]

    def each(fn, *cols):
        return [fn(*args) for args in zip(*cols)]

    stack2 = lambda x, y: jnp.concatenate([stack(x), stack(y)], axis=0).astype(BF16)
    incl2 = jnp.concatenate([incl, incl], axis=1)

    def prepare(seqs):
        per_chain = lambda ref: [ref[:, sl] for _ in seqs for sl in sls]
        full = _rwkv_inputs((r_ref, k_ref, v_ref), small_ref, (mu_r_ref, mu_k_ref, mu_v_ref), mu_s_ref,
                            w0_ref, w2_ref, a0_ref, a2_ref, g2_ref, (prev_r, prev_k, prev_v), prev_s, seqs)
        r, k, v, lw, a, g = ([x[j][:, sl] for j in range(len(seqs)) for sl in sls] for x in full)
        yield
        kk = each(lambda k_, w_: k_ * w_, k, per_chain(kk_ref))
        kk = each(lambda x: x / jnp.maximum(jnp.sqrt(head_sum(x * x)), 1e-12), kk)
        kr = each(lambda k_, a_, w_: k_ * (1.0 + (a_ - 1.0) * w_), k, a, per_chain(ka_ref))
        yield
        cum = each(lambda x: sum(_dot(lower, part) for part in _split3(x)[:2]), lw)
        yield
        gam = each(jnp.exp, cum)
        ginv = each(lambda x: jnp.exp(-x), cum)
        g_end = each(lambda x: x[ch - 1:ch, :], gam)
        yield
        a_t = each(lambda kk_, c_, lw_: -kk_ * jnp.exp(c_ - lw_), kk, cum, lw)
        r_t = each(lambda r_, g_: r_ * g_, r, gam)
        lhs = each(stack2, a_t, r_t)
        yield
        b_h = each(lambda kk_, a_, gi_: kk_ * a_ * gi_, kk, a, ginv)
        k_h = each(lambda kr_, gi_: kr_ * gi_, kr, ginv)
        rhs = each(stack2, b_h, k_h)
        yield
        vs = each(lambda v_: stack(v_).astype(BF16), v)
        bk = each(lambda b_, k_, ge_: stack2(b_ * ge_, k_ * ge_), b_h, k_h, g_end)
        yield
        bonus = each(lambda r_, kr_, w_, v_: head_sum(r_ * kr_ * w_) * v_, r, kr, per_chain(rk_ref), v)
        return lhs, rhs, vs, bk, g_end, bonus, g

    def advance(seqs, prepared):
        lhs, rhs, vs, bk, g_end, bonus, g = prepared
        per_chain = lambda ref: [ref[:, sl] for _ in seqs for sl in sls]
        slots = [q * n_tiles + p for q in seqs for p in range(n_tiles)]
        sc = each(_dot_nt, lhs, rhs)
        yield
        ab = each(lambda s: jnp.where(strict, s[0:n2, 0:n2], 0.0).astype(BF16), sc)
        ak = each(lambda s: jnp.where(strict, s[0:n2, n2:2 * n2], 0.0).astype(BF16), sc)
        rbk = each(lambda s: jnp.where(incl2, s[n2:2 * n2, :], 0.0).astype(BF16), sc)
        yield
        st = [state[i] for i in slots]
        ah = each(lambda l_, s_: _dot_nt(l_, s_.astype(BF16)), lhs, st)
        yield
        x = each(lambda ah_, ak_, vs_: ah_[0:n2] + _dot(ak_, vs_), ah, ak, vs)
        yield
        pw = ab
        n_lvl = ch.bit_length() - 1
        for lvl in range(n_lvl):
            x = each(lambda x_, p_: x_ + _dot(p_, x_.astype(BF16)), x, pw)
            yield
            if lvl + 1 < n_lvl:
                pw = each(lambda p_: _dot(p_, p_).astype(BF16), pw)
                yield
        uv = each(lambda x_, vs_: jnp.concatenate([x_.astype(BF16), vs_], axis=0), x, vs)
        ys = each(lambda ah_, m_, uv_: ah_[n2:2 * n2] + _dot(m_, uv_), ah, rbk, uv)
        yield
        y = each(lambda ys_: ys_[0:ch] + ys_[ch:n2], ys)
        new_st = each(lambda s_, ge_, uv_, bk_: s_ * ge_ + _dot_tn(uv_, bk_), st, g_end, uv, bk)
        for i, s_new in zip(slots, new_st):
            state[i] = s_new
        yield
        mu = each(lambda y_: head_sum(y_) * (1.0 / hd), y)
        yc = each(lambda y_, m_: y_ - m_, y, mu)
        var = each(lambda c_: head_sum(c_ * c_) * (1.0 / hd), yc)
        yield
        yn = each(lambda c_, v_, g_, b_: c_ * lax.rsqrt(v_ + RWKV_LNX_EPS) * g_ + b_,
                  yc, var, per_chain(lng_ref), per_chain(lnb_ref))
        for j, i in enumerate(slots):
            y_ref[i // n_tiles, :, sls[i % n_tiles]] = ((yn[j] + bonus[j]) * g[j]).astype(BF16)

    starts = [sum(seq_groups[:i]) for i in range(len(seq_groups))]
    groups = [tuple(range(s, s + n)) for s, n in zip(starts, seq_groups)]
    prepared = _run_stages(prepare(groups[0]))
    for gi, seqs in enumerate(groups):
        nxt = prepare(groups[gi + 1]) if gi + 1 < len(groups) else None
        prepared = _run_stages(advance(seqs, prepared), nxt)


def _slab_spec(w, n_steps, step_of):
    rep = 1
    while w.shape[0] % (n_steps // rep * 16):
        rep *= 2
        assert rep <= n_steps
    return pl.BlockSpec((w.shape[0] // (n_steps // rep), w.shape[1]),
                        lambda *idx: (step_of(*idx) // rep, 0))


def _rwkv(main, small, mus, mu_s, w0, w2p, a0, a2p, g2, head_rows, cast_weights, *, batch, seq,
          seq_groups=(2, 2)):
    n_seq = sum(seq_groups)
    t = main.shape[0]
    width = RWKV_HEADS * RWKV_HEAD_DIM
    nsm = small.shape[1]
    col0 = (4 * FOX_HEADS * FOX_HEAD_DIM) // width
    ch = RWKV_CHUNK
    steps = seq // ch
    n_tiles = width // LANES
    assert batch % n_seq == 0
    groups = batch // n_seq
    grouped = lambda x: x.reshape(groups, n_seq, seq, x.shape[-1])
    act = lambda cols, blk: pl.BlockSpec((None, n_seq, ch, cols), lambda b, c: (b, 0, c, blk))
    row = lambda n: pl.BlockSpec((1, n), lambda b, c: (0, 0))
    whole = lambda w: pl.BlockSpec(w.shape, lambda b, c: (0, 0))
    slabs = [_slab_spec(w, groups * steps, lambda b, c: b * steps + c) for w in cast_weights]
    pipelined = n_seq * (4 * _nbytes((ch, width), BF16) + _nbytes((ch, nsm), F32))
    pipelined += sum(_nbytes(s.block_shape, F32) + _nbytes(s.block_shape, BF16) for s in slabs)
    pipelined += sum(_nbytes(w.shape, BF16) for w in (w2p, a2p, g2))
    scratch = [(n_seq * n_tiles, LANES, LANES)] + [(n_seq, 8, width)] * 3 + [(n_seq, 8, nsm)]
    y, *copies = pl.pallas_call(
        functools.partial(_rwkv_kernel, n_tiles=n_tiles, seq_groups=seq_groups, n_cast=len(slabs)),
        grid=(groups, steps),
        in_specs=[act(width, col0), act(width, col0 + 1), act(width, col0 + 2), act(nsm, 0),
                  row(width), row(width), row(width), row(nsm),
                  row(width), whole(w2p), row(width), whole(a2p), whole(g2)]
        + [row(width)] * 5 + slabs,
        out_specs=[act(width, 0)] + slabs,
        out_shape=[jax.ShapeDtypeStruct((groups, n_seq, seq, width), BF16)]
        + [jax.ShapeDtypeStruct(w.shape, BF16) for w in cast_weights],
        scratch_shapes=[pltpu.VMEM(s, F32) for s in scratch],
        compiler_params=pltpu.CompilerParams(
            dimension_semantics=("arbitrary", "arbitrary"),
            vmem_limit_bytes=_vmem_limit(pipelined, sum(_nbytes(s, F32) for s in scratch), 24 << 20),
        ),
        name="rwkv",
    )(grouped(main), grouped(main), grouped(main), grouped(small), *mus, mu_s, w0, w2p, a0, a2p, g2,
      *head_rows, *cast_weights)
    return y.reshape(t, width), copies


def _mixout_kernel(o_ref, y_ref, ga0_ref, ga1_ref, gb0_ref, gb1_ref, wa_ref, wb_ref, wo_ref, x_ref,
                   gate_ref, lng_ref, lnb_ref, scale_ref, shift_ref, x1_ref, h2_ref, *, n_sub):
    tm, d = x_ref.shape
    rows = tm // n_sub
    half = d // 2
    sl = [slice(s * rows, (s + 1) * rows) for s in range(n_sub)]

    def branches(s):
        return _dot(o_ref[sl[s], :], wa_ref[...]), _dot(y_ref[sl[s], :], wb_ref[...])

    def mixed(s, ua, ub):
        parts = []
        for c, (ga, gb) in enumerate(((ga0_ref, gb0_ref), (ga1_ref, gb1_ref))):
            cs = slice(c * half, (c + 1) * half)
            parts.append(_sigmoid(ga[sl[s], :].astype(F32)) * ua[:, cs]
                         + _sigmoid(gb[sl[s], :].astype(F32)) * ub[:, cs])
        return _dot(jnp.concatenate(parts, axis=1).astype(BF16), wo_ref[...])

    def norms(s, mix):
        z = DEEPNORM_ALPHA * x_ref[sl[s], :] + gate_ref[0] * mix
        x1 = _layernorm(z) * lng_ref[...] + lnb_ref[...]
        x1_ref[sl[s], :] = x1
        h2_ref[sl[s], :] = (_layernorm(x1) * (1.0 + scale_ref[0]) + shift_ref[0]).astype(BF16)

    u = branches(0)
    mix_prev = None
    for s in range(n_sub):
        u_next = branches(s + 1) if s + 1 < n_sub else None
        mix = mixed(s, *u)
        if mix_prev is not None:
            norms(s - 1, mix_prev)
        u, mix_prev = u_next, mix
    norms(n_sub - 1, mix_prev)


def _mixout(o, y, main, wa, wb, wo, x2d, gate1, ln_g, ln_b, scale2, shift2, *, seq, tm=256, n_sub=2):
    t, d = x2d.shape
    kdim = o.shape[1]
    half = d // 2
    gate_blk0 = (main.shape[1] - 2 * d) // half
    tiles_per_batch = seq // tm
    mod = pl.BlockSpec((1, 1, d), lambda i: (i // tiles_per_batch, 0, 0))
    row = pl.BlockSpec((1, d), lambda i: (0, 0))
    gate = lambda c: pl.BlockSpec((tm, half), lambda i: (i, gate_blk0 + c))
    resident = lambda w: pl.BlockSpec(w.shape, lambda i: (0, 0), pipeline_mode=pl.Buffered(1))
    pipelined = (2 * _nbytes((tm, kdim), BF16) + 4 * _nbytes((tm, half), BF16)
                 + 2 * _nbytes((tm, d), F32) + _nbytes((tm, d), BF16))
    weights = _nbytes(wa.shape, BF16) + _nbytes(wb.shape, BF16) + _nbytes(wo.shape, BF16)
    return pl.pallas_call(
        functools.partial(_mixout_kernel, n_sub=n_sub),
        grid=(t // tm,),
        in_specs=[
            pl.BlockSpec((tm, kdim), lambda i: (i, 0)),
            pl.BlockSpec((tm, kdim), lambda i: (i, 0)),
            gate(0), gate(1), gate(2), gate(3),
            resident(wa), resident(wb), resident(wo),
            pl.BlockSpec((tm, d), lambda i: (i, 0)),
            mod, row, row, mod, mod,
        ],
        out_specs=[pl.BlockSpec((tm, d), lambda i: (i, 0)),
                   pl.BlockSpec((tm, d), lambda i: (i, 0))],
        out_shape=[jax.ShapeDtypeStruct((t, d), F32), jax.ShapeDtypeStruct((t, d), BF16)],
        compiler_params=pltpu.CompilerParams(
            dimension_semantics=("arbitrary",),
            vmem_limit_bytes=_vmem_limit(pipelined, weights, 8 * _nbytes((tm, d), F32)),
        ),
        name="mixout",
    )(o, y, main, main, main, main, wa, wb, wo, x2d, gate1, ln_g, ln_b, scale2, shift2)


def _ffn_kernel(h_ref, wg_ref, wu_ref, wd_ref, x1_ref, gate_ref, lng_ref, lnb_ref, o_ref, acc):
    f = pl.program_id(1)
    @pl.when(f == 0)
    def _():
        acc[...] = jnp.zeros_like(acc)

    h = h_ref[...]
    g = _dot(h, wg_ref[...])
    u = _dot(h, wu_ref[...])
    act = (g * _sigmoid(g) * u).astype(BF16)
    acc[...] += _dot(act, wd_ref[...])

    @pl.when(f == pl.num_programs(1) - 1)
    def _():
        z = DEEPNORM_ALPHA * x1_ref[...] + gate_ref[0] * acc[...]
        o_ref[...] = _layernorm(z) * lng_ref[...] + lnb_ref[...]


def _ffn(h2, w_gu, w_down, x1, gate2, ln_g, ln_b, *, seq, tm=512, tf=512):
    t, d = x1.shape
    d_ff = w_down.shape[0]
    nf = d_ff // tf
    tiles_per_batch = seq // tm
    mod = pl.BlockSpec((1, 1, d), lambda i, f: (i // tiles_per_batch, 0, 0))
    row = pl.BlockSpec((1, d), lambda i, f: (0, 0))
    pipelined = (_nbytes((tm, d), BF16) + 3 * _nbytes((d, tf), BF16) + 2 * _nbytes((tm, d), F32))
    return pl.pallas_call(
        _ffn_kernel,
        grid=(t // tm, nf),
        in_specs=[
            pl.BlockSpec((tm, d), lambda i, f: (i, 0)),
            pl.BlockSpec((d, tf), lambda i, f: (0, f)),
            pl.BlockSpec((d, tf), lambda i, f: (0, nf + f)),
            pl.BlockSpec((tf, d), lambda i, f: (f, 0)),
            pl.BlockSpec((tm, d), lambda i, f: (i, 0)),
            mod, row, row,
        ],
        out_specs=pl.BlockSpec((tm, d), lambda i, f: (i, 0)),
        out_shape=jax.ShapeDtypeStruct((t, d), F32),
        scratch_shapes=[pltpu.VMEM((tm, d), F32)],
        compiler_params=pltpu.CompilerParams(
            dimension_semantics=("arbitrary", "arbitrary"),
            vmem_limit_bytes=_vmem_limit(pipelined, _nbytes((tm, d), F32),
                                         2 * _nbytes((tm, d), F32) + 4 * _nbytes((tm, tf), F32)),
        ),
        name="ffn",
    )(h2, w_gu, w_gu, w_down, x1, gate2, ln_g, ln_b)


def _pad_rows(w, rows):
    return jnp.zeros((rows, w.shape[1]), w.dtype).at[:w.shape[0]].set(w)


def kernel(x, c, w_ada, b_ada, w_in, b_fgate, q_norm_g, k_norm_g, rwkv_mu, rwkv_w0, rwkv_w2, rwkv_a0, rwkv_a2, rwkv_g2, rwkv_k_k, rwkv_k_a, rwkv_r_k, rwkv_lnx_g, rwkv_lnx_b, w_branch_a, w_branch_b, w_out, ln1_g, ln1_b, w_ffn_gu, w_ffn_down, ln2_g, ln2_b):
    batch, seq, d = x.shape
    t = batch * seq
    fox_w = FOX_HEADS * FOX_HEAD_DIM
    rw_w = RWKV_HEADS * RWKV_HEAD_DIM
    fox_cols = 4 * fox_w + FOX_HEADS
    lora0 = fox_cols + 3 * rw_w
    gate0 = lora0 + DECAY_LORA + AAA_LORA + GATE_LORA
    assert seq % RWKV_CHUNK == 0 and w_in.shape[1] == gate0 + 2 * d

    lora_w = gate0 - lora0
    mu_r = rwkv_mu[None, 0:rw_w]
    mu_k = rwkv_mu[None, rw_w:2 * rw_w]
    mu_v = rwkv_mu[None, 2 * rw_w:3 * rw_w]
    mu_s = jnp.zeros((1, INPROJ_TN), rwkv_mu.dtype).at[0, :lora_w].set(rwkv_mu[3 * rw_w:])
    w2p = _pad_rows(rwkv_w2, LORA_PAD).astype(BF16)
    a2p = _pad_rows(rwkv_a2, LORA_PAD).astype(BF16)
    g2 = rwkv_g2.astype(BF16)
    row = lambda v: v.reshape(1, -1)

    c_pad = jnp.zeros((8, d), c.dtype).at[:batch].set(c)
    mod_head = _ada(c_pad, w_ada, row(b_ada), 2 * d)[:batch].reshape(batch, 2, 1, d)
    shift1, scale1 = mod_head[:, 0], mod_head[:, 1]

    x2d = x.reshape(t, d)
    main, small, ft = _inproj(x2d, scale1, shift1, w_in.T, row(q_norm_g), row(k_norm_g), seq=seq,
                              group_rows=[(0, 4 * fox_w), (fox_cols, 3 * rw_w), (gate0, 2 * d)],
                              lora_row=lora0, fgate_row=4 * fox_w, tn=INPROJ_TN)

    blocks_per_seq = seq // LANES
    ft_rows = ft.reshape(FOX_HEADS * batch * blocks_per_seq, LANES)
    bias_rows = jnp.broadcast_to(b_fgate[:, None, None], (FOX_HEADS, batch * blocks_per_seq, LANES))
    cum = _fox_prep(ft_rows, bias_rows.reshape(ft_rows.shape), blocks_per_seq=blocks_per_seq)
    cum = cum.reshape(FOX_HEADS, batch, 1, seq)
    o, mod_tail = _fox_attn(main, cum, c_pad, w_ada, row(b_ada), 2 * d, batch=batch, seq=seq)
    gate1, shift2, scale2, gate2 = (mod_tail[:batch].reshape(batch, 4, 1, d)[:, i] for i in range(4))

    y, (wa, wb, wo, wgu, wdn) = _rwkv(
        main, small, (mu_r, mu_k, mu_v), mu_s, row(rwkv_w0), w2p, row(rwkv_a0), a2p, g2,
        [row(p) for p in (rwkv_k_k, rwkv_k_a, rwkv_r_k, rwkv_lnx_g, rwkv_lnx_b)],
        [w_branch_a, w_branch_b, w_out, w_ffn_gu, w_ffn_down], batch=batch, seq=seq)

    x1, h2 = _mixout(o, y, main, wa, wb, wo, x2d, gate1, row(ln1_g), row(ln1_b), scale2, shift2, seq=seq)

    out = _ffn(h2, wgu, wdn, x1, gate2, row(ln2_g), row(ln2_b), seq=seq)
    return out.reshape(batch, seq, d)
```

```python
import functools

import jax
import jax.numpy as jnp
from jax import lax
from jax.experimental import pallas as pl
from jax.experimental.pallas import tpu as pltpu

F32 = jnp.float32
BF16 = jnp.bfloat16

LANES = 128
VMEM_CAP_BYTES = 60000 * 1024

FOX_HEADS = 8
FOX_HEAD_DIM = 128
RWKV_HEADS = 16
RWKV_HEAD_DIM = 64
DECAY_LORA = 96
AAA_LORA = 96
GATE_LORA = 256
LORA_PAD = 128
INPROJ_TN = 512
RWKV_LNX_EPS = 64e-5
DEPTH = 1
DEEPNORM_ALPHA = (2.0 * DEPTH) ** 0.25
LN_EPS = 1e-5
RMS_EPS = 1e-6
RWKV_CHUNK = 64
HEADS_PER_TILE = LANES // RWKV_HEAD_DIM


def _vmem_limit(pipelined_bytes, resident_bytes=0, temp_bytes=0):
    need = 2 * pipelined_bytes + resident_bytes + temp_bytes + (2 << 20)
    return int(min(VMEM_CAP_BYTES, need))


def _nbytes(shape, dtype):
    n = 1
    for s in shape:
        n *= s
    return n * jnp.dtype(dtype).itemsize


def _layernorm(x):
    mu = jnp.mean(x, axis=-1, keepdims=True)
    xc = x - mu
    var = jnp.mean(xc * xc, axis=-1, keepdims=True)
    return xc * lax.rsqrt(var + LN_EPS)


def _sigmoid(x):
    return 1.0 / (1.0 + jnp.exp(-x))


def _dot(a, b):
    return jnp.dot(a, b, preferred_element_type=F32)


def _dot_nt(a, b):
    return lax.dot_general(a, b, (((1,), (1,)), ((), ())), preferred_element_type=F32)


def _dot_tn(a, b):
    return lax.dot_general(a, b, (((0,), (0,)), ((), ())), preferred_element_type=F32)


def _split3(x):
    hi = x.astype(BF16)
    r1 = x - hi.astype(F32)
    mid = r1.astype(BF16)
    lo = (r1 - mid.astype(F32)).astype(BF16)
    return hi, mid, lo


def _ada_kernel(c_ref, w_ref, b_ref, o_ref):
    c = c_ref[...]
    s = (c * _sigmoid(c)).astype(BF16)
    o_ref[...] = _dot(s, w_ref[...].astype(BF16)) + b_ref[...]


def _ada(c_pad, w_ada, b_ada, n, *, tn=1024):
    rows, d = c_pad.shape
    return pl.pallas_call(
        _ada_kernel,
        grid=(n // tn,),
        in_specs=[
            pl.BlockSpec((rows, d), lambda j: (0, 0)),
            pl.BlockSpec((d, tn), lambda j: (0, j)),
            pl.BlockSpec((1, tn), lambda j: (0, j)),
        ],
        out_specs=pl.BlockSpec((rows, tn), lambda j: (0, j)),
        out_shape=jax.ShapeDtypeStruct((rows, n), F32),
        compiler_params=pltpu.CompilerParams(
            dimension_semantics=("arbitrary",),
            vmem_limit_bytes=_vmem_limit(_nbytes((d, tn), F32), temp_bytes=_nbytes((d, tn), BF16)),
        ),
        name="ada",
    )(c_pad, w_ada, b_ada)


def _inproj_kernel(x_ref, scale_ref, shift_ref, wt_ref, wft_ref, qg_ref, kg_ref,
                   main_ref, small_ref, ft_ref, h_scr, *, n_q, n_k, n_main):
    j = pl.program_id(1)
    tn = main_ref.shape[1]

    @pl.when(j == 0)
    def _():
        h = _layernorm(x_ref[...]) * (1.0 + scale_ref[0]) + shift_ref[0]
        h_scr[...] = h.astype(BF16)

    def proj():
        w = wt_ref[...].astype(BF16)
        half = h_scr.shape[0] // 2
        return jnp.concatenate([_dot_nt(h_scr[0:half, :], w), _dot_nt(h_scr[half:, :], w)], axis=0)

    @pl.when(j < n_q + n_k)
    def _():
        acc = proj()
        gain = jnp.where(j < n_q, qg_ref[...] * (FOX_HEAD_DIM ** -0.5), kg_ref[...])
        for hh in range(tn // FOX_HEAD_DIM):
            sl = slice(hh * FOX_HEAD_DIM, (hh + 1) * FOX_HEAD_DIM)
            a = acc[:, sl]
            ms = jnp.mean(a * a, axis=-1, keepdims=True)
            main_ref[:, sl] = (a * lax.rsqrt(ms + RMS_EPS) * gain).astype(BF16)

    @pl.when((j >= n_q + n_k) & (j < n_main))
    def _():
        main_ref[...] = proj().astype(BF16)

    @pl.when(j == n_main)
    def _():
        small_ref[...] = proj()
        ft_ref[...] = _dot_nt(wft_ref[...].astype(BF16), h_scr[...])


def _inproj(x2d, scale1, shift1, w_in_t, qg, kg, *, seq, group_rows, lora_row, fgate_row, tm=1024, tn=512):
    t, d = x2d.shape
    fox_w = FOX_HEADS * FOX_HEAD_DIM
    assert all(first % 8 == 0 and count % tn == 0 for first, count in group_rows)
    assert lora_row % 8 == 0 and fgate_row % 8 == 0 and lora_row + tn <= w_in_t.shape[0]
    n_main = sum(count // tn for _, count in group_rows)
    tiles_per_batch = seq // tm
    kern = functools.partial(_inproj_kernel, n_q=fox_w // tn, n_k=fox_w // tn, n_main=n_main)
    pipelined = (_nbytes((tm, d), F32) + _nbytes((tn, d), F32) + _nbytes((tm, tn), BF16)
                 + _nbytes((tm, tn), F32) + _nbytes((8, tm), F32))

    def w_rows(i, j):
        row, hi = jnp.int32(lora_row), n_main
        for first, count in reversed(group_rows):
            lo = hi - count // tn
            row = jnp.where(j < hi, first + tn * (j - lo), row)
            hi = lo
        return pl.multiple_of(row, 8), 0

    return pl.pallas_call(
        kern,
        grid=(t // tm, n_main + 1),
        in_specs=[
            pl.BlockSpec((tm, d), lambda i, j: (jnp.minimum(i + jnp.minimum(j, 1), t // tm - 1), 0)),
            pl.BlockSpec((1, 1, d), lambda i, j: (i // tiles_per_batch, 0, 0)),
            pl.BlockSpec((1, 1, d), lambda i, j: (i // tiles_per_batch, 0, 0)),
            pl.BlockSpec((pl.Element(tn), pl.Element(d)), w_rows),
            pl.BlockSpec((pl.Element(FOX_HEADS), pl.Element(d)), lambda i, j: (fgate_row, 0)),
            pl.BlockSpec((1, FOX_HEAD_DIM), lambda i, j: (0, 0)),
            pl.BlockSpec((1, FOX_HEAD_DIM), lambda i, j: (0, 0)),
        ],
        out_specs=[
            pl.BlockSpec((tm, tn), lambda i, j: (i, jnp.minimum(j, n_main - 1))),
            pl.BlockSpec((tm, tn), lambda i, j: (i, 0)),
            pl.BlockSpec((FOX_HEADS, tm), lambda i, j: (0, i)),
        ],
        out_shape=[
            jax.ShapeDtypeStruct((t, n_main * tn), BF16),
            jax.ShapeDtypeStruct((t, tn), F32),
            jax.ShapeDtypeStruct((FOX_HEADS, t), F32),
        ],
        scratch_shapes=[pltpu.VMEM((tm, d), BF16)],
        compiler_params=pltpu.CompilerParams(
            dimension_semantics=("arbitrary", "arbitrary"),
            vmem_limit_bytes=_vmem_limit(pipelined, _nbytes((tm, d), BF16),
                                         3 * _nbytes((tm, d), F32)),
        ),
        name="inproj",
    )(x2d, scale1, shift1, w_in_t, w_in_t, qg, kg)


def _fox_prep_kernel(ft_ref, bias_ref, o_ref, *, blocks_per_seq):
    x = ft_ref[...] + bias_ref[...]
    lf = jnp.minimum(x, 0.0) - jnp.log1p(jnp.exp(-jnp.abs(x)))
    rows, width = lf.shape
    r = lax.broadcasted_iota(jnp.int32, (width, width), 0)
    c = lax.broadcasted_iota(jnp.int32, (width, width), 1)
    upper = (r <= c).astype(BF16)
    within = sum(_dot(p, upper) for p in _split3(lf))
    tot = jnp.broadcast_to(within[:, width - 1:width], (rows, width))
    rr = lax.broadcasted_iota(jnp.int32, (rows, rows), 0)
    cc = lax.broadcasted_iota(jnp.int32, (rows, rows), 1)
    prior = ((rr // blocks_per_seq == cc // blocks_per_seq) & (cc < rr)).astype(BF16)
    offset = sum(_dot(prior, p) for p in _split3(tot))
    o_ref[...] = within + offset


def _fox_prep(ft_rows, bias_rows, *, blocks_per_seq):
    return pl.pallas_call(
        functools.partial(_fox_prep_kernel, blocks_per_seq=blocks_per_seq),
        out_shape=jax.ShapeDtypeStruct(ft_rows.shape, F32),
        name="fox_prep",
    )(ft_rows, bias_rows)


def _fox_attn_kernel(q_ref, k_ref, v_ref, og_ref, cum_ref, c_ref, wada_ref, bada_ref, o_ref, mod_ref, *, tq):
    _ada_kernel(c_ref, wada_ref, bada_ref, mod_ref)

    seq = q_ref.shape[0]
    n_q = seq // tq
    negcum = -cum_ref[0, 0]
    row = lax.broadcasted_iota(jnp.int32, (tq, tq), 0)
    col = lax.broadcasted_iota(jnp.int32, (tq, tq), 1)
    causal = col <= row

    def logits(qi):
        q0, kend = qi * tq, (qi + 1) * tq
        q = q_ref[q0:kend, :]
        diag = _dot_nt(q, k_ref[q0:kend, :]) + negcum[:, q0:kend]
        diag = jnp.where(causal, diag, -jnp.inf)
        past = _dot_nt(q, k_ref[0:q0, :]) + negcum[:, 0:q0] if qi else None
        return past, diag

    def attend(qi, past, diag):
        q0, kend = qi * tq, (qi + 1) * tq
        m = jnp.max(diag, axis=-1, keepdims=True)
        if past is not None:
            m = jnp.maximum(m, jnp.max(past, axis=-1, keepdims=True))
        p = jnp.exp(diag - m)
        l = jnp.sum(p, axis=-1, keepdims=True)
        o = _dot(p.astype(BF16), v_ref[q0:kend, :])
        if past is not None:
            p = jnp.exp(past - m)
            l = l + jnp.sum(p, axis=-1, keepdims=True)
            o = o + _dot(p.astype(BF16), v_ref[0:q0, :])
        gate = _sigmoid(og_ref[q0:kend, :].astype(F32))
        o_ref[q0:kend, :] = (o / l * gate).astype(BF16)

    nxt = logits(0)
    for qi in range(n_q):
        cur, nxt = nxt, (logits(qi + 1) if qi + 1 < n_q else None)
        attend(qi, *cur)


def _fox_attn(main, cum, c_pad, w_ada, b_ada, ada_col0, *, batch, seq, tq=256):
    t = main.shape[0]
    hd = FOX_HEAD_DIM
    nh = FOX_HEADS
    rows, d = c_pad.shape
    n_ada = w_ada.shape[1] - ada_col0
    tn = n_ada // (batch * nh)
    assert n_ada % (batch * nh) == 0 and tn % LANES == 0 and ada_col0 % tn == 0
    slab = lambda r: pl.BlockSpec((r, tn), lambda b, h: (0, ada_col0 // tn + b * nh + h))
    blk = lambda off: pl.BlockSpec((seq, hd), lambda b, h: (b, off + h))
    pipelined = 5 * _nbytes((seq, hd), BF16) + _nbytes((8, seq), F32) + _nbytes((d, tn), F32)
    return pl.pallas_call(
        functools.partial(_fox_attn_kernel, tq=tq),
        grid=(batch, nh),
        in_specs=[blk(0), blk(nh), blk(2 * nh), blk(3 * nh),
                  pl.BlockSpec((1, 1, 1, seq), lambda b, h: (h, b, 0, 0)),
                  pl.BlockSpec((rows, d), lambda b, h: (0, 0)), slab(d), slab(1)],
        out_specs=[pl.BlockSpec((seq, hd), lambda b, h: (b, h)),
                   pl.BlockSpec((rows, tn), lambda b, h: (0, b * nh + h))],
        out_shape=[jax.ShapeDtypeStruct((t, nh * hd), BF16), jax.ShapeDtypeStruct((rows, n_ada), F32)],
        compiler_params=pltpu.CompilerParams(
            dimension_semantics=("arbitrary", "arbitrary"),
            vmem_limit_bytes=_vmem_limit(pipelined, 0, 6 * _nbytes((tq, seq), F32)),
        ),
        name="fox_attn",
    )(main, main, main, main, cum, c_pad, w_ada, b_ada)


def _run_stages(gen, side=None):
    result = {}
    live = [g for g in (gen, side) if g is not None]
    while live:
        for g in list(live):
            try:
                next(g)
            except StopIteration as stop:
                result[g] = stop.value
                live.remove(g)
    return result[gen if side is None else side]


def _rwkv_inputs(raw_refs, small_ref, mu_refs, mu_s_ref, w0_ref, w2_ref, a0_ref, a2_ref, g2_ref,
                 prev_refs, prev_s, seqs):
    def shift_mix(x, prev_ref, q, mu):
        rows = x.shape[0]
        xp = pltpu.roll(x, 1, axis=0)
        row = lax.broadcasted_iota(jnp.int32, x.shape, 0)
        xp = jnp.where(row == 0, prev_ref[q, 0:1, :], xp)
        prev_ref[q, 0:1, :] = x[rows - 1:rows, :]
        return x + (xp - x) * mu

    r, k, v = ([shift_mix(ref[q].astype(F32), prev, q, mu[...]) for q in seqs]
               for ref, prev, mu in zip(raw_refs, prev_refs, mu_refs))
    sm = jnp.concatenate([shift_mix(small_ref[q], prev_s, q, mu_s_ref[...]) for q in seqs], axis=0)
    xw = sm[:, 0:LORA_PAD]
    xa = sm[:, DECAY_LORA:DECAY_LORA + LORA_PAD]
    xg = sm[:, DECAY_LORA + AAA_LORA:DECAY_LORA + AAA_LORA + GATE_LORA]

    z = w0_ref[...] + _dot(jnp.tanh(xw).astype(BF16), w2_ref[...])
    w_raw = -(jnp.maximum(-z, 0.0) + jnp.log(1.0 + jnp.exp(-jnp.abs(z)))) - 0.5
    lw = -jnp.exp(w_raw)
    a = _sigmoid(a0_ref[...] + _dot(xa.astype(BF16), a2_ref[...]))
    g = _dot(_sigmoid(xg).astype(BF16), g2_ref[...])
    ch = lw.shape[0] // len(seqs)
    per_seq = lambda x: [x[j * ch:(j + 1) * ch, :] for j in range(len(seqs))]
    return r, k, v, per_seq(lw), per_seq(a), per_seq(g)


def _rwkv_kernel(r_ref, k_ref, v_ref, small_ref, mu_r_ref, mu_k_ref, mu_v_ref, mu_s_ref,
                 w0_ref, w2_ref, a0_ref, a2_ref, g2_ref,
                 kk_ref, ka_ref, rk_ref, lng_ref, lnb_ref, *rest, n_tiles, seq_groups, n_cast):
    cast_src, y_ref, cast_dst = rest[:n_cast], rest[n_cast], rest[n_cast + 1:2 * n_cast + 1]
    state, prev_r, prev_k, prev_v, prev_s = rest[2 * n_cast + 1:]
    for src, dst in zip(cast_src, cast_dst):
        dst[...] = src[...].astype(BF16)

    @pl.when(pl.program_id(1) == 0)
    def _():
        for ref in (state, prev_r, prev_k, prev_v, prev_s):
            ref[...] = jnp.zeros_like(ref)

    ch = y_ref.shape[1]
    hd = RWKV_HEAD_DIM
    lane = lax.broadcasted_iota(jnp.int32, (1, LANES), 1)
    m0 = lane < hd
    n2 = HEADS_PER_TILE * ch
    rr = lax.broadcasted_iota(jnp.int32, (n2, n2), 0)
    cc = lax.broadcasted_iota(jnp.int32, (n2, n2), 1)
    same = (rr // ch) == (cc // ch)
    strict = same & (rr > cc)
    incl = same & (rr >= cc)
    lr = lax.broadcasted_iota(jnp.int32, (ch, ch), 0)
    lc = lax.broadcasted_iota(jnp.int32, (ch, ch), 1)
    lower = (lc <= lr).astype(BF16)

    def head_sum(x):
        s0 = jnp.sum(jnp.where(m0, x, 0.0), axis=-1, keepdims=True)
        s1 = jnp.sum(jnp.where(m0, 0.0, x), axis=-1, keepdims=True)
        return jnp.where(m0, s0, s1)

    def stack(x):
        return jnp.concatenate([jnp.where(m0, x, 0.0), jnp.where(m0, 0.0, x)], axis=0)

    sls = [slice(p * LANES, (p + 1) * LANES) for p in range(n_tiles)]

    def each(fn, *cols):
        return [fn(*args) for args in zip(*cols)]

    stack2 = lambda x, y: jnp.concatenate([stack(x), stack(y)], axis=0).astype(BF16)
    incl2 = jnp.concatenate([incl, incl], axis=1)

    def prepare(seqs):
        per_chain = lambda ref: [ref[:, sl] for _ in seqs for sl in sls]
        full = _rwkv_inputs((r_ref, k_ref, v_ref), small_ref, (mu_r_ref, mu_k_ref, mu_v_ref), mu_s_ref,
                            w0_ref, w2_ref, a0_ref, a2_ref, g2_ref, (prev_r, prev_k, prev_v), prev_s, seqs)
        r, k, v, lw, a, g = ([x[j][:, sl] for j in range(len(seqs)) for sl in sls] for x in full)
        yield
        kk = each(lambda k_, w_: k_ * w_, k, per_chain(kk_ref))
        kk = each(lambda x: x * lax.rsqrt(jnp.maximum(head_sum(x * x), 1e-24)), kk)
        kr = each(lambda k_, a_, w_: k_ * (1.0 + (a_ - 1.0) * w_), k, a, per_chain(ka_ref))
        yield
        cum = each(lambda x: sum(_dot(lower, part) for part in _split3(x)[:2]), lw)
        yield
        gam = each(jnp.exp, cum)
        ginv = each(lambda x: jnp.exp(-x), cum)
        g_end = each(lambda x: x[ch - 1:ch, :], gam)
        yield
        a_t = each(lambda kk_, c_, lw_: -kk_ * jnp.exp(c_ - lw_), kk, cum, lw)
        r_t = each(lambda r_, g_: r_ * g_, r, gam)
        lhs = each(stack2, a_t, r_t)
        yield
        b_h = each(lambda kk_, a_, gi_: kk_ * a_ * gi_, kk, a, ginv)
        k_h = each(lambda kr_, gi_: kr_ * gi_, kr, ginv)
        rhs = each(stack2, b_h, k_h)
        yield
        vs = each(lambda v_: stack(v_).astype(BF16), v)
        bk = each(lambda b_, k_, ge_: stack2(b_ * ge_, k_ * ge_), b_h, k_h, g_end)
        yield
        bonus = each(lambda r_, kr_, w_, v_: head_sum(r_ * kr_ * w_) * v_, r, kr, per_chain(rk_ref), v)
        return lhs, rhs, vs, bk, g_end, bonus, g

    def advance(seqs, prepared):
        lhs, rhs, vs, bk, g_end, bonus, g = prepared
        per_chain = lambda ref: [ref[:, sl] for _ in seqs for sl in sls]
        slots = [q * n_tiles + p for q in seqs for p in range(n_tiles)]
        sc = each(_dot_nt, lhs, rhs)
        yield
        ab = each(lambda s: jnp.where(strict, s[0:n2, 0:n2], 0.0).astype(BF16), sc)
        ak = each(lambda s: jnp.where(strict, s[0:n2, n2:2 * n2], 0.0).astype(BF16), sc)
        rbk = each(lambda s: jnp.where(incl2, s[n2:2 * n2, :], 0.0).astype(BF16), sc)
        yield
        st = [state[i] for i in slots]
        ah = each(lambda l_, s_: _dot_nt(l_, s_.astype(BF16)), lhs, st)
        yield
        x = each(lambda ah_, ak_, vs_: ah_[0:n2] + _dot(ak_, vs_), ah, ak, vs)
        yield
        pw = ab
        n_lvl = ch.bit_length() - 1
        for lvl in range(n_lvl):
            x = each(lambda x_, p_: x_ + _dot(p_, x_.astype(BF16)), x, pw)
            yield
            if lvl + 1 < n_lvl:
                pw = each(lambda p_: _dot(p_, p_).astype(BF16), pw)
                yield
        uv = each(lambda x_, vs_: jnp.concatenate([x_.astype(BF16), vs_], axis=0), x, vs)
        ys = each(lambda ah_, m_, uv_: ah_[n2:2 * n2] + _dot(m_, uv_), ah, rbk, uv)
        yield
        y = each(lambda ys_: ys_[0:ch] + ys_[ch:n2], ys)
        new_st = each(lambda s_, ge_, uv_, bk_: s_ * ge_ + _dot_tn(uv_, bk_), st, g_end, uv, bk)
        for i, s_new in zip(slots, new_st):
            state[i] = s_new
        yield
        mu = each(lambda y_: head_sum(y_) * (1.0 / hd), y)
        yc = each(lambda y_, m_: y_ - m_, y, mu)
        var = each(lambda c_: head_sum(c_ * c_) * (1.0 / hd), yc)
        yield
        yn = each(lambda c_, v_, g_, b_: c_ * lax.rsqrt(v_ + RWKV_LNX_EPS) * g_ + b_,
                  yc, var, per_chain(lng_ref), per_chain(lnb_ref))
        for j, i in enumerate(slots):
            y_ref[i // n_tiles, :, sls[i % n_tiles]] = ((yn[j] + bonus[j]) * g[j]).astype(BF16)

    starts = [sum(seq_groups[:i]) for i in range(len(seq_groups))]
    groups = [tuple(range(s, s + n)) for s, n in zip(starts, seq_groups)]
    prepared = _run_stages(prepare(groups[0]))
    for gi, seqs in enumerate(groups):
        nxt = prepare(groups[gi + 1]) if gi + 1 < len(groups) else None
        prepared = _run_stages(advance(seqs, prepared), nxt)


def _slab_spec(w, n_steps, step_of):
    rep = 1
    while w.shape[0] % (n_steps // rep * 16):
        rep *= 2
        assert rep <= n_steps
    return pl.BlockSpec((w.shape[0] // (n_steps // rep), w.shape[1]),
                        lambda *idx: (step_of(*idx) // rep, 0))


def _rwkv(main, small, mus, mu_s, w0, w2p, a0, a2p, g2, head_rows, cast_weights, *, batch, seq,
          seq_groups=(2, 2)):
    n_seq = sum(seq_groups)
    t = main.shape[0]
    width = RWKV_HEADS * RWKV_HEAD_DIM
    nsm = small.shape[1]
    col0 = (4 * FOX_HEADS * FOX_HEAD_DIM) // width
    ch = RWKV_CHUNK
    steps = seq // ch
    n_tiles = width // LANES
    assert batch % n_seq == 0
    groups = batch // n_seq
    grouped = lambda x: x.reshape(groups, n_seq, seq, x.shape[-1])
    act = lambda cols, blk: pl.BlockSpec((None, n_seq, ch, cols), lambda b, c: (b, 0, c, blk))
    row = lambda n: pl.BlockSpec((1, n), lambda b, c: (0, 0))
    whole = lambda w: pl.BlockSpec(w.shape, lambda b, c: (0, 0))
    slabs = [_slab_spec(w, groups * steps, lambda b, c: b * steps + c) for w in cast_weights]
    pipelined = n_seq * (4 * _nbytes((ch, width), BF16) + _nbytes((ch, nsm), F32))
    pipelined += sum(_nbytes(s.block_shape, F32) + _nbytes(s.block_shape, BF16) for s in slabs)
    pipelined += sum(_nbytes(w.shape, BF16) for w in (w2p, a2p, g2))
    scratch = [(n_seq * n_tiles, LANES, LANES)] + [(n_seq, 8, width)] * 3 + [(n_seq, 8, nsm)]
    y, *copies = pl.pallas_call(
        functools.partial(_rwkv_kernel, n_tiles=n_tiles, seq_groups=seq_groups, n_cast=len(slabs)),
        grid=(groups, steps),
        in_specs=[act(width, col0), act(width, col0 + 1), act(width, col0 + 2), act(nsm, 0),
                  row(width), row(width), row(width), row(nsm),
                  row(width), whole(w2p), row(width), whole(a2p), whole(g2)]
        + [row(width)] * 5 + slabs,
        out_specs=[act(width, 0)] + slabs,
        out_shape=[jax.ShapeDtypeStruct((groups, n_seq, seq, width), BF16)]
        + [jax.ShapeDtypeStruct(w.shape, BF16) for w in cast_weights],
        scratch_shapes=[pltpu.VMEM(s, F32) for s in scratch],
        compiler_params=pltpu.CompilerParams(
            dimension_semantics=("arbitrary", "arbitrary"),
            vmem_limit_bytes=_vmem_limit(pipelined, sum(_nbytes(s, F32) for s in scratch), 24 << 20),
        ),
        name="rwkv",
    )(grouped(main), grouped(main), grouped(main), grouped(small), *mus, mu_s, w0, w2p, a0, a2p, g2,
      *head_rows, *cast_weights)
    return y.reshape(t, width), copies


def _mixout_kernel(o_ref, y_ref, ga0_ref, ga1_ref, gb0_ref, gb1_ref, wa_ref, wb_ref, wo_ref, x_ref,
                   gate_ref, lng_ref, lnb_ref, scale_ref, shift_ref, x1_ref, h2_ref, *, n_sub):
    tm, d = x_ref.shape
    rows = tm // n_sub
    half = d // 2
    sl = [slice(s * rows, (s + 1) * rows) for s in range(n_sub)]

    def branches(s):
        return _dot(o_ref[sl[s], :], wa_ref[...]), _dot(y_ref[sl[s], :], wb_ref[...])

    def mixed(s, ua, ub):
        parts = []
        for c, (ga, gb) in enumerate(((ga0_ref, gb0_ref), (ga1_ref, gb1_ref))):
            cs = slice(c * half, (c + 1) * half)
            parts.append(_sigmoid(ga[sl[s], :].astype(F32)) * ua[:, cs]
                         + _sigmoid(gb[sl[s], :].astype(F32)) * ub[:, cs])
        return _dot(jnp.concatenate(parts, axis=1).astype(BF16), wo_ref[...])

    def norms(s, mix):
        z = DEEPNORM_ALPHA * x_ref[sl[s], :] + gate_ref[0] * mix
        x1 = _layernorm(z) * lng_ref[...] + lnb_ref[...]
        x1_ref[sl[s], :] = x1
        h2_ref[sl[s], :] = (_layernorm(x1) * (1.0 + scale_ref[0]) + shift_ref[0]).astype(BF16)

    u = branches(0)
    mix_prev = None
    for s in range(n_sub):
        u_next = branches(s + 1) if s + 1 < n_sub else None
        mix = mixed(s, *u)
        if mix_prev is not None:
            norms(s - 1, mix_prev)
        u, mix_prev = u_next, mix
    norms(n_sub - 1, mix_prev)


def _mixout(o, y, main, wa, wb, wo, x2d, gate1, ln_g, ln_b, scale2, shift2, *, seq, tm=256, n_sub=2):
    t, d = x2d.shape
    kdim = o.shape[1]
    half = d // 2
    gate_blk0 = (main.shape[1] - 2 * d) // half
    tiles_per_batch = seq // tm
    mod = pl.BlockSpec((1, 1, d), lambda i: (i // tiles_per_batch, 0, 0))
    row = pl.BlockSpec((1, d), lambda i: (0, 0))
    gate = lambda c: pl.BlockSpec((tm, half), lambda i: (i, gate_blk0 + c))
    resident = lambda w: pl.BlockSpec(w.shape, lambda i: (0, 0), pipeline_mode=pl.Buffered(1))
    pipelined = (2 * _nbytes((tm, kdim), BF16) + 4 * _nbytes((tm, half), BF16)
                 + 2 * _nbytes((tm, d), F32) + _nbytes((tm, d), BF16))
    weights = _nbytes(wa.shape, BF16) + _nbytes(wb.shape, BF16) + _nbytes(wo.shape, BF16)
    return pl.pallas_call(
        functools.partial(_mixout_kernel, n_sub=n_sub),
        grid=(t // tm,),
        in_specs=[
            pl.BlockSpec((tm, kdim), lambda i: (i, 0)),
            pl.BlockSpec((tm, kdim), lambda i: (i, 0)),
            gate(0), gate(1), gate(2), gate(3),
            resident(wa), resident(wb), resident(wo),
            pl.BlockSpec((tm, d), lambda i: (i, 0)),
            mod, row, row, mod, mod,
        ],
        out_specs=[pl.BlockSpec((tm, d), lambda i: (i, 0)),
                   pl.BlockSpec((tm, d), lambda i: (i, 0))],
        out_shape=[jax.ShapeDtypeStruct((t, d), F32), jax.ShapeDtypeStruct((t, d), BF16)],
        compiler_params=pltpu.CompilerParams(
            dimension_semantics=("arbitrary",),
            vmem_limit_bytes=_vmem_limit(pipelined, weights, 8 * _nbytes((tm, d), F32)),
        ),
        name="mixout",
    )(o, y, main, main, main, main, wa, wb, wo, x2d, gate1, ln_g, ln_b, scale2, shift2)


def _ffn_kernel(h_ref, wg_ref, wu_ref, wd_ref, x1_ref, gate_ref, lng_ref, lnb_ref, o_ref, acc):
    f = pl.program_id(1)
    @pl.when(f == 0)
    def _():
        acc[...] = jnp.zeros_like(acc)

    h = h_ref[...]
    g = _dot(h, wg_ref[...])
    u = _dot(h, wu_ref[...])
    act = (g * _sigmoid(g) * u).astype(BF16)
    acc[...] += _dot(act, wd_ref[...])

    @pl.when(f == pl.num_programs(1) - 1)
    def _():
        z = DEEPNORM_ALPHA * x1_ref[...] + gate_ref[0] * acc[...]
        o_ref[...] = _layernorm(z) * lng_ref[...] + lnb_ref[...]


def _ffn(h2, w_gu, w_down, x1, gate2, ln_g, ln_b, *, seq, tm=512, tf=512):
    t, d = x1.shape
    d_ff = w_down.shape[0]
    nf = d_ff // tf
    tiles_per_batch = seq // tm
    mod = pl.BlockSpec((1, 1, d), lambda i, f: (i // tiles_per_batch, 0, 0))
    row = pl.BlockSpec((1, d), lambda i, f: (0, 0))
    pipelined = (_nbytes((tm, d), BF16) + 3 * _nbytes((d, tf), BF16) + 2 * _nbytes((tm, d), F32))
    return pl.pallas_call(
        _ffn_kernel,
        grid=(t // tm, nf),
        in_specs=[
            pl.BlockSpec((tm, d), lambda i, f: (i, 0)),
            pl.BlockSpec((d, tf), lambda i, f: (0, f)),
            pl.BlockSpec((d, tf), lambda i, f: (0, nf + f)),
            pl.BlockSpec((tf, d), lambda i, f: (f, 0)),
            pl.BlockSpec((tm, d), lambda i, f: (i, 0)),
            mod, row, row,
        ],
        out_specs=pl.BlockSpec((tm, d), lambda i, f: (i, 0)),
        out_shape=jax.ShapeDtypeStruct((t, d), F32),
        scratch_shapes=[pltpu.VMEM((tm, d), F32)],
        compiler_params=pltpu.CompilerParams(
            dimension_semantics=("arbitrary", "arbitrary"),
            vmem_limit_bytes=_vmem_limit(pipelined, _nbytes((tm, d), F32),
                                         2 * _nbytes((tm, d), F32) + 4 * _nbytes((tm, tf), F32)),
        ),
        name="ffn",
    )(h2, w_gu, w_gu, w_down, x1, gate2, ln_g, ln_b)


def _pad_rows(w, rows):
    return jnp.zeros((rows, w.shape[1]), w.dtype).at[:w.shape[0]].set(w)


def kernel(x, c, w_ada, b_ada, w_in, b_fgate, q_norm_g, k_norm_g, rwkv_mu, rwkv_w0, rwkv_w2, rwkv_a0, rwkv_a2, rwkv_g2, rwkv_k_k, rwkv_k_a, rwkv_r_k, rwkv_lnx_g, rwkv_lnx_b, w_branch_a, w_branch_b, w_out, ln1_g, ln1_b, w_ffn_gu, w_ffn_down, ln2_g, ln2_b):
    batch, seq, d = x.shape
    t = batch * seq
    fox_w = FOX_HEADS * FOX_HEAD_DIM
    rw_w = RWKV_HEADS * RWKV_HEAD_DIM
    fox_cols = 4 * fox_w + FOX_HEADS
    lora0 = fox_cols + 3 * rw_w
    gate0 = lora0 + DECAY_LORA + AAA_LORA + GATE_LORA
    assert seq % RWKV_CHUNK == 0 and w_in.shape[1] == gate0 + 2 * d

    lora_w = gate0 - lora0
    mu_r = rwkv_mu[None, 0:rw_w]
    mu_k = rwkv_mu[None, rw_w:2 * rw_w]
    mu_v = rwkv_mu[None, 2 * rw_w:3 * rw_w]
    mu_s = jnp.zeros((1, INPROJ_TN), rwkv_mu.dtype).at[0, :lora_w].set(rwkv_mu[3 * rw_w:])
    w2p = _pad_rows(rwkv_w2, LORA_PAD).astype(BF16)
    a2p = _pad_rows(rwkv_a2, LORA_PAD).astype(BF16)
    g2 = rwkv_g2.astype(BF16)
    row = lambda v: v.reshape(1, -1)

    c_pad = jnp.zeros((8, d), c.dtype).at[:batch].set(c)
    mod_head = _ada(c_pad, w_ada, row(b_ada), 2 * d)[:batch].reshape(batch, 2, 1, d)
    shift1, scale1 = mod_head[:, 0], mod_head[:, 1]

    x2d = x.reshape(t, d)
    main, small, ft = _inproj(x2d, scale1, shift1, w_in.T, row(q_norm_g), row(k_norm_g), seq=seq,
                              group_rows=[(0, 4 * fox_w), (fox_cols, 3 * rw_w), (gate0, 2 * d)],
                              lora_row=lora0, fgate_row=4 * fox_w, tn=INPROJ_TN)

    blocks_per_seq = seq // LANES
    ft_rows = ft.reshape(FOX_HEADS * batch * blocks_per_seq, LANES)
    bias_rows = jnp.broadcast_to(b_fgate[:, None, None], (FOX_HEADS, batch * blocks_per_seq, LANES))
    cum = _fox_prep(ft_rows, bias_rows.reshape(ft_rows.shape), blocks_per_seq=blocks_per_seq)
    cum = cum.reshape(FOX_HEADS, batch, 1, seq)
    o, mod_tail = _fox_attn(main, cum, c_pad, w_ada, row(b_ada), 2 * d, batch=batch, seq=seq)
    gate1, shift2, scale2, gate2 = (mod_tail[:batch].reshape(batch, 4, 1, d)[:, i] for i in range(4))

    y, (wa, wb, wo, wgu, wdn) = _rwkv(
        main, small, (mu_r, mu_k, mu_v), mu_s, row(rwkv_w0), w2p, row(rwkv_a0), a2p, g2,
        [row(p) for p in (rwkv_k_k, rwkv_k_a, rwkv_r_k, rwkv_lnx_g, rwkv_lnx_b)],
        [w_branch_a, w_branch_b, w_out, w_ffn_gu, w_ffn_down], batch=batch, seq=seq)

    x1, h2 = _mixout(o, y, main, wa, wb, wo, x2d, gate1, row(ln1_g), row(ln1_b), scale2, shift2, seq=seq)

    out = _ffn(h2, wgu, wdn, x1, gate2, row(ln2_g), row(ln2_b), seq=seq)
    return out.reshape(batch, seq, d)
```

```python
import functools

import jax
import jax.numpy as jnp
from jax import lax
from jax.experimental import pallas as pl
from jax.experimental.pallas import tpu as pltpu

F32 = jnp.float32
BF16 = jnp.bfloat16

LANES = 128
VMEM_CAP_BYTES = 60000 * 1024

FOX_HEADS = 8
FOX_HEAD_DIM = 128
RWKV_HEADS = 16
RWKV_HEAD_DIM = 64
DECAY_LORA = 96
AAA_LORA = 96
GATE_LORA = 256
LORA_PAD = 128
INPROJ_TN = 512
RWKV_LNX_EPS = 64e-5
DEPTH = 1
DEEPNORM_ALPHA = (2.0 * DEPTH) ** 0.25
LN_EPS = 1e-5
RMS_EPS = 1e-6
RWKV_CHUNK = 64
HEADS_PER_TILE = LANES // RWKV_HEAD_DIM


def _vmem_limit(pipelined_bytes, resident_bytes=0, temp_bytes=0):
    need = 2 * pipelined_bytes + resident_bytes + temp_bytes + (2 << 20)
    return int(min(VMEM_CAP_BYTES, need))


def _nbytes(shape, dtype):
    n = 1
    for s in shape:
        n *= s
    return n * jnp.dtype(dtype).itemsize


def _layernorm(x):
    mu = jnp.mean(x, axis=-1, keepdims=True)
    xc = x - mu
    var = jnp.mean(xc * xc, axis=-1, keepdims=True)
    return xc * lax.rsqrt(var + LN_EPS)


def _sigmoid(x):
    return 1.0 / (1.0 + jnp.exp(-x))


def _dot(a, b):
    return jnp.dot(a, b, preferred_element_type=F32)


def _dot_nt(a, b):
    return lax.dot_general(a, b, (((1,), (1,)), ((), ())), preferred_element_type=F32)


def _dot_tn(a, b):
    return lax.dot_general(a, b, (((0,), (0,)), ((), ())), preferred_element_type=F32)


def _split3(x):
    hi = x.astype(BF16)
    r1 = x - hi.astype(F32)
    mid = r1.astype(BF16)
    lo = (r1 - mid.astype(F32)).astype(BF16)
    return hi, mid, lo


def _ada_kernel(c_ref, w_ref, b_ref, o_ref):
    c = c_ref[...]
    s = (c * _sigmoid(c)).astype(BF16)
    o_ref[...] = _dot(s, w_ref[...].astype(BF16)) + b_ref[...]


def _ada(c_pad, w_ada, b_ada, n, *, tn=1024):
    rows, d = c_pad.shape
    return pl.pallas_call(
        _ada_kernel,
        grid=(n // tn,),
        in_specs=[
            pl.BlockSpec((rows, d), lambda j: (0, 0)),
            pl.BlockSpec((d, tn), lambda j: (0, j)),
            pl.BlockSpec((1, tn), lambda j: (0, j)),
        ],
        out_specs=pl.BlockSpec((rows, tn), lambda j: (0, j)),
        out_shape=jax.ShapeDtypeStruct((rows, n), F32),
        compiler_params=pltpu.CompilerParams(
            dimension_semantics=("arbitrary",),
            vmem_limit_bytes=_vmem_limit(_nbytes((d, tn), F32), temp_bytes=_nbytes((d, tn), BF16)),
        ),
        name="ada",
    )(c_pad, w_ada, b_ada)


def _inproj_kernel(x_ref, scale_ref, shift_ref, wt_ref, wft_ref, qg_ref, kg_ref,
                   main_ref, small_ref, ft_ref, h_scr, *, n_q, n_k, n_main):
    j = pl.program_id(1)
    tn = main_ref.shape[1]

    @pl.when(j == 0)
    def _():
        h = _layernorm(x_ref[...]) * (1.0 + scale_ref[0]) + shift_ref[0]
        h_scr[...] = h.astype(BF16)

    def proj():
        w = wt_ref[...].astype(BF16)
        half = h_scr.shape[0] // 2
        return jnp.concatenate([_dot_nt(h_scr[0:half, :], w), _dot_nt(h_scr[half:, :], w)], axis=0)

    @pl.when(j < n_q + n_k)
    def _():
        acc = proj()
        gain = jnp.where(j < n_q, qg_ref[...] * (FOX_HEAD_DIM ** -0.5), kg_ref[...])
        for hh in range(tn // FOX_HEAD_DIM):
            sl = slice(hh * FOX_HEAD_DIM, (hh + 1) * FOX_HEAD_DIM)
            a = acc[:, sl]
            ms = jnp.mean(a * a, axis=-1, keepdims=True)
            main_ref[:, sl] = (a * lax.rsqrt(ms + RMS_EPS) * gain).astype(BF16)

    @pl.when((j >= n_q + n_k) & (j < n_main))
    def _():
        main_ref[...] = proj().astype(BF16)

    @pl.when(j == n_main)
    def _():
        small_ref[...] = proj()
        ft_ref[...] = _dot_nt(wft_ref[...].astype(BF16), h_scr[...])


def _inproj(x2d, scale1, shift1, w_in_t, qg, kg, *, seq, group_rows, lora_row, fgate_row, tm=1024, tn=512):
    t, d = x2d.shape
    fox_w = FOX_HEADS * FOX_HEAD_DIM
    assert all(first % 8 == 0 and count % tn == 0 for first, count in group_rows)
    assert lora_row % 8 == 0 and fgate_row % 8 == 0 and lora_row + tn <= w_in_t.shape[0]
    n_main = sum(count // tn for _, count in group_rows)
    tiles_per_batch = seq // tm
    kern = functools.partial(_inproj_kernel, n_q=fox_w // tn, n_k=fox_w // tn, n_main=n_main)
    pipelined = (_nbytes((tm, d), F32) + _nbytes((tn, d), F32) + _nbytes((tm, tn), BF16)
                 + _nbytes((tm, tn), F32) + _nbytes((8, tm), F32))

    def w_rows(i, j):
        row, hi = jnp.int32(lora_row), n_main
        for first, count in reversed(group_rows):
            lo = hi - count // tn
            row = jnp.where(j < hi, first + tn * (j - lo), row)
            hi = lo
        return pl.multiple_of(row, 8), 0

    return pl.pallas_call(
        kern,
        grid=(t // tm, n_main + 1),
        in_specs=[
            pl.BlockSpec((tm, d), lambda i, j: (jnp.minimum(i + jnp.minimum(j, 1), t // tm - 1), 0)),
            pl.BlockSpec((1, 1, d), lambda i, j: (i // tiles_per_batch, 0, 0)),
            pl.BlockSpec((1, 1, d), lambda i, j: (i // tiles_per_batch, 0, 0)),
            pl.BlockSpec((pl.Element(tn), pl.Element(d)), w_rows),
            pl.BlockSpec((pl.Element(FOX_HEADS), pl.Element(d)), lambda i, j: (fgate_row, 0)),
            pl.BlockSpec((1, FOX_HEAD_DIM), lambda i, j: (0, 0)),
            pl.BlockSpec((1, FOX_HEAD_DIM), lambda i, j: (0, 0)),
        ],
        out_specs=[
            pl.BlockSpec((tm, tn), lambda i, j: (i, jnp.minimum(j, n_main - 1))),
            pl.BlockSpec((tm, tn), lambda i, j: (i, 0)),
            pl.BlockSpec((FOX_HEADS, tm), lambda i, j: (0, i)),
        ],
        out_shape=[
            jax.ShapeDtypeStruct((t, n_main * tn), BF16),
            jax.ShapeDtypeStruct((t, tn), F32),
            jax.ShapeDtypeStruct((FOX_HEADS, t), F32),
        ],
        scratch_shapes=[pltpu.VMEM((tm, d), BF16)],
        compiler_params=pltpu.CompilerParams(
            dimension_semantics=("arbitrary", "arbitrary"),
            vmem_limit_bytes=_vmem_limit(pipelined, _nbytes((tm, d), BF16),
                                         3 * _nbytes((tm, d), F32)),
        ),
        name="inproj",
    )(x2d, scale1, shift1, w_in_t, w_in_t, qg, kg)


def _fox_prep_kernel(ft_ref, bias_ref, o_ref, *, blocks_per_seq):
    x = ft_ref[...] + bias_ref[...]
    lf = jnp.minimum(x, 0.0) - jnp.log1p(jnp.exp(-jnp.abs(x)))
    rows, width = lf.shape
    r = lax.broadcasted_iota(jnp.int32, (width, width), 0)
    c = lax.broadcasted_iota(jnp.int32, (width, width), 1)
    upper = (r <= c).astype(BF16)
    within = sum(_dot(p, upper) for p in _split3(lf))
    tot = jnp.broadcast_to(within[:, width - 1:width], (rows, width))
    rr = lax.broadcasted_iota(jnp.int32, (rows, rows), 0)
    cc = lax.broadcasted_iota(jnp.int32, (rows, rows), 1)
    prior = ((rr // blocks_per_seq == cc // blocks_per_seq) & (cc < rr)).astype(BF16)
    offset = sum(_dot(prior, p) for p in _split3(tot))
    o_ref[...] = within + offset


def _fox_prep(ft_rows, bias_rows, *, blocks_per_seq):
    return pl.pallas_call(
        functools.partial(_fox_prep_kernel, blocks_per_seq=blocks_per_seq),
        out_shape=jax.ShapeDtypeStruct(ft_rows.shape, F32),
        name="fox_prep",
    )(ft_rows, bias_rows)


def _fox_attn_kernel(q_ref, k_ref, v_ref, og_ref, cum_ref, c_ref, wada_ref, bada_ref, o_ref, mod_ref, *, tq):
    _ada_kernel(c_ref, wada_ref, bada_ref, mod_ref)

    seq = q_ref.shape[0]
    n_q = seq // tq
    negcum = -cum_ref[0, 0]
    row = lax.broadcasted_iota(jnp.int32, (tq, tq), 0)
    col = lax.broadcasted_iota(jnp.int32, (tq, tq), 1)
    causal = col <= row

    def logits(qi):
        q0, kend = qi * tq, (qi + 1) * tq
        q = q_ref[q0:kend, :]
        diag = _dot_nt(q, k_ref[q0:kend, :]) + negcum[:, q0:kend]
        diag = jnp.where(causal, diag, -jnp.inf)
        past = _dot_nt(q, k_ref[0:q0, :]) + negcum[:, 0:q0] if qi else None
        return past, diag

    def attend(qi, past, diag):
        q0, kend = qi * tq, (qi + 1) * tq
        m = jnp.max(diag, axis=-1, keepdims=True)
        if past is not None:
            m = jnp.maximum(m, jnp.max(past, axis=-1, keepdims=True))
        p = jnp.exp(diag - m)
        l = jnp.sum(p, axis=-1, keepdims=True)
        o = _dot(p.astype(BF16), v_ref[q0:kend, :])
        if past is not None:
            p = jnp.exp(past - m)
            l = l + jnp.sum(p, axis=-1, keepdims=True)
            o = o + _dot(p.astype(BF16), v_ref[0:q0, :])
        gate = _sigmoid(og_ref[q0:kend, :].astype(F32))
        o_ref[q0:kend, :] = (o / l * gate).astype(BF16)

    nxt = logits(0)
    for qi in range(n_q):
        cur, nxt = nxt, (logits(qi + 1) if qi + 1 < n_q else None)
        attend(qi, *cur)


def _fox_attn(main, cum, c_pad, w_ada, b_ada, ada_col0, *, batch, seq, tq=256):
    t = main.shape[0]
    hd = FOX_HEAD_DIM
    nh = FOX_HEADS
    rows, d = c_pad.shape
    n_ada = w_ada.shape[1] - ada_col0
    tn = n_ada // (batch * nh)
    assert n_ada % (batch * nh) == 0 and tn % LANES == 0 and ada_col0 % tn == 0
    slab = lambda r: pl.BlockSpec((r, tn), lambda b, h: (0, ada_col0 // tn + b * nh + h))
    blk = lambda off: pl.BlockSpec((seq, hd), lambda b, h: (b, off + h))
    pipelined = 5 * _nbytes((seq, hd), BF16) + _nbytes((8, seq), F32) + _nbytes((d, tn), F32)
    return pl.pallas_call(
        functools.partial(_fox_attn_kernel, tq=tq),
        grid=(batch, nh),
        in_specs=[blk(0), blk(nh), blk(2 * nh), blk(3 * nh),
                  pl.BlockSpec((1, 1, 1, seq), lambda b, h: (h, b, 0, 0)),
                  pl.BlockSpec((rows, d), lambda b, h: (0, 0)), slab(d), slab(1)],
        out_specs=[pl.BlockSpec((seq, hd), lambda b, h: (b, h)),
                   pl.BlockSpec((rows, tn), lambda b, h: (0, b * nh + h))],
        out_shape=[jax.ShapeDtypeStruct((t, nh * hd), BF16), jax.ShapeDtypeStruct((rows, n_ada), F32)],
        compiler_params=pltpu.CompilerParams(
            dimension_semantics=("arbitrary", "arbitrary"),
            vmem_limit_bytes=_vmem_limit(pipelined, 0, 6 * _nbytes((tq, seq), F32)),
        ),
        name="fox_attn",
    )(main, main, main, main, cum, c_pad, w_ada, b_ada)


def _run_stages(gen, side=None):
    result = {}
    live = [g for g in (gen, side) if g is not None]
    while live:
        for g in list(live):
            try:
                next(g)
            except StopIteration as stop:
                result[g] = stop.value
                live.remove(g)
    return result[gen if side is None else side]


def _rwkv_inputs(raw_refs, small_ref, mu_refs, mu_s_ref, w0_ref, w2_ref, a0_ref, a2_ref, g2_ref,
                 prev_refs, prev_s, seqs):
    def shift_mix(x, prev_ref, q, mu):
        rows = x.shape[0]
        xp = pltpu.roll(x, 1, axis=0)
        row = lax.broadcasted_iota(jnp.int32, x.shape, 0)
        xp = jnp.where(row == 0, prev_ref[q, 0:1, :], xp)
        prev_ref[q, 0:1, :] = x[rows - 1:rows, :]
        return x + (xp - x) * mu

    r, k, v = ([shift_mix(ref[q].astype(F32), prev, q, mu[...]) for q in seqs]
               for ref, prev, mu in zip(raw_refs, prev_refs, mu_refs))
    sm = jnp.concatenate([shift_mix(small_ref[q], prev_s, q, mu_s_ref[...]) for q in seqs], axis=0)
    xw = sm[:, 0:LORA_PAD]
    xa = sm[:, DECAY_LORA:DECAY_LORA + LORA_PAD]
    xg = sm[:, DECAY_LORA + AAA_LORA:DECAY_LORA + AAA_LORA + GATE_LORA]

    z = w0_ref[...] + _dot(jnp.tanh(xw).astype(BF16), w2_ref[...])
    w_raw = -(jnp.maximum(-z, 0.0) + jnp.log(1.0 + jnp.exp(-jnp.abs(z)))) - 0.5
    lw = -jnp.exp(w_raw)
    a = _sigmoid(a0_ref[...] + _dot(xa.astype(BF16), a2_ref[...]))
    g = _dot(_sigmoid(xg).astype(BF16), g2_ref[...])
    ch = lw.shape[0] // len(seqs)
    per_seq = lambda x: [x[j * ch:(j + 1) * ch, :] for j in range(len(seqs))]
    return r, k, v, per_seq(lw), per_seq(a), per_seq(g)


def _rwkv_kernel(r_ref, k_ref, v_ref, small_ref, mu_r_ref, mu_k_ref, mu_v_ref, mu_s_ref,
                 w0_ref, w2_ref, a0_ref, a2_ref, g2_ref,
                 kk_ref, ka_ref, rk_ref, lng_ref, lnb_ref, *rest, n_tiles, seq_groups, n_cast):
    cast_src, y_ref, cast_dst = rest[:n_cast], rest[n_cast], rest[n_cast + 1:2 * n_cast + 1]
    state, prev_r, prev_k, prev_v, prev_s = rest[2 * n_cast + 1:]
    for src, dst in zip(cast_src, cast_dst):
        dst[...] = src[...].astype(BF16)

    @pl.when(pl.program_id(1) == 0)
    def _():
        for ref in (state, prev_r, prev_k, prev_v, prev_s):
            ref[...] = jnp.zeros_like(ref)

    ch = y_ref.shape[1]
    hd = RWKV_HEAD_DIM
    lane = lax.broadcasted_iota(jnp.int32, (1, LANES), 1)
    m0 = lane < hd
    n2 = HEADS_PER_TILE * ch
    rr = lax.broadcasted_iota(jnp.int32, (n2, n2), 0)
    cc = lax.broadcasted_iota(jnp.int32, (n2, n2), 1)
    same = (rr // ch) == (cc // ch)
    strict = same & (rr > cc)
    incl = same & (rr >= cc)
    lr = lax.broadcasted_iota(jnp.int32, (ch, ch), 0)
    lc = lax.broadcasted_iota(jnp.int32, (ch, ch), 1)
    lower = (lc <= lr).astype(BF16)

    def head_sum(x):
        s0 = jnp.sum(jnp.where(m0, x, 0.0), axis=-1, keepdims=True)
        s1 = jnp.sum(jnp.where(m0, 0.0, x), axis=-1, keepdims=True)
        return jnp.where(m0, s0, s1)

    def stack(x):
        return jnp.concatenate([jnp.where(m0, x, 0.0), jnp.where(m0, 0.0, x)], axis=0)

    sls = [slice(p * LANES, (p + 1) * LANES) for p in range(n_tiles)]

    def each(fn, *cols):
        return [fn(*args) for args in zip(*cols)]

    stack2 = lambda x, y: jnp.concatenate([stack(x), stack(y)], axis=0).astype(BF16)
    incl2 = jnp.concatenate([incl, incl], axis=1)

    def prepare(seqs):
        per_chain = lambda ref: [ref[:, sl] for _ in seqs for sl in sls]
        full = _rwkv_inputs((r_ref, k_ref, v_ref), small_ref, (mu_r_ref, mu_k_ref, mu_v_ref), mu_s_ref,
                            w0_ref, w2_ref, a0_ref, a2_ref, g2_ref, (prev_r, prev_k, prev_v), prev_s, seqs)
        r, k, v, lw, a, g = ([x[j][:, sl] for j in range(len(seqs)) for sl in sls] for x in full)
        yield
        kk = each(lambda k_, w_: k_ * w_, k, per_chain(kk_ref))
        kk = each(lambda x: x * lax.rsqrt(jnp.maximum(head_sum(x * x), 1e-24)), kk)
        kr = each(lambda k_, a_, w_: k_ * (1.0 + (a_ - 1.0) * w_), k, a, per_chain(ka_ref))
        yield
        cum = each(lambda x: sum(_dot(lower, part) for part in _split3(x)[:2]), lw)
        yield
        gam = each(jnp.exp, cum)
        ginv = each(lambda x: jnp.exp(-x), cum)
        g_end = each(lambda x: x[ch - 1:ch, :], gam)
        yield
        a_t = each(lambda kk_, c_, lw_: -kk_ * jnp.exp(c_ - lw_), kk, cum, lw)
        r_t = each(lambda r_, g_: r_ * g_, r, gam)
        lhs = each(stack2, a_t, r_t)
        yield
        b_h = each(lambda kk_, a_, gi_: kk_ * a_ * gi_, kk, a, ginv)
        k_h = each(lambda kr_, gi_: kr_ * gi_, kr, ginv)
        rhs = each(stack2, b_h, k_h)
        yield
        vs = each(lambda v_: stack(v_).astype(BF16), v)
        bk = each(lambda b_, k_, ge_: stack2(b_ * ge_, k_ * ge_), b_h, k_h, g_end)
        yield
        bonus = each(lambda r_, kr_, w_, v_: head_sum(r_ * kr_ * w_) * v_, r, kr, per_chain(rk_ref), v)
        return lhs, rhs, vs, bk, g_end, bonus, g

    def advance(seqs, prepared):
        lhs, rhs, vs, bk, g_end, bonus, g = prepared
        per_chain = lambda ref: [ref[:, sl] for _ in seqs for sl in sls]
        slots = [q * n_tiles + p for q in seqs for p in range(n_tiles)]
        sc = each(_dot_nt, lhs, rhs)
        yield
        ab = each(lambda s: jnp.where(strict, s[0:n2, 0:n2], 0.0).astype(BF16), sc)
        ak = each(lambda s: jnp.where(strict, s[0:n2, n2:2 * n2], 0.0).astype(BF16), sc)
        rbk = each(lambda s: jnp.where(incl2, s[n2:2 * n2, :], 0.0).astype(BF16), sc)
        yield
        st = [state[i] for i in slots]
        ah = each(lambda l_, s_: _dot_nt(l_, s_.astype(BF16)), lhs, st)
        yield
        x = each(lambda ah_, ak_, vs_: ah_[0:n2] + _dot(ak_, vs_), ah, ak, vs)
        yield
        pw = ab
        n_lvl = ch.bit_length() - 1
        for lvl in range(n_lvl):
            x = each(lambda x_, p_: x_ + _dot(p_, x_.astype(BF16)), x, pw)
            yield
            if lvl + 1 < n_lvl:
                pw = each(lambda p_: _dot(p_, p_).astype(BF16), pw)
                yield
        uv = each(lambda x_, vs_: jnp.concatenate([x_.astype(BF16), vs_], axis=0), x, vs)
        ys = each(lambda ah_, m_, uv_: ah_[n2:2 * n2] + _dot(m_, uv_), ah, rbk, uv)
        yield
        y = each(lambda ys_: ys_[0:ch] + ys_[ch:n2], ys)
        new_st = each(lambda s_, ge_, uv_, bk_: s_ * ge_ + _dot_tn(uv_, bk_), st, g_end, uv, bk)
        for i, s_new in zip(slots, new_st):
            state[i] = s_new
        yield
        mu = each(lambda y_: head_sum(y_) * (1.0 / hd), y)
        yc = each(lambda y_, m_: y_ - m_, y, mu)
        var = each(lambda c_: head_sum(c_ * c_) * (1.0 / hd), yc)
        yield
        yn = each(lambda c_, v_, g_, b_: c_ * lax.rsqrt(v_ + RWKV_LNX_EPS) * g_ + b_,
                  yc, var, per_chain(lng_ref), per_chain(lnb_ref))
        for j, i in enumerate(slots):
            y_ref[i // n_tiles, :, sls[i % n_tiles]] = ((yn[j] + bonus[j]) * g[j]).astype(BF16)

    starts = [sum(seq_groups[:i]) for i in range(len(seq_groups))]
    groups = [tuple(range(s, s + n)) for s, n in zip(starts, seq_groups)]
    prepared = _run_stages(prepare(groups[0]))
    for gi, seqs in enumerate(groups):
        nxt = prepare(groups[gi + 1]) if gi + 1 < len(groups) else None
        prepared = _run_stages(advance(seqs, prepared), nxt)


def _slab_spec(w, n_steps, step_of):
    rep = 1
    while w.shape[0] % (n_steps // rep * 16):
        rep *= 2
        assert rep <= n_steps
    return pl.BlockSpec((w.shape[0] // (n_steps // rep), w.shape[1]),
                        lambda *idx: (step_of(*idx) // rep, 0))


def _rwkv(main, small, mus, mu_s, w0, w2p, a0, a2p, g2, head_rows, cast_weights, *, batch, seq,
          seq_groups=(2, 2)):
    n_seq = sum(seq_groups)
    t = main.shape[0]
    width = RWKV_HEADS * RWKV_HEAD_DIM
    nsm = small.shape[1]
    col0 = (4 * FOX_HEADS * FOX_HEAD_DIM) // width
    ch = RWKV_CHUNK
    steps = seq // ch
    n_tiles = width // LANES
    assert batch % n_seq == 0
    groups = batch // n_seq
    grouped = lambda x: x.reshape(groups, n_seq, seq, x.shape[-1])
    act = lambda cols, blk: pl.BlockSpec((None, n_seq, ch, cols), lambda b, c: (b, 0, c, blk))
    row = lambda n: pl.BlockSpec((1, n), lambda b, c: (0, 0))
    whole = lambda w: pl.BlockSpec(w.shape, lambda b, c: (0, 0))
    slabs = [_slab_spec(w, groups * steps, lambda b, c: b * steps + c) for w in cast_weights]
    pipelined = n_seq * (4 * _nbytes((ch, width), BF16) + _nbytes((ch, nsm), F32))
    pipelined += sum(_nbytes(s.block_shape, F32) + _nbytes(s.block_shape, BF16) for s in slabs)
    pipelined += sum(_nbytes(w.shape, BF16) for w in (w2p, a2p, g2))
    scratch = [(n_seq * n_tiles, LANES, LANES)] + [(n_seq, 8, width)] * 3 + [(n_seq, 8, nsm)]
    y, *copies = pl.pallas_call(
        functools.partial(_rwkv_kernel, n_tiles=n_tiles, seq_groups=seq_groups, n_cast=len(slabs)),
        grid=(groups, steps),
        in_specs=[act(width, col0), act(width, col0 + 1), act(width, col0 + 2), act(nsm, 0),
                  row(width), row(width), row(width), row(nsm),
                  row(width), whole(w2p), row(width), whole(a2p), whole(g2)]
        + [row(width)] * 5 + slabs,
        out_specs=[act(width, 0)] + slabs,
        out_shape=[jax.ShapeDtypeStruct((groups, n_seq, seq, width), BF16)]
        + [jax.ShapeDtypeStruct(w.shape, BF16) for w in cast_weights],
        scratch_shapes=[pltpu.VMEM(s, F32) for s in scratch],
        compiler_params=pltpu.CompilerParams(
            dimension_semantics=("arbitrary", "arbitrary"),
            vmem_limit_bytes=_vmem_limit(pipelined, sum(_nbytes(s, F32) for s in scratch), 24 << 20),
        ),
        name="rwkv",
    )(grouped(main), grouped(main), grouped(main), grouped(small), *mus, mu_s, w0, w2p, a0, a2p, g2,
      *head_rows, *cast_weights)
    return y.reshape(t, width), copies


def _mixout_kernel(o_ref, y_ref, ga0_ref, ga1_ref, gb0_ref, gb1_ref, wa_ref, wb_ref, wo_ref, x_ref,
                   gate_ref, lng_ref, lnb_ref, scale_ref, shift_ref, x1_ref, h2_ref, *, n_sub):
    tm, d = x_ref.shape
    rows = tm // n_sub
    half = d // 2
    sl = [slice(s * rows, (s + 1) * rows) for s in range(n_sub)]

    def branches(s):
        return _dot(o_ref[sl[s], :], wa_ref[...]), _dot(y_ref[sl[s], :], wb_ref[...])

    def mixed(s, ua, ub):
        parts = []
        for c, (ga, gb) in enumerate(((ga0_ref, gb0_ref), (ga1_ref, gb1_ref))):
            cs = slice(c * half, (c + 1) * half)
            parts.append(_sigmoid(ga[sl[s], :].astype(F32)) * ua[:, cs]
                         + _sigmoid(gb[sl[s], :].astype(F32)) * ub[:, cs])
        return _dot(jnp.concatenate(parts, axis=1).astype(BF16), wo_ref[...])

    def norms(s, mix):
        z = DEEPNORM_ALPHA * x_ref[sl[s], :] + gate_ref[0] * mix
        x1 = _layernorm(z) * lng_ref[...] + lnb_ref[...]
        x1_ref[sl[s], :] = x1
        h2_ref[sl[s], :] = (_layernorm(x1) * (1.0 + scale_ref[0]) + shift_ref[0]).astype(BF16)

    u = branches(0)
    mix_prev = None
    for s in range(n_sub):
        u_next = branches(s + 1) if s + 1 < n_sub else None
        mix = mixed(s, *u)
        if mix_prev is not None:
            norms(s - 1, mix_prev)
        u, mix_prev = u_next, mix
    norms(n_sub - 1, mix_prev)


def _mixout(o, y, main, wa, wb, wo, x2d, gate1, ln_g, ln_b, scale2, shift2, *, seq, tm=256, n_sub=2):
    t, d = x2d.shape
    kdim = o.shape[1]
    half = d // 2
    gate_blk0 = (main.shape[1] - 2 * d) // half
    tiles_per_batch = seq // tm
    mod = pl.BlockSpec((1, 1, d), lambda i: (i // tiles_per_batch, 0, 0))
    row = pl.BlockSpec((1, d), lambda i: (0, 0))
    gate = lambda c: pl.BlockSpec((tm, half), lambda i: (i, gate_blk0 + c))
    resident = lambda w: pl.BlockSpec(w.shape, lambda i: (0, 0), pipeline_mode=pl.Buffered(1))
    pipelined = (2 * _nbytes((tm, kdim), BF16) + 4 * _nbytes((tm, half), BF16)
                 + 2 * _nbytes((tm, d), F32) + _nbytes((tm, d), BF16))
    weights = _nbytes(wa.shape, BF16) + _nbytes(wb.shape, BF16) + _nbytes(wo.shape, BF16)
    return pl.pallas_call(
        functools.partial(_mixout_kernel, n_sub=n_sub),
        grid=(t // tm,),
        in_specs=[
            pl.BlockSpec((tm, kdim), lambda i: (i, 0)),
            pl.BlockSpec((tm, kdim), lambda i: (i, 0)),
            gate(0), gate(1), gate(2), gate(3),
            resident(wa), resident(wb), resident(wo),
            pl.BlockSpec((tm, d), lambda i: (i, 0)),
            mod, row, row, mod, mod,
        ],
        out_specs=[pl.BlockSpec((tm, d), lambda i: (i, 0)),
                   pl.BlockSpec((tm, d), lambda i: (i, 0))],
        out_shape=[jax.ShapeDtypeStruct((t, d), F32), jax.ShapeDtypeStruct((t, d), BF16)],
        compiler_params=pltpu.CompilerParams(
            dimension_semantics=("arbitrary",),
            vmem_limit_bytes=_vmem_limit(pipelined, weights, 8 * _nbytes((tm, d), F32)),
        ),
        name="mixout",
    )(o, y, main, main, main, main, wa, wb, wo, x2d, gate1, ln_g, ln_b, scale2, shift2)


def _ffn_kernel(h_ref, wg_ref, wu_ref, wd_ref, x1_ref, gate_ref, lng_ref, lnb_ref, o_ref, acc):
    i, f = pl.program_id(0), pl.program_id(1)
    finishing = i == pl.num_programs(0) - 1

    def partial_sum():
        h = h_ref[...]
        g = _dot(h, wg_ref[...])
        u = _dot(h, wu_ref[...])
        act = (g * _sigmoid(g) * u).astype(BF16)
        return _dot(act, wd_ref[...])

    def finish_previous_tile():
        z = DEEPNORM_ALPHA * x1_ref[...] + gate_ref[0] * acc[...]
        o_ref[...] = _layernorm(z) * lng_ref[...] + lnb_ref[...]

    @pl.when((f == 0) & (i == 0))
    def _():
        acc[...] = partial_sum()

    @pl.when((f == 0) & (i > 0) & jnp.logical_not(finishing))
    def _():
        finish_previous_tile()
        acc[...] = partial_sum()

    @pl.when((f == 0) & finishing)
    def _():
        finish_previous_tile()

    @pl.when((f > 0) & jnp.logical_not(finishing))
    def _():
        acc[...] += partial_sum()


def _ffn(h2, w_gu, w_down, x1, gate2, ln_g, ln_b, *, seq, tm=512, tf=512):
    t, d = x1.shape
    d_ff = w_down.shape[0]
    nf = d_ff // tf
    n_i = t // tm
    tiles_per_batch = seq // tm
    lag = lambda i, f: jnp.maximum(i - 1 + jnp.minimum(f, 1), 0)
    cur = lambda i: jnp.minimum(i, n_i - 1)
    wf = lambda i, f: jnp.where(i < n_i, f, nf - 1)
    mod = pl.BlockSpec((1, 1, d), lambda i, f: (jnp.minimum(lag(i, f), n_i - 1) // tiles_per_batch, 0, 0))
    row = pl.BlockSpec((1, d), lambda i, f: (0, 0))
    lagged = pl.BlockSpec((tm, d), lambda i, f: (jnp.minimum(lag(i, f), n_i - 1), 0))
    pipelined = (_nbytes((tm, d), BF16) + 3 * _nbytes((d, tf), BF16) + 2 * _nbytes((tm, d), F32))
    return pl.pallas_call(
        _ffn_kernel,
        grid=(n_i + 1, nf),
        in_specs=[
            pl.BlockSpec((tm, d), lambda i, f: (cur(i), 0)),
            pl.BlockSpec((d, tf), lambda i, f: (0, wf(i, f))),
            pl.BlockSpec((d, tf), lambda i, f: (0, nf + wf(i, f))),
            pl.BlockSpec((tf, d), lambda i, f: (wf(i, f), 0)),
            lagged, mod, row, row,
        ],
        out_specs=lagged,
        out_shape=jax.ShapeDtypeStruct((t, d), F32),
        scratch_shapes=[pltpu.VMEM((tm, d), F32)],
        compiler_params=pltpu.CompilerParams(
            dimension_semantics=("arbitrary", "arbitrary"),
            vmem_limit_bytes=_vmem_limit(pipelined, _nbytes((tm, d), F32),
                                         2 * _nbytes((tm, d), F32) + 4 * _nbytes((tm, tf), F32)),
        ),
        name="ffn",
    )(h2, w_gu, w_gu, w_down, x1, gate2, ln_g, ln_b)


def _pad_rows(w, rows):
    return jnp.zeros((rows, w.shape[1]), w.dtype).at[:w.shape[0]].set(w)


def kernel(x, c, w_ada, b_ada, w_in, b_fgate, q_norm_g, k_norm_g, rwkv_mu, rwkv_w0, rwkv_w2, rwkv_a0, rwkv_a2, rwkv_g2, rwkv_k_k, rwkv_k_a, rwkv_r_k, rwkv_lnx_g, rwkv_lnx_b, w_branch_a, w_branch_b, w_out, ln1_g, ln1_b, w_ffn_gu, w_ffn_down, ln2_g, ln2_b):
    batch, seq, d = x.shape
    t = batch * seq
    fox_w = FOX_HEADS * FOX_HEAD_DIM
    rw_w = RWKV_HEADS * RWKV_HEAD_DIM
    fox_cols = 4 * fox_w + FOX_HEADS
    lora0 = fox_cols + 3 * rw_w
    gate0 = lora0 + DECAY_LORA + AAA_LORA + GATE_LORA
    assert seq % RWKV_CHUNK == 0 and w_in.shape[1] == gate0 + 2 * d

    lora_w = gate0 - lora0
    mu_r = rwkv_mu[None, 0:rw_w]
    mu_k = rwkv_mu[None, rw_w:2 * rw_w]
    mu_v = rwkv_mu[None, 2 * rw_w:3 * rw_w]
    mu_s = jnp.zeros((1, INPROJ_TN), rwkv_mu.dtype).at[0, :lora_w].set(rwkv_mu[3 * rw_w:])
    w2p = _pad_rows(rwkv_w2, LORA_PAD).astype(BF16)
    a2p = _pad_rows(rwkv_a2, LORA_PAD).astype(BF16)
    g2 = rwkv_g2.astype(BF16)
    row = lambda v: v.reshape(1, -1)

    c_pad = jnp.zeros((8, d), c.dtype).at[:batch].set(c)
    mod_head = _ada(c_pad, w_ada, row(b_ada), 2 * d)[:batch].reshape(batch, 2, 1, d)
    shift1, scale1 = mod_head[:, 0], mod_head[:, 1]

    x2d = x.reshape(t, d)
    main, small, ft = _inproj(x2d, scale1, shift1, w_in.T, row(q_norm_g), row(k_norm_g), seq=seq,
                              group_rows=[(0, 4 * fox_w), (fox_cols, 3 * rw_w), (gate0, 2 * d)],
                              lora_row=lora0, fgate_row=4 * fox_w, tn=INPROJ_TN)

    blocks_per_seq = seq // LANES
    ft_rows = ft.reshape(FOX_HEADS * batch * blocks_per_seq, LANES)
    bias_rows = jnp.broadcast_to(b_fgate[:, None, None], (FOX_HEADS, batch * blocks_per_seq, LANES))
    cum = _fox_prep(ft_rows, bias_rows.reshape(ft_rows.shape), blocks_per_seq=blocks_per_seq)
    cum = cum.reshape(FOX_HEADS, batch, 1, seq)
    o, mod_tail = _fox_attn(main, cum, c_pad, w_ada, row(b_ada), 2 * d, batch=batch, seq=seq)
    gate1, shift2, scale2, gate2 = (mod_tail[:batch].reshape(batch, 4, 1, d)[:, i] for i in range(4))

    y, (wa, wb, wo, wgu, wdn) = _rwkv(
        main, small, (mu_r, mu_k, mu_v), mu_s, row(rwkv_w0), w2p, row(rwkv_a0), a2p, g2,
        [row(p) for p in (rwkv_k_k, rwkv_k_a, rwkv_r_k, rwkv_lnx_g, rwkv_lnx_b)],
        [w_branch_a, w_branch_b, w_out, w_ffn_gu, w_ffn_down], batch=batch, seq=seq)

    x1, h2 = _mixout(o, y, main, wa, wb, wo, x2d, gate1, row(ln1_g), row(ln1_b), scale2, shift2, seq=seq)

    out = _ffn(h2, wgu, wdn, x1, gate2, row(ln2_g), row(ln2_b), seq=seq)
    return out.reshape(batch, seq, d)
```

```python
import functools

import jax
import jax.numpy as jnp
from jax import lax
from jax.experimental import pallas as pl
from jax.experimental.pallas import tpu as pltpu

F32 = jnp.float32
BF16 = jnp.bfloat16

LANES = 128
VMEM_CAP_BYTES = 60000 * 1024

FOX_HEADS = 8
FOX_HEAD_DIM = 128
RWKV_HEADS = 16
RWKV_HEAD_DIM = 64
DECAY_LORA = 96
AAA_LORA = 96
GATE_LORA = 256
LORA_PAD = 128
INPROJ_TN = 512
RWKV_LNX_EPS = 64e-5
DEPTH = 1
DEEPNORM_ALPHA = (2.0 * DEPTH) ** 0.25
LN_EPS = 1e-5
RMS_EPS = 1e-6
RWKV_CHUNK = 64
HEADS_PER_TILE = LANES // RWKV_HEAD_DIM


def _vmem_limit(pipelined_bytes, resident_bytes=0, temp_bytes=0):
    need = 2 * pipelined_bytes + resident_bytes + temp_bytes + (2 << 20)
    return int(min(VMEM_CAP_BYTES, need))


def _nbytes(shape, dtype):
    n = 1
    for s in shape:
        n *= s
    return n * jnp.dtype(dtype).itemsize


def _layernorm(x):
    mu = jnp.mean(x, axis=-1, keepdims=True)
    xc = x - mu
    var = jnp.mean(xc * xc, axis=-1, keepdims=True)
    return xc * lax.rsqrt(var + LN_EPS)


def _sigmoid(x):
    return 1.0 / (1.0 + jnp.exp(-x))


def _dot(a, b):
    return jnp.dot(a, b, preferred_element_type=F32)


def _dot_nt(a, b):
    return lax.dot_general(a, b, (((1,), (1,)), ((), ())), preferred_element_type=F32)


def _dot_tn(a, b):
    return lax.dot_general(a, b, (((0,), (0,)), ((), ())), preferred_element_type=F32)


def _split3(x):
    hi = x.astype(BF16)
    r1 = x - hi.astype(F32)
    mid = r1.astype(BF16)
    lo = (r1 - mid.astype(F32)).astype(BF16)
    return hi, mid, lo


def _ada_kernel(c_ref, w_ref, b_ref, o_ref):
    c = c_ref[...]
    s = (c * _sigmoid(c)).astype(BF16)
    o_ref[...] = _dot(s, w_ref[...].astype(BF16)) + b_ref[...]


def _ada(c_pad, w_ada, b_ada, n, *, tn=1024):
    rows, d = c_pad.shape
    return pl.pallas_call(
        _ada_kernel,
        grid=(n // tn,),
        in_specs=[
            pl.BlockSpec((rows, d), lambda j: (0, 0)),
            pl.BlockSpec((d, tn), lambda j: (0, j)),
            pl.BlockSpec((1, tn), lambda j: (0, j)),
        ],
        out_specs=pl.BlockSpec((rows, tn), lambda j: (0, j)),
        out_shape=jax.ShapeDtypeStruct((rows, n), F32),
        compiler_params=pltpu.CompilerParams(
            dimension_semantics=("arbitrary",),
            vmem_limit_bytes=_vmem_limit(_nbytes((d, tn), F32), temp_bytes=_nbytes((d, tn), BF16)),
        ),
        name="ada",
    )(c_pad, w_ada, b_ada)


def _inproj_kernel(x_ref, scale_ref, shift_ref, wt_ref, wft_ref, qg_ref, kg_ref,
                   main_ref, small_ref, ft_ref, h_scr, *, n_q, n_k, n_main):
    j = pl.program_id(1)
    tn = main_ref.shape[1]

    @pl.when(j == 0)
    def _():
        h = _layernorm(x_ref[...]) * (1.0 + scale_ref[0]) + shift_ref[0]
        h_scr[...] = h.astype(BF16)

    def proj():
        w = wt_ref[...].astype(BF16)
        half = h_scr.shape[0] // 2
        return jnp.concatenate([_dot_nt(h_scr[0:half, :], w), _dot_nt(h_scr[half:, :], w)], axis=0)

    @pl.when(j < n_q + n_k)
    def _():
        acc = proj()
        gain = jnp.where(j < n_q, qg_ref[...] * (FOX_HEAD_DIM ** -0.5), kg_ref[...])
        for hh in range(tn // FOX_HEAD_DIM):
            sl = slice(hh * FOX_HEAD_DIM, (hh + 1) * FOX_HEAD_DIM)
            a = acc[:, sl]
            ms = jnp.mean(a * a, axis=-1, keepdims=True)
            main_ref[:, sl] = (a * lax.rsqrt(ms + RMS_EPS) * gain).astype(BF16)

    @pl.when((j >= n_q + n_k) & (j < n_main))
    def _():
        main_ref[...] = proj().astype(BF16)

    @pl.when(j == n_main)
    def _():
        small_ref[...] = proj()
        ft_ref[...] = _dot_nt(wft_ref[...].astype(BF16), h_scr[...])


def _inproj(x2d, scale1, shift1, w_in_t, qg, kg, *, seq, group_rows, lora_row, fgate_row, tm=1024, tn=512):
    t, d = x2d.shape
    fox_w = FOX_HEADS * FOX_HEAD_DIM
    assert all(first % 8 == 0 and count % tn == 0 for first, count in group_rows)
    assert lora_row % 8 == 0 and fgate_row % 8 == 0 and lora_row + tn <= w_in_t.shape[0]
    n_main = sum(count // tn for _, count in group_rows)
    tiles_per_batch = seq // tm
    kern = functools.partial(_inproj_kernel, n_q=fox_w // tn, n_k=fox_w // tn, n_main=n_main)
    pipelined = (_nbytes((tm, d), F32) + _nbytes((tn, d), F32) + _nbytes((tm, tn), BF16)
                 + _nbytes((tm, tn), F32) + _nbytes((8, tm), F32))

    def w_rows(i, j):
        row, hi = jnp.int32(lora_row), n_main
        for first, count in reversed(group_rows):
            lo = hi - count // tn
            row = jnp.where(j < hi, first + tn * (j - lo), row)
            hi = lo
        return pl.multiple_of(row, 8), 0

    return pl.pallas_call(
        kern,
        grid=(t // tm, n_main + 1),
        in_specs=[
            pl.BlockSpec((tm, d), lambda i, j: (jnp.minimum(i + jnp.minimum(j, 1), t // tm - 1), 0)),
            pl.BlockSpec((1, 1, d), lambda i, j: (i // tiles_per_batch, 0, 0)),
            pl.BlockSpec((1, 1, d), lambda i, j: (i // tiles_per_batch, 0, 0)),
            pl.BlockSpec((pl.Element(tn), pl.Element(d)), w_rows),
            pl.BlockSpec((pl.Element(FOX_HEADS), pl.Element(d)), lambda i, j: (fgate_row, 0)),
            pl.BlockSpec((1, FOX_HEAD_DIM), lambda i, j: (0, 0)),
            pl.BlockSpec((1, FOX_HEAD_DIM), lambda i, j: (0, 0)),
        ],
        out_specs=[
            pl.BlockSpec((tm, tn), lambda i, j: (i, jnp.minimum(j, n_main - 1))),
            pl.BlockSpec((tm, tn), lambda i, j: (i, 0)),
            pl.BlockSpec((FOX_HEADS, tm), lambda i, j: (0, i)),
        ],
        out_shape=[
            jax.ShapeDtypeStruct((t, n_main * tn), BF16),
            jax.ShapeDtypeStruct((t, tn), F32),
            jax.ShapeDtypeStruct((FOX_HEADS, t), F32),
        ],
        scratch_shapes=[pltpu.VMEM((tm, d), BF16)],
        compiler_params=pltpu.CompilerParams(
            dimension_semantics=("arbitrary", "arbitrary"),
            vmem_limit_bytes=_vmem_limit(pipelined, _nbytes((tm, d), BF16),
                                         3 * _nbytes((tm, d), F32)),
        ),
        name="inproj",
    )(x2d, scale1, shift1, w_in_t, w_in_t, qg, kg)


def _fox_prep_kernel(ft_ref, bias_ref, o_ref, *, blocks_per_seq):
    x = ft_ref[...] + bias_ref[...]
    lf = jnp.minimum(x, 0.0) - jnp.log1p(jnp.exp(-jnp.abs(x)))
    rows, width = lf.shape
    r = lax.broadcasted_iota(jnp.int32, (width, width), 0)
    c = lax.broadcasted_iota(jnp.int32, (width, width), 1)
    upper = (r <= c).astype(BF16)
    within = sum(_dot(p, upper) for p in _split3(lf))
    tot = jnp.broadcast_to(within[:, width - 1:width], (rows, width))
    rr = lax.broadcasted_iota(jnp.int32, (rows, rows), 0)
    cc = lax.broadcasted_iota(jnp.int32, (rows, rows), 1)
    prior = ((rr // blocks_per_seq == cc // blocks_per_seq) & (cc < rr)).astype(BF16)
    offset = sum(_dot(prior, p) for p in _split3(tot))
    o_ref[...] = within + offset


def _fox_prep(ft_rows, bias_rows, *, blocks_per_seq):
    return pl.pallas_call(
        functools.partial(_fox_prep_kernel, blocks_per_seq=blocks_per_seq),
        out_shape=jax.ShapeDtypeStruct(ft_rows.shape, F32),
        name="fox_prep",
    )(ft_rows, bias_rows)


def _fox_attn_kernel(q_ref, k_ref, v_ref, og_ref, cum_ref, c_ref, wada_ref, bada_ref, o_ref, mod_ref, *, tq):
    _ada_kernel(c_ref, wada_ref, bada_ref, mod_ref)

    seq = q_ref.shape[0]
    n_q = seq // tq
    negcum = -cum_ref[0, 0]
    row = lax.broadcasted_iota(jnp.int32, (tq, tq), 0)
    col = lax.broadcasted_iota(jnp.int32, (tq, tq), 1)
    causal = col <= row

    def logits(qi):
        q0, kend = qi * tq, (qi + 1) * tq
        q = q_ref[q0:kend, :]
        diag = _dot_nt(q, k_ref[q0:kend, :]) + negcum[:, q0:kend]
        diag = jnp.where(causal, diag, -jnp.inf)
        past = _dot_nt(q, k_ref[0:q0, :]) + negcum[:, 0:q0] if qi else None
        return past, diag

    def attend(qi, past, diag):
        q0, kend = qi * tq, (qi + 1) * tq
        m = jnp.max(diag, axis=-1, keepdims=True)
        if past is not None:
            m = jnp.maximum(m, jnp.max(past, axis=-1, keepdims=True))
        p = jnp.exp(diag - m)
        l = jnp.sum(p, axis=-1, keepdims=True)
        o = _dot(p.astype(BF16), v_ref[q0:kend, :])
        if past is not None:
            p = jnp.exp(past - m)
            l = l + jnp.sum(p, axis=-1, keepdims=True)
            o = o + _dot(p.astype(BF16), v_ref[0:q0, :])
        gate = _sigmoid(og_ref[q0:kend, :].astype(F32))
        o_ref[q0:kend, :] = (o / l * gate).astype(BF16)

    nxt = logits(0)
    for qi in range(n_q):
        cur, nxt = nxt, (logits(qi + 1) if qi + 1 < n_q else None)
        attend(qi, *cur)


def _fox_attn(main, cum, c_pad, w_ada, b_ada, ada_col0, *, batch, seq, tq=256):
    t = main.shape[0]
    hd = FOX_HEAD_DIM
    nh = FOX_HEADS
    rows, d = c_pad.shape
    n_ada = w_ada.shape[1] - ada_col0
    tn = n_ada // (batch * nh)
    assert n_ada % (batch * nh) == 0 and tn % LANES == 0 and ada_col0 % tn == 0
    slab = lambda r: pl.BlockSpec((r, tn), lambda b, h: (0, ada_col0 // tn + b * nh + h))
    blk = lambda off: pl.BlockSpec((seq, hd), lambda b, h: (b, off + h))
    pipelined = 5 * _nbytes((seq, hd), BF16) + _nbytes((8, seq), F32) + _nbytes((d, tn), F32)
    return pl.pallas_call(
        functools.partial(_fox_attn_kernel, tq=tq),
        grid=(batch, nh),
        in_specs=[blk(0), blk(nh), blk(2 * nh), blk(3 * nh),
                  pl.BlockSpec((1, 1, 1, seq), lambda b, h: (h, b, 0, 0)),
                  pl.BlockSpec((rows, d), lambda b, h: (0, 0)), slab(d), slab(1)],
        out_specs=[pl.BlockSpec((seq, hd), lambda b, h: (b, h)),
                   pl.BlockSpec((rows, tn), lambda b, h: (0, b * nh + h))],
        out_shape=[jax.ShapeDtypeStruct((t, nh * hd), BF16), jax.ShapeDtypeStruct((rows, n_ada), F32)],
        compiler_params=pltpu.CompilerParams(
            dimension_semantics=("arbitrary", "arbitrary"),
            vmem_limit_bytes=VMEM_CAP_BYTES,
        ),
        name="fox_attn",
    )(main, main, main, main, cum, c_pad, w_ada, b_ada)


def _run_stages(gen, side=None):
    result = {}
    live = [g for g in (gen, side) if g is not None]
    while live:
        for g in list(live):
            try:
                next(g)
            except StopIteration as stop:
                result[g] = stop.value
                live.remove(g)
    return result[gen if side is None else side]


def _rwkv_inputs(raw_refs, small_ref, mu_refs, mu_s_ref, w0_ref, w2_ref, a0_ref, a2_ref, g2_ref,
                 prev_refs, prev_s, seqs):
    def shift_mix(x, prev_ref, q, mu):
        rows = x.shape[0]
        xp = pltpu.roll(x, 1, axis=0)
        row = lax.broadcasted_iota(jnp.int32, x.shape, 0)
        xp = jnp.where(row == 0, prev_ref[q, 0:1, :], xp)
        prev_ref[q, 0:1, :] = x[rows - 1:rows, :]
        return x + (xp - x) * mu

    r, k, v = ([shift_mix(ref[q].astype(F32), prev, q, mu[...]) for q in seqs]
               for ref, prev, mu in zip(raw_refs, prev_refs, mu_refs))
    sm = jnp.concatenate([shift_mix(small_ref[q], prev_s, q, mu_s_ref[...]) for q in seqs], axis=0)
    xw = sm[:, 0:LORA_PAD]
    xa = sm[:, DECAY_LORA:DECAY_LORA + LORA_PAD]
    xg = sm[:, DECAY_LORA + AAA_LORA:DECAY_LORA + AAA_LORA + GATE_LORA]

    z = w0_ref[...] + _dot(jnp.tanh(xw).astype(BF16), w2_ref[...])
    w_raw = -(jnp.maximum(-z, 0.0) + jnp.log(1.0 + jnp.exp(-jnp.abs(z)))) - 0.5
    lw = -jnp.exp(w_raw)
    a = _sigmoid(a0_ref[...] + _dot(xa.astype(BF16), a2_ref[...]))
    g = _dot(_sigmoid(xg).astype(BF16), g2_ref[...])
    ch = lw.shape[0] // len(seqs)
    per_seq = lambda x: [x[j * ch:(j + 1) * ch, :] for j in range(len(seqs))]
    return r, k, v, per_seq(lw), per_seq(a), per_seq(g)


def _rwkv_kernel(r_ref, k_ref, v_ref, small_ref, mu_r_ref, mu_k_ref, mu_v_ref, mu_s_ref,
                 w0_ref, w2_ref, a0_ref, a2_ref, g2_ref,
                 kk_ref, ka_ref, rk_ref, lng_ref, lnb_ref, *rest, n_tiles, seq_groups, n_cast):
    cast_src, y_ref, cast_dst = rest[:n_cast], rest[n_cast], rest[n_cast + 1:2 * n_cast + 1]
    state, prev_r, prev_k, prev_v, prev_s = rest[2 * n_cast + 1:]
    for src, dst in zip(cast_src, cast_dst):
        dst[...] = src[...].astype(BF16)

    @pl.when(pl.program_id(1) == 0)
    def _():
        for ref in (state, prev_r, prev_k, prev_v, prev_s):
            ref[...] = jnp.zeros_like(ref)

    ch = y_ref.shape[1]
    hd = RWKV_HEAD_DIM
    lane = lax.broadcasted_iota(jnp.int32, (1, LANES), 1)
    m0 = lane < hd
    n2 = HEADS_PER_TILE * ch
    rr = lax.broadcasted_iota(jnp.int32, (n2, n2), 0)
    cc = lax.broadcasted_iota(jnp.int32, (n2, n2), 1)
    same = (rr // ch) == (cc // ch)
    strict = same & (rr > cc)
    incl = same & (rr >= cc)
    lr = lax.broadcasted_iota(jnp.int32, (ch, ch), 0)
    lc = lax.broadcasted_iota(jnp.int32, (ch, ch), 1)
    lower = (lc <= lr).astype(BF16)

    def head_sum(x):
        s0 = jnp.sum(jnp.where(m0, x, 0.0), axis=-1, keepdims=True)
        s1 = jnp.sum(jnp.where(m0, 0.0, x), axis=-1, keepdims=True)
        return jnp.where(m0, s0, s1)

    def stack(x):
        return jnp.concatenate([jnp.where(m0, x, 0.0), jnp.where(m0, 0.0, x)], axis=0)

    sls = [slice(p * LANES, (p + 1) * LANES) for p in range(n_tiles)]

    def each(fn, *cols):
        return [fn(*args) for args in zip(*cols)]

    stack2 = lambda x, y: jnp.concatenate([stack(x), stack(y)], axis=0).astype(BF16)
    incl2 = jnp.concatenate([incl, incl], axis=1)

    def prepare(seqs):
        per_chain = lambda ref: [ref[:, sl] for _ in seqs for sl in sls]
        full = _rwkv_inputs((r_ref, k_ref, v_ref), small_ref, (mu_r_ref, mu_k_ref, mu_v_ref), mu_s_ref,
                            w0_ref, w2_ref, a0_ref, a2_ref, g2_ref, (prev_r, prev_k, prev_v), prev_s, seqs)
        r, k, v, lw, a, g = ([x[j][:, sl] for j in range(len(seqs)) for sl in sls] for x in full)
        yield
        kk = each(lambda k_, w_: k_ * w_, k, per_chain(kk_ref))
        kk = each(lambda x: x * lax.rsqrt(jnp.maximum(head_sum(x * x), 1e-24)), kk)
        kr = each(lambda k_, a_, w_: k_ * (1.0 + (a_ - 1.0) * w_), k, a, per_chain(ka_ref))
        yield
        cum = each(lambda x: sum(_dot(lower, part) for part in _split3(x)[:2]), lw)
        yield
        gam = each(jnp.exp, cum)
        ginv = each(lambda x: jnp.exp(-x), cum)
        g_end = each(lambda x: x[ch - 1:ch, :], gam)
        yield
        a_t = each(lambda kk_, c_, lw_: -kk_ * jnp.exp(c_ - lw_), kk, cum, lw)
        r_t = each(lambda r_, g_: r_ * g_, r, gam)
        lhs = each(stack2, a_t, r_t)
        yield
        b_h = each(lambda kk_, a_, gi_: kk_ * a_ * gi_, kk, a, ginv)
        k_h = each(lambda kr_, gi_: kr_ * gi_, kr, ginv)
        rhs = each(stack2, b_h, k_h)
        yield
        vs = each(lambda v_: stack(v_).astype(BF16), v)
        bk = each(lambda b_, k_, ge_: stack2(b_ * ge_, k_ * ge_), b_h, k_h, g_end)
        yield
        bonus = each(lambda r_, kr_, w_, v_: head_sum(r_ * kr_ * w_) * v_, r, kr, per_chain(rk_ref), v)
        return lhs, rhs, vs, bk, g_end, bonus, g

    def advance(seqs, prepared):
        lhs, rhs, vs, bk, g_end, bonus, g = prepared
        per_chain = lambda ref: [ref[:, sl] for _ in seqs for sl in sls]
        slots = [q * n_tiles + p for q in seqs for p in range(n_tiles)]
        sc = each(_dot_nt, lhs, rhs)
        yield
        ab = each(lambda s: jnp.where(strict, s[0:n2, 0:n2], 0.0).astype(BF16), sc)
        ak = each(lambda s: jnp.where(strict, s[0:n2, n2:2 * n2], 0.0).astype(BF16), sc)
        rbk = each(lambda s: jnp.where(incl2, s[n2:2 * n2, :], 0.0).astype(BF16), sc)
        yield
        st = [state[i] for i in slots]
        ah = each(lambda l_, s_: _dot_nt(l_, s_.astype(BF16)), lhs, st)
        yield
        x = each(lambda ah_, ak_, vs_: ah_[0:n2] + _dot(ak_, vs_), ah, ak, vs)
        yield
        pw = ab
        n_lvl = ch.bit_length() - 1
        for lvl in range(n_lvl):
            x = each(lambda x_, p_: x_ + _dot(p_, x_.astype(BF16)), x, pw)
            yield
            if lvl + 1 < n_lvl:
                pw = each(lambda p_: _dot(p_, p_).astype(BF16), pw)
                yield
        uv = each(lambda x_, vs_: jnp.concatenate([x_.astype(BF16), vs_], axis=0), x, vs)
        ys = each(lambda ah_, m_, uv_: ah_[n2:2 * n2] + _dot(m_, uv_), ah, rbk, uv)
        yield
        y = each(lambda ys_: ys_[0:ch] + ys_[ch:n2], ys)
        new_st = each(lambda s_, ge_, uv_, bk_: s_ * ge_ + _dot_tn(uv_, bk_), st, g_end, uv, bk)
        for i, s_new in zip(slots, new_st):
            state[i] = s_new
        yield
        mu = each(lambda y_: head_sum(y_) * (1.0 / hd), y)
        yc = each(lambda y_, m_: y_ - m_, y, mu)
        var = each(lambda c_: head_sum(c_ * c_) * (1.0 / hd), yc)
        yield
        yn = each(lambda c_, v_, g_, b_: c_ * lax.rsqrt(v_ + RWKV_LNX_EPS) * g_ + b_,
                  yc, var, per_chain(lng_ref), per_chain(lnb_ref))
        for j, i in enumerate(slots):
            y_ref[i // n_tiles, :, sls[i % n_tiles]] = ((yn[j] + bonus[j]) * g[j]).astype(BF16)

    starts = [sum(seq_groups[:i]) for i in range(len(seq_groups))]
    groups = [tuple(range(s, s + n)) for s, n in zip(starts, seq_groups)]
    prepared = _run_stages(prepare(groups[0]))
    for gi, seqs in enumerate(groups):
        nxt = prepare(groups[gi + 1]) if gi + 1 < len(groups) else None
        prepared = _run_stages(advance(seqs, prepared), nxt)


def _slab_spec(w, n_steps, step_of):
    rep = 1
    while w.shape[0] % (n_steps // rep * 16):
        rep *= 2
        assert rep <= n_steps
    return pl.BlockSpec((w.shape[0] // (n_steps // rep), w.shape[1]),
                        lambda *idx: (step_of(*idx) // rep, 0))


def _rwkv(main, small, mus, mu_s, w0, w2p, a0, a2p, g2, head_rows, cast_weights, *, batch, seq,
          seq_groups=(2, 2)):
    n_seq = sum(seq_groups)
    t = main.shape[0]
    width = RWKV_HEADS * RWKV_HEAD_DIM
    nsm = small.shape[1]
    col0 = (4 * FOX_HEADS * FOX_HEAD_DIM) // width
    ch = RWKV_CHUNK
    steps = seq // ch
    n_tiles = width // LANES
    assert batch % n_seq == 0
    groups = batch // n_seq
    grouped = lambda x: x.reshape(groups, n_seq, seq, x.shape[-1])
    act = lambda cols, blk: pl.BlockSpec((None, n_seq, ch, cols), lambda b, c: (b, 0, c, blk))
    row = lambda n: pl.BlockSpec((1, n), lambda b, c: (0, 0))
    whole = lambda w: pl.BlockSpec(w.shape, lambda b, c: (0, 0))
    slabs = [_slab_spec(w, groups * steps, lambda b, c: b * steps + c) for w in cast_weights]
    pipelined = n_seq * (4 * _nbytes((ch, width), BF16) + _nbytes((ch, nsm), F32))
    pipelined += sum(_nbytes(s.block_shape, F32) + _nbytes(s.block_shape, BF16) for s in slabs)
    pipelined += sum(_nbytes(w.shape, BF16) for w in (w2p, a2p, g2))
    scratch = [(n_seq * n_tiles, LANES, LANES)] + [(n_seq, 8, width)] * 3 + [(n_seq, 8, nsm)]
    y, *copies = pl.pallas_call(
        functools.partial(_rwkv_kernel, n_tiles=n_tiles, seq_groups=seq_groups, n_cast=len(slabs)),
        grid=(groups, steps),
        in_specs=[act(width, col0), act(width, col0 + 1), act(width, col0 + 2), act(nsm, 0),
                  row(width), row(width), row(width), row(nsm),
                  row(width), whole(w2p), row(width), whole(a2p), whole(g2)]
        + [row(width)] * 5 + slabs,
        out_specs=[act(width, 0)] + slabs,
        out_shape=[jax.ShapeDtypeStruct((groups, n_seq, seq, width), BF16)]
        + [jax.ShapeDtypeStruct(w.shape, BF16) for w in cast_weights],
        scratch_shapes=[pltpu.VMEM(s, F32) for s in scratch],
        compiler_params=pltpu.CompilerParams(
            dimension_semantics=("arbitrary", "arbitrary"),
            vmem_limit_bytes=_vmem_limit(pipelined, sum(_nbytes(s, F32) for s in scratch), 24 << 20),
        ),
        name="rwkv",
    )(grouped(main), grouped(main), grouped(main), grouped(small), *mus, mu_s, w0, w2p, a0, a2p, g2,
      *head_rows, *cast_weights)
    return y.reshape(t, width), copies


def _mixout_kernel(o_ref, y_ref, ga0_ref, ga1_ref, gb0_ref, gb1_ref, wa_ref, wb_ref, wo_ref, x_ref,
                   gate_ref, lng_ref, lnb_ref, scale_ref, shift_ref, x1_ref, h2_ref, *, n_sub):
    tm, d = x_ref.shape
    rows = tm // n_sub
    half = d // 2
    sl = [slice(s * rows, (s + 1) * rows) for s in range(n_sub)]

    def branches(s):
        return _dot(o_ref[sl[s], :], wa_ref[...]), _dot(y_ref[sl[s], :], wb_ref[...])

    def mixed(s, ua, ub):
        parts = []
        for c, (ga, gb) in enumerate(((ga0_ref, gb0_ref), (ga1_ref, gb1_ref))):
            cs = slice(c * half, (c + 1) * half)
            parts.append(_sigmoid(ga[sl[s], :].astype(F32)) * ua[:, cs]
                         + _sigmoid(gb[sl[s], :].astype(F32)) * ub[:, cs])
        return _dot(jnp.concatenate(parts, axis=1).astype(BF16), wo_ref[...])

    def norms(s, mix):
        z = DEEPNORM_ALPHA * x_ref[sl[s], :] + gate_ref[0] * mix
        x1 = _layernorm(z) * lng_ref[...] + lnb_ref[...]
        x1_ref[sl[s], :] = x1
        h2_ref[sl[s], :] = (_layernorm(x1) * (1.0 + scale_ref[0]) + shift_ref[0]).astype(BF16)

    u = branches(0)
    mix_prev = None
    for s in range(n_sub):
        u_next = branches(s + 1) if s + 1 < n_sub else None
        mix = mixed(s, *u)
        if mix_prev is not None:
            norms(s - 1, mix_prev)
        u, mix_prev = u_next, mix
    norms(n_sub - 1, mix_prev)


def _mixout(o, y, main, wa, wb, wo, x2d, gate1, ln_g, ln_b, scale2, shift2, *, seq, tm=256, n_sub=2):
    t, d = x2d.shape
    kdim = o.shape[1]
    half = d // 2
    gate_blk0 = (main.shape[1] - 2 * d) // half
    tiles_per_batch = seq // tm
    mod = pl.BlockSpec((1, 1, d), lambda i: (i // tiles_per_batch, 0, 0))
    row = pl.BlockSpec((1, d), lambda i: (0, 0))
    gate = lambda c: pl.BlockSpec((tm, half), lambda i: (i, gate_blk0 + c))
    resident = lambda w: pl.BlockSpec(w.shape, lambda i: (0, 0), pipeline_mode=pl.Buffered(1))
    pipelined = (2 * _nbytes((tm, kdim), BF16) + 4 * _nbytes((tm, half), BF16)
                 + 2 * _nbytes((tm, d), F32) + _nbytes((tm, d), BF16))
    weights = _nbytes(wa.shape, BF16) + _nbytes(wb.shape, BF16) + _nbytes(wo.shape, BF16)
    return pl.pallas_call(
        functools.partial(_mixout_kernel, n_sub=n_sub),
        grid=(t // tm,),
        in_specs=[
            pl.BlockSpec((tm, kdim), lambda i: (i, 0)),
            pl.BlockSpec((tm, kdim), lambda i: (i, 0)),
            gate(0), gate(1), gate(2), gate(3),
            resident(wa), resident(wb), resident(wo),
            pl.BlockSpec((tm, d), lambda i: (i, 0)),
            mod, row, row, mod, mod,
        ],
        out_specs=[pl.BlockSpec((tm, d), lambda i: (i, 0)),
                   pl.BlockSpec((tm, d), lambda i: (i, 0))],
        out_shape=[jax.ShapeDtypeStruct((t, d), F32), jax.ShapeDtypeStruct((t, d), BF16)],
        compiler_params=pltpu.CompilerParams(
            dimension_semantics=("arbitrary",),
            vmem_limit_bytes=_vmem_limit(pipelined, weights, 8 * _nbytes((tm, d), F32)),
        ),
        name="mixout",
    )(o, y, main, main, main, main, wa, wb, wo, x2d, gate1, ln_g, ln_b, scale2, shift2)


def _ffn_kernel(h_ref, wg_ref, wu_ref, wd_ref, x1_ref, gate_ref, lng_ref, lnb_ref, o_ref, acc):
    f = pl.program_id(1)
    @pl.when(f == 0)
    def _():
        acc[...] = jnp.zeros_like(acc)

    h = h_ref[...]
    g = _dot(h, wg_ref[...])
    u = _dot(h, wu_ref[...])
    act = (g * _sigmoid(g) * u).astype(BF16)
    acc[...] += _dot(act, wd_ref[...])

    @pl.when(f == pl.num_programs(1) - 1)
    def _():
        z = DEEPNORM_ALPHA * x1_ref[...] + gate_ref[0] * acc[...]
        o_ref[...] = _layernorm(z) * lng_ref[...] + lnb_ref[...]


def _ffn(h2, w_gu, w_down, x1, gate2, ln_g, ln_b, *, seq, tm=512, tf=512):
    t, d = x1.shape
    d_ff = w_down.shape[0]
    nf = d_ff // tf
    tiles_per_batch = seq // tm
    mod = pl.BlockSpec((1, 1, d), lambda i, f: (i // tiles_per_batch, 0, 0))
    row = pl.BlockSpec((1, d), lambda i, f: (0, 0))
    pipelined = (_nbytes((tm, d), BF16) + 3 * _nbytes((d, tf), BF16) + 2 * _nbytes((tm, d), F32))
    return pl.pallas_call(
        _ffn_kernel,
        grid=(t // tm, nf),
        in_specs=[
            pl.BlockSpec((tm, d), lambda i, f: (i, 0)),
            pl.BlockSpec((d, tf), lambda i, f: (0, f)),
            pl.BlockSpec((d, tf), lambda i, f: (0, nf + f)),
            pl.BlockSpec((tf, d), lambda i, f: (f, 0)),
            pl.BlockSpec((tm, d), lambda i, f: (i, 0)),
            mod, row, row,
        ],
        out_specs=pl.BlockSpec((tm, d), lambda i, f: (i, 0)),
        out_shape=jax.ShapeDtypeStruct((t, d), F32),
        scratch_shapes=[pltpu.VMEM((tm, d), F32)],
        compiler_params=pltpu.CompilerParams(
            dimension_semantics=("arbitrary", "arbitrary"),
            vmem_limit_bytes=_vmem_limit(pipelined, _nbytes((tm, d), F32),
                                         2 * _nbytes((tm, d), F32) + 4 * _nbytes((tm, tf), F32)),
        ),
        name="ffn",
    )(h2, w_gu, w_gu, w_down, x1, gate2, ln_g, ln_b)


def _pad_rows(w, rows):
    return jnp.zeros((rows, w.shape[1]), w.dtype).at[:w.shape[0]].set(w)


def kernel(x, c, w_ada, b_ada, w_in, b_fgate, q_norm_g, k_norm_g, rwkv_mu, rwkv_w0, rwkv_w2, rwkv_a0, rwkv_a2, rwkv_g2, rwkv_k_k, rwkv_k_a, rwkv_r_k, rwkv_lnx_g, rwkv_lnx_b, w_branch_a, w_branch_b, w_out, ln1_g, ln1_b, w_ffn_gu, w_ffn_down, ln2_g, ln2_b):
    batch, seq, d = x.shape
    t = batch * seq
    fox_w = FOX_HEADS * FOX_HEAD_DIM
    rw_w = RWKV_HEADS * RWKV_HEAD_DIM
    fox_cols = 4 * fox_w + FOX_HEADS
    lora0 = fox_cols + 3 * rw_w
    gate0 = lora0 + DECAY_LORA + AAA_LORA + GATE_LORA
    assert seq % RWKV_CHUNK == 0 and w_in.shape[1] == gate0 + 2 * d

    lora_w = gate0 - lora0
    mu_r = rwkv_mu[None, 0:rw_w]
    mu_k = rwkv_mu[None, rw_w:2 * rw_w]
    mu_v = rwkv_mu[None, 2 * rw_w:3 * rw_w]
    mu_s = jnp.zeros((1, INPROJ_TN), rwkv_mu.dtype).at[0, :lora_w].set(rwkv_mu[3 * rw_w:])
    w2p = _pad_rows(rwkv_w2, LORA_PAD).astype(BF16)
    a2p = _pad_rows(rwkv_a2, LORA_PAD).astype(BF16)
    g2 = rwkv_g2.astype(BF16)
    row = lambda v: v.reshape(1, -1)

    c_pad = jnp.zeros((8, d), c.dtype).at[:batch].set(c)
    mod_head = _ada(c_pad, w_ada, row(b_ada), 2 * d)[:batch].reshape(batch, 2, 1, d)
    shift1, scale1 = mod_head[:, 0], mod_head[:, 1]

    x2d = x.reshape(t, d)
    main, small, ft = _inproj(x2d, scale1, shift1, w_in.T, row(q_norm_g), row(k_norm_g), seq=seq,
                              group_rows=[(0, 4 * fox_w), (fox_cols, 3 * rw_w), (gate0, 2 * d)],
                              lora_row=lora0, fgate_row=4 * fox_w, tn=INPROJ_TN)

    blocks_per_seq = seq // LANES
    ft_rows = ft.reshape(FOX_HEADS * batch * blocks_per_seq, LANES)
    bias_rows = jnp.broadcast_to(b_fgate[:, None, None], (FOX_HEADS, batch * blocks_per_seq, LANES))
    cum = _fox_prep(ft_rows, bias_rows.reshape(ft_rows.shape), blocks_per_seq=blocks_per_seq)
    cum = cum.reshape(FOX_HEADS, batch, 1, seq)
    o, mod_tail = _fox_attn(main, cum, c_pad, w_ada, row(b_ada), 2 * d, batch=batch, seq=seq)
    gate1, shift2, scale2, gate2 = (mod_tail[:batch].reshape(batch, 4, 1, d)[:, i] for i in range(4))

    y, (wa, wb, wo, wgu, wdn) = _rwkv(
        main, small, (mu_r, mu_k, mu_v), mu_s, row(rwkv_w0), w2p, row(rwkv_a0), a2p, g2,
        [row(p) for p in (rwkv_k_k, rwkv_k_a, rwkv_r_k, rwkv_lnx_g, rwkv_lnx_b)],
        [w_branch_a, w_branch_b, w_out, w_ffn_gu, w_ffn_down], batch=batch, seq=seq)

    x1, h2 = _mixout(o, y, main, wa, wb, wo, x2d, gate1, row(ln1_g), row(ln1_b), scale2, shift2, seq=seq)

    out = _ffn(h2, wgu, wdn, x1, gate2, row(ln2_g), row(ln2_b), seq=seq)
    return out.reshape(batch, seq, d)
```

```python
import functools

import jax
import jax.numpy as jnp
from jax import lax
from jax.experimental import pallas as pl
from jax.experimental.pallas import tpu as pltpu

F32 = jnp.float32
BF16 = jnp.bfloat16

LANES = 128
VMEM_CAP_BYTES = 60000 * 1024

FOX_HEADS = 8
FOX_HEAD_DIM = 128
RWKV_HEADS = 16
RWKV_HEAD_DIM = 64
DECAY_LORA = 96
AAA_LORA = 96
GATE_LORA = 256
LORA_PAD = 128
INPROJ_TN = 512
RWKV_LNX_EPS = 64e-5
DEPTH = 1
DEEPNORM_ALPHA = (2.0 * DEPTH) ** 0.25
LN_EPS = 1e-5
RMS_EPS = 1e-6
RWKV_CHUNK = 64
HEADS_PER_TILE = LANES // RWKV_HEAD_DIM


def _vmem_limit(pipelined_bytes, resident_bytes=0, temp_bytes=0):
    need = 2 * pipelined_bytes + resident_bytes + temp_bytes + (2 << 20)
    return int(min(VMEM_CAP_BYTES, need))


def _nbytes(shape, dtype):
    n = 1
    for s in shape:
        n *= s
    return n * jnp.dtype(dtype).itemsize


def _layernorm(x):
    mu = jnp.mean(x, axis=-1, keepdims=True)
    xc = x - mu
    var = jnp.mean(xc * xc, axis=-1, keepdims=True)
    return xc * lax.rsqrt(var + LN_EPS)


def _sigmoid(x):
    return 1.0 / (1.0 + jnp.exp(-x))


def _dot(a, b):
    return jnp.dot(a, b, preferred_element_type=F32)


def _dot_nt(a, b):
    return lax.dot_general(a, b, (((1,), (1,)), ((), ())), preferred_element_type=F32)


def _dot_tn(a, b):
    return lax.dot_general(a, b, (((0,), (0,)), ((), ())), preferred_element_type=F32)


def _split3(x):
    hi = x.astype(BF16)
    r1 = x - hi.astype(F32)
    mid = r1.astype(BF16)
    lo = (r1 - mid.astype(F32)).astype(BF16)
    return hi, mid, lo


def _ada_kernel(c_ref, w_ref, b_ref, o_ref):
    c = c_ref[...]
    s = (c * _sigmoid(c)).astype(BF16)
    o_ref[...] = _dot(s, w_ref[...].astype(BF16)) + b_ref[...]


def _ada(c_pad, w_ada, b_ada, n, *, tn=1024):
    rows, d = c_pad.shape
    return pl.pallas_call(
        _ada_kernel,
        grid=(n // tn,),
        in_specs=[
            pl.BlockSpec((rows, d), lambda j: (0, 0)),
            pl.BlockSpec((d, tn), lambda j: (0, j)),
            pl.BlockSpec((1, tn), lambda j: (0, j)),
        ],
        out_specs=pl.BlockSpec((rows, tn), lambda j: (0, j)),
        out_shape=jax.ShapeDtypeStruct((rows, n), F32),
        compiler_params=pltpu.CompilerParams(
            dimension_semantics=("arbitrary",),
            vmem_limit_bytes=_vmem_limit(_nbytes((d, tn), F32), temp_bytes=_nbytes((d, tn), BF16)),
        ),
        name="ada",
    )(c_pad, w_ada, b_ada)


def _inproj_kernel(x_ref, scale_ref, shift_ref, wt_ref, wft_ref, qg_ref, kg_ref,
                   main_ref, small_ref, ft_ref, h_scr, *, n_q, n_k, n_main):
    j = pl.program_id(1)
    tn = main_ref.shape[1]

    @pl.when(j == 0)
    def _():
        h = _layernorm(x_ref[...]) * (1.0 + scale_ref[0]) + shift_ref[0]
        h_scr[...] = h.astype(BF16)

    def proj():
        w = wt_ref[...].astype(BF16)
        half = h_scr.shape[0] // 2
        return jnp.concatenate([_dot_nt(h_scr[0:half, :], w), _dot_nt(h_scr[half:, :], w)], axis=0)

    @pl.when(j < n_q + n_k)
    def _():
        acc = proj()
        gain = jnp.where(j < n_q, qg_ref[...] * (FOX_HEAD_DIM ** -0.5), kg_ref[...])
        for hh in range(tn // FOX_HEAD_DIM):
            sl = slice(hh * FOX_HEAD_DIM, (hh + 1) * FOX_HEAD_DIM)
            a = acc[:, sl]
            ms = jnp.mean(a * a, axis=-1, keepdims=True)
            main_ref[:, sl] = (a * lax.rsqrt(ms + RMS_EPS) * gain).astype(BF16)

    @pl.when((j >= n_q + n_k) & (j < n_main))
    def _():
        main_ref[...] = proj().astype(BF16)

    @pl.when(j == n_main)
    def _():
        small_ref[...] = proj()
        ft_ref[...] = _dot_nt(wft_ref[...].astype(BF16), h_scr[...])


def _inproj(x2d, scale1, shift1, w_in_t, qg, kg, *, seq, group_rows, lora_row, fgate_row, tm=1024, tn=512):
    t, d = x2d.shape
    fox_w = FOX_HEADS * FOX_HEAD_DIM
    assert all(first % 8 == 0 and count % tn == 0 for first, count in group_rows)
    assert lora_row % 8 == 0 and fgate_row % 8 == 0 and lora_row + tn <= w_in_t.shape[0]
    n_main = sum(count // tn for _, count in group_rows)
    tiles_per_batch = seq // tm
    kern = functools.partial(_inproj_kernel, n_q=fox_w // tn, n_k=fox_w // tn, n_main=n_main)
    pipelined = (_nbytes((tm, d), F32) + _nbytes((tn, d), F32) + _nbytes((tm, tn), BF16)
                 + _nbytes((tm, tn), F32) + _nbytes((8, tm), F32))

    def w_rows(i, j):
        row, hi = jnp.int32(lora_row), n_main
        for first, count in reversed(group_rows):
            lo = hi - count // tn
            row = jnp.where(j < hi, first + tn * (j - lo), row)
            hi = lo
        return pl.multiple_of(row, 8), 0

    return pl.pallas_call(
        kern,
        grid=(t // tm, n_main + 1),
        in_specs=[
            pl.BlockSpec((tm, d), lambda i, j: (jnp.minimum(i + jnp.minimum(j, 1), t // tm - 1), 0)),
            pl.BlockSpec((1, 1, d), lambda i, j: (i // tiles_per_batch, 0, 0)),
            pl.BlockSpec((1, 1, d), lambda i, j: (i // tiles_per_batch, 0, 0)),
            pl.BlockSpec((pl.Element(tn), pl.Element(d)), w_rows),
            pl.BlockSpec((pl.Element(FOX_HEADS), pl.Element(d)), lambda i, j: (fgate_row, 0)),
            pl.BlockSpec((1, FOX_HEAD_DIM), lambda i, j: (0, 0)),
            pl.BlockSpec((1, FOX_HEAD_DIM), lambda i, j: (0, 0)),
        ],
        out_specs=[
            pl.BlockSpec((tm, tn), lambda i, j: (i, jnp.minimum(j, n_main - 1))),
            pl.BlockSpec((tm, tn), lambda i, j: (i, 0)),
            pl.BlockSpec((FOX_HEADS, tm), lambda i, j: (0, i)),
        ],
        out_shape=[
            jax.ShapeDtypeStruct((t, n_main * tn), BF16),
            jax.ShapeDtypeStruct((t, tn), F32),
            jax.ShapeDtypeStruct((FOX_HEADS, t), F32),
        ],
        scratch_shapes=[pltpu.VMEM((tm, d), BF16)],
        compiler_params=pltpu.CompilerParams(
            dimension_semantics=("arbitrary", "arbitrary"),
            vmem_limit_bytes=_vmem_limit(pipelined, _nbytes((tm, d), BF16),
                                         3 * _nbytes((tm, d), F32)),
        ),
        name="inproj",
    )(x2d, scale1, shift1, w_in_t, w_in_t, qg, kg)


def _fox_prep_kernel(ft_ref, bias_ref, o_ref, *, blocks_per_seq):
    x = ft_ref[...] + bias_ref[...]
    lf = jnp.minimum(x, 0.0) - jnp.log1p(jnp.exp(-jnp.abs(x)))
    rows, width = lf.shape
    r = lax.broadcasted_iota(jnp.int32, (width, width), 0)
    c = lax.broadcasted_iota(jnp.int32, (width, width), 1)
    upper = (r <= c).astype(BF16)
    within = sum(_dot(p, upper) for p in _split3(lf))
    tot = jnp.broadcast_to(within[:, width - 1:width], (rows, width))
    rr = lax.broadcasted_iota(jnp.int32, (rows, rows), 0)
    cc = lax.broadcasted_iota(jnp.int32, (rows, rows), 1)
    prior = ((rr // blocks_per_seq == cc // blocks_per_seq) & (cc < rr)).astype(BF16)
    offset = sum(_dot(prior, p) for p in _split3(tot))
    o_ref[...] = within + offset


def _fox_prep(ft_rows, bias_rows, *, blocks_per_seq):
    return pl.pallas_call(
        functools.partial(_fox_prep_kernel, blocks_per_seq=blocks_per_seq),
        out_shape=jax.ShapeDtypeStruct(ft_rows.shape, F32),
        name="fox_prep",
    )(ft_rows, bias_rows)


def _fox_attn_kernel(q_ref, k_ref, v_ref, og_ref, cum_ref, c_ref, wada_ref, bada_ref, o_ref, mod_ref, *, tq):
    _ada_kernel(c_ref, wada_ref, bada_ref, mod_ref)

    seq = q_ref.shape[0]
    n_q = seq // tq
    negcum = -cum_ref[0, 0]
    row = lax.broadcasted_iota(jnp.int32, (tq, tq), 0)
    col = lax.broadcasted_iota(jnp.int32, (tq, tq), 1)
    causal = col <= row

    def logits(qi):
        q0, kend = qi * tq, (qi + 1) * tq
        q = q_ref[q0:kend, :]
        diag = _dot_nt(q, k_ref[q0:kend, :]) + negcum[:, q0:kend]
        diag = jnp.where(causal, diag, -jnp.inf)
        past = _dot_nt(q, k_ref[0:q0, :]) + negcum[:, 0:q0] if qi else None
        return past, diag

    def attend(qi, past, diag):
        q0, kend = qi * tq, (qi + 1) * tq
        m = jnp.max(diag, axis=-1, keepdims=True)
        if past is not None:
            m = jnp.maximum(m, jnp.max(past, axis=-1, keepdims=True))
        p = jnp.exp(diag - m)
        l = jnp.sum(p, axis=-1, keepdims=True)
        o = _dot(p.astype(BF16), v_ref[q0:kend, :])
        if past is not None:
            p = jnp.exp(past - m)
            l = l + jnp.sum(p, axis=-1, keepdims=True)
            o = o + _dot(p.astype(BF16), v_ref[0:q0, :])
        gate = _sigmoid(og_ref[q0:kend, :].astype(F32))
        o_ref[q0:kend, :] = (o / l * gate).astype(BF16)

    nxt = logits(0)
    for qi in range(n_q):
        cur, nxt = nxt, (logits(qi + 1) if qi + 1 < n_q else None)
        attend(qi, *cur)


def _fox_attn(main, cum, c_pad, w_ada, b_ada, ada_col0, *, batch, seq, tq=256):
    t = main.shape[0]
    hd = FOX_HEAD_DIM
    nh = FOX_HEADS
    rows, d = c_pad.shape
    n_ada = w_ada.shape[1] - ada_col0
    tn = n_ada // (batch * nh)
    assert n_ada % (batch * nh) == 0 and tn % LANES == 0 and ada_col0 % tn == 0
    slab = lambda r: pl.BlockSpec((r, tn), lambda b, h: (0, ada_col0 // tn + b * nh + h))
    blk = lambda off: pl.BlockSpec((seq, hd), lambda b, h: (b, off + h))
    pipelined = 5 * _nbytes((seq, hd), BF16) + _nbytes((8, seq), F32) + _nbytes((d, tn), F32)
    return pl.pallas_call(
        functools.partial(_fox_attn_kernel, tq=tq),
        grid=(batch, nh),
        in_specs=[blk(0), blk(nh), blk(2 * nh), blk(3 * nh),
                  pl.BlockSpec((1, 1, 1, seq), lambda b, h: (h, b, 0, 0)),
                  pl.BlockSpec((rows, d), lambda b, h: (0, 0)), slab(d), slab(1)],
        out_specs=[pl.BlockSpec((seq, hd), lambda b, h: (b, h)),
                   pl.BlockSpec((rows, tn), lambda b, h: (0, b * nh + h))],
        out_shape=[jax.ShapeDtypeStruct((t, nh * hd), BF16), jax.ShapeDtypeStruct((rows, n_ada), F32)],
        compiler_params=pltpu.CompilerParams(
            dimension_semantics=("arbitrary", "arbitrary"),
            vmem_limit_bytes=VMEM_CAP_BYTES,
        ),
        name="fox_attn",
    )(main, main, main, main, cum, c_pad, w_ada, b_ada)


def _run_stages(gen, side=None):
    result = {}
    live = [g for g in (gen, side) if g is not None]
    while live:
        for g in list(live):
            try:
                next(g)
            except StopIteration as stop:
                result[g] = stop.value
                live.remove(g)
    return result[gen if side is None else side]


def _rwkv_inputs(raw_refs, small_ref, mu_refs, mu_s_ref, w0_ref, w2_ref, a0_ref, a2_ref, g2_ref,
                 prev_refs, prev_s, seqs):
    def shift_mix(x, prev_ref, q, mu):
        rows = x.shape[0]
        xp = pltpu.roll(x, 1, axis=0)
        row = lax.broadcasted_iota(jnp.int32, x.shape, 0)
        xp = jnp.where(row == 0, prev_ref[q, 0:1, :], xp)
        prev_ref[q, 0:1, :] = x[rows - 1:rows, :]
        return x + (xp - x) * mu

    r, k, v = ([shift_mix(ref[q].astype(F32), prev, q, mu[...]) for q in seqs]
               for ref, prev, mu in zip(raw_refs, prev_refs, mu_refs))
    sm = jnp.concatenate([shift_mix(small_ref[q], prev_s, q, mu_s_ref[...]) for q in seqs], axis=0)
    xw = sm[:, 0:LORA_PAD]
    xa = sm[:, DECAY_LORA:DECAY_LORA + LORA_PAD]
    xg = sm[:, DECAY_LORA + AAA_LORA:DECAY_LORA + AAA_LORA + GATE_LORA]

    z = w0_ref[...] + _dot(jnp.tanh(xw).astype(BF16), w2_ref[...])
    w_raw = -(jnp.maximum(-z, 0.0) + jnp.log(1.0 + jnp.exp(-jnp.abs(z)))) - 0.5
    lw = -jnp.exp(w_raw)
    a = _sigmoid(a0_ref[...] + _dot(xa.astype(BF16), a2_ref[...]))
    g = _dot(_sigmoid(xg).astype(BF16), g2_ref[...])
    ch = lw.shape[0] // len(seqs)
    per_seq = lambda x: [x[j * ch:(j + 1) * ch, :] for j in range(len(seqs))]
    return r, k, v, per_seq(lw), per_seq(a), per_seq(g)


def _rwkv_kernel(r_ref, k_ref, v_ref, small_ref, mu_r_ref, mu_k_ref, mu_v_ref, mu_s_ref,
                 w0_ref, w2_ref, a0_ref, a2_ref, g2_ref,
                 kk_ref, ka_ref, rk_ref, lng_ref, lnb_ref, *rest, n_tiles, seq_groups, n_cast):
    cast_src, y_ref, cast_dst = rest[:n_cast], rest[n_cast], rest[n_cast + 1:2 * n_cast + 1]
    state, prev_r, prev_k, prev_v, prev_s = rest[2 * n_cast + 1:]
    for src, dst in zip(cast_src, cast_dst):
        dst[...] = src[...].astype(BF16)

    @pl.when(pl.program_id(1) == 0)
    def _():
        for ref in (state, prev_r, prev_k, prev_v, prev_s):
            ref[...] = jnp.zeros_like(ref)

    ch = y_ref.shape[1]
    hd = RWKV_HEAD_DIM
    lane = lax.broadcasted_iota(jnp.int32, (1, LANES), 1)
    m0 = lane < hd
    n2 = HEADS_PER_TILE * ch
    rr = lax.broadcasted_iota(jnp.int32, (n2, n2), 0)
    cc = lax.broadcasted_iota(jnp.int32, (n2, n2), 1)
    same = (rr // ch) == (cc // ch)
    strict = same & (rr > cc)
    incl = same & (rr >= cc)
    lr = lax.broadcasted_iota(jnp.int32, (ch, ch), 0)
    lc = lax.broadcasted_iota(jnp.int32, (ch, ch), 1)
    lower = (lc <= lr).astype(BF16)

    def head_sum(x):
        s0 = jnp.sum(jnp.where(m0, x, 0.0), axis=-1, keepdims=True)
        s1 = jnp.sum(jnp.where(m0, 0.0, x), axis=-1, keepdims=True)
        return jnp.where(m0, s0, s1)

    def stack(x):
        return jnp.concatenate([jnp.where(m0, x, 0.0), jnp.where(m0, 0.0, x)], axis=0)

    sls = [slice(p * LANES, (p + 1) * LANES) for p in range(n_tiles)]

    def each(fn, *cols):
        return [fn(*args) for args in zip(*cols)]

    stack2 = lambda x, y: jnp.concatenate([stack(x), stack(y)], axis=0).astype(BF16)
    incl2 = jnp.concatenate([incl, incl], axis=1)

    def prepare(seqs):
        per_chain = lambda ref: [ref[:, sl] for _ in seqs for sl in sls]
        full = _rwkv_inputs((r_ref, k_ref, v_ref), small_ref, (mu_r_ref, mu_k_ref, mu_v_ref), mu_s_ref,
                            w0_ref, w2_ref, a0_ref, a2_ref, g2_ref, (prev_r, prev_k, prev_v), prev_s, seqs)
        r, k, v, lw, a, g = ([x[j][:, sl] for j in range(len(seqs)) for sl in sls] for x in full)
        yield
        kk = each(lambda k_, w_: k_ * w_, k, per_chain(kk_ref))
        kk = each(lambda x: x * lax.rsqrt(jnp.maximum(head_sum(x * x), 1e-24)), kk)
        kr = each(lambda k_, a_, w_: k_ * (1.0 + (a_ - 1.0) * w_), k, a, per_chain(ka_ref))
        yield
        cum = each(lambda x: sum(_dot(lower, part) for part in _split3(x)[:2]), lw)
        yield
        gam = each(jnp.exp, cum)
        ginv = each(lambda x: jnp.exp(-x), cum)
        g_end = each(lambda x: x[ch - 1:ch, :], gam)
        yield
        a_t = each(lambda kk_, c_, lw_: -kk_ * jnp.exp(c_ - lw_), kk, cum, lw)
        r_t = each(lambda r_, g_: r_ * g_, r, gam)
        lhs = each(stack2, a_t, r_t)
        yield
        b_h = each(lambda kk_, a_, gi_: kk_ * a_ * gi_, kk, a, ginv)
        k_h = each(lambda kr_, gi_: kr_ * gi_, kr, ginv)
        rhs = each(stack2, b_h, k_h)
        yield
        vs = each(lambda v_: stack(v_).astype(BF16), v)
        bk = each(lambda b_, k_, ge_: stack2(b_ * ge_, k_ * ge_), b_h, k_h, g_end)
        yield
        bonus = each(lambda r_, kr_, w_, v_: head_sum(r_ * kr_ * w_) * v_, r, kr, per_chain(rk_ref), v)
        return lhs, rhs, vs, bk, g_end, bonus, g

    def advance(seqs, prepared):
        lhs, rhs, vs, bk, g_end, bonus, g = prepared
        per_chain = lambda ref: [ref[:, sl] for _ in seqs for sl in sls]
        slots = [q * n_tiles + p for q in seqs for p in range(n_tiles)]
        sc = each(_dot_nt, lhs, rhs)
        yield
        ab = each(lambda s: jnp.where(strict, s[0:n2, 0:n2], 0.0).astype(BF16), sc)
        ak = each(lambda s: jnp.where(strict, s[0:n2, n2:2 * n2], 0.0).astype(BF16), sc)
        rbk = each(lambda s: jnp.where(incl2, s[n2:2 * n2, :], 0.0).astype(BF16), sc)
        yield
        st = [state[i] for i in slots]
        ah = each(lambda l_, s_: _dot_nt(l_, s_.astype(BF16)), lhs, st)
        yield
        x = each(lambda ah_, ak_, vs_: ah_[0:n2] + _dot(ak_, vs_), ah, ak, vs)
        yield
        pw = ab
        n_lvl = ch.bit_length() - 1
        for lvl in range(n_lvl):
            x = each(lambda x_, p_: x_ + _dot(p_, x_.astype(BF16)), x, pw)
            yield
            if lvl + 1 < n_lvl:
                pw = each(lambda p_: _dot(p_, p_).astype(BF16), pw)
                yield
        uv = each(lambda x_, vs_: jnp.concatenate([x_.astype(BF16), vs_], axis=0), x, vs)
        ys = each(lambda ah_, m_, uv_: ah_[n2:2 * n2] + _dot(m_, uv_), ah, rbk, uv)
        yield
        y = each(lambda ys_: ys_[0:ch] + ys_[ch:n2], ys)
        new_st = each(lambda s_, ge_, uv_, bk_: s_ * ge_ + _dot_tn(uv_, bk_), st, g_end, uv, bk)
        for i, s_new in zip(slots, new_st):
            state[i] = s_new
        yield
        mu = each(lambda y_: head_sum(y_) * (1.0 / hd), y)
        yc = each(lambda y_, m_: y_ - m_, y, mu)
        var = each(lambda c_: head_sum(c_ * c_) * (1.0 / hd), yc)
        yield
        yn = each(lambda c_, v_, g_, b_: c_ * lax.rsqrt(v_ + RWKV_LNX_EPS) * g_ + b_,
                  yc, var, per_chain(lng_ref), per_chain(lnb_ref))
        for j, i in enumerate(slots):
            y_ref[i // n_tiles, :, sls[i % n_tiles]] = ((yn[j] + bonus[j]) * g[j]).astype(BF16)

    starts = [sum(seq_groups[:i]) for i in range(len(seq_groups))]
    groups = [tuple(range(s, s + n)) for s, n in zip(starts, seq_groups)]
    prepared = _run_stages(prepare(groups[0]))
    for gi, seqs in enumerate(groups):
        nxt = prepare(groups[gi + 1]) if gi + 1 < len(groups) else None
        prepared = _run_stages(advance(seqs, prepared), nxt)


def _slab_spec(w, n_steps, step_of):
    rep = 1
    while w.shape[0] % (n_steps // rep * 16):
        rep *= 2
        assert rep <= n_steps
    return pl.BlockSpec((w.shape[0] // (n_steps // rep), w.shape[1]),
                        lambda *idx: (step_of(*idx) // rep, 0))


def _rwkv(main, small, mus, mu_s, w0, w2p, a0, a2p, g2, head_rows, cast_weights, *, batch, seq,
          seq_groups=(2, 2)):
    n_seq = sum(seq_groups)
    t = main.shape[0]
    width = RWKV_HEADS * RWKV_HEAD_DIM
    nsm = small.shape[1]
    col0 = (4 * FOX_HEADS * FOX_HEAD_DIM) // width
    ch = RWKV_CHUNK
    steps = seq // ch
    n_tiles = width // LANES
    assert batch % n_seq == 0
    groups = batch // n_seq
    grouped = lambda x: x.reshape(groups, n_seq, seq, x.shape[-1])
    act = lambda cols, blk: pl.BlockSpec((None, n_seq, ch, cols), lambda b, c: (b, 0, c, blk))
    row = lambda n: pl.BlockSpec((1, n), lambda b, c: (0, 0))
    whole = lambda w: pl.BlockSpec(w.shape, lambda b, c: (0, 0))
    slabs = [_slab_spec(w, groups * steps, lambda b, c: b * steps + c) for w in cast_weights]
    pipelined = n_seq * (4 * _nbytes((ch, width), BF16) + _nbytes((ch, nsm), F32))
    pipelined += sum(_nbytes(s.block_shape, F32) + _nbytes(s.block_shape, BF16) for s in slabs)
    pipelined += sum(_nbytes(w.shape, BF16) for w in (w2p, a2p, g2))
    scratch = [(n_seq * n_tiles, LANES, LANES)] + [(n_seq, 8, width)] * 3 + [(n_seq, 8, nsm)]
    y, *copies = pl.pallas_call(
        functools.partial(_rwkv_kernel, n_tiles=n_tiles, seq_groups=seq_groups, n_cast=len(slabs)),
        grid=(groups, steps),
        in_specs=[act(width, col0), act(width, col0 + 1), act(width, col0 + 2), act(nsm, 0),
                  row(width), row(width), row(width), row(nsm),
                  row(width), whole(w2p), row(width), whole(a2p), whole(g2)]
        + [row(width)] * 5 + slabs,
        out_specs=[act(width, 0)] + slabs,
        out_shape=[jax.ShapeDtypeStruct((groups, n_seq, seq, width), BF16)]
        + [jax.ShapeDtypeStruct(w.shape, BF16) for w in cast_weights],
        scratch_shapes=[pltpu.VMEM(s, F32) for s in scratch],
        compiler_params=pltpu.CompilerParams(
            dimension_semantics=("arbitrary", "arbitrary"),
            vmem_limit_bytes=_vmem_limit(pipelined, sum(_nbytes(s, F32) for s in scratch), 24 << 20),
        ),
        name="rwkv",
    )(grouped(main), grouped(main), grouped(main), grouped(small), *mus, mu_s, w0, w2p, a0, a2p, g2,
      *head_rows, *cast_weights)
    return y.reshape(t, width), copies


def _mixout_kernel(o_ref, y_ref, ga0_ref, ga1_ref, gb0_ref, gb1_ref, wa_ref, wb_ref, wo_ref, x_ref,
                   gate_ref, lng_ref, lnb_ref, scale_ref, shift_ref, x1_ref, h2_ref, *, n_sub):
    tm, d = x_ref.shape
    rows = tm // n_sub
    half = d // 2
    sl = [slice(s * rows, (s + 1) * rows) for s in range(n_sub)]

    def branches(s):
        return _dot(o_ref[sl[s], :], wa_ref[...]), _dot(y_ref[sl[s], :], wb_ref[...])

    def mixed(s, ua, ub):
        parts = []
        for c, (ga, gb) in enumerate(((ga0_ref, gb0_ref), (ga1_ref, gb1_ref))):
            cs = slice(c * half, (c + 1) * half)
            parts.append(_sigmoid(ga[sl[s], :].astype(F32)) * ua[:, cs]
                         + _sigmoid(gb[sl[s], :].astype(F32)) * ub[:, cs])
        return _dot(jnp.concatenate(parts, axis=1).astype(BF16), wo_ref[...])

    def norms(s, mix):
        z = DEEPNORM_ALPHA * x_ref[sl[s], :] + gate_ref[0] * mix
        x1 = _layernorm(z) * lng_ref[...] + lnb_ref[...]
        x1_ref[sl[s], :] = x1
        h2_ref[sl[s], :] = (_layernorm(x1) * (1.0 + scale_ref[0]) + shift_ref[0]).astype(BF16)

    u = branches(0)
    mix_prev = None
    for s in range(n_sub):
        u_next = branches(s + 1) if s + 1 < n_sub else None
        mix = mixed(s, *u)
        if mix_prev is not None:
            norms(s - 1, mix_prev)
        u, mix_prev = u_next, mix
    norms(n_sub - 1, mix_prev)


def _mixout(o, y, main, wa, wb, wo, x2d, gate1, ln_g, ln_b, scale2, shift2, *, seq, tm=256, n_sub=2):
    t, d = x2d.shape
    kdim = o.shape[1]
    half = d // 2
    gate_blk0 = (main.shape[1] - 2 * d) // half
    tiles_per_batch = seq // tm
    mod = pl.BlockSpec((1, 1, d), lambda i: (i // tiles_per_batch, 0, 0))
    row = pl.BlockSpec((1, d), lambda i: (0, 0))
    gate = lambda c: pl.BlockSpec((tm, half), lambda i: (i, gate_blk0 + c))
    resident = lambda w: pl.BlockSpec(w.shape, lambda i: (0, 0), pipeline_mode=pl.Buffered(1))
    pipelined = (2 * _nbytes((tm, kdim), BF16) + 4 * _nbytes((tm, half), BF16)
                 + 2 * _nbytes((tm, d), F32) + _nbytes((tm, d), BF16))
    weights = _nbytes(wa.shape, BF16) + _nbytes(wb.shape, BF16) + _nbytes(wo.shape, BF16)
    return pl.pallas_call(
        functools.partial(_mixout_kernel, n_sub=n_sub),
        grid=(t // tm,),
        in_specs=[
            pl.BlockSpec((tm, kdim), lambda i: (i, 0)),
            pl.BlockSpec((tm, kdim), lambda i: (i, 0)),
            gate(0), gate(1), gate(2), gate(3),
            resident(wa), resident(wb), resident(wo),
            pl.BlockSpec((tm, d), lambda i: (i, 0)),
            mod, row, row, mod, mod,
        ],
        out_specs=[pl.BlockSpec((tm, d), lambda i: (i, 0)),
                   pl.BlockSpec((tm, d), lambda i: (i, 0))],
        out_shape=[jax.ShapeDtypeStruct((t, d), F32), jax.ShapeDtypeStruct((t, d), BF16)],
        compiler_params=pltpu.CompilerParams(
            dimension_semantics=("arbitrary",),
            vmem_limit_bytes=_vmem_limit(pipelined, weights, 8 * _nbytes((tm, d), F32)),
        ),
        name="mixout",
    )(o, y, main, main, main, main, wa, wb, wo, x2d, gate1, ln_g, ln_b, scale2, shift2)


def _ffn_kernel(h_ref, wg_ref, wu_ref, wd_ref, x1_ref, gate_ref, lng_ref, lnb_ref, o_ref, acc):
    f = pl.program_id(1)
    @pl.when(f == 0)
    def _():
        acc[...] = jnp.zeros_like(acc)

    h = h_ref[...]
    g = _dot(h, wg_ref[...])
    u = _dot(h, wu_ref[...])
    act = (g * _sigmoid(g) * u).astype(BF16)
    acc[...] += _dot(act, wd_ref[...])

    @pl.when(f == pl.num_programs(1) - 1)
    def _():
        z = DEEPNORM_ALPHA * x1_ref[...] + gate_ref[0] * acc[...]
        o_ref[...] = _layernorm(z) * lng_ref[...] + lnb_ref[...]


def _ffn(h2, w_gu, w_down, x1, gate2, ln_g, ln_b, *, seq, tm=512, tf=512):
    t, d = x1.shape
    d_ff = w_down.shape[0]
    nf = d_ff // tf
    tiles_per_batch = seq // tm
    mod = pl.BlockSpec((1, 1, d), lambda i, f: (i // tiles_per_batch, 0, 0))
    row = pl.BlockSpec((1, d), lambda i, f: (0, 0))
    pipelined = (_nbytes((tm, d), BF16) + 3 * _nbytes((d, tf), BF16) + 2 * _nbytes((tm, d), F32))
    return pl.pallas_call(
        _ffn_kernel,
        grid=(t // tm, nf),
        in_specs=[
            pl.BlockSpec((tm, d), lambda i, f: (i, 0)),
            pl.BlockSpec((d, tf), lambda i, f: (0, f)),
            pl.BlockSpec((d, tf), lambda i, f: (0, nf + f)),
            pl.BlockSpec((tf, d), lambda i, f: (f, 0)),
            pl.BlockSpec((tm, d), lambda i, f: (jnp.maximum(i - 1 + jnp.minimum(f, 1), 0), 0)),
            mod, row, row,
        ],
        out_specs=pl.BlockSpec((tm, d), lambda i, f: (i, 0)),
        out_shape=jax.ShapeDtypeStruct((t, d), F32),
        scratch_shapes=[pltpu.VMEM((tm, d), F32)],
        compiler_params=pltpu.CompilerParams(
            dimension_semantics=("arbitrary", "arbitrary"),
            vmem_limit_bytes=_vmem_limit(pipelined, _nbytes((tm, d), F32),
                                         2 * _nbytes((tm, d), F32) + 4 * _nbytes((tm, tf), F32)),
        ),
        name="ffn",
    )(h2, w_gu, w_gu, w_down, x1, gate2, ln_g, ln_b)


def _pad_rows(w, rows):
    return jnp.zeros((rows, w.shape[1]), w.dtype).at[:w.shape[0]].set(w)


def kernel(x, c, w_ada, b_ada, w_in, b_fgate, q_norm_g, k_norm_g, rwkv_mu, rwkv_w0, rwkv_w2, rwkv_a0, rwkv_a2, rwkv_g2, rwkv_k_k, rwkv_k_a, rwkv_r_k, rwkv_lnx_g, rwkv_lnx_b, w_branch_a, w_branch_b, w_out, ln1_g, ln1_b, w_ffn_gu, w_ffn_down, ln2_g, ln2_b):
    batch, seq, d = x.shape
    t = batch * seq
    fox_w = FOX_HEADS * FOX_HEAD_DIM
    rw_w = RWKV_HEADS * RWKV_HEAD_DIM
    fox_cols = 4 * fox_w + FOX_HEADS
    lora0 = fox_cols + 3 * rw_w
    gate0 = lora0 + DECAY_LORA + AAA_LORA + GATE_LORA
    assert seq % RWKV_CHUNK == 0 and w_in.shape[1] == gate0 + 2 * d

    lora_w = gate0 - lora0
    mu_r = rwkv_mu[None, 0:rw_w]
    mu_k = rwkv_mu[None, rw_w:2 * rw_w]
    mu_v = rwkv_mu[None, 2 * rw_w:3 * rw_w]
    mu_s = jnp.zeros((1, INPROJ_TN), rwkv_mu.dtype).at[0, :lora_w].set(rwkv_mu[3 * rw_w:])
    w2p = _pad_rows(rwkv_w2, LORA_PAD).astype(BF16)
    a2p = _pad_rows(rwkv_a2, LORA_PAD).astype(BF16)
    g2 = rwkv_g2.astype(BF16)
    row = lambda v: v.reshape(1, -1)

    c_pad = jnp.zeros((8, d), c.dtype).at[:batch].set(c)
    mod_head = _ada(c_pad, w_ada, row(b_ada), 2 * d)[:batch].reshape(batch, 2, 1, d)
    shift1, scale1 = mod_head[:, 0], mod_head[:, 1]

    x2d = x.reshape(t, d)
    main, small, ft = _inproj(x2d, scale1, shift1, w_in.T, row(q_norm_g), row(k_norm_g), seq=seq,
                              group_rows=[(0, 4 * fox_w), (fox_cols, 3 * rw_w), (gate0, 2 * d)],
                              lora_row=lora0, fgate_row=4 * fox_w, tn=INPROJ_TN)

    blocks_per_seq = seq // LANES
    ft_rows = ft.reshape(FOX_HEADS * batch * blocks_per_seq, LANES)
    bias_rows = jnp.broadcast_to(b_fgate[:, None, None], (FOX_HEADS, batch * blocks_per_seq, LANES))
    cum = _fox_prep(ft_rows, bias_rows.reshape(ft_rows.shape), blocks_per_seq=blocks_per_seq)
    cum = cum.reshape(FOX_HEADS, batch, 1, seq)
    o, mod_tail = _fox_attn(main, cum, c_pad, w_ada, row(b_ada), 2 * d, batch=batch, seq=seq)
    gate1, shift2, scale2, gate2 = (mod_tail[:batch].reshape(batch, 4, 1, d)[:, i] for i in range(4))

    y, (wa, wb, wo, wgu, wdn) = _rwkv(
        main, small, (mu_r, mu_k, mu_v), mu_s, row(rwkv_w0), w2p, row(rwkv_a0), a2p, g2,
        [row(p) for p in (rwkv_k_k, rwkv_k_a, rwkv_r_k, rwkv_lnx_g, rwkv_lnx_b)],
        [w_branch_a, w_branch_b, w_out, w_ffn_gu, w_ffn_down], batch=batch, seq=seq)

    x1, h2 = _mixout(o, y, main, wa, wb, wo, x2d, gate1, row(ln1_g), row(ln1_b), scale2, shift2, seq=seq)

    out = _ffn(h2, wgu, wdn, x1, gate2, row(ln2_g), row(ln2_b), seq=seq)
    return out.reshape(batch, seq, d)
```

```python
import functools

import jax
import jax.numpy as jnp
from jax import lax
from jax.experimental import pallas as pl
from jax.experimental.pallas import tpu as pltpu

F32 = jnp.float32
BF16 = jnp.bfloat16

LANES = 128
VMEM_CAP_BYTES = 60000 * 1024

FOX_HEADS = 8
FOX_HEAD_DIM = 128
RWKV_HEADS = 16
RWKV_HEAD_DIM = 64
DECAY_LORA = 96
AAA_LORA = 96
GATE_LORA = 256
LORA_PAD = 128
INPROJ_TN = 512
RWKV_LNX_EPS = 64e-5
DEPTH = 1
DEEPNORM_ALPHA = (2.0 * DEPTH) ** 0.25
LN_EPS = 1e-5
RMS_EPS = 1e-6
RWKV_CHUNK = 64
HEADS_PER_TILE = LANES // RWKV_HEAD_DIM


def _vmem_limit(pipelined_bytes, resident_bytes=0, temp_bytes=0):
    need = 2 * pipelined_bytes + resident_bytes + temp_bytes + (2 << 20)
    return int(min(VMEM_CAP_BYTES, need))


def _nbytes(shape, dtype):
    n = 1
    for s in shape:
        n *= s
    return n * jnp.dtype(dtype).itemsize


def _layernorm(x):
    mu = jnp.mean(x, axis=-1, keepdims=True)
    xc = x - mu
    var = jnp.mean(xc * xc, axis=-1, keepdims=True)
    return xc * lax.rsqrt(var + LN_EPS)


def _sigmoid(x):
    return 1.0 / (1.0 + jnp.exp(-x))


def _dot(a, b):
    return jnp.dot(a, b, preferred_element_type=F32)


def _dot_nt(a, b):
    return lax.dot_general(a, b, (((1,), (1,)), ((), ())), preferred_element_type=F32)


def _dot_tn(a, b):
    return lax.dot_general(a, b, (((0,), (0,)), ((), ())), preferred_element_type=F32)


def _split3(x):
    hi = x.astype(BF16)
    r1 = x - hi.astype(F32)
    mid = r1.astype(BF16)
    lo = (r1 - mid.astype(F32)).astype(BF16)
    return hi, mid, lo


def _ada_kernel(c_ref, w_ref, b_ref, o_ref):
    c = c_ref[...]
    s = (c * _sigmoid(c)).astype(BF16)
    o_ref[...] = _dot(s, w_ref[...].astype(BF16)) + b_ref[...]


def _ada(c_pad, w_ada, b_ada, n, *, tn=1024):
    rows, d = c_pad.shape
    return pl.pallas_call(
        _ada_kernel,
        grid=(n // tn,),
        in_specs=[
            pl.BlockSpec((rows, d), lambda j: (0, 0)),
            pl.BlockSpec((d, tn), lambda j: (0, j)),
            pl.BlockSpec((1, tn), lambda j: (0, j)),
        ],
        out_specs=pl.BlockSpec((rows, tn), lambda j: (0, j)),
        out_shape=jax.ShapeDtypeStruct((rows, n), F32),
        compiler_params=pltpu.CompilerParams(
            dimension_semantics=("arbitrary",),
            vmem_limit_bytes=_vmem_limit(_nbytes((d, tn), F32), temp_bytes=_nbytes((d, tn), BF16)),
        ),
        name="ada",
    )(c_pad, w_ada, b_ada)


def _inproj_kernel(x_ref, scale_ref, shift_ref, wt_ref, wft_ref, qg_ref, kg_ref,
                   main_ref, small_ref, ft_ref, h_scr, *, n_q, n_k, n_main):
    j = pl.program_id(1)
    tn = main_ref.shape[1]

    @pl.when(j == 0)
    def _():
        h = _layernorm(x_ref[...]) * (1.0 + scale_ref[0]) + shift_ref[0]
        h_scr[...] = h.astype(BF16)

    def proj():
        w = wt_ref[...].astype(BF16)
        half = h_scr.shape[0] // 2
        return jnp.concatenate([_dot_nt(h_scr[0:half, :], w), _dot_nt(h_scr[half:, :], w)], axis=0)

    @pl.when(j < n_q + n_k)
    def _():
        acc = proj()
        gain = jnp.where(j < n_q, qg_ref[...] * (FOX_HEAD_DIM ** -0.5), kg_ref[...])
        for hh in range(tn // FOX_HEAD_DIM):
            sl = slice(hh * FOX_HEAD_DIM, (hh + 1) * FOX_HEAD_DIM)
            a = acc[:, sl]
            ms = jnp.mean(a * a, axis=-1, keepdims=True)
            main_ref[:, sl] = (a * lax.rsqrt(ms + RMS_EPS) * gain).astype(BF16)

    @pl.when((j >= n_q + n_k) & (j < n_main))
    def _():
        main_ref[...] = proj().astype(BF16)

    @pl.when(j == n_main)
    def _():
        small_ref[...] = proj()
        ft_ref[...] = _dot_nt(wft_ref[...].astype(BF16), h_scr[...])


def _inproj(x2d, scale1, shift1, w_in_t, qg, kg, *, seq, group_rows, lora_row, fgate_row, tm=1024, tn=512):
    t, d = x2d.shape
    fox_w = FOX_HEADS * FOX_HEAD_DIM
    assert all(first % 8 == 0 and count % tn == 0 for first, count in group_rows)
    assert lora_row % 8 == 0 and fgate_row % 8 == 0 and lora_row + tn <= w_in_t.shape[0]
    n_main = sum(count // tn for _, count in group_rows)
    tiles_per_batch = seq // tm
    kern = functools.partial(_inproj_kernel, n_q=fox_w // tn, n_k=fox_w // tn, n_main=n_main)
    pipelined = (_nbytes((tm, d), F32) + _nbytes((tn, d), F32) + _nbytes((tm, tn), BF16)
                 + _nbytes((tm, tn), F32) + _nbytes((8, tm), F32))

    def w_rows(i, j):
        row, hi = jnp.int32(lora_row), n_main
        for first, count in reversed(group_rows):
            lo = hi - count // tn
            row = jnp.where(j < hi, first + tn * (j - lo), row)
            hi = lo
        return pl.multiple_of(row, 8), 0

    return pl.pallas_call(
        kern,
        grid=(t // tm, n_main + 1),
        in_specs=[
            pl.BlockSpec((tm, d), lambda i, j: (jnp.minimum(i + jnp.minimum(j, 1), t // tm - 1), 0)),
            pl.BlockSpec((1, 1, d), lambda i, j: (i // tiles_per_batch, 0, 0)),
            pl.BlockSpec((1, 1, d), lambda i, j: (i // tiles_per_batch, 0, 0)),
            pl.BlockSpec((pl.Element(tn), pl.Element(d)), w_rows),
            pl.BlockSpec((pl.Element(FOX_HEADS), pl.Element(d)), lambda i, j: (fgate_row, 0)),
            pl.BlockSpec((1, FOX_HEAD_DIM), lambda i, j: (0, 0)),
            pl.BlockSpec((1, FOX_HEAD_DIM), lambda i, j: (0, 0)),
        ],
        out_specs=[
            pl.BlockSpec((tm, tn), lambda i, j: (i, jnp.minimum(j, n_main - 1))),
            pl.BlockSpec((tm, tn), lambda i, j: (i, 0)),
            pl.BlockSpec((FOX_HEADS, tm), lambda i, j: (0, i)),
        ],
        out_shape=[
            jax.ShapeDtypeStruct((t, n_main * tn), BF16),
            jax.ShapeDtypeStruct((t, tn), F32),
            jax.ShapeDtypeStruct((FOX_HEADS, t), F32),
        ],
        scratch_shapes=[pltpu.VMEM((tm, d), BF16)],
        compiler_params=pltpu.CompilerParams(
            dimension_semantics=("arbitrary", "arbitrary"),
            vmem_limit_bytes=_vmem_limit(pipelined, _nbytes((tm, d), BF16),
                                         3 * _nbytes((tm, d), F32)),
        ),
        name="inproj",
    )(x2d, scale1, shift1, w_in_t, w_in_t, qg, kg)


def _fox_prep_kernel(ft_ref, bias_ref, o_ref, *, blocks_per_seq):
    x = ft_ref[...] + bias_ref[...]
    lf = jnp.minimum(x, 0.0) - jnp.log1p(jnp.exp(-jnp.abs(x)))
    rows, width = lf.shape
    r = lax.broadcasted_iota(jnp.int32, (width, width), 0)
    c = lax.broadcasted_iota(jnp.int32, (width, width), 1)
    upper = (r <= c).astype(BF16)
    within = sum(_dot(p, upper) for p in _split3(lf))
    tot = jnp.broadcast_to(within[:, width - 1:width], (rows, width))
    rr = lax.broadcasted_iota(jnp.int32, (rows, rows), 0)
    cc = lax.broadcasted_iota(jnp.int32, (rows, rows), 1)
    prior = ((rr // blocks_per_seq == cc // blocks_per_seq) & (cc < rr)).astype(BF16)
    offset = sum(_dot(prior, p) for p in _split3(tot))
    o_ref[...] = within + offset


def _fox_prep(ft_rows, bias_rows, *, blocks_per_seq):
    return pl.pallas_call(
        functools.partial(_fox_prep_kernel, blocks_per_seq=blocks_per_seq),
        out_shape=jax.ShapeDtypeStruct(ft_rows.shape, F32),
        name="fox_prep",
    )(ft_rows, bias_rows)


def _fox_attn_kernel(q_ref, k_ref, v_ref, og_ref, cum_ref, c_ref, wada_ref, bada_ref, o_ref, mod_ref, *, tq):
    _ada_kernel(c_ref, wada_ref, bada_ref, mod_ref)

    seq = q_ref.shape[0]
    n_q = seq // tq
    negcum = -cum_ref[0, 0]
    row = lax.broadcasted_iota(jnp.int32, (tq, tq), 0)
    col = lax.broadcasted_iota(jnp.int32, (tq, tq), 1)
    causal = col <= row

    def logits(qi):
        q0, kend = qi * tq, (qi + 1) * tq
        q = q_ref[q0:kend, :]
        diag = _dot_nt(q, k_ref[q0:kend, :]) + negcum[:, q0:kend]
        diag = jnp.where(causal, diag, -jnp.inf)
        past = _dot_nt(q, k_ref[0:q0, :]) + negcum[:, 0:q0] if qi else None
        return past, diag

    def attend(qi, past, diag):
        q0, kend = qi * tq, (qi + 1) * tq
        m = jnp.max(diag, axis=-1, keepdims=True)
        if past is not None:
            m = jnp.maximum(m, jnp.max(past, axis=-1, keepdims=True))
        p = jnp.exp(diag - m)
        l = jnp.sum(p, axis=-1, keepdims=True)
        o = _dot(p.astype(BF16), v_ref[q0:kend, :])
        if past is not None:
            p = jnp.exp(past - m)
            l = l + jnp.sum(p, axis=-1, keepdims=True)
            o = o + _dot(p.astype(BF16), v_ref[0:q0, :])
        gate = _sigmoid(og_ref[q0:kend, :].astype(F32))
        o_ref[q0:kend, :] = (o / l * gate).astype(BF16)

    nxt = logits(0)
    for qi in range(n_q):
        cur, nxt = nxt, (logits(qi + 1) if qi + 1 < n_q else None)
        attend(qi, *cur)


def _fox_attn(main, cum, c_pad, w_ada, b_ada, ada_col0, *, batch, seq, tq=256):
    t = main.shape[0]
    hd = FOX_HEAD_DIM
    nh = FOX_HEADS
    rows, d = c_pad.shape
    n_ada = w_ada.shape[1] - ada_col0
    tn = n_ada // (batch * nh)
    assert n_ada % (batch * nh) == 0 and tn % LANES == 0 and ada_col0 % tn == 0
    slab = lambda r: pl.BlockSpec((r, tn), lambda b, h: (0, ada_col0 // tn + b * nh + h))
    blk = lambda off: pl.BlockSpec((seq, hd), lambda b, h: (b, off + h))
    pipelined = 5 * _nbytes((seq, hd), BF16) + _nbytes((8, seq), F32) + _nbytes((d, tn), F32)
    return pl.pallas_call(
        functools.partial(_fox_attn_kernel, tq=tq),
        grid=(batch, nh),
        in_specs=[blk(0), blk(nh), blk(2 * nh), blk(3 * nh),
                  pl.BlockSpec((1, 1, 1, seq), lambda b, h: (h, b, 0, 0)),
                  pl.BlockSpec((rows, d), lambda b, h: (0, 0)), slab(d), slab(1)],
        out_specs=[pl.BlockSpec((seq, hd), lambda b, h: (b, h)),
                   pl.BlockSpec((rows, tn), lambda b, h: (0, b * nh + h))],
        out_shape=[jax.ShapeDtypeStruct((t, nh * hd), BF16), jax.ShapeDtypeStruct((rows, n_ada), F32)],
        compiler_params=pltpu.CompilerParams(
            dimension_semantics=("arbitrary", "arbitrary"),
            vmem_limit_bytes=VMEM_CAP_BYTES,
        ),
        name="fox_attn",
    )(main, main, main, main, cum, c_pad, w_ada, b_ada)


def _run_stages(gen, side=None):
    result = {}
    live = [g for g in (gen, side) if g is not None]
    while live:
        for g in list(live):
            try:
                next(g)
            except StopIteration as stop:
                result[g] = stop.value
                live.remove(g)
    return result[gen if side is None else side]


def _rwkv_inputs(raw_refs, small_ref, mu_refs, mu_s_ref, w0_ref, w2_ref, a0_ref, a2_ref, g2_ref,
                 prev_refs, prev_s, seqs):
    def shift_mix(x, prev_ref, q, mu):
        rows = x.shape[0]
        xp = pltpu.roll(x, 1, axis=0)
        row = lax.broadcasted_iota(jnp.int32, x.shape, 0)
        xp = jnp.where(row == 0, prev_ref[q, 0:1, :], xp)
        prev_ref[q, 0:1, :] = x[rows - 1:rows, :]
        return x + (xp - x) * mu

    r, k, v = ([shift_mix(ref[q].astype(F32), prev, q, mu[...]) for q in seqs]
               for ref, prev, mu in zip(raw_refs, prev_refs, mu_refs))
    sm = jnp.concatenate([shift_mix(small_ref[q], prev_s, q, mu_s_ref[...]) for q in seqs], axis=0)
    xw = sm[:, 0:LORA_PAD]
    xa = sm[:, DECAY_LORA:DECAY_LORA + LORA_PAD]
    xg = sm[:, DECAY_LORA + AAA_LORA:DECAY_LORA + AAA_LORA + GATE_LORA]

    z = w0_ref[...] + _dot(jnp.tanh(xw).astype(BF16), w2_ref[...])
    w_raw = -(jnp.maximum(-z, 0.0) + jnp.log(1.0 + jnp.exp(-jnp.abs(z)))) - 0.5
    lw = -jnp.exp(w_raw)
    a = _sigmoid(a0_ref[...] + _dot(xa.astype(BF16), a2_ref[...]))
    g = _dot(_sigmoid(xg).astype(BF16), g2_ref[...])
    ch = lw.shape[0] // len(seqs)
    per_seq = lambda x: [x[j * ch:(j + 1) * ch, :] for j in range(len(seqs))]
    return r, k, v, per_seq(lw), per_seq(a), per_seq(g)


def _rwkv_kernel(r_ref, k_ref, v_ref, small_ref, mu_r_ref, mu_k_ref, mu_v_ref, mu_s_ref,
                 w0_ref, w2_ref, a0_ref, a2_ref, g2_ref,
                 kk_ref, ka_ref, rk_ref, lng_ref, lnb_ref, *rest, n_tiles, seq_groups, n_cast):
    cast_src, y_ref, cast_dst = rest[:n_cast], rest[n_cast], rest[n_cast + 1:2 * n_cast + 1]
    state, prev_r, prev_k, prev_v, prev_s = rest[2 * n_cast + 1:]
    for src, dst in zip(cast_src, cast_dst):
        dst[...] = src[...].astype(BF16)

    @pl.when(pl.program_id(1) == 0)
    def _():
        for ref in (state, prev_r, prev_k, prev_v, prev_s):
            ref[...] = jnp.zeros_like(ref)

    ch = y_ref.shape[1]
    hd = RWKV_HEAD_DIM
    lane = lax.broadcasted_iota(jnp.int32, (1, LANES), 1)
    m0 = lane < hd
    n2 = HEADS_PER_TILE * ch
    rr = lax.broadcasted_iota(jnp.int32, (n2, n2), 0)
    cc = lax.broadcasted_iota(jnp.int32, (n2, n2), 1)
    same = (rr // ch) == (cc // ch)
    strict = same & (rr > cc)
    incl = same & (rr >= cc)
    lr = lax.broadcasted_iota(jnp.int32, (ch, ch), 0)
    lc = lax.broadcasted_iota(jnp.int32, (ch, ch), 1)
    lower = (lc <= lr).astype(BF16)

    def head_sum(x):
        s0 = jnp.sum(jnp.where(m0, x, 0.0), axis=-1, keepdims=True)
        s1 = jnp.sum(jnp.where(m0, 0.0, x), axis=-1, keepdims=True)
        return jnp.where(m0, s0, s1)

    def stack(x):
        return jnp.concatenate([jnp.where(m0, x, 0.0), jnp.where(m0, 0.0, x)], axis=0)

    sls = [slice(p * LANES, (p + 1) * LANES) for p in range(n_tiles)]

    def each(fn, *cols):
        return [fn(*args) for args in zip(*cols)]

    stack2 = lambda x, y: jnp.concatenate([stack(x), stack(y)], axis=0).astype(BF16)
    incl2 = jnp.concatenate([incl, incl], axis=1)

    def prepare(seqs):
        per_chain = lambda ref: [ref[:, sl] for _ in seqs for sl in sls]
        full = _rwkv_inputs((r_ref, k_ref, v_ref), small_ref, (mu_r_ref, mu_k_ref, mu_v_ref), mu_s_ref,
                            w0_ref, w2_ref, a0_ref, a2_ref, g2_ref, (prev_r, prev_k, prev_v), prev_s, seqs)
        r, k, v, lw, a, g = ([x[j][:, sl] for j in range(len(seqs)) for sl in sls] for x in full)
        yield
        kk = each(lambda k_, w_: k_ * w_, k, per_chain(kk_ref))
        kk = each(lambda x: x * lax.rsqrt(jnp.maximum(head_sum(x * x), 1e-24)), kk)
        kr = each(lambda k_, a_, w_: k_ * (1.0 + (a_ - 1.0) * w_), k, a, per_chain(ka_ref))
        yield
        cum = each(lambda x: sum(_dot(lower, part) for part in _split3(x)[:2]), lw)
        yield
        gam = each(jnp.exp, cum)
        ginv = each(lambda x: jnp.exp(-x), cum)
        g_end = each(lambda x: x[ch - 1:ch, :], gam)
        yield
        a_t = each(lambda kk_, c_, lw_: -kk_ * jnp.exp(c_ - lw_), kk, cum, lw)
        r_t = each(lambda r_, g_: r_ * g_, r, gam)
        lhs = each(stack2, a_t, r_t)
        yield
        b_h = each(lambda kk_, a_, gi_: kk_ * a_ * gi_, kk, a, ginv)
        k_h = each(lambda kr_, gi_: kr_ * gi_, kr, ginv)
        rhs = each(stack2, b_h, k_h)
        yield
        vs = each(lambda v_: stack(v_).astype(BF16), v)
        bk = each(lambda b_, k_, ge_: stack2(b_ * ge_, k_ * ge_), b_h, k_h, g_end)
        yield
        bonus = each(lambda r_, kr_, w_, v_: head_sum(r_ * kr_ * w_) * v_, r, kr, per_chain(rk_ref), v)
        return lhs, rhs, vs, bk, g_end, bonus, g

    def advance(seqs, prepared):
        lhs, rhs, vs, bk, g_end, bonus, g = prepared
        per_chain = lambda ref: [ref[:, sl] for _ in seqs for sl in sls]
        slots = [q * n_tiles + p for q in seqs for p in range(n_tiles)]
        sc = each(_dot_nt, lhs, rhs)
        yield
        ab = each(lambda s: jnp.where(strict, s[0:n2, 0:n2], 0.0).astype(BF16), sc)
        ak = each(lambda s: jnp.where(strict, s[0:n2, n2:2 * n2], 0.0).astype(BF16), sc)
        rbk = each(lambda s: jnp.where(incl2, s[n2:2 * n2, :], 0.0).astype(BF16), sc)
        yield
        st = [state[i] for i in slots]
        ah = each(lambda l_, s_: _dot_nt(l_, s_.astype(BF16)), lhs, st)
        yield
        x = each(lambda ah_, ak_, vs_: ah_[0:n2] + _dot(ak_, vs_), ah, ak, vs)
        yield
        pw = ab
        n_lvl = ch.bit_length() - 1
        for lvl in range(n_lvl):
            x = each(lambda x_, p_: x_ + _dot(p_, x_.astype(BF16)), x, pw)
            yield
            if lvl + 1 < n_lvl:
                pw = each(lambda p_: _dot(p_, p_).astype(BF16), pw)
                yield
        uv = each(lambda x_, vs_: jnp.concatenate([x_.astype(BF16), vs_], axis=0), x, vs)
        ys = each(lambda ah_, m_, uv_: ah_[n2:2 * n2] + _dot(m_, uv_), ah, rbk, uv)
        yield
        y = each(lambda ys_: ys_[0:ch] + ys_[ch:n2], ys)
        new_st = each(lambda s_, ge_, uv_, bk_: s_ * ge_ + _dot_tn(uv_, bk_), st, g_end, uv, bk)
        for i, s_new in zip(slots, new_st):
            state[i] = s_new
        yield
        mu = each(lambda y_: head_sum(y_) * (1.0 / hd), y)
        yc = each(lambda y_, m_: y_ - m_, y, mu)
        var = each(lambda c_: head_sum(c_ * c_) * (1.0 / hd), yc)
        yield
        yn = each(lambda c_, v_, g_, b_: c_ * lax.rsqrt(v_ + RWKV_LNX_EPS) * g_ + b_,
                  yc, var, per_chain(lng_ref), per_chain(lnb_ref))
        for j, i in enumerate(slots):
            y_ref[i // n_tiles, :, sls[i % n_tiles]] = ((yn[j] + bonus[j]) * g[j]).astype(BF16)

    starts = [sum(seq_groups[:i]) for i in range(len(seq_groups))]
    groups = [tuple(range(s, s + n)) for s, n in zip(starts, seq_groups)]
    prepared = _run_stages(prepare(groups[0]))
    for gi, seqs in enumerate(groups):
        nxt = prepare(groups[gi + 1]) if gi + 1 < len(groups) else None
        prepared = _run_stages(advance(seqs, prepared), nxt)


def _slab_spec(w, n_steps, step_of):
    rep = 1
    while w.shape[0] % (n_steps // rep * 16):
        rep *= 2
        assert rep <= n_steps
    return pl.BlockSpec((w.shape[0] // (n_steps // rep), w.shape[1]),
                        lambda *idx: (step_of(*idx) // rep, 0))


def _rwkv(main, small, mus, mu_s, w0, w2p, a0, a2p, g2, head_rows, cast_weights, *, batch, seq,
          seq_groups=(2, 2)):
    n_seq = sum(seq_groups)
    t = main.shape[0]
    width = RWKV_HEADS * RWKV_HEAD_DIM
    nsm = small.shape[1]
    col0 = (4 * FOX_HEADS * FOX_HEAD_DIM) // width
    ch = RWKV_CHUNK
    steps = seq // ch
    n_tiles = width // LANES
    assert batch % n_seq == 0
    groups = batch // n_seq
    grouped = lambda x: x.reshape(groups, n_seq, seq, x.shape[-1])
    act = lambda cols, blk: pl.BlockSpec((None, n_seq, ch, cols), lambda b, c: (b, 0, c, blk))
    row = lambda n: pl.BlockSpec((1, n), lambda b, c: (0, 0))
    whole = lambda w: pl.BlockSpec(w.shape, lambda b, c: (0, 0))
    slabs = [_slab_spec(w, groups * steps, lambda b, c: b * steps + c) for w in cast_weights]
    pipelined = n_seq * (4 * _nbytes((ch, width), BF16) + _nbytes((ch, nsm), F32))
    pipelined += sum(_nbytes(s.block_shape, F32) + _nbytes(s.block_shape, BF16) for s in slabs)
    pipelined += sum(_nbytes(w.shape, BF16) for w in (w2p, a2p, g2))
    scratch = [(n_seq * n_tiles, LANES, LANES)] + [(n_seq, 8, width)] * 3 + [(n_seq, 8, nsm)]
    y, *copies = pl.pallas_call(
        functools.partial(_rwkv_kernel, n_tiles=n_tiles, seq_groups=seq_groups, n_cast=len(slabs)),
        grid=(groups, steps),
        in_specs=[act(width, col0), act(width, col0 + 1), act(width, col0 + 2), act(nsm, 0),
                  row(width), row(width), row(width), row(nsm),
                  row(width), whole(w2p), row(width), whole(a2p), whole(g2)]
        + [row(width)] * 5 + slabs,
        out_specs=[act(width, 0)] + slabs,
        out_shape=[jax.ShapeDtypeStruct((groups, n_seq, seq, width), BF16)]
        + [jax.ShapeDtypeStruct(w.shape, BF16) for w in cast_weights],
        scratch_shapes=[pltpu.VMEM(s, F32) for s in scratch],
        compiler_params=pltpu.CompilerParams(
            dimension_semantics=("arbitrary", "arbitrary"),
            vmem_limit_bytes=_vmem_limit(pipelined, sum(_nbytes(s, F32) for s in scratch), 24 << 20),
        ),
        name="rwkv",
    )(grouped(main), grouped(main), grouped(main), grouped(small), *mus, mu_s, w0, w2p, a0, a2p, g2,
      *head_rows, *cast_weights)
    return y.reshape(t, width), copies


def _mixout_kernel(o_ref, y_ref, ga0_ref, ga1_ref, gb0_ref, gb1_ref, wa_hbm, wb_hbm, wo_hbm, x_ref,
                   gate_ref, lng_ref, lnb_ref, scale_ref, shift_ref, x1_ref, h2_ref,
                   wa_ref, wb_ref, wo_ref, u_ref, sems, *, n_sub):
    @pl.when(pl.program_id(0) == 0)
    def _():
        copies = [pltpu.make_async_copy(src, dst, sems.at[k])
                  for k, (src, dst) in enumerate(((wa_hbm, wa_ref), (wb_hbm, wb_ref), (wo_hbm, wo_ref)))]
        for c in copies:
            c.start()
        for c in copies:
            c.wait()

    tm, d = x_ref.shape
    rows = tm // n_sub
    half = d // 2
    sl = [slice(s * rows, (s + 1) * rows) for s in range(n_sub)]

    def branches(s):
        u_ref[s % 2, 0] = _dot(o_ref[sl[s], :], wa_ref[...])
        u_ref[s % 2, 1] = _dot(y_ref[sl[s], :], wb_ref[...])

    def mixed(s):
        parts = []
        for c, (ga, gb) in enumerate(((ga0_ref, gb0_ref), (ga1_ref, gb1_ref))):
            cs = slice(c * half, (c + 1) * half)
            parts.append(_sigmoid(ga[sl[s], :].astype(F32)) * u_ref[s % 2, 0, :, cs]
                         + _sigmoid(gb[sl[s], :].astype(F32)) * u_ref[s % 2, 1, :, cs])
        x1_ref[sl[s], :] = _dot(jnp.concatenate(parts, axis=1).astype(BF16), wo_ref[...])

    def norms(s):
        z = DEEPNORM_ALPHA * x_ref[sl[s], :] + gate_ref[0] * x1_ref[sl[s], :]
        x1 = _layernorm(z) * lng_ref[...] + lnb_ref[...]
        x1_ref[sl[s], :] = x1
        h2_ref[sl[s], :] = (_layernorm(x1) * (1.0 + scale_ref[0]) + shift_ref[0]).astype(BF16)

    branches(0)
    for s in range(n_sub):
        if s + 1 < n_sub:
            branches(s + 1)
        mixed(s)
        if s > 0:
            norms(s - 1)
    norms(n_sub - 1)


def _mixout(o, y, main, wa, wb, wo, x2d, gate1, ln_g, ln_b, scale2, shift2, *, seq, tm=512, n_sub=4):
    t, d = x2d.shape
    kdim = o.shape[1]
    half = d // 2
    gate_blk0 = (main.shape[1] - 2 * d) // half
    tiles_per_batch = seq // tm
    mod = pl.BlockSpec((1, 1, d), lambda i: (i // tiles_per_batch, 0, 0))
    row = pl.BlockSpec((1, d), lambda i: (0, 0))
    gate = lambda c: pl.BlockSpec((tm, half), lambda i: (i, gate_blk0 + c))
    in_hbm = pl.BlockSpec(memory_space=pl.ANY)
    pipelined = (2 * _nbytes((tm, kdim), BF16) + 4 * _nbytes((tm, half), BF16)
                 + 2 * _nbytes((tm, d), F32) + _nbytes((tm, d), BF16))
    weights = _nbytes(wa.shape, BF16) + _nbytes(wb.shape, BF16) + _nbytes(wo.shape, BF16)
    return pl.pallas_call(
        functools.partial(_mixout_kernel, n_sub=n_sub),
        grid=(t // tm,),
        in_specs=[
            pl.BlockSpec((tm, kdim), lambda i: (i, 0)),
            pl.BlockSpec((tm, kdim), lambda i: (i, 0)),
            gate(0), gate(1), gate(2), gate(3),
            in_hbm, in_hbm, in_hbm,
            pl.BlockSpec((tm, d), lambda i: (i, 0)),
            mod, row, row, mod, mod,
        ],
        out_specs=[pl.BlockSpec((tm, d), lambda i: (i, 0)),
                   pl.BlockSpec((tm, d), lambda i: (i, 0))],
        out_shape=[jax.ShapeDtypeStruct((t, d), F32), jax.ShapeDtypeStruct((t, d), BF16)],
        scratch_shapes=[pltpu.VMEM(wa.shape, BF16), pltpu.VMEM(wb.shape, BF16), pltpu.VMEM(wo.shape, BF16),
                        pltpu.VMEM((2, 2, tm // n_sub, d), F32), pltpu.SemaphoreType.DMA((3,))],
        compiler_params=pltpu.CompilerParams(
            dimension_semantics=("arbitrary",),
            vmem_limit_bytes=_vmem_limit(pipelined, weights, 8 * _nbytes((tm // n_sub, d), F32)),
        ),
        name="mixout",
    )(o, y, main, main, main, main, wa, wb, wo, x2d, gate1, ln_g, ln_b, scale2, shift2)


def _ffn_kernel(h_ref, wg_ref, wu_ref, wd_ref, x1_ref, gate_ref, lng_ref, lnb_ref, o_ref, acc):
    f = pl.program_id(1)
    @pl.when(f == 0)
    def _():
        acc[...] = jnp.zeros_like(acc)

    h = h_ref[...]
    g = _dot(h, wg_ref[...])
    u = _dot(h, wu_ref[...])
    act = (g * _sigmoid(g) * u).astype(BF16)
    acc[...] += _dot(act, wd_ref[...])

    @pl.when(f == pl.num_programs(1) - 1)
    def _():
        z = DEEPNORM_ALPHA * x1_ref[...] + gate_ref[0] * acc[...]
        o_ref[...] = _layernorm(z) * lng_ref[...] + lnb_ref[...]


def _ffn(h2, w_gu, w_down, x1, gate2, ln_g, ln_b, *, seq, tm=512, tf=512):
    t, d = x1.shape
    d_ff = w_down.shape[0]
    nf = d_ff // tf
    tiles_per_batch = seq // tm
    mod = pl.BlockSpec((1, 1, d), lambda i, f: (i // tiles_per_batch, 0, 0))
    row = pl.BlockSpec((1, d), lambda i, f: (0, 0))
    pipelined = (_nbytes((tm, d), BF16) + 3 * _nbytes((d, tf), BF16) + 2 * _nbytes((tm, d), F32))
    return pl.pallas_call(
        _ffn_kernel,
        grid=(t // tm, nf),
        in_specs=[
            pl.BlockSpec((tm, d), lambda i, f: (i, 0)),
            pl.BlockSpec((d, tf), lambda i, f: (0, f)),
            pl.BlockSpec((d, tf), lambda i, f: (0, nf + f)),
            pl.BlockSpec((tf, d), lambda i, f: (f, 0)),
            pl.BlockSpec((tm, d), lambda i, f: (i, 0)),
            mod, row, row,
        ],
        out_specs=pl.BlockSpec((tm, d), lambda i, f: (i, 0)),
        out_shape=jax.ShapeDtypeStruct((t, d), F32),
        scratch_shapes=[pltpu.VMEM((tm, d), F32)],
        compiler_params=pltpu.CompilerParams(
            dimension_semantics=("arbitrary", "arbitrary"),
            vmem_limit_bytes=_vmem_limit(pipelined, _nbytes((tm, d), F32),
                                         2 * _nbytes((tm, d), F32) + 4 * _nbytes((tm, tf), F32)),
        ),
        name="ffn",
    )(h2, w_gu, w_gu, w_down, x1, gate2, ln_g, ln_b)


def _pad_rows(w, rows):
    return jnp.zeros((rows, w.shape[1]), w.dtype).at[:w.shape[0]].set(w)


def kernel(x, c, w_ada, b_ada, w_in, b_fgate, q_norm_g, k_norm_g, rwkv_mu, rwkv_w0, rwkv_w2, rwkv_a0, rwkv_a2, rwkv_g2, rwkv_k_k, rwkv_k_a, rwkv_r_k, rwkv_lnx_g, rwkv_lnx_b, w_branch_a, w_branch_b, w_out, ln1_g, ln1_b, w_ffn_gu, w_ffn_down, ln2_g, ln2_b):
    batch, seq, d = x.shape
    t = batch * seq
    fox_w = FOX_HEADS * FOX_HEAD_DIM
    rw_w = RWKV_HEADS * RWKV_HEAD_DIM
    fox_cols = 4 * fox_w + FOX_HEADS
    lora0 = fox_cols + 3 * rw_w
    gate0 = lora0 + DECAY_LORA + AAA_LORA + GATE_LORA
    assert seq % RWKV_CHUNK == 0 and w_in.shape[1] == gate0 + 2 * d

    lora_w = gate0 - lora0
    mu_r = rwkv_mu[None, 0:rw_w]
    mu_k = rwkv_mu[None, rw_w:2 * rw_w]
    mu_v = rwkv_mu[None, 2 * rw_w:3 * rw_w]
    mu_s = jnp.zeros((1, INPROJ_TN), rwkv_mu.dtype).at[0, :lora_w].set(rwkv_mu[3 * rw_w:])
    w2p = _pad_rows(rwkv_w2, LORA_PAD).astype(BF16)
    a2p = _pad_rows(rwkv_a2, LORA_PAD).astype(BF16)
    g2 = rwkv_g2.astype(BF16)
    row = lambda v: v.reshape(1, -1)

    c_pad = jnp.zeros((8, d), c.dtype).at[:batch].set(c)
    mod_head = _ada(c_pad, w_ada, row(b_ada), 2 * d)[:batch].reshape(batch, 2, 1, d)
    shift1, scale1 = mod_head[:, 0], mod_head[:, 1]

    x2d = x.reshape(t, d)
    main, small, ft = _inproj(x2d, scale1, shift1, w_in.T, row(q_norm_g), row(k_norm_g), seq=seq,
                              group_rows=[(0, 4 * fox_w), (fox_cols, 3 * rw_w), (gate0, 2 * d)],
                              lora_row=lora0, fgate_row=4 * fox_w, tn=INPROJ_TN)

    blocks_per_seq = seq // LANES
    ft_rows = ft.reshape(FOX_HEADS * batch * blocks_per_seq, LANES)
    bias_rows = jnp.broadcast_to(b_fgate[:, None, None], (FOX_HEADS, batch * blocks_per_seq, LANES))
    cum = _fox_prep(ft_rows, bias_rows.reshape(ft_rows.shape), blocks_per_seq=blocks_per_seq)
    cum = cum.reshape(FOX_HEADS, batch, 1, seq)
    o, mod_tail = _fox_attn(main, cum, c_pad, w_ada, row(b_ada), 2 * d, batch=batch, seq=seq)
    gate1, shift2, scale2, gate2 = (mod_tail[:batch].reshape(batch, 4, 1, d)[:, i] for i in range(4))

    y, (wa, wb, wo, wgu, wdn) = _rwkv(
        main, small, (mu_r, mu_k, mu_v), mu_s, row(rwkv_w0), w2p, row(rwkv_a0), a2p, g2,
        [row(p) for p in (rwkv_k_k, rwkv_k_a, rwkv_r_k, rwkv_lnx_g, rwkv_lnx_b)],
        [w_branch_a, w_branch_b, w_out, w_ffn_gu, w_ffn_down], batch=batch, seq=seq)

    x1, h2 = _mixout(o, y, main, wa, wb, wo, x2d, gate1, row(ln1_g), row(ln1_b), scale2, shift2, seq=seq)

    out = _ffn(h2, wgu, wdn, x1, gate2, row(ln2_g), row(ln2_b), seq=seq)
    return out.reshape(batch, seq, d)
```
